```python
import math
import jax, jax.numpy as jnp
from jax import lax
import numpy as np

D_MODEL = 2048
BATCH = 4
SEQ = 4096
DEPTH = 2

GRID_W = 64
CTX_LEN = 256

N_HEADS = D_MODEL // 128
QK_NOPE = 128
QK_ROPE = 64
V_DIM = 128
Q_RANK = 384
KV_RANK = 512
ROPE_THETA = 10000.0
Q_BLOCK = 128

POOL_DIM = 1024
POOL_GROUPS = 4
POOL_GROUP_DIM = POOL_DIM // POOL_GROUPS
POOL_WINDOWS = (2, 4, 8, 16)
POOL_GROUP_OUT = D_MODEL // POOL_GROUPS

KV_COLS = KV_RANK + QK_ROPE
Q_END = KV_COLS + Q_RANK
POOL_END = Q_END + POOL_DIM
IN_COLS = POOL_END + 2 * D_MODEL

N_EXPERTS = 16
N_GROUPS = 4
EXPERTS_PER_GROUP = N_EXPERTS // N_GROUPS
TOP_K = 2
D_EXPERT = 512
D_SHARED = 512

EPS = 1e-6
N_MOD = 6

kernel_name = "hybrid_mla_pool_moe_dit_block"


def _rmsnorm(x, g):
    xf = x.astype(jnp.float32)
    y = xf * lax.rsqrt(jnp.mean(xf * xf, axis=-1, keepdims=True) + EPS)
    return y.astype(x.dtype) * g


def _modulate(h, shift, scale):
    return h * (1 + scale) + shift


def _axial_rope_tables(rows):
    row = jnp.repeat(jnp.arange(rows), GRID_W).astype(jnp.float32)
    col = jnp.tile(jnp.arange(GRID_W), rows).astype(jnp.float32)
    axis_dim = QK_ROPE // 2
    inv = ROPE_THETA ** (-jnp.arange(0, axis_dim, 2, dtype=jnp.float32) / axis_dim)
    ang = jnp.concatenate([row[:, None] * inv, col[:, None] * inv], axis=-1)
    return jnp.cos(ang), jnp.sin(ang)


def _apply_rope(x, cos, sin):
    half = x.shape[-1] // 2
    x1, x2 = x[..., :half], x[..., half:]
    cos = cos.astype(x.dtype)
    sin = sin.astype(x.dtype)
    return jnp.concatenate([x1 * cos - x2 * sin, x2 * cos + x1 * sin], axis=-1)


def _mla_keys(p_kv, kv_norm_g, w_kvb, cos, sin):
    b, n, _ = p_kv.shape
    kv = (_rmsnorm(p_kv[..., :KV_RANK], kv_norm_g) @ w_kvb).reshape(b, n, N_HEADS, QK_NOPE + V_DIM)
    k_nope, v = kv[..., :QK_NOPE], kv[..., QK_NOPE:]
    k_pe = p_kv[..., KV_RANK:]
    if cos is not None:
        k_pe = _apply_rope(k_pe, cos, sin)
    k_pe = jnp.broadcast_to(k_pe[:, :, None, :], (b, n, N_HEADS, QK_ROPE))
    return jnp.concatenate([k_nope, k_pe], axis=-1), v


def _mla_queries(q_lat, q_norm_g, w_qb, cos, sin):
    b, n, _ = q_lat.shape
    q = (_rmsnorm(q_lat, q_norm_g) @ w_qb).reshape(b, n, N_HEADS, QK_NOPE + QK_ROPE)
    q_nope, q_pe = q[..., :QK_NOPE], q[..., QK_NOPE:]
    if cos is not None:
        q_pe = _apply_rope(q_pe, cos[:, None, :], sin[:, None, :])
    return jnp.concatenate([q_nope, q_pe], axis=-1)


def _attend(q, k, v):
    b, n, h, dq = q.shape
    nb = n // Q_BLOCK
    scale = 1.0 / math.sqrt(dq)
    qb = q.reshape(b, nb, Q_BLOCK, h, dq).transpose(1, 0, 2, 3, 4)

    def block(q_blk):
        s = jnp.einsum('bqhd,bkhd->bhqk', q_blk, k, preferred_element_type=jnp.float32) * scale
        p = jax.nn.softmax(s, axis=-1).astype(v.dtype)
        return jnp.einsum('bhqk,bkhd->bqhd', p, v)

    o = lax.map(block, qb)
    return o.transpose(1, 0, 2, 3, 4).reshape(b, n, h * v.shape[-1])


def _multiscale_pool(u, w_pool, pool_scale):
    b, n, _ = u.shape
    uf = u.astype(jnp.float32)
    cs = jnp.concatenate([jnp.zeros((b, 1, POOL_DIM), jnp.float32), jnp.cumsum(uf, axis=1)], axis=1)
    t = jnp.arange(n)
    outs = []
    for g, w in enumerate(POOL_WINDOWS):
        lo = jnp.clip(t - w // 2, 0, n)
        hi = jnp.clip(t + w // 2, 0, n)
        sl = slice(g * POOL_GROUP_DIM, (g + 1) * POOL_GROUP_DIM)
        csg = cs[..., sl]
        mean = (csg[:, hi] - csg[:, lo]) / (hi - lo).astype(jnp.float32)[:, None]
        outs.append(mean - uf[..., sl])
    z = jnp.stack(outs, axis=2).astype(u.dtype)
    y = jnp.einsum('bngc,gcd->bngd', z, w_pool).reshape(b, n, D_MODEL)
    return y * pool_scale


def _mixer_out(p, k_all, v_all, cos, sin, q_norm_g, w_qb, w_pool, pool_scale, w_out):
    q = _mla_queries(p[..., KV_COLS:Q_END], q_norm_g, w_qb, cos, sin)
    y_mla = _attend(q, k_all, v_all)
    y_pool = _multiscale_pool(p[..., Q_END:POOL_END], w_pool, pool_scale)
    g_mla = jax.nn.sigmoid(p[..., POOL_END:POOL_END + D_MODEL])
    g_pool = jax.nn.sigmoid(p[..., POOL_END + D_MODEL:])
    return (g_mla * y_mla + g_pool * y_pool) @ w_out


def _swiglu(t, wg, wu, wd):
    return (jax.nn.silu(t @ wg) * (t @ wu)) @ wd


def _moe(h, w_router, router_bias, wg, wu, wd, sg, su, sd):
    shape = h.shape
    t = h.reshape(-1, D_MODEL)
    scores = jax.nn.sigmoid((t @ w_router).astype(jnp.float32))
    sel = scores + router_bias.astype(jnp.float32)
    grp = sel.reshape(-1, N_GROUPS, EXPERTS_PER_GROUP)
    grp_score = lax.top_k(grp, TOP_K)[0].sum(axis=-1)
    g_idx = jnp.argmax(grp_score, axis=-1)
    in_grp = jnp.take_along_axis(grp, g_idx[:, None, None], axis=1)[:, 0]
    _, local = lax.top_k(in_grp, TOP_K)
    expert = g_idx[:, None] * EXPERTS_PER_GROUP + local
    w = jnp.take_along_axis(scores, expert, axis=-1)
    w = w / jnp.sum(w, axis=-1, keepdims=True)
    combine = jnp.einsum('tk,tke->te', w, jax.nn.one_hot(expert, N_EXPERTS, dtype=jnp.float32)).astype(t.dtype)
    y = _swiglu(t, sg, su, sd)
    for e in range(N_EXPERTS):
        y = y + combine[:, e:e + 1] * _swiglu(t, wg[e], wu[e], wd[e])
    return y.reshape(shape)


def setup_inputs(seed: int = 0) -> dict:
    key = jax.random.key(seed)
    ks = jax.random.split(key, 32)
    L, D = DEPTH, D_MODEL

    def nrm(k, shape, scale):
        return jax.random.normal(k, shape, jnp.float32) * scale

    def gain(k, shape):
        return 1.0 + nrm(k, shape, 0.02)

    return {
        'x': nrm(ks[0], (BATCH, SEQ, D), 1.0),
        'c': nrm(ks[1], (BATCH, D), 1.0),
        'ctx': nrm(ks[2], (BATCH, CTX_LEN, D), 1.0),
        'c_ctx': nrm(ks[3], (D,), 1.0),
        'w_ada': nrm(ks[4], (L, D, N_MOD * D), 0.5 * D ** -0.5),
        'b_ada': nrm(ks[5], (L, N_MOD * D), 0.02),
        'norm_mix_g': gain(ks[6], (L, D)),
        'norm_ffn_g': gain(ks[7], (L, D)),
        'w_in': nrm(ks[8], (L, D, IN_COLS), D ** -0.5),
        'q_norm_g': gain(ks[9], (L, Q_RANK)),
        'kv_norm_g': gain(ks[10], (L, KV_RANK)),
        'w_qb': nrm(ks[11], (L, Q_RANK, N_HEADS * (QK_NOPE + QK_ROPE)), Q_RANK ** -0.5),
        'w_kvb': nrm(ks[12], (L, KV_RANK, N_HEADS * (QK_NOPE + V_DIM)), KV_RANK ** -0.5),
        'w_pool': nrm(ks[13], (L, POOL_GROUPS, POOL_GROUP_DIM, POOL_GROUP_OUT), POOL_GROUP_DIM ** -0.5),
        'pool_scale': 1.0 + nrm(ks[14], (L, D), 0.1),
        'w_out': nrm(ks[15], (L, D, D), D ** -0.5),
        'w_router': nrm(ks[16], (D, N_EXPERTS), D ** -0.5),
        'router_bias': nrm(ks[17], (N_EXPERTS,), 0.01),
        'w_exp_gate': nrm(ks[18], (L, N_EXPERTS, D, D_EXPERT), D ** -0.5),
        'w_exp_up': nrm(ks[19], (L, N_EXPERTS, D, D_EXPERT), D ** -0.5),
        'w_exp_down': nrm(ks[20], (L, N_EXPERTS, D_EXPERT, D), D_EXPERT ** -0.5),
        'w_sh_gate': nrm(ks[21], (L, D, D_SHARED), D ** -0.5),
        'w_sh_up': nrm(ks[22], (L, D, D_SHARED), D ** -0.5),
        'w_sh_down': nrm(ks[23], (L, D_SHARED, D), D_SHARED ** -0.5),
        'final_norm_g': gain(ks[24], (D,)),
    }


def reference(x, c, ctx, c_ctx, w_ada, b_ada, norm_mix_g, norm_ffn_g, w_in, q_norm_g, kv_norm_g,
              w_qb, w_kvb, w_pool, pool_scale, w_out, w_router, router_bias,
              w_exp_gate, w_exp_up, w_exp_down, w_sh_gate, w_sh_up, w_sh_down, final_norm_g):
    seq = x.shape[1]
    rows = seq // GRID_W
    cos, sin = _axial_rope_tables(rows)
    xc = ctx
    for l in range(DEPTH):
        last = l == DEPTH - 1
        mod = jax.nn.silu(c) @ w_ada[l] + b_ada[l]
        mod_c = jax.nn.silu(c_ctx) @ w_ada[l] + b_ada[l]
        sh1, sc1, g1, sh2, sc2, g2 = [m[:, None, :] for m in jnp.split(mod, N_MOD, axis=-1)]
        sh1c, sc1c, g1c, sh2c, sc2c, g2c = jnp.split(mod_c, N_MOD, axis=-1)

        h = _modulate(_rmsnorm(x, norm_mix_g[l]), sh1, sc1)
        hc = _modulate(_rmsnorm(xc, norm_mix_g[l]), sh1c, sc1c)
        p = h @ w_in[l]
        pc = hc @ (w_in[l][:, :KV_COLS] if last else w_in[l])
        k_lat, v_lat = _mla_keys(p[..., :KV_COLS], kv_norm_g[l], w_kvb[l], cos, sin)
        k_ctx, v_ctx = _mla_keys(pc[..., :KV_COLS], kv_norm_g[l], w_kvb[l], None, None)
        k_all = jnp.concatenate([k_ctx, k_lat], axis=1)
        v_all = jnp.concatenate([v_ctx, v_lat], axis=1)
        y = _mixer_out(p, k_all, v_all, cos, sin, q_norm_g[l], w_qb[l], w_pool[l], pool_scale[l], w_out[l])
        x = x + g1 * y
        if not last:
            yc = _mixer_out(pc, k_ctx, v_ctx, None, None, q_norm_g[l], w_qb[l], w_pool[l], pool_scale[l], w_out[l])
            xc = xc + g1c * yc

        h2 = _modulate(_rmsnorm(x, norm_ffn_g[l]), sh2, sc2)
        moe_args = (w_router, router_bias, w_exp_gate[l], w_exp_up[l], w_exp_down[l],
                    w_sh_gate[l], w_sh_up[l], w_sh_down[l])
        if last:
            x = x + g2 * _moe(h2, *moe_args)
        else:
            h2c = _modulate(_rmsnorm(xc, norm_ffn_g[l]), sh2c, sc2c)
            n_ctx = xc.shape[1]
            f = _moe(jnp.concatenate([h2c, h2], axis=1), *moe_args)
            xc = xc + g2c * f[:, :n_ctx]
            x = x + g2 * f[:, n_ctx:]
    return _rmsnorm(x, final_norm_g)
```

```python
import functools
import math

import jax
import jax.numpy as jnp
from jax import lax
from jax.experimental import pallas as pl
from jax.experimental.pallas import tpu as pltpu

F32 = jnp.float32
BF16 = jnp.bfloat16

D_MODEL = 2048
GRID_W = 64
N_HEADS = 16
QK_NOPE = 128
QK_ROPE = 64
V_DIM = 128
Q_RANK = 384
KV_RANK = 512
ROPE_THETA = 10000.0
POOL_DIM = 1024
POOL_GROUPS = 4
POOL_GROUP_DIM = 256
POOL_WINDOWS = (2, 4, 8, 16)
POOL_GROUP_OUT = 512
KV_COLS = KV_RANK + QK_ROPE
Q_END = KV_COLS + Q_RANK
POOL_END = Q_END + POOL_DIM
N_EXPERTS = 16
N_GROUPS = 4
EXPERTS_PER_GROUP = 4
D_EXPERT = 512
EPS = 1e-6
N_MOD = 6

LANES = 128
HEAD_W = 2 * LANES
LAT_W = 1024
P_COLS = LAT_W + POOL_DIM + 2 * D_MODEL
POOL_HALO = 16
MOD_ROWS = 8
Q_SCALE = (1.0 / math.sqrt(QK_NOPE + QK_ROPE)) * math.log2(math.e)
VMEM_LIMIT = 56 * 1024 * 1024


def _sigmoid(x):
    return 1.0 / (1.0 + jnp.exp(-x))


def _rms(xf, g):
    ms = jnp.mean(xf * xf, axis=-1, keepdims=True)
    return xf * lax.rsqrt(ms + EPS) * g


def _params(n_axes):
    return pltpu.CompilerParams(dimension_semantics=("arbitrary",) * n_axes,
                                vmem_limit_bytes=VMEM_LIMIT)


def _ada_kernel(c_ref, w_ref, b_ref, o_ref):
    cf = c_ref[...]
    a = (cf * _sigmoid(cf)).astype(BF16)
    o_ref[0] = jnp.dot(a, w_ref[0].astype(BF16), preferred_element_type=F32) + b_ref[0]


def _ada_mod(c_rows, w_ada, b_ada):
    depth, d, n = w_ada.shape
    tn = 1024
    return pl.pallas_call(
        _ada_kernel,
        grid=(depth, n // tn),
        in_specs=[pl.BlockSpec((MOD_ROWS, d), lambda l, j: (0, 0)),
                  pl.BlockSpec((1, d, tn), lambda l, j: (l, 0, j)),
                  pl.BlockSpec((1, 1, tn), lambda l, j: (l, 0, j))],
        out_specs=pl.BlockSpec((1, MOD_ROWS, tn), lambda l, j: (l, 0, j)),
        out_shape=jax.ShapeDtypeStruct((depth, MOD_ROWS, n), F32),
        compiler_params=_params(2),
        name="ada_mod",
    )(c_rows, w_ada, b_ada.reshape(depth, 1, n))


def _mod_spec(k, brow):
    return pl.BlockSpec((1, 1, D_MODEL), lambda i, *_: (brow(i) * N_MOD + k, 0, 0))


def _inproj_kernel(x_ref, g_ref, sh_ref, sc_ref, w_ref, o_ref, h_scr, *, n_plain):
    j = pl.program_id(1)

    @pl.when(j == 0)
    def _():
        y = _rms(x_ref[...], g_ref[...])
        h_scr[...] = (y * (1.0 + sc_ref[0]) + sh_ref[0]).astype(BF16)

    acc = jnp.dot(h_scr[...], w_ref[...], preferred_element_type=F32)

    @pl.when(j < n_plain)
    def _():
        o_ref[...] = acc.astype(BF16)

    @pl.when(j >= n_plain)
    def _():
        o_ref[...] = _sigmoid(acc).astype(BF16)


def _in_proj(x2, mod3, brow, g, w, n_cols):
    t, d = x2.shape
    tm = min(512, t)
    tn = 512
    return pl.pallas_call(
        functools.partial(_inproj_kernel, n_plain=(LAT_W + POOL_DIM) // tn),
        grid=(t // tm, n_cols // tn),
        in_specs=[pl.BlockSpec((tm, d), lambda i, j: (i, 0)),
                  pl.BlockSpec((1, d), lambda i, j: (0, 0)),
                  _mod_spec(0, lambda i: brow(i * tm)),
                  _mod_spec(1, lambda i: brow(i * tm)),
                  pl.BlockSpec((d, tn), lambda i, j: (0, j))],
        out_specs=pl.BlockSpec((tm, tn), lambda i, j: (i, j)),
        out_shape=jax.ShapeDtypeStruct((t, n_cols), BF16),
        scratch_shapes=[pltpu.VMEM((tm, d), BF16)],
        compiler_params=_params(2),
        name="in_proj",
    )(x2, g.reshape(1, d), mod3, mod3, w)


def _qkv_kernel(lat_ref, cos_ref, sin_ref, gq_ref, gkv_ref, wq_ref, wkv_ref, *out_refs, with_q):
    if with_q:
        q_ref, k_ref, v_ref = out_refs
    else:
        k_ref, v_ref = out_refs
    cos = cos_ref[...]
    sin = sin_ref[...]

    def rope(grp):
        return grp * cos + pltpu.roll(grp, 2 * (QK_ROPE // 2), axis=1) * sin

    kvn = _rms(lat_ref[:, :KV_RANK].astype(F32), gkv_ref[...]).astype(BF16)
    kpe = rope(lat_ref[:, KV_RANK + Q_RANK:].astype(F32)).astype(BF16)
    for h in range(N_HEADS):
        kv = jnp.dot(kvn, wkv_ref[:, h * HEAD_W:(h + 1) * HEAD_W], preferred_element_type=F32)
        k_ref[0, h, :, :QK_NOPE] = kv[:, :QK_NOPE].astype(BF16)
        k_ref[0, h, :, QK_NOPE:] = kpe
        v_ref[0, h] = kv[:, QK_NOPE:].astype(BF16)
    if with_q:
        qn = _rms(lat_ref[:, KV_RANK:KV_RANK + Q_RANK].astype(F32), gq_ref[...]).astype(BF16)
        for h in range(N_HEADS):
            qh = jnp.dot(qn, wq_ref[:, h * HEAD_W:(h + 1) * HEAD_W], preferred_element_type=F32)
            q_ref[0, h, :, :QK_NOPE] = (qh[:, :QK_NOPE] * Q_SCALE).astype(BF16)
            q_ref[0, h, :, QK_NOPE:] = (rope(qh[:, QK_NOPE:]) * Q_SCALE).astype(BF16)


def _qkv(p, b, n, cos_t, sin_t, gq, gkv, wq, wkv, with_q):
    tm = min(256, n)
    tpb = n // tm
    rope_tiles = cos_t.shape[0] // tm
    head_spec = lambda w: pl.BlockSpec((1, N_HEADS, tm, w), lambda i: (i // tpb, 0, i % tpb, 0))
    tab_spec = pl.BlockSpec((tm, LANES), lambda i: ((i % tpb) % rope_tiles, 0))
    out_shape = [jax.ShapeDtypeStruct((b, N_HEADS, n, HEAD_W), BF16),
                 jax.ShapeDtypeStruct((b, N_HEADS, n, V_DIM), BF16)]
    out_specs = [head_spec(HEAD_W), head_spec(V_DIM)]
    if with_q:
        out_shape = [jax.ShapeDtypeStruct((b, N_HEADS, n, HEAD_W), BF16)] + out_shape
        out_specs = [head_spec(HEAD_W)] + out_specs
    return pl.pallas_call(
        functools.partial(_qkv_kernel, with_q=with_q),
        grid=(b * tpb,),
        in_specs=[pl.BlockSpec((tm, LAT_W), lambda i: (i, 0)),
                  tab_spec, tab_spec,
                  pl.BlockSpec((1, Q_RANK), lambda i: (0, 0)),
                  pl.BlockSpec((1, KV_RANK), lambda i: (0, 0)),
                  pl.BlockSpec(wq.shape, lambda i: (0, 0)),
                  pl.BlockSpec(wkv.shape, lambda i: (0, 0))],
        out_specs=out_specs,
        out_shape=out_shape,
        compiler_params=_params(1),
        name="qkv",
    )(p, cos_t, sin_t, gq.reshape(1, Q_RANK), gkv.reshape(1, KV_RANK), wq, wkv)


def _lane_fold(x, op):
    r = x[:, :LANES]
    for c in range(LANES, x.shape[1], LANES):
        r = op(r, x[:, c:c + LANES])
    return r


def _attn_kernel(q_ref, *refs, n_seg, kb):
    k_refs, v_refs = refs[:n_seg], refs[n_seg:2 * n_seg]
    o_ref, s_scr = refs[2 * n_seg], refs[2 * n_seg + 1]
    q = q_ref[0, 0]
    chunks = []
    off = 0
    for kr, vr in zip(k_refs, v_refs):
        nk = kr.shape[2]
        for c0 in range(0, nk, kb):
            n = min(kb, nk - c0)
            chunks.append((kr, vr, c0, n, off))
            off += n
    mp = None
    for kr, vr, c0, n, o in chunks:
        s = lax.dot_general(q, kr[0, 0, c0:c0 + n, :], (((1,), (1,)), ((), ())),
                            preferred_element_type=F32)
        s_scr[:, o:o + n] = s
        m = _lane_fold(s, jnp.maximum)
        mp = m if mp is None else jnp.maximum(mp, m)
    mrow = jnp.max(mp, axis=-1, keepdims=True)
    lp = None
    acc = None
    for kr, vr, c0, n, o in chunks:
        pr = jnp.exp2(s_scr[:, o:o + n] - mrow)
        ls = _lane_fold(pr, jnp.add)
        lp = ls if lp is None else lp + ls
        pv = jnp.dot(pr.astype(BF16), vr[0, 0, c0:c0 + n, :], preferred_element_type=F32)
        acc = pv if acc is None else acc + pv
    l = jnp.sum(lp, axis=-1, keepdims=True)
    o_ref[0] = (acc / l).astype(BF16)


def _attention(q, ks, vs):
    b, hh, n, _ = q.shape
    tq = min(256, n)
    kb = 512
    nk_total = sum(k.shape[2] for k in ks)
    seg_spec = lambda a: pl.BlockSpec((1, 1) + a.shape[2:], lambda bi, h, i: (bi, h, 0, 0))
    return pl.pallas_call(
        functools.partial(_attn_kernel, n_seg=len(ks), kb=kb),
        grid=(b, hh, n // tq),
        in_specs=[pl.BlockSpec((1, 1, tq, HEAD_W), lambda bi, h, i: (bi, h, i, 0))]
                 + [seg_spec(a) for a in ks] + [seg_spec(a) for a in vs],
        out_specs=pl.BlockSpec((1, tq, V_DIM), lambda bi, h, i: (bi, i, h)),
        out_shape=jax.ShapeDtypeStruct((b, n, hh * V_DIM), BF16),
        scratch_shapes=[pltpu.VMEM((tq, nk_total), F32)],
        compiler_params=_params(3),
        name="attention",
    )(q, *ks, *vs)


def _top2sum(a, b, c, d):
    s1, t1 = jnp.maximum(a, b), jnp.minimum(a, b)
    s2, t2 = jnp.maximum(c, d), jnp.minimum(c, d)
    return jnp.maximum(s1, s2) + jnp.maximum(jnp.minimum(s1, s2), jnp.maximum(t1, t2))


def _route(logits_t, bias):
    sc = _sigmoid(logits_t)
    sel = sc + bias
    sel_r = [sel[e:e + 1, :] for e in range(N_EXPERTS)]
    sc_r = [sc[e:e + 1, :] for e in range(N_EXPERTS)]
    epg = EXPERTS_PER_GROUP
    gs = [_top2sum(*sel_r[g * epg:(g + 1) * epg]) for g in range(N_GROUPS)]
    best, gi = gs[0], jnp.zeros(gs[0].shape, jnp.int32)
    for g in range(1, N_GROUPS):
        upd = gs[g] > best
        best = jnp.where(upd, gs[g], best)
        gi = jnp.where(upd, g, gi)

    def pick_group(rows, k):
        r = rows[k]
        for g in range(1, N_GROUPS):
            r = jnp.where(gi == g, rows[g * epg + k], r)
        return r

    in_sel = [pick_group(sel_r, k) for k in range(epg)]
    in_sc = [pick_group(sc_r, k) for k in range(epg)]
    b1, i1 = in_sel[0], jnp.zeros(gi.shape, jnp.int32)
    for k in range(1, epg):
        upd = in_sel[k] > b1
        b1 = jnp.where(upd, in_sel[k], b1)
        i1 = jnp.where(upd, k, i1)
    b2, i2 = None, None
    for k in range(epg):
        cand = jnp.where(i1 == k, -jnp.inf, in_sel[k])
        if b2 is None:
            b2, i2 = cand, jnp.zeros(gi.shape, jnp.int32)
        else:
            upd = cand > b2
            b2 = jnp.where(upd, cand, b2)
            i2 = jnp.where(upd, k, i2)

    def pick_local(idx):
        r = in_sc[0]
        for k in range(1, epg):
            r = jnp.where(idx == k, in_sc[k], r)
        return r

    s1, s2 = pick_local(i1), pick_local(i2)
    den = s1 + s2
    return gi * epg + i1, gi * epg + i2, s1 / den, s2 / den


def _mixout_kernel(y_ref, u_ref, up_ref, un_ref, gm_ref, gp_ref, wp_ref, ps_ref, wo_ref, x_ref,
                   g1_ref, gf_ref, sh2_ref, sc2_ref, wr_ref, rb_ref,
                   xo_ref, h2_ref, comb_ref, *, tm, tpb, n_seq):
    i = pl.program_id(0)
    base = (i % tpb) * tm
    uc = u_ref[...]
    uext = jnp.concatenate([up_ref[...], uc, un_ref[...]], axis=0)
    r = lax.broadcasted_iota(jnp.int32, (tm, tm + 2 * POOL_HALO), 0)
    c = lax.broadcasted_iota(jnp.int32, (tm, tm + 2 * POOL_HALO), 1)
    rel = c - POOL_HALO - r
    jpos = base - POOL_HALO + c
    valid = (jpos >= 0) & (jpos < n_seq)
    tpos = base + lax.broadcasted_iota(jnp.int32, (tm, 1), 0)
    parts = []
    for g, w in enumerate(POOL_WINDOWS):
        sl = slice(g * POOL_GROUP_DIM, (g + 1) * POOL_GROUP_DIM)
        band = jnp.where(valid & (rel >= -(w // 2)) & (rel < w // 2), 1.0, 0.0).astype(BF16)
        cnt = jnp.clip(tpos + w // 2, 0, n_seq) - jnp.clip(tpos - w // 2, 0, n_seq)
        wsum = jnp.dot(band, uext[:, sl], preferred_element_type=F32)
        z = wsum / cnt.astype(F32) - uc[:, sl].astype(F32)
        parts.append(jnp.dot(z.astype(BF16), wp_ref[g], preferred_element_type=F32))
    ypool = jnp.concatenate(parts, axis=1) * ps_ref[...]
    mix = gm_ref[...].astype(F32) * y_ref[...].astype(F32) + gp_ref[...].astype(F32) * ypool
    y = jnp.dot(mix.astype(BF16), wo_ref[...], preferred_element_type=F32)
    xn = x_ref[...] + g1_ref[0] * y
    xo_ref[...] = xn
    h2 = (_rms(xn, gf_ref[...]) * (1.0 + sc2_ref[0]) + sh2_ref[0]).astype(BF16)
    h2_ref[...] = h2
    logits_t = lax.dot_general(wr_ref[...], h2, (((1,), (1,)), ((), ())), preferred_element_type=F32)
    e1, e2, w1, w2 = _route(logits_t, rb_ref[...])
    eid = lax.broadcasted_iota(jnp.int32, (N_EXPERTS, tm), 0)
    comb_ref[...] = jnp.where(eid == e1, w1, 0.0) + jnp.where(eid == e2, w2, 0.0)


def _mixer_out(y, p, x2, mod3, brow, n_seq, wp, ps, wo, gf, wr_t, rb):
    t, d = x2.shape
    tm = min(256, n_seq)
    tpb = n_seq // tm
    hpt = tm // POOL_HALO
    n_halo = t // POOL_HALO
    row = lambda i: (i, 0)
    const2 = lambda i: (0, 0)
    mrow = lambda i: brow(i * tm)
    return pl.pallas_call(
        functools.partial(_mixout_kernel, tm=tm, tpb=tpb, n_seq=n_seq),
        grid=(t // tm,),
        in_specs=[pl.BlockSpec((tm, d), row),
                  pl.BlockSpec((tm, POOL_DIM), lambda i: (i, LAT_W // POOL_DIM)),
                  pl.BlockSpec((POOL_HALO, POOL_DIM),
                               lambda i: (jnp.maximum(i * hpt - 1, 0), LAT_W // POOL_DIM)),
                  pl.BlockSpec((POOL_HALO, POOL_DIM),
                               lambda i: (jnp.minimum((i + 1) * hpt, n_halo - 1), LAT_W // POOL_DIM)),
                  pl.BlockSpec((tm, d), lambda i: (i, (LAT_W + POOL_DIM) // d)),
                  pl.BlockSpec((tm, d), lambda i: (i, (LAT_W + POOL_DIM) // d + 1)),
                  pl.BlockSpec(wp.shape, lambda i: (0, 0, 0)),
                  pl.BlockSpec((1, d), const2),
                  pl.BlockSpec((d, d), const2),
                  pl.BlockSpec((tm, d), row),
                  _mod_spec(2, mrow),
                  pl.BlockSpec((1, d), const2),
                  _mod_spec(3, mrow),
                  _mod_spec(4, mrow),
                  pl.BlockSpec((N_EXPERTS, d), const2),
                  pl.BlockSpec((N_EXPERTS, 1), const2)],
        out_specs=[pl.BlockSpec((tm, d), row),
                   pl.BlockSpec((tm, d), row),
                   pl.BlockSpec((N_EXPERTS, tm), lambda i: (0, i))],
        out_shape=[jax.ShapeDtypeStruct((t, d), F32),
                   jax.ShapeDtypeStruct((t, d), BF16),
                   jax.ShapeDtypeStruct((N_EXPERTS, t), F32)],
        compiler_params=_params(1),
        name="mixer_out",
    )(y, p, p, p, p, p, wp, ps.reshape(1, d), wo, x2, mod3, gf.reshape(1, d), mod3, mod3, wr_t,
      rb.reshape(N_EXPERTS, 1))


def _moe_kernel(h_ref, comb_ref, wg_ref, wu_ref, wd_ref, x_ref, g2_ref, gfin_ref, o_ref, acc_scr,
                *, final):
    e = pl.program_id(1)

    @pl.when(e == 0)
    def _():
        acc_scr[...] = jnp.zeros_like(acc_scr)

    h = h_ref[...]
    comb = comb_ref[...]
    lane = lax.broadcasted_iota(jnp.int32, comb.shape, 1)
    col = jnp.sum(jnp.where(lane == e, comb, 0.0), axis=1, keepdims=True)
    gate = jnp.dot(h, wg_ref[0], preferred_element_type=F32)
    up = jnp.dot(h, wu_ref[0], preferred_element_type=F32)
    a = (gate * _sigmoid(gate)) * up * col
    acc_scr[...] += jnp.dot(a.astype(BF16), wd_ref[0], preferred_element_type=F32)

    @pl.when(e == pl.num_programs(1) - 1)
    def _():
        xn = x_ref[...] + g2_ref[0] * acc_scr[...]
        if final:
            xn = _rms(xn, gfin_ref[...])
        o_ref[...] = xn


def _moe(h2, comb, wg, wu, wd, x2, mod3, brow, gfin, final):
    t, d = x2.shape
    tm = min(512, t)
    ne = wg.shape[0]
    row = lambda i, e: (i, 0)
    return pl.pallas_call(
        functools.partial(_moe_kernel, final=final),
        grid=(t // tm, ne),
        in_specs=[pl.BlockSpec((tm, d), row),
                  pl.BlockSpec((tm, LANES), row),
                  pl.BlockSpec((1, d, D_EXPERT), lambda i, e: (e, 0, 0)),
                  pl.BlockSpec((1, d, D_EXPERT), lambda i, e: (e, 0, 0)),
                  pl.BlockSpec((1, D_EXPERT, d), lambda i, e: (e, 0, 0)),
                  pl.BlockSpec((tm, d), row),
                  _mod_spec(5, lambda i: brow(i * tm)),
                  pl.BlockSpec((1, d), lambda i, e: (0, 0))],
        out_specs=pl.BlockSpec((tm, d), row),
        out_shape=jax.ShapeDtypeStruct((t, d), F32),
        scratch_shapes=[pltpu.VMEM((tm, d), F32)],
        compiler_params=_params(2),
        name="moe",
    )(h2, comb, wg, wu, wd, x2, mod3, gfin.reshape(1, d))


def _rope_tables(seq):
    rows = seq // GRID_W
    row = jnp.repeat(jnp.arange(rows), GRID_W).astype(F32)
    col = jnp.tile(jnp.arange(GRID_W), rows).astype(F32)
    axis_dim = QK_ROPE // 2
    inv = ROPE_THETA ** (-jnp.arange(0, axis_dim, 2, dtype=F32) / axis_dim)
    ang = jnp.concatenate([row[:, None] * inv, col[:, None] * inv], axis=-1)
    cos, sin = jnp.cos(ang), jnp.sin(ang)
    zero = jnp.zeros_like(cos)
    return (jnp.concatenate([cos, cos, zero, zero], axis=1),
            jnp.concatenate([-sin, sin, zero, zero], axis=1))


def _rope_group(w, start):
    half = QK_ROPE // 2
    x1, x2 = w[:, start:start + half], w[:, start + half:start + 2 * half]
    return jnp.concatenate([x1, x2, x2, x1], axis=1)


def _relayout_w_in(w):
    return jnp.concatenate([w[:, :KV_RANK], w[:, KV_COLS:Q_END], _rope_group(w, KV_RANK),
                            w[:, Q_END:]], axis=1).astype(BF16)


def _relayout_w_qb(w):
    dq = QK_NOPE + QK_ROPE
    cols = []
    for h in range(N_HEADS):
        cols += [w[:, h * dq:h * dq + QK_NOPE], _rope_group(w, h * dq + QK_NOPE)]
    return jnp.concatenate(cols, axis=1).astype(BF16)


def kernel(x, c, ctx, c_ctx, w_ada, b_ada, norm_mix_g, norm_ffn_g, w_in, q_norm_g, kv_norm_g, w_qb, w_kvb, w_pool, pool_scale, w_out, w_router, router_bias, w_exp_gate, w_exp_up, w_exp_down, w_sh_gate, w_sh_up, w_sh_down, final_norm_g):
    bsz, seq, d = x.shape
    n_ctx = ctx.shape[1]
    depth = w_ada.shape[0]
    assert bsz < MOD_ROWS and d == D_MODEL

    c_rows = jnp.concatenate([c, c_ctx[None], jnp.zeros((MOD_ROWS - bsz - 1, d), F32)], axis=0)
    mod = _ada_mod(c_rows, w_ada, b_ada)

    cos_t, sin_t = _rope_tables(seq)
    ones_t = jnp.concatenate([jnp.ones((n_ctx, 2 * (QK_ROPE // 2)), F32),
                              jnp.zeros((n_ctx, LANES - QK_ROPE), F32)], axis=1)
    zeros_t = jnp.zeros((n_ctx, LANES), F32)
    wr_t = w_router.T.astype(BF16)
    lat_row = lambda r: r // seq
    ctx_row = lambda r: bsz

    x2 = x.reshape(bsz * seq, d)
    xc2 = ctx.reshape(bsz * n_ctx, d)
    for l in range(depth):
        last = l == depth - 1
        mod3 = mod[l].reshape(MOD_ROWS * N_MOD, 1, d)
        w_in_r = _relayout_w_in(w_in[l])
        wq = _relayout_w_qb(w_qb[l])
        wkv = w_kvb[l].astype(BF16)
        wp = w_pool[l].astype(BF16)
        wo = w_out[l].astype(BF16)
        wg = jnp.concatenate([w_exp_gate[l], w_sh_gate[l][None]], axis=0).astype(BF16)
        wu = jnp.concatenate([w_exp_up[l], w_sh_up[l][None]], axis=0).astype(BF16)
        wd = jnp.concatenate([w_exp_down[l], w_sh_down[l][None]], axis=0).astype(BF16)

        def comb_cols(comb_t):
            t = comb_t.shape[1]
            return jnp.concatenate([comb_t.T, jnp.ones((t, 1), F32),
                                    jnp.zeros((t, LANES - N_EXPERTS - 1), F32)], axis=1)

        p = _in_proj(x2, mod3, lat_row, norm_mix_g[l], w_in_r, P_COLS)
        pc = _in_proj(xc2, mod3, ctx_row, norm_mix_g[l], w_in_r, LAT_W if last else P_COLS)
        q, k_lat, v_lat = _qkv(p, bsz, seq, cos_t, sin_t, q_norm_g[l], kv_norm_g[l], wq, wkv, True)
        ctx_out = _qkv(pc, bsz, n_ctx, ones_t, zeros_t, q_norm_g[l], kv_norm_g[l], wq, wkv, not last)
        k_ctx, v_ctx = ctx_out[-2:]
        y = _attention(q, [k_ctx, k_lat], [v_ctx, v_lat]).reshape(bsz * seq, d)
        x2, h2, comb_t = _mixer_out(y, p, x2, mod3, lat_row, seq, wp, pool_scale[l], wo,
                                    norm_ffn_g[l], wr_t, router_bias)
        x2 = _moe(h2, comb_cols(comb_t), wg, wu, wd, x2, mod3, lat_row, final_norm_g, last)
        if not last:
            yc = _attention(ctx_out[0], [k_ctx], [v_ctx]).reshape(bsz * n_ctx, d)
            xc2, h2c, comb_c = _mixer_out(yc, pc, xc2, mod3, ctx_row, n_ctx, wp, pool_scale[l], wo,
                                          norm_ffn_g[l], wr_t, router_bias)
            xc2 = _moe(h2c, comb_cols(comb_c), wg, wu, wd, xc2, mod3, ctx_row, final_norm_g, False)
    return x2.reshape(bsz, seq, d)
```

```python
import functools
import math

import jax
import jax.numpy as jnp
from jax import lax
from jax.experimental import pallas as pl
from jax.experimental.pallas import tpu as pltpu
from jax.experimental.pallas import tpu_sc as plsc

F32 = jnp.float32
BF16 = jnp.bfloat16

D_MODEL = 2048
GRID_W = 64
N_HEADS = 16
QK_NOPE = 128
QK_ROPE = 64
V_DIM = 128
Q_RANK = 384
KV_RANK = 512
ROPE_THETA = 10000.0
POOL_DIM = 1024
POOL_GROUPS = 4
POOL_GROUP_DIM = 256
POOL_WINDOWS = (2, 4, 8, 16)
POOL_GROUP_OUT = 512
KV_COLS = KV_RANK + QK_ROPE
Q_END = KV_COLS + Q_RANK
POOL_END = Q_END + POOL_DIM
N_EXPERTS = 16
N_GROUPS = 4
EXPERTS_PER_GROUP = 4
D_EXPERT = 512
EPS = 1e-6
N_MOD = 6

LANES = 128
HEAD_W = 2 * LANES
LAT_W = 1024
P_COLS = LAT_W + POOL_DIM + 2 * D_MODEL
POOL_HALO = 16
MOD_ROWS = 8
Q_SCALE = (1.0 / math.sqrt(QK_NOPE + QK_ROPE)) * math.log2(math.e)
VMEM_LIMIT = 56 * 1024 * 1024
SUBLANES = 8
ROW_CHUNKS = D_MODEL // 2 // LANES
ROUTE_ROWS = SUBLANES
MOE_TILE = 256
SC_WINDOW = 128


def _sigmoid(x):
    return 1.0 / (1.0 + jnp.exp(-x))


def _bf16_bits(x):
    u = pltpu.bitcast(x, jnp.uint32)
    return (u + jnp.uint32(0x7FFF) + ((u >> 16) & jnp.uint32(1))) >> 16


def _pack_rows(y):
    half = y.shape[1] // 2
    return (_bf16_bits(y[:, half:]) << 16) | _bf16_bits(y[:, :half])


def _store_chunks(ref, w):
    for j in range(ROW_CHUNKS):
        ref[:, j, :] = w[:, j * LANES:(j + 1) * LANES]


def _load_unpack(ref):
    w = jnp.concatenate([ref[:, j, :] for j in range(ROW_CHUNKS)], axis=1)
    lo = pltpu.bitcast(w << 16, F32)
    hi = pltpu.bitcast(w & jnp.uint32(0xFFFF0000), F32)
    return jnp.concatenate([lo, hi], axis=1)


def _rms(xf, g):
    ms = jnp.mean(xf * xf, axis=-1, keepdims=True)
    return xf * lax.rsqrt(ms + EPS) * g


def _params(n_axes):
    return pltpu.CompilerParams(dimension_semantics=("arbitrary",) * n_axes,
                                vmem_limit_bytes=VMEM_LIMIT)


def _ada_kernel(c_ref, w_ref, b_ref, o_ref):
    cf = c_ref[...]
    a = (cf * _sigmoid(cf)).astype(BF16)
    o_ref[0] = jnp.dot(a, w_ref[0].astype(BF16), preferred_element_type=F32) + b_ref[0]


def _ada_mod(c_rows, w_ada, b_ada):
    depth, d, n = w_ada.shape
    tn = 1024
    return pl.pallas_call(
        _ada_kernel,
        grid=(depth, n // tn),
        in_specs=[pl.BlockSpec((MOD_ROWS, d), lambda l, j: (0, 0)),
                  pl.BlockSpec((1, d, tn), lambda l, j: (l, 0, j)),
                  pl.BlockSpec((1, 1, tn), lambda l, j: (l, 0, j))],
        out_specs=pl.BlockSpec((1, MOD_ROWS, tn), lambda l, j: (l, 0, j)),
        out_shape=jax.ShapeDtypeStruct((depth, MOD_ROWS, n), F32),
        compiler_params=_params(2),
        name="ada_mod",
    )(c_rows, w_ada, b_ada.reshape(depth, 1, n))


def _mod_spec(k, brow):
    return pl.BlockSpec((1, 1, D_MODEL), lambda i, *_: (brow(i) * N_MOD + k, 0, 0))


def _inproj_kernel(x_ref, g_ref, sh_ref, sc_ref, w_ref, o_ref, h_scr, *, n_plain):
    j = pl.program_id(1)

    @pl.when(j == 0)
    def _():
        y = _rms(x_ref[...], g_ref[...])
        h_scr[...] = (y * (1.0 + sc_ref[0]) + sh_ref[0]).astype(BF16)

    acc = jnp.dot(h_scr[...], w_ref[...], preferred_element_type=F32)

    @pl.when(j < n_plain)
    def _():
        o_ref[...] = acc.astype(BF16)

    @pl.when(j >= n_plain)
    def _():
        o_ref[...] = _sigmoid(acc).astype(BF16)


def _in_proj(x2, mod3, brow, g, w, n_cols):
    t, d = x2.shape
    tm = min(512, t)
    tn = 512
    return pl.pallas_call(
        functools.partial(_inproj_kernel, n_plain=(LAT_W + POOL_DIM) // tn),
        grid=(t // tm, n_cols // tn),
        in_specs=[pl.BlockSpec((tm, d), lambda i, j: (i, 0)),
                  pl.BlockSpec((1, d), lambda i, j: (0, 0)),
                  _mod_spec(0, lambda i: brow(i * tm)),
                  _mod_spec(1, lambda i: brow(i * tm)),
                  pl.BlockSpec((d, tn), lambda i, j: (0, j))],
        out_specs=pl.BlockSpec((tm, tn), lambda i, j: (i, j)),
        out_shape=jax.ShapeDtypeStruct((t, n_cols), BF16),
        scratch_shapes=[pltpu.VMEM((tm, d), BF16)],
        compiler_params=_params(2),
        name="in_proj",
    )(x2, g.reshape(1, d), mod3, mod3, w)


def _qkv_kernel(lat_ref, cos_ref, sin_ref, gq_ref, gkv_ref, wq_ref, wkv_ref, *out_refs, with_q):
    if with_q:
        q_ref, k_ref, v_ref = out_refs
    else:
        k_ref, v_ref = out_refs
    cos = cos_ref[...]
    sin = sin_ref[...]

    def rope(grp):
        return grp * cos + pltpu.roll(grp, 2 * (QK_ROPE // 2), axis=1) * sin

    kvn = _rms(lat_ref[:, :KV_RANK].astype(F32), gkv_ref[...]).astype(BF16)
    kpe = rope(lat_ref[:, KV_RANK + Q_RANK:].astype(F32)).astype(BF16)
    for h in range(N_HEADS):
        kv = jnp.dot(kvn, wkv_ref[:, h * HEAD_W:(h + 1) * HEAD_W], preferred_element_type=F32)
        k_ref[0, h, :, :QK_NOPE] = kv[:, :QK_NOPE].astype(BF16)
        k_ref[0, h, :, QK_NOPE:] = kpe
        v_ref[0, h] = kv[:, QK_NOPE:].astype(BF16)
    if with_q:
        qn = _rms(lat_ref[:, KV_RANK:KV_RANK + Q_RANK].astype(F32), gq_ref[...]).astype(BF16)
        for h in range(N_HEADS):
            qh = jnp.dot(qn, wq_ref[:, h * HEAD_W:(h + 1) * HEAD_W], preferred_element_type=F32)
            q_ref[0, h, :, :QK_NOPE] = (qh[:, :QK_NOPE] * Q_SCALE).astype(BF16)
            q_ref[0, h, :, QK_NOPE:] = (rope(qh[:, QK_NOPE:]) * Q_SCALE).astype(BF16)


def _qkv(p, b, n, cos_t, sin_t, gq, gkv, wq, wkv, with_q):
    tm = min(256, n)
    tpb = n // tm
    rope_tiles = cos_t.shape[0] // tm
    head_spec = lambda w: pl.BlockSpec((1, N_HEADS, tm, w), lambda i: (i // tpb, 0, i % tpb, 0))
    tab_spec = pl.BlockSpec((tm, LANES), lambda i: ((i % tpb) % rope_tiles, 0))
    out_shape = [jax.ShapeDtypeStruct((b, N_HEADS, n, HEAD_W), BF16),
                 jax.ShapeDtypeStruct((b, N_HEADS, n, V_DIM), BF16)]
    out_specs = [head_spec(HEAD_W), head_spec(V_DIM)]
    if with_q:
        out_shape = [jax.ShapeDtypeStruct((b, N_HEADS, n, HEAD_W), BF16)] + out_shape
        out_specs = [head_spec(HEAD_W)] + out_specs
    return pl.pallas_call(
        functools.partial(_qkv_kernel, with_q=with_q),
        grid=(b * tpb,),
        in_specs=[pl.BlockSpec((tm, LAT_W), lambda i: (i, 0)),
                  tab_spec, tab_spec,
                  pl.BlockSpec((1, Q_RANK), lambda i: (0, 0)),
                  pl.BlockSpec((1, KV_RANK), lambda i: (0, 0)),
                  pl.BlockSpec(wq.shape, lambda i: (0, 0)),
                  pl.BlockSpec(wkv.shape, lambda i: (0, 0))],
        out_specs=out_specs,
        out_shape=out_shape,
        compiler_params=_params(1),
        name="qkv",
    )(p, cos_t, sin_t, gq.reshape(1, Q_RANK), gkv.reshape(1, KV_RANK), wq, wkv)


def _lane_fold(x, op):
    r = x[:, :LANES]
    for c in range(LANES, x.shape[1], LANES):
        r = op(r, x[:, c:c + LANES])
    return r


def _attn_kernel(q_ref, *refs, n_seg, kb):
    k_refs, v_refs = refs[:n_seg], refs[n_seg:2 * n_seg]
    o_ref, s_scr = refs[2 * n_seg], refs[2 * n_seg + 1]
    q = q_ref[0, 0]
    chunks = []
    off = 0
    for kr, vr in zip(k_refs, v_refs):
        nk = kr.shape[2]
        for c0 in range(0, nk, kb):
            n = min(kb, nk - c0)
            chunks.append((kr, vr, c0, n, off))
            off += n
    mp = None
    for kr, vr, c0, n, o in chunks:
        s = lax.dot_general(q, kr[0, 0, c0:c0 + n, :], (((1,), (1,)), ((), ())),
                            preferred_element_type=F32)
        s_scr[:, o:o + n] = s
        m = _lane_fold(s, jnp.maximum)
        mp = m if mp is None else jnp.maximum(mp, m)
    mrow = jnp.max(mp, axis=-1, keepdims=True)
    lp = None
    acc = None
    for kr, vr, c0, n, o in chunks:
        pr = jnp.exp2(s_scr[:, o:o + n] - mrow)
        ls = _lane_fold(pr, jnp.add)
        lp = ls if lp is None else lp + ls
        pv = jnp.dot(pr.astype(BF16), vr[0, 0, c0:c0 + n, :], preferred_element_type=F32)
        acc = pv if acc is None else acc + pv
    l = jnp.sum(lp, axis=-1, keepdims=True)
    o_ref[0] = (acc / l).astype(BF16)


def _attention(q, ks, vs):
    b, hh, n, _ = q.shape
    tq = min(256, n)
    kb = 512
    nk_total = sum(k.shape[2] for k in ks)
    seg_spec = lambda a: pl.BlockSpec((1, 1) + a.shape[2:], lambda bi, h, i: (bi, h, 0, 0))
    return pl.pallas_call(
        functools.partial(_attn_kernel, n_seg=len(ks), kb=kb),
        grid=(b, hh, n // tq),
        in_specs=[pl.BlockSpec((1, 1, tq, HEAD_W), lambda bi, h, i: (bi, h, i, 0))]
                 + [seg_spec(a) for a in ks] + [seg_spec(a) for a in vs],
        out_specs=pl.BlockSpec((1, tq, V_DIM), lambda bi, h, i: (bi, i, h)),
        out_shape=jax.ShapeDtypeStruct((b, n, hh * V_DIM), BF16),
        scratch_shapes=[pltpu.VMEM((tq, nk_total), F32)],
        compiler_params=_params(3),
        name="attention",
    )(q, *ks, *vs)


def _top2sum(a, b, c, d):
    s1, t1 = jnp.maximum(a, b), jnp.minimum(a, b)
    s2, t2 = jnp.maximum(c, d), jnp.minimum(c, d)
    return jnp.maximum(s1, s2) + jnp.maximum(jnp.minimum(s1, s2), jnp.maximum(t1, t2))


def _route(logits_t, bias):
    sc = _sigmoid(logits_t)
    sel = sc + bias
    sel_r = [sel[e:e + 1, :] for e in range(N_EXPERTS)]
    sc_r = [sc[e:e + 1, :] for e in range(N_EXPERTS)]
    epg = EXPERTS_PER_GROUP
    gs = [_top2sum(*sel_r[g * epg:(g + 1) * epg]) for g in range(N_GROUPS)]
    best, gi = gs[0], jnp.zeros(gs[0].shape, jnp.int32)
    for g in range(1, N_GROUPS):
        upd = gs[g] > best
        best = jnp.where(upd, gs[g], best)
        gi = jnp.where(upd, g, gi)

    def pick_group(rows, k):
        r = rows[k]
        for g in range(1, N_GROUPS):
            r = jnp.where(gi == g, rows[g * epg + k], r)
        return r

    in_sel = [pick_group(sel_r, k) for k in range(epg)]
    in_sc = [pick_group(sc_r, k) for k in range(epg)]
    b1, i1 = in_sel[0], jnp.zeros(gi.shape, jnp.int32)
    for k in range(1, epg):
        upd = in_sel[k] > b1
        b1 = jnp.where(upd, in_sel[k], b1)
        i1 = jnp.where(upd, k, i1)
    b2, i2 = None, None
    for k in range(epg):
        cand = jnp.where(i1 == k, -jnp.inf, in_sel[k])
        if b2 is None:
            b2, i2 = cand, jnp.zeros(gi.shape, jnp.int32)
        else:
            upd = cand > b2
            b2 = jnp.where(upd, cand, b2)
            i2 = jnp.where(upd, k, i2)

    def pick_local(idx):
        r = in_sc[0]
        for k in range(1, epg):
            r = jnp.where(idx == k, in_sc[k], r)
        return r

    s1, s2 = pick_local(i1), pick_local(i2)
    den = s1 + s2
    return gi * epg + i1, gi * epg + i2, s1 / den, s2 / den


def _mixout_kernel(y_ref, u_ref, up_ref, un_ref, gm_ref, gp_ref, wp_ref, ps_ref, wo_ref, x_ref,
                   g1_ref, gf_ref, sh2_ref, sc2_ref, wr_ref, rb_ref,
                   xo_ref, h2_ref, route_ref, *, tm, tpb, n_seq, packed):
    i = pl.program_id(0)
    base = (i % tpb) * tm
    uc = u_ref[...]
    uext = jnp.concatenate([up_ref[...], uc, un_ref[...]], axis=0)
    r = lax.broadcasted_iota(jnp.int32, (tm, tm + 2 * POOL_HALO), 0)
    c = lax.broadcasted_iota(jnp.int32, (tm, tm + 2 * POOL_HALO), 1)
    rel = c - POOL_HALO - r
    jpos = base - POOL_HALO + c
    valid = (jpos >= 0) & (jpos < n_seq)
    tpos = base + lax.broadcasted_iota(jnp.int32, (tm, 1), 0)
    parts = []
    for g, w in enumerate(POOL_WINDOWS):
        sl = slice(g * POOL_GROUP_DIM, (g + 1) * POOL_GROUP_DIM)
        band = jnp.where(valid & (rel >= -(w // 2)) & (rel < w // 2), 1.0, 0.0).astype(BF16)
        cnt = jnp.clip(tpos + w // 2, 0, n_seq) - jnp.clip(tpos - w // 2, 0, n_seq)
        wsum = jnp.dot(band, uext[:, sl], preferred_element_type=F32)
        z = wsum / cnt.astype(F32) - uc[:, sl].astype(F32)
        parts.append(jnp.dot(z.astype(BF16), wp_ref[g], preferred_element_type=F32))
    ypool = jnp.concatenate(parts, axis=1) * ps_ref[...]
    mix = gm_ref[...].astype(F32) * y_ref[...].astype(F32) + gp_ref[...].astype(F32) * ypool
    y = jnp.dot(mix.astype(BF16), wo_ref[...], preferred_element_type=F32)
    xn = x_ref[...] + g1_ref[0] * y
    xo_ref[...] = xn
    h2f = _rms(xn, gf_ref[...]) * (1.0 + sc2_ref[0]) + sh2_ref[0]
    h2 = h2f.astype(BF16)
    if packed:
        _store_chunks(h2_ref, _pack_rows(h2f))
    else:
        h2_ref[...] = h2
    logits_t = lax.dot_general(wr_ref[...], h2, (((1,), (1,)), ((), ())), preferred_element_type=F32)
    e1, e2, w1, w2 = _route(logits_t, rb_ref[...])
    zero = jnp.zeros((ROUTE_ROWS - 4, tm), F32)
    route_ref[...] = jnp.concatenate([e1.astype(F32), e2.astype(F32), w1, w2, zero], axis=0)


def _mixer_out(y, p, x2, mod3, brow, n_seq, wp, ps, wo, gf, wr_t, rb, packed):
    t, d = x2.shape
    tm = min(256, n_seq)
    tpb = n_seq // tm
    hpt = tm // POOL_HALO
    n_halo = t // POOL_HALO
    row = lambda i: (i, 0)
    const2 = lambda i: (0, 0)
    mrow = lambda i: brow(i * tm)
    return pl.pallas_call(
        functools.partial(_mixout_kernel, tm=tm, tpb=tpb, n_seq=n_seq, packed=packed),
        grid=(t // tm,),
        in_specs=[pl.BlockSpec((tm, d), row),
                  pl.BlockSpec((tm, POOL_DIM), lambda i: (i, LAT_W // POOL_DIM)),
                  pl.BlockSpec((POOL_HALO, POOL_DIM),
                               lambda i: (jnp.maximum(i * hpt - 1, 0), LAT_W // POOL_DIM)),
                  pl.BlockSpec((POOL_HALO, POOL_DIM),
                               lambda i: (jnp.minimum((i + 1) * hpt, n_halo - 1), LAT_W // POOL_DIM)),
                  pl.BlockSpec((tm, d), lambda i: (i, (LAT_W + POOL_DIM) // d)),
                  pl.BlockSpec((tm, d), lambda i: (i, (LAT_W + POOL_DIM) // d + 1)),
                  pl.BlockSpec(wp.shape, lambda i: (0, 0, 0)),
                  pl.BlockSpec((1, d), const2),
                  pl.BlockSpec((d, d), const2),
                  pl.BlockSpec((tm, d), row),
                  _mod_spec(2, mrow),
                  pl.BlockSpec((1, d), const2),
                  _mod_spec(3, mrow),
                  _mod_spec(4, mrow),
                  pl.BlockSpec((N_EXPERTS, d), const2),
                  pl.BlockSpec((N_EXPERTS, 1), const2)],
        out_specs=[pl.BlockSpec((tm, d), row),
                   (pl.BlockSpec((tm, ROW_CHUNKS, LANES), lambda i: (i, 0, 0)) if packed
                    else pl.BlockSpec((tm, d), row)),
                   pl.BlockSpec((ROUTE_ROWS, tm), lambda i: (0, i))],
        out_shape=[jax.ShapeDtypeStruct((t, d), F32),
                   (jax.ShapeDtypeStruct((t, ROW_CHUNKS, LANES), jnp.uint32) if packed
                    else jax.ShapeDtypeStruct((t, d), BF16)),
                   jax.ShapeDtypeStruct((ROUTE_ROWS, t), F32)],
        compiler_params=_params(1),
        name="mixer_out",
    )(y, p, p, p, p, p, wp, ps.reshape(1, d), wo, x2, mod3, gf.reshape(1, d), mod3, mod3, wr_t,
      rb.reshape(N_EXPERTS, 1))


def _moe_kernel(h_ref, comb_ref, wg_ref, wu_ref, wd_ref, x_ref, g2_ref, gfin_ref, o_ref, acc_scr,
                *, final):
    e = pl.program_id(1)

    @pl.when(e == 0)
    def _():
        acc_scr[...] = jnp.zeros_like(acc_scr)

    h = h_ref[...]
    comb = comb_ref[...]
    lane = lax.broadcasted_iota(jnp.int32, comb.shape, 1)
    col = jnp.sum(jnp.where(lane == e, comb, 0.0), axis=1, keepdims=True)
    gate = jnp.dot(h, wg_ref[0], preferred_element_type=F32)
    up = jnp.dot(h, wu_ref[0], preferred_element_type=F32)
    a = (gate * _sigmoid(gate)) * up * col
    acc_scr[...] += jnp.dot(a.astype(BF16), wd_ref[0], preferred_element_type=F32)

    @pl.when(e == pl.num_programs(1) - 1)
    def _():
        xn = x_ref[...] + g2_ref[0] * acc_scr[...]
        if final:
            xn = _rms(xn, gfin_ref[...])
        o_ref[...] = xn


def _moe(h2, comb, wg, wu, wd, x2, mod3, brow, gfin, final):
    t, d = x2.shape
    tm = min(512, t)
    ne = wg.shape[0]
    row = lambda i, e: (i, 0)
    return pl.pallas_call(
        functools.partial(_moe_kernel, final=final),
        grid=(t // tm, ne),
        in_specs=[pl.BlockSpec((tm, d), row),
                  pl.BlockSpec((tm, LANES), row),
                  pl.BlockSpec((1, d, D_EXPERT), lambda i, e: (e, 0, 0)),
                  pl.BlockSpec((1, d, D_EXPERT), lambda i, e: (e, 0, 0)),
                  pl.BlockSpec((1, D_EXPERT, d), lambda i, e: (e, 0, 0)),
                  pl.BlockSpec((tm, d), row),
                  _mod_spec(5, lambda i: brow(i * tm)),
                  pl.BlockSpec((1, d), lambda i, e: (0, 0))],
        out_specs=pl.BlockSpec((tm, d), row),
        out_shape=jax.ShapeDtypeStruct((t, d), F32),
        scratch_shapes=[pltpu.VMEM((tm, d), F32)],
        compiler_params=_params(2),
        name="moe",
    )(h2, comb, wg, wu, wd, x2, mod3, gfin.reshape(1, d))


def _sc_gather_rows(table, idx):
    n, m = table.shape[0], idx.shape[0]
    n_idx = m * ROW_CHUNKS
    assert n_idx % SC_WINDOW == 0
    idx8 = (idx[:, None] * ROW_CHUNKS + jnp.arange(ROW_CHUNKS, dtype=jnp.int32)[None, :]).reshape(1, n_idx)
    mesh = plsc.VectorSubcoreMesh(core_axis_name="core", subcore_axis_name="subcore")

    @pl.kernel(out_type=jax.ShapeDtypeStruct((n_idx, LANES), jnp.uint32), mesh=mesh)
    def gather(t_hbm, i_hbm, o_hbm):
        def body(i_vmem, o_vmem):
            pltpu.sync_copy(t_hbm.at[i_vmem.at[0]], o_vmem)

        pltpu.emit_pipeline(
            body,
            grid=(n_idx // SC_WINDOW,),
            in_specs=[pl.BlockSpec((1, SC_WINDOW), index_map=lambda i: (0, i))],
            out_specs=[pl.BlockSpec((SC_WINDOW, LANES), index_map=lambda i: (i, 0))],
            core_axis_name=("core", "subcore"),
            dimension_semantics=(pltpu.PARALLEL,),
        )(i_hbm, o_hbm)

    return gather(table.reshape(n * ROW_CHUNKS, LANES), idx8).reshape(m, ROW_CHUNKS, LANES)


def _dispatch_plan(route):
    t = route.shape[1]
    n_pairs = 2 * t
    n_slots = n_pairs + N_EXPERTS * MOE_TILE
    n_tiles = n_slots // MOE_TILE
    e = jnp.concatenate([route[0], route[1]]).astype(jnp.int32)
    onehot = (e[:, None] == jnp.arange(N_EXPERTS, dtype=jnp.int32)[None, :]).astype(jnp.int32)
    csum = jnp.cumsum(onehot, axis=0)
    counts = csum[-1]
    rank = jnp.sum(onehot * csum, axis=1) - 1
    padded = ((counts + MOE_TILE - 1) // MOE_TILE) * MOE_TILE
    ends = jnp.cumsum(padded)
    starts = ends - padded
    pair_slot = starts[e] + rank
    _, order = lax.sort((e, jnp.arange(n_pairs, dtype=jnp.int32)), num_keys=1, is_stable=True)
    n_used = ends[-1] // MOE_TILE
    tile_id = jnp.arange(n_tiles, dtype=jnp.int32)
    tile_e = jnp.minimum(jnp.searchsorted(ends, tile_id * MOE_TILE, side="right"), N_EXPERTS - 1)
    tile_expert = jnp.where(tile_id < n_used, tile_e, tile_e[jnp.maximum(n_used - 1, 0)]).astype(jnp.int32)
    slot = jnp.arange(n_slots, dtype=jnp.int32)
    slot_e = tile_e[slot // MOE_TILE]
    j = slot - starts[slot_e]
    unpadded_start = jnp.cumsum(counts) - counts
    src = order[jnp.clip(unpadded_start[slot_e] + j, 0, n_pairs - 1)]
    slot_token = jnp.where(j < counts[slot_e], src % t, 0).astype(jnp.int32)
    return slot_token, pair_slot.astype(jnp.int32), tile_expert, n_used.reshape(1).astype(jnp.int32)


def _ffn_kernel(te_ref, nu_ref, xs_ref, wg_ref, wu_ref, wd_ref, ys_ref):
    i = pl.program_id(0)

    @pl.when(i < nu_ref[0])
    def _():
        xb = _load_unpack(xs_ref).astype(BF16)
        gate = jnp.dot(xb, wg_ref[0], preferred_element_type=F32)
        up = jnp.dot(xb, wu_ref[0], preferred_element_type=F32)
        a = ((gate * _sigmoid(gate)) * up).astype(BF16)
        _store_chunks(ys_ref, _pack_rows(jnp.dot(a, wd_ref[0], preferred_element_type=F32)))

    @pl.when(i >= nu_ref[0])
    def _():
        ys_ref[...] = jnp.zeros_like(ys_ref)


def _expert_ffn(xs, tile_expert, n_used, wg, wu, wd):
    n_slots = xs.shape[0]
    d = D_MODEL
    blk = pl.BlockSpec((MOE_TILE, ROW_CHUNKS, LANES), lambda i, te, nu: (i, 0, 0))
    return pl.pallas_call(
        _ffn_kernel,
        grid_spec=pltpu.PrefetchScalarGridSpec(
            num_scalar_prefetch=2,
            grid=(n_slots // MOE_TILE,),
            in_specs=[blk,
                      pl.BlockSpec((1, d, D_EXPERT), lambda i, te, nu: (te[i], 0, 0)),
                      pl.BlockSpec((1, d, D_EXPERT), lambda i, te, nu: (te[i], 0, 0)),
                      pl.BlockSpec((1, D_EXPERT, d), lambda i, te, nu: (te[i], 0, 0))],
            out_specs=blk),
        out_shape=jax.ShapeDtypeStruct(xs.shape, jnp.uint32),
        compiler_params=_params(1),
        name="expert_ffn",
    )(tile_expert, n_used, xs, wg, wu, wd)


def _combine_kernel(h_ref, y1_ref, y2_ref, w_ref, sg_ref, su_ref, sd_ref, x_ref, g2_ref, gfin_ref, o_ref,
                    *, final):
    hb = _load_unpack(h_ref).astype(BF16)
    gate = jnp.dot(hb, sg_ref[...], preferred_element_type=F32)
    up = jnp.dot(hb, su_ref[...], preferred_element_type=F32)
    a = ((gate * _sigmoid(gate)) * up).astype(BF16)
    y = jnp.dot(a, sd_ref[...], preferred_element_type=F32)
    w = w_ref[...]
    y = y + w[:, 0:1] * _load_unpack(y1_ref) + w[:, 1:2] * _load_unpack(y2_ref)
    xn = x_ref[...] + g2_ref[0] * y
    if final:
        xn = _rms(xn, gfin_ref[...])
    o_ref[...] = xn


def _moe_combine(h2p, yg, w12, sg, su, sd, x2, mod3, brow, gfin, final):
    t, d = x2.shape
    tm = min(256, t)
    nt = t // tm
    row = lambda i: (i, 0)
    const2 = lambda i: (0, 0)
    pk = lambda off: pl.BlockSpec((tm, ROW_CHUNKS, LANES), lambda i: (i + off, 0, 0))
    return pl.pallas_call(
        functools.partial(_combine_kernel, final=final),
        grid=(nt,),
        in_specs=[pk(0), pk(0), pk(nt),
                  pl.BlockSpec((tm, LANES), row),
                  pl.BlockSpec(sg.shape, const2),
                  pl.BlockSpec(su.shape, const2),
                  pl.BlockSpec(sd.shape, const2),
                  pl.BlockSpec((tm, d), row),
                  _mod_spec(5, lambda i: brow(i * tm)),
                  pl.BlockSpec((1, d), const2)],
        out_specs=pl.BlockSpec((tm, d), row),
        out_shape=jax.ShapeDtypeStruct((t, d), F32),
        compiler_params=_params(1),
        name="moe_combine",
    )(h2p, yg, yg, w12, sg, su, sd, x2, mod3, gfin.reshape(1, d))


def _moe_routed(h2p, route, wg, wu, wd, sg, su, sd, x2, mod3, brow, gfin, final):
    t = x2.shape[0]
    slot_token, pair_slot, tile_expert, n_used = _dispatch_plan(route)
    xs = _sc_gather_rows(h2p, slot_token)
    ys = _expert_ffn(xs, tile_expert, n_used, wg, wu, wd)
    yg = _sc_gather_rows(ys, pair_slot)
    w12 = jnp.concatenate([route[2:4].T, jnp.zeros((t, LANES - 2), F32)], axis=1)
    return _moe_combine(h2p, yg, w12, sg, su, sd, x2, mod3, brow, gfin, final)


def _rope_tables(seq):
    rows = seq // GRID_W
    row = jnp.repeat(jnp.arange(rows), GRID_W).astype(F32)
    col = jnp.tile(jnp.arange(GRID_W), rows).astype(F32)
    axis_dim = QK_ROPE // 2
    inv = ROPE_THETA ** (-jnp.arange(0, axis_dim, 2, dtype=F32) / axis_dim)
    ang = jnp.concatenate([row[:, None] * inv, col[:, None] * inv], axis=-1)
    cos, sin = jnp.cos(ang), jnp.sin(ang)
    zero = jnp.zeros_like(cos)
    return (jnp.concatenate([cos, cos, zero, zero], axis=1),
            jnp.concatenate([-sin, sin, zero, zero], axis=1))


def _rope_group(w, start):
    half = QK_ROPE // 2
    x1, x2 = w[:, start:start + half], w[:, start + half:start + 2 * half]
    return jnp.concatenate([x1, x2, x2, x1], axis=1)


def _relayout_w_in(w):
    return jnp.concatenate([w[:, :KV_RANK], w[:, KV_COLS:Q_END], _rope_group(w, KV_RANK),
                            w[:, Q_END:]], axis=1).astype(BF16)


def _relayout_w_qb(w):
    dq = QK_NOPE + QK_ROPE
    cols = []
    for h in range(N_HEADS):
        cols += [w[:, h * dq:h * dq + QK_NOPE], _rope_group(w, h * dq + QK_NOPE)]
    return jnp.concatenate(cols, axis=1).astype(BF16)


def kernel(x, c, ctx, c_ctx, w_ada, b_ada, norm_mix_g, norm_ffn_g, w_in, q_norm_g, kv_norm_g, w_qb, w_kvb, w_pool, pool_scale, w_out, w_router, router_bias, w_exp_gate, w_exp_up, w_exp_down, w_sh_gate, w_sh_up, w_sh_down, final_norm_g):
    bsz, seq, d = x.shape
    n_ctx = ctx.shape[1]
    depth = w_ada.shape[0]
    assert bsz < MOD_ROWS and d == D_MODEL

    c_rows = jnp.concatenate([c, c_ctx[None], jnp.zeros((MOD_ROWS - bsz - 1, d), F32)], axis=0)
    mod = _ada_mod(c_rows, w_ada, b_ada)

    cos_t, sin_t = _rope_tables(seq)
    ones_t = jnp.concatenate([jnp.ones((n_ctx, 2 * (QK_ROPE // 2)), F32),
                              jnp.zeros((n_ctx, LANES - QK_ROPE), F32)], axis=1)
    zeros_t = jnp.zeros((n_ctx, LANES), F32)
    wr_t = w_router.T.astype(BF16)
    lat_row = lambda r: r // seq
    ctx_row = lambda r: bsz

    x2 = x.reshape(bsz * seq, d)
    xc2 = ctx.reshape(bsz * n_ctx, d)
    for l in range(depth):
        last = l == depth - 1
        mod3 = mod[l].reshape(MOD_ROWS * N_MOD, 1, d)
        w_in_r = _relayout_w_in(w_in[l])
        wq = _relayout_w_qb(w_qb[l])
        wkv = w_kvb[l].astype(BF16)
        wp = w_pool[l].astype(BF16)
        wo = w_out[l].astype(BF16)
        wg, wu, wd = w_exp_gate[l].astype(BF16), w_exp_up[l].astype(BF16), w_exp_down[l].astype(BF16)
        sg, su, sd = w_sh_gate[l].astype(BF16), w_sh_up[l].astype(BF16), w_sh_down[l].astype(BF16)

        def comb_cols(route):
            t = route.shape[1]
            eid = jnp.arange(N_EXPERTS, dtype=F32)[None, :]
            comb = (jnp.where(route[0][:, None] == eid, route[2][:, None], 0.0)
                    + jnp.where(route[1][:, None] == eid, route[3][:, None], 0.0))
            return jnp.concatenate([comb, jnp.ones((t, 1), F32),
                                    jnp.zeros((t, LANES - N_EXPERTS - 1), F32)], axis=1)

        p = _in_proj(x2, mod3, lat_row, norm_mix_g[l], w_in_r, P_COLS)
        pc = _in_proj(xc2, mod3, ctx_row, norm_mix_g[l], w_in_r, LAT_W if last else P_COLS)
        q, k_lat, v_lat = _qkv(p, bsz, seq, cos_t, sin_t, q_norm_g[l], kv_norm_g[l], wq, wkv, True)
        ctx_out = _qkv(pc, bsz, n_ctx, ones_t, zeros_t, q_norm_g[l], kv_norm_g[l], wq, wkv, not last)
        k_ctx, v_ctx = ctx_out[-2:]
        y = _attention(q, [k_ctx, k_lat], [v_ctx, v_lat]).reshape(bsz * seq, d)
        x2, h2p, route = _mixer_out(y, p, x2, mod3, lat_row, seq, wp, pool_scale[l], wo,
                                    norm_ffn_g[l], wr_t, router_bias, True)
        x2 = _moe_routed(h2p, route, wg, wu, wd, sg, su, sd, x2, mod3, lat_row, final_norm_g, last)
        if not last:
            yc = _attention(ctx_out[0], [k_ctx], [v_ctx]).reshape(bsz * n_ctx, d)
            xc2, h2c, route_c = _mixer_out(yc, pc, xc2, mod3, ctx_row, n_ctx, wp, pool_scale[l], wo,
                                           norm_ffn_g[l], wr_t, router_bias, False)
            xc2 = _moe(h2c, comb_cols(route_c),
                       jnp.concatenate([wg, sg[None]], axis=0), jnp.concatenate([wu, su[None]], axis=0),
                       jnp.concatenate([wd, sd[None]], axis=0), xc2, mod3, ctx_row, final_norm_g, False)
    return x2.reshape(bsz, seq, d)
```

```python
import functools
import math

import jax
import jax.numpy as jnp
from jax import lax
from jax.experimental import pallas as pl
from jax.experimental.pallas import tpu as pltpu
from jax.experimental.pallas import tpu_sc as plsc

F32 = jnp.float32
BF16 = jnp.bfloat16

D_MODEL = 2048
GRID_W = 64
N_HEADS = 16
QK_NOPE = 128
QK_ROPE = 64
V_DIM = 128
Q_RANK = 384
KV_RANK = 512
ROPE_THETA = 10000.0
POOL_DIM = 1024
POOL_GROUPS = 4
POOL_GROUP_DIM = 256
POOL_WINDOWS = (2, 4, 8, 16)
POOL_GROUP_OUT = 512
KV_COLS = KV_RANK + QK_ROPE
Q_END = KV_COLS + Q_RANK
POOL_END = Q_END + POOL_DIM
N_EXPERTS = 16
N_GROUPS = 4
EXPERTS_PER_GROUP = 4
D_EXPERT = 512
EPS = 1e-6
N_MOD = 6

LANES = 128
HEAD_W = 2 * LANES
LAT_W = 1024
P_COLS = LAT_W + POOL_DIM + 2 * D_MODEL
POOL_HALO = 16
MOD_ROWS = 8
Q_SCALE = (1.0 / math.sqrt(QK_NOPE + QK_ROPE)) * math.log2(math.e)
VMEM_LIMIT = 56 * 1024 * 1024
SUBLANES = 8
ROW_CHUNKS = D_MODEL // 2 // LANES
ROUTE_ROWS = SUBLANES
MOE_TILE = 256
SC_WINDOW = 128
ATTN_TQ = 256
ATTN_KB = 512


def _sigmoid(x):
    return 1.0 / (1.0 + jnp.exp(-x))


def _bf16_bits(x):
    u = pltpu.bitcast(x, jnp.uint32)
    return (u + jnp.uint32(0x7FFF) + ((u >> 16) & jnp.uint32(1))) >> 16


def _pack_rows(y):
    half = y.shape[1] // 2
    return (_bf16_bits(y[:, half:]) << 16) | _bf16_bits(y[:, :half])


def _store_chunks(ref, w):
    for j in range(ROW_CHUNKS):
        ref[:, j, :] = w[:, j * LANES:(j + 1) * LANES]


def _load_unpack(ref):
    w = jnp.concatenate([ref[:, j, :] for j in range(ROW_CHUNKS)], axis=1)
    lo = pltpu.bitcast(w << 16, F32)
    hi = pltpu.bitcast(w & jnp.uint32(0xFFFF0000), F32)
    return jnp.concatenate([lo, hi], axis=1)


def _rms(xf, g):
    ms = jnp.mean(xf * xf, axis=-1, keepdims=True)
    return xf * lax.rsqrt(ms + EPS) * g


def _params(n_axes):
    return pltpu.CompilerParams(dimension_semantics=("arbitrary",) * n_axes,
                                vmem_limit_bytes=VMEM_LIMIT)


def _ada_kernel(c_ref, w_ref, b_ref, o_ref):
    cf = c_ref[...]
    a = (cf * _sigmoid(cf)).astype(BF16)
    o_ref[0] = jnp.dot(a, w_ref[0].astype(BF16), preferred_element_type=F32) + b_ref[0]


def _ada_mod(c_rows, w_ada, b_ada):
    depth, d, n = w_ada.shape
    tn = 1024
    return pl.pallas_call(
        _ada_kernel,
        grid=(depth, n // tn),
        in_specs=[pl.BlockSpec((MOD_ROWS, d), lambda l, j: (0, 0)),
                  pl.BlockSpec((1, d, tn), lambda l, j: (l, 0, j)),
                  pl.BlockSpec((1, 1, tn), lambda l, j: (l, 0, j))],
        out_specs=pl.BlockSpec((1, MOD_ROWS, tn), lambda l, j: (l, 0, j)),
        out_shape=jax.ShapeDtypeStruct((depth, MOD_ROWS, n), F32),
        compiler_params=_params(2),
        name="ada_mod",
    )(c_rows, w_ada, b_ada.reshape(depth, 1, n))


def _mod_spec(k, brow):
    return pl.BlockSpec((1, 1, D_MODEL), lambda i, *_: (brow(i) * N_MOD + k, 0, 0))


def _inproj_kernel(x_ref, g_ref, sh_ref, sc_ref, w_ref, o_ref, h_scr, *, n_plain):
    j = pl.program_id(1)

    @pl.when(j == 0)
    def _():
        y = _rms(x_ref[...], g_ref[...])
        h_scr[...] = (y * (1.0 + sc_ref[0]) + sh_ref[0]).astype(BF16)

    acc = jnp.dot(h_scr[...], w_ref[...], preferred_element_type=F32)

    @pl.when(j < n_plain)
    def _():
        o_ref[...] = acc.astype(BF16)

    @pl.when(j >= n_plain)
    def _():
        o_ref[...] = _sigmoid(acc).astype(BF16)


def _in_proj(x2, mod3, brow, g, w, n_cols):
    t, d = x2.shape
    tm = min(512, t)
    tn = 512
    return pl.pallas_call(
        functools.partial(_inproj_kernel, n_plain=(LAT_W + POOL_DIM) // tn),
        grid=(t // tm, n_cols // tn),
        in_specs=[pl.BlockSpec((tm, d), lambda i, j: (i, 0)),
                  pl.BlockSpec((1, d), lambda i, j: (0, 0)),
                  _mod_spec(0, lambda i: brow(i * tm)),
                  _mod_spec(1, lambda i: brow(i * tm)),
                  pl.BlockSpec((d, tn), lambda i, j: (0, j))],
        out_specs=pl.BlockSpec((tm, tn), lambda i, j: (i, j)),
        out_shape=jax.ShapeDtypeStruct((t, n_cols), BF16),
        scratch_shapes=[pltpu.VMEM((tm, d), BF16)],
        compiler_params=_params(2),
        name="in_proj",
    )(x2, g.reshape(1, d), mod3, mod3, w)


def _qkv_kernel(lat_ref, cos_ref, sin_ref, gq_ref, gkv_ref, wq_ref, wk_ref, wvt_ref, *out_refs, with_q):
    if with_q:
        q_ref, k_ref, v_ref = out_refs
    else:
        k_ref, v_ref = out_refs
    cos = cos_ref[...]
    sin = sin_ref[...]

    def rope(grp):
        return grp * cos + pltpu.roll(grp, 2 * (QK_ROPE // 2), axis=1) * sin

    kvn = _rms(lat_ref[:, :KV_RANK].astype(F32), gkv_ref[...]).astype(BF16)
    kpe = rope(lat_ref[:, KV_RANK + Q_RANK:].astype(F32)).astype(BF16)
    for h in range(N_HEADS):
        kn = jnp.dot(kvn, wk_ref[:, h * QK_NOPE:(h + 1) * QK_NOPE], preferred_element_type=F32)
        k_ref[0, h, :, :QK_NOPE] = kn.astype(BF16)
        k_ref[0, h, :, QK_NOPE:] = kpe
        vt = lax.dot_general(wvt_ref[h], kvn, (((1,), (1,)), ((), ())), preferred_element_type=F32)
        v_ref[0, h] = vt.astype(BF16)
    if with_q:
        qn = _rms(lat_ref[:, KV_RANK:KV_RANK + Q_RANK].astype(F32), gq_ref[...]).astype(BF16)
        for h in range(N_HEADS):
            qh = jnp.dot(qn, wq_ref[:, h * HEAD_W:(h + 1) * HEAD_W], preferred_element_type=F32)
            q_ref[0, h, :, :QK_NOPE] = (qh[:, :QK_NOPE] * Q_SCALE).astype(BF16)
            q_ref[0, h, :, QK_NOPE:] = (rope(qh[:, QK_NOPE:]) * Q_SCALE).astype(BF16)


def _qkv(p, b, n, cos_t, sin_t, gq, gkv, wq, wk, wvt, with_q):
    tm = min(256, n)
    tpb = n // tm
    rope_tiles = cos_t.shape[0] // tm
    head_spec = pl.BlockSpec((1, N_HEADS, tm, HEAD_W), lambda i: (i // tpb, 0, i % tpb, 0))
    tab_spec = pl.BlockSpec((tm, LANES), lambda i: ((i % tpb) % rope_tiles, 0))
    out_shape = [jax.ShapeDtypeStruct((b, N_HEADS, n, HEAD_W), BF16),
                 jax.ShapeDtypeStruct((b, N_HEADS, V_DIM, n), BF16)]
    out_specs = [head_spec,
                 pl.BlockSpec((1, N_HEADS, V_DIM, tm), lambda i: (i // tpb, 0, 0, i % tpb))]
    if with_q:
        out_shape = [jax.ShapeDtypeStruct((b, N_HEADS, n, HEAD_W), BF16)] + out_shape
        out_specs = [head_spec] + out_specs
    return pl.pallas_call(
        functools.partial(_qkv_kernel, with_q=with_q),
        grid=(b * tpb,),
        in_specs=[pl.BlockSpec((tm, LAT_W), lambda i: (i, 0)),
                  tab_spec, tab_spec,
                  pl.BlockSpec((1, Q_RANK), lambda i: (0, 0)),
                  pl.BlockSpec((1, KV_RANK), lambda i: (0, 0)),
                  pl.BlockSpec(wq.shape, lambda i: (0, 0)),
                  pl.BlockSpec(wk.shape, lambda i: (0, 0)),
                  pl.BlockSpec(wvt.shape, lambda i: (0, 0, 0))],
        out_specs=out_specs,
        out_shape=out_shape,
        compiler_params=_params(1),
        name="qkv",
    )(p, cos_t, sin_t, gq.reshape(1, Q_RANK), gkv.reshape(1, KV_RANK), wq, wk, wvt)


def _row_fold(x, op):
    parts = [x[r:r + SUBLANES] for r in range(0, x.shape[0], SUBLANES)]
    a, b = parts[0], parts[1]
    for i in range(2, len(parts) - 1, 2):
        a, b = op(a, parts[i]), op(b, parts[i + 1])
    if len(parts) % 2:
        a = op(a, parts[-1])
    return op(a, b)


def _attn_kernel(q_ref, *refs, n_seg, kb, tq):
    k_refs, vt_refs = refs[:n_seg], refs[n_seg:2 * n_seg]
    o_ref, s_a, s_b, m_a, m_b = refs[2 * n_seg:]
    chunks = []
    off = 0
    for kr, vr in zip(k_refs, vt_refs):
        nk = kr.shape[2]
        for c0 in range(0, nk, kb):
            n = min(kb, nk - c0)
            chunks.append((kr, vr, c0, n, off))
            off += n

    def scores(t, s_buf, m_buf):
        q = q_ref[0, 0, pl.ds(pl.multiple_of(t * tq, tq), tq), :]
        mp = None
        for kr, vr, c0, n, o in chunks:
            s = lax.dot_general(kr[0, 0, c0:c0 + n, :], q, (((1,), (1,)), ((), ())),
                                preferred_element_type=F32)
            s_buf[o:o + n, :] = s
            m = _row_fold(s, jnp.maximum)
            mp = m if mp is None else jnp.maximum(mp, m)
        m_buf[...] = mp

    def values(t, s_buf, m_buf):
        mrow = jnp.max(m_buf[...], axis=0, keepdims=True)
        lp = None
        acc = None
        for kr, vr, c0, n, o in chunks:
            pr = jnp.exp2(s_buf[o:o + n, :] - mrow)
            ls = _row_fold(pr, jnp.add)
            lp = ls if lp is None else lp + ls
            pv = jnp.dot(vr[0, 0, :, c0:c0 + n], pr.astype(BF16), preferred_element_type=F32)
            acc = pv if acc is None else acc + pv
        l = jnp.sum(lp, axis=0, keepdims=True)
        o_ref[0, pl.ds(pl.multiple_of(t * tq, tq), tq), :] = (acc / l).T.astype(BF16)

    nt = q_ref.shape[2] // tq
    scores(0, s_a, m_a)
    if nt > 1:
        def pair(j, carry):
            scores(2 * j + 1, s_b, m_b)
            values(2 * j, s_a, m_a)
            scores(2 * j + 2, s_a, m_a)
            values(2 * j + 1, s_b, m_b)
            return carry

        lax.fori_loop(0, nt // 2 - 1, pair, 0)
        scores(nt - 1, s_b, m_b)
        values(nt - 2, s_a, m_a)
        values(nt - 1, s_b, m_b)
    else:
        values(0, s_a, m_a)


def _attention(q, ks, vts):
    b, hh, n, _ = q.shape
    tq = min(ATTN_TQ, n)
    nt = n // tq
    assert nt == 1 or nt % 2 == 0
    nk_total = sum(k.shape[2] for k in ks)
    seg_spec = lambda a: pl.BlockSpec((1, 1) + a.shape[2:], lambda bi, h: (bi, h, 0, 0))
    return pl.pallas_call(
        functools.partial(_attn_kernel, n_seg=len(ks), kb=ATTN_KB, tq=tq),
        grid=(b, hh),
        in_specs=[pl.BlockSpec((1, 1, n, HEAD_W), lambda bi, h: (bi, h, 0, 0))]
                 + [seg_spec(a) for a in ks] + [seg_spec(a) for a in vts],
        out_specs=pl.BlockSpec((1, n, V_DIM), lambda bi, h: (bi, 0, h)),
        out_shape=jax.ShapeDtypeStruct((b, n, hh * V_DIM), BF16),
        scratch_shapes=[pltpu.VMEM((nk_total, tq), F32), pltpu.VMEM((nk_total, tq), F32),
                        pltpu.VMEM((SUBLANES, tq), F32), pltpu.VMEM((SUBLANES, tq), F32)],
        compiler_params=_params(2),
        name="attention",
    )(q, *ks, *vts)


def _top2sum(a, b, c, d):
    s1, t1 = jnp.maximum(a, b), jnp.minimum(a, b)
    s2, t2 = jnp.maximum(c, d), jnp.minimum(c, d)
    return jnp.maximum(s1, s2) + jnp.maximum(jnp.minimum(s1, s2), jnp.maximum(t1, t2))


def _route(logits_t, bias):
    sc = _sigmoid(logits_t)
    sel = sc + bias
    sel_r = [sel[e:e + 1, :] for e in range(N_EXPERTS)]
    sc_r = [sc[e:e + 1, :] for e in range(N_EXPERTS)]
    epg = EXPERTS_PER_GROUP
    gs = [_top2sum(*sel_r[g * epg:(g + 1) * epg]) for g in range(N_GROUPS)]
    best, gi = gs[0], jnp.zeros(gs[0].shape, jnp.int32)
    for g in range(1, N_GROUPS):
        upd = gs[g] > best
        best = jnp.where(upd, gs[g], best)
        gi = jnp.where(upd, g, gi)

    def pick_group(rows, k):
        r = rows[k]
        for g in range(1, N_GROUPS):
            r = jnp.where(gi == g, rows[g * epg + k], r)
        return r

    in_sel = [pick_group(sel_r, k) for k in range(epg)]
    in_sc = [pick_group(sc_r, k) for k in range(epg)]
    b1, i1 = in_sel[0], jnp.zeros(gi.shape, jnp.int32)
    for k in range(1, epg):
        upd = in_sel[k] > b1
        b1 = jnp.where(upd, in_sel[k], b1)
        i1 = jnp.where(upd, k, i1)
    b2, i2 = None, None
    for k in range(epg):
        cand = jnp.where(i1 == k, -jnp.inf, in_sel[k])
        if b2 is None:
            b2, i2 = cand, jnp.zeros(gi.shape, jnp.int32)
        else:
            upd = cand > b2
            b2 = jnp.where(upd, cand, b2)
            i2 = jnp.where(upd, k, i2)

    def pick_local(idx):
        r = in_sc[0]
        for k in range(1, epg):
            r = jnp.where(idx == k, in_sc[k], r)
        return r

    s1, s2 = pick_local(i1), pick_local(i2)
    den = s1 + s2
    return gi * epg + i1, gi * epg + i2, s1 / den, s2 / den


def _mixout_kernel(y_ref, u_ref, up_ref, un_ref, gm_ref, gp_ref, wp_ref, ps_ref, wo_ref, x_ref,
                   g1_ref, gf_ref, sh2_ref, sc2_ref, wr_ref, rb_ref,
                   xo_ref, h2_ref, route_ref, *, tm, tpb, n_seq, packed):
    i = pl.program_id(0)
    base = (i % tpb) * tm
    uc = u_ref[...]
    uext = jnp.concatenate([up_ref[...], uc, un_ref[...]], axis=0)
    r = lax.broadcasted_iota(jnp.int32, (tm, tm + 2 * POOL_HALO), 0)
    c = lax.broadcasted_iota(jnp.int32, (tm, tm + 2 * POOL_HALO), 1)
    rel = c - POOL_HALO - r
    jpos = base - POOL_HALO + c
    valid = (jpos >= 0) & (jpos < n_seq)
    tpos = base + lax.broadcasted_iota(jnp.int32, (tm, 1), 0)
    parts = []
    for g, w in enumerate(POOL_WINDOWS):
        sl = slice(g * POOL_GROUP_DIM, (g + 1) * POOL_GROUP_DIM)
        band = jnp.where(valid & (rel >= -(w // 2)) & (rel < w // 2), 1.0, 0.0).astype(BF16)
        cnt = jnp.clip(tpos + w // 2, 0, n_seq) - jnp.clip(tpos - w // 2, 0, n_seq)
        wsum = jnp.dot(band, uext[:, sl], preferred_element_type=F32)
        z = wsum / cnt.astype(F32) - uc[:, sl].astype(F32)
        parts.append(jnp.dot(z.astype(BF16), wp_ref[g], preferred_element_type=F32))
    ypool = jnp.concatenate(parts, axis=1) * ps_ref[...]
    mix = gm_ref[...].astype(F32) * y_ref[...].astype(F32) + gp_ref[...].astype(F32) * ypool
    y = jnp.dot(mix.astype(BF16), wo_ref[...], preferred_element_type=F32)
    xn = x_ref[...] + g1_ref[0] * y
    xo_ref[...] = xn
    h2f = _rms(xn, gf_ref[...]) * (1.0 + sc2_ref[0]) + sh2_ref[0]
    h2 = h2f.astype(BF16)
    if packed:
        _store_chunks(h2_ref, _pack_rows(h2f))
    else:
        h2_ref[...] = h2
    logits_t = lax.dot_general(wr_ref[...], h2, (((1,), (1,)), ((), ())), preferred_element_type=F32)
    e1, e2, w1, w2 = _route(logits_t, rb_ref[...])
    zero = jnp.zeros((ROUTE_ROWS - 4, tm), F32)
    route_ref[...] = jnp.concatenate([e1.astype(F32), e2.astype(F32), w1, w2, zero], axis=0)


def _mixer_out(y, p, x2, mod3, brow, n_seq, wp, ps, wo, gf, wr_t, rb, packed):
    t, d = x2.shape
    tm = min(256, n_seq)
    tpb = n_seq // tm
    hpt = tm // POOL_HALO
    n_halo = t // POOL_HALO
    row = lambda i: (i, 0)
    const2 = lambda i: (0, 0)
    mrow = lambda i: brow(i * tm)
    return pl.pallas_call(
        functools.partial(_mixout_kernel, tm=tm, tpb=tpb, n_seq=n_seq, packed=packed),
        grid=(t // tm,),
        in_specs=[pl.BlockSpec((tm, d), row),
                  pl.BlockSpec((tm, POOL_DIM), lambda i: (i, LAT_W // POOL_DIM)),
                  pl.BlockSpec((POOL_HALO, POOL_DIM),
                               lambda i: (jnp.maximum(i * hpt - 1, 0), LAT_W // POOL_DIM)),
                  pl.BlockSpec((POOL_HALO, POOL_DIM),
                               lambda i: (jnp.minimum((i + 1) * hpt, n_halo - 1), LAT_W // POOL_DIM)),
                  pl.BlockSpec((tm, d), lambda i: (i, (LAT_W + POOL_DIM) // d)),
                  pl.BlockSpec((tm, d), lambda i: (i, (LAT_W + POOL_DIM) // d + 1)),
                  pl.BlockSpec(wp.shape, lambda i: (0, 0, 0)),
                  pl.BlockSpec((1, d), const2),
                  pl.BlockSpec((d, d), const2),
                  pl.BlockSpec((tm, d), row),
                  _mod_spec(2, mrow),
                  pl.BlockSpec((1, d), const2),
                  _mod_spec(3, mrow),
                  _mod_spec(4, mrow),
                  pl.BlockSpec((N_EXPERTS, d), const2),
                  pl.BlockSpec((N_EXPERTS, 1), const2)],
        out_specs=[pl.BlockSpec((tm, d), row),
                   (pl.BlockSpec((tm, ROW_CHUNKS, LANES), lambda i: (i, 0, 0)) if packed
                    else pl.BlockSpec((tm, d), row)),
                   pl.BlockSpec((ROUTE_ROWS, tm), lambda i: (0, i))],
        out_shape=[jax.ShapeDtypeStruct((t, d), F32),
                   (jax.ShapeDtypeStruct((t, ROW_CHUNKS, LANES), jnp.uint32) if packed
                    else jax.ShapeDtypeStruct((t, d), BF16)),
                   jax.ShapeDtypeStruct((ROUTE_ROWS, t), F32)],
        compiler_params=_params(1),
        name="mixer_out",
    )(y, p, p, p, p, p, wp, ps.reshape(1, d), wo, x2, mod3, gf.reshape(1, d), mod3, mod3, wr_t,
      rb.reshape(N_EXPERTS, 1))


def _moe_kernel(h_ref, comb_ref, wg_ref, wu_ref, wd_ref, x_ref, g2_ref, gfin_ref, o_ref, acc_scr,
                *, final):
    e = pl.program_id(1)

    @pl.when(e == 0)
    def _():
        acc_scr[...] = jnp.zeros_like(acc_scr)

    h = h_ref[...]
    comb = comb_ref[...]
    lane = lax.broadcasted_iota(jnp.int32, comb.shape, 1)
    col = jnp.sum(jnp.where(lane == e, comb, 0.0), axis=1, keepdims=True)
    gate = jnp.dot(h, wg_ref[0], preferred_element_type=F32)
    up = jnp.dot(h, wu_ref[0], preferred_element_type=F32)
    a = (gate * _sigmoid(gate)) * up * col
    acc_scr[...] += jnp.dot(a.astype(BF16), wd_ref[0], preferred_element_type=F32)

    @pl.when(e == pl.num_programs(1) - 1)
    def _():
        xn = x_ref[...] + g2_ref[0] * acc_scr[...]
        if final:
            xn = _rms(xn, gfin_ref[...])
        o_ref[...] = xn


def _moe(h2, comb, wg, wu, wd, x2, mod3, brow, gfin, final):
    t, d = x2.shape
    tm = min(512, t)
    ne = wg.shape[0]
    row = lambda i, e: (i, 0)
    return pl.pallas_call(
        functools.partial(_moe_kernel, final=final),
        grid=(t // tm, ne),
        in_specs=[pl.BlockSpec((tm, d), row),
                  pl.BlockSpec((tm, LANES), row),
                  pl.BlockSpec((1, d, D_EXPERT), lambda i, e: (e, 0, 0)),
                  pl.BlockSpec((1, d, D_EXPERT), lambda i, e: (e, 0, 0)),
                  pl.BlockSpec((1, D_EXPERT, d), lambda i, e: (e, 0, 0)),
                  pl.BlockSpec((tm, d), row),
                  _mod_spec(5, lambda i: brow(i * tm)),
                  pl.BlockSpec((1, d), lambda i, e: (0, 0))],
        out_specs=pl.BlockSpec((tm, d), row),
        out_shape=jax.ShapeDtypeStruct((t, d), F32),
        scratch_shapes=[pltpu.VMEM((tm, d), F32)],
        compiler_params=_params(2),
        name="moe",
    )(h2, comb, wg, wu, wd, x2, mod3, gfin.reshape(1, d))


def _sc_gather_rows(table, idx):
    n, m = table.shape[0], idx.shape[0]
    n_idx = m * ROW_CHUNKS
    assert n_idx % SC_WINDOW == 0
    idx8 = (idx[:, None] * ROW_CHUNKS + jnp.arange(ROW_CHUNKS, dtype=jnp.int32)[None, :]).reshape(1, n_idx)
    mesh = plsc.VectorSubcoreMesh(core_axis_name="core", subcore_axis_name="subcore")

    @pl.kernel(out_type=jax.ShapeDtypeStruct((n_idx, LANES), jnp.uint32), mesh=mesh)
    def gather(t_hbm, i_hbm, o_hbm):
        def body(i_vmem, o_vmem):
            pltpu.sync_copy(t_hbm.at[i_vmem.at[0]], o_vmem)

        pltpu.emit_pipeline(
            body,
            grid=(n_idx // SC_WINDOW,),
            in_specs=[pl.BlockSpec((1, SC_WINDOW), index_map=lambda i: (0, i))],
            out_specs=[pl.BlockSpec((SC_WINDOW, LANES), index_map=lambda i: (i, 0))],
            core_axis_name=("core", "subcore"),
            dimension_semantics=(pltpu.PARALLEL,),
        )(i_hbm, o_hbm)

    return gather(table.reshape(n * ROW_CHUNKS, LANES), idx8).reshape(m, ROW_CHUNKS, LANES)


def _dispatch_plan(route):
    t = route.shape[1]
    n_pairs = 2 * t
    n_slots = n_pairs + N_EXPERTS * MOE_TILE
    n_tiles = n_slots // MOE_TILE
    e = jnp.concatenate([route[0], route[1]]).astype(jnp.int32)
    onehot = (e[:, None] == jnp.arange(N_EXPERTS, dtype=jnp.int32)[None, :]).astype(jnp.int32)
    csum = jnp.cumsum(onehot, axis=0)
    counts = csum[-1]
    rank = jnp.sum(onehot * csum, axis=1) - 1
    padded = ((counts + MOE_TILE - 1) // MOE_TILE) * MOE_TILE
    ends = jnp.cumsum(padded)
    starts = ends - padded
    pair_slot = starts[e] + rank
    _, order = lax.sort((e, jnp.arange(n_pairs, dtype=jnp.int32)), num_keys=1, is_stable=True)
    n_used = ends[-1] // MOE_TILE
    tile_id = jnp.arange(n_tiles, dtype=jnp.int32)
    tile_e = jnp.minimum(jnp.searchsorted(ends, tile_id * MOE_TILE, side="right"), N_EXPERTS - 1)
    tile_expert = jnp.where(tile_id < n_used, tile_e, tile_e[jnp.maximum(n_used - 1, 0)]).astype(jnp.int32)
    slot = jnp.arange(n_slots, dtype=jnp.int32)
    slot_e = tile_e[slot // MOE_TILE]
    j = slot - starts[slot_e]
    unpadded_start = jnp.cumsum(counts) - counts
    src = order[jnp.clip(unpadded_start[slot_e] + j, 0, n_pairs - 1)]
    slot_token = jnp.where(j < counts[slot_e], src % t, 0).astype(jnp.int32)
    return slot_token, pair_slot.astype(jnp.int32), tile_expert, n_used.reshape(1).astype(jnp.int32)


def _ffn_kernel(te_ref, nu_ref, xs_ref, wg_ref, wu_ref, wd_ref, ys_ref):
    i = pl.program_id(0)

    @pl.when(i < nu_ref[0])
    def _():
        xb = _load_unpack(xs_ref).astype(BF16)
        gate = jnp.dot(xb, wg_ref[0], preferred_element_type=F32)
        up = jnp.dot(xb, wu_ref[0], preferred_element_type=F32)
        a = ((gate * _sigmoid(gate)) * up).astype(BF16)
        _store_chunks(ys_ref, _pack_rows(jnp.dot(a, wd_ref[0], preferred_element_type=F32)))

    @pl.when(i >= nu_ref[0])
    def _():
        ys_ref[...] = jnp.zeros_like(ys_ref)


def _expert_ffn(xs, tile_expert, n_used, wg, wu, wd):
    n_slots = xs.shape[0]
    d = D_MODEL
    blk = pl.BlockSpec((MOE_TILE, ROW_CHUNKS, LANES), lambda i, te, nu: (i, 0, 0))
    return pl.pallas_call(
        _ffn_kernel,
        grid_spec=pltpu.PrefetchScalarGridSpec(
            num_scalar_prefetch=2,
            grid=(n_slots // MOE_TILE,),
            in_specs=[blk,
                      pl.BlockSpec((1, d, D_EXPERT), lambda i, te, nu: (te[i], 0, 0)),
                      pl.BlockSpec((1, d, D_EXPERT), lambda i, te, nu: (te[i], 0, 0)),
                      pl.BlockSpec((1, D_EXPERT, d), lambda i, te, nu: (te[i], 0, 0))],
            out_specs=blk),
        out_shape=jax.ShapeDtypeStruct(xs.shape, jnp.uint32),
        compiler_params=_params(1),
        name="expert_ffn",
    )(tile_expert, n_used, xs, wg, wu, wd)


def _combine_kernel(h_ref, y1_ref, y2_ref, w_ref, sg_ref, su_ref, sd_ref, x_ref, g2_ref, gfin_ref, o_ref,
                    *, final):
    hb = _load_unpack(h_ref).astype(BF16)
    gate = jnp.dot(hb, sg_ref[...], preferred_element_type=F32)
    up = jnp.dot(hb, su_ref[...], preferred_element_type=F32)
    a = ((gate * _sigmoid(gate)) * up).astype(BF16)
    y = jnp.dot(a, sd_ref[...], preferred_element_type=F32)
    w = w_ref[...]
    y = y + w[:, 0:1] * _load_unpack(y1_ref) + w[:, 1:2] * _load_unpack(y2_ref)
    xn = x_ref[...] + g2_ref[0] * y
    if final:
        xn = _rms(xn, gfin_ref[...])
    o_ref[...] = xn


def _moe_combine(h2p, yg, w12, sg, su, sd, x2, mod3, brow, gfin, final):
    t, d = x2.shape
    tm = min(256, t)
    nt = t // tm
    row = lambda i: (i, 0)
    const2 = lambda i: (0, 0)
    pk = lambda off: pl.BlockSpec((tm, ROW_CHUNKS, LANES), lambda i: (i + off, 0, 0))
    return pl.pallas_call(
        functools.partial(_combine_kernel, final=final),
        grid=(nt,),
        in_specs=[pk(0), pk(0), pk(nt),
                  pl.BlockSpec((tm, LANES), row),
                  pl.BlockSpec(sg.shape, const2),
                  pl.BlockSpec(su.shape, const2),
                  pl.BlockSpec(sd.shape, const2),
                  pl.BlockSpec((tm, d), row),
                  _mod_spec(5, lambda i: brow(i * tm)),
                  pl.BlockSpec((1, d), const2)],
        out_specs=pl.BlockSpec((tm, d), row),
        out_shape=jax.ShapeDtypeStruct((t, d), F32),
        compiler_params=_params(1),
        name="moe_combine",
    )(h2p, yg, yg, w12, sg, su, sd, x2, mod3, gfin.reshape(1, d))


def _moe_routed(h2p, route, wg, wu, wd, sg, su, sd, x2, mod3, brow, gfin, final):
    t = x2.shape[0]
    slot_token, pair_slot, tile_expert, n_used = _dispatch_plan(route)
    xs = _sc_gather_rows(h2p, slot_token)
    ys = _expert_ffn(xs, tile_expert, n_used, wg, wu, wd)
    yg = _sc_gather_rows(ys, pair_slot)
    w12 = jnp.concatenate([route[2:4].T, jnp.zeros((t, LANES - 2), F32)], axis=1)
    return _moe_combine(h2p, yg, w12, sg, su, sd, x2, mod3, brow, gfin, final)


def _rope_tables(seq):
    rows = seq // GRID_W
    row = jnp.repeat(jnp.arange(rows), GRID_W).astype(F32)
    col = jnp.tile(jnp.arange(GRID_W), rows).astype(F32)
    axis_dim = QK_ROPE // 2
    inv = ROPE_THETA ** (-jnp.arange(0, axis_dim, 2, dtype=F32) / axis_dim)
    ang = jnp.concatenate([row[:, None] * inv, col[:, None] * inv], axis=-1)
    cos, sin = jnp.cos(ang), jnp.sin(ang)
    zero = jnp.zeros_like(cos)
    return (jnp.concatenate([cos, cos, zero, zero], axis=1),
            jnp.concatenate([-sin, sin, zero, zero], axis=1))


def _rope_group(w, start):
    half = QK_ROPE // 2
    x1, x2 = w[:, start:start + half], w[:, start + half:start + 2 * half]
    return jnp.concatenate([x1, x2, x2, x1], axis=1)


def _relayout_w_in(w):
    return jnp.concatenate([w[:, :KV_RANK], w[:, KV_COLS:Q_END], _rope_group(w, KV_RANK),
                            w[:, Q_END:]], axis=1).astype(BF16)


def _relayout_w_qb(w):
    dq = QK_NOPE + QK_ROPE
    cols = []
    for h in range(N_HEADS):
        cols += [w[:, h * dq:h * dq + QK_NOPE], _rope_group(w, h * dq + QK_NOPE)]
    return jnp.concatenate(cols, axis=1).astype(BF16)


def kernel(x, c, ctx, c_ctx, w_ada, b_ada, norm_mix_g, norm_ffn_g, w_in, q_norm_g, kv_norm_g, w_qb, w_kvb, w_pool, pool_scale, w_out, w_router, router_bias, w_exp_gate, w_exp_up, w_exp_down, w_sh_gate, w_sh_up, w_sh_down, final_norm_g):
    bsz, seq, d = x.shape
    n_ctx = ctx.shape[1]
    depth = w_ada.shape[0]
    assert bsz < MOD_ROWS and d == D_MODEL

    c_rows = jnp.concatenate([c, c_ctx[None], jnp.zeros((MOD_ROWS - bsz - 1, d), F32)], axis=0)
    mod = _ada_mod(c_rows, w_ada, b_ada)

    cos_t, sin_t = _rope_tables(seq)
    ones_t = jnp.concatenate([jnp.ones((n_ctx, 2 * (QK_ROPE // 2)), F32),
                              jnp.zeros((n_ctx, LANES - QK_ROPE), F32)], axis=1)
    zeros_t = jnp.zeros((n_ctx, LANES), F32)
    wr_t = w_router.T.astype(BF16)
    lat_row = lambda r: r // seq
    ctx_row = lambda r: bsz

    x2 = x.reshape(bsz * seq, d)
    xc2 = ctx.reshape(bsz * n_ctx, d)
    for l in range(depth):
        last = l == depth - 1
        mod3 = mod[l].reshape(MOD_ROWS * N_MOD, 1, d)
        w_in_r = _relayout_w_in(w_in[l])
        wq = _relayout_w_qb(w_qb[l])
        w_kv3 = w_kvb[l].reshape(KV_RANK, N_HEADS, QK_NOPE + V_DIM)
        wk = w_kv3[:, :, :QK_NOPE].reshape(KV_RANK, N_HEADS * QK_NOPE).astype(BF16)
        wvt = w_kv3[:, :, QK_NOPE:].transpose(1, 2, 0).astype(BF16)
        wp = w_pool[l].astype(BF16)
        wo = w_out[l].astype(BF16)
        wg, wu, wd = w_exp_gate[l].astype(BF16), w_exp_up[l].astype(BF16), w_exp_down[l].astype(BF16)
        sg, su, sd = w_sh_gate[l].astype(BF16), w_sh_up[l].astype(BF16), w_sh_down[l].astype(BF16)

        def comb_cols(route):
            t = route.shape[1]
            eid = jnp.arange(N_EXPERTS, dtype=F32)[None, :]
            comb = (jnp.where(route[0][:, None] == eid, route[2][:, None], 0.0)
                    + jnp.where(route[1][:, None] == eid, route[3][:, None], 0.0))
            return jnp.concatenate([comb, jnp.ones((t, 1), F32),
                                    jnp.zeros((t, LANES - N_EXPERTS - 1), F32)], axis=1)

        p = _in_proj(x2, mod3, lat_row, norm_mix_g[l], w_in_r, P_COLS)
        pc = _in_proj(xc2, mod3, ctx_row, norm_mix_g[l], w_in_r, LAT_W if last else P_COLS)
        q, k_lat, v_lat = _qkv(p, bsz, seq, cos_t, sin_t, q_norm_g[l], kv_norm_g[l], wq, wk, wvt, True)
        ctx_out = _qkv(pc, bsz, n_ctx, ones_t, zeros_t, q_norm_g[l], kv_norm_g[l], wq, wk, wvt, not last)
        k_ctx, v_ctx = ctx_out[-2:]
        y = _attention(q, [k_ctx, k_lat], [v_ctx, v_lat]).reshape(bsz * seq, d)
        x2, h2p, route = _mixer_out(y, p, x2, mod3, lat_row, seq, wp, pool_scale[l], wo,
                                    norm_ffn_g[l], wr_t, router_bias, True)
        x2 = _moe_routed(h2p, route, wg, wu, wd, sg, su, sd, x2, mod3, lat_row, final_norm_g, last)
        if not last:
            yc = _attention(ctx_out[0], [k_ctx], [v_ctx]).reshape(bsz * n_ctx, d)
            xc2, h2c, route_c = _mixer_out(yc, pc, xc2, mod3, ctx_row, n_ctx, wp, pool_scale[l], wo,
                                           norm_ffn_g[l], wr_t, router_bias, False)
            xc2 = _moe(h2c, comb_cols(route_c),
                       jnp.concatenate([wg, sg[None]], axis=0), jnp.concatenate([wu, su[None]], axis=0),
                       jnp.concatenate([wd, sd[None]], axis=0), xc2, mod3, ctx_row, final_norm_g, False)
    return x2.reshape(bsz, seq, d)
```

```python
import functools
import math

import jax
import jax.numpy as jnp
from jax import lax
from jax.experimental import pallas as pl
from jax.experimental.pallas import tpu as pltpu
from jax.experimental.pallas import tpu_sc as plsc

F32 = jnp.float32
BF16 = jnp.bfloat16

D_MODEL = 2048
GRID_W = 64
N_HEADS = 16
QK_NOPE = 128
QK_ROPE = 64
V_DIM = 128
Q_RANK = 384
KV_RANK = 512
ROPE_THETA = 10000.0
POOL_DIM = 1024
POOL_GROUPS = 4
POOL_GROUP_DIM = 256
POOL_WINDOWS = (2, 4, 8, 16)
POOL_GROUP_OUT = 512
KV_COLS = KV_RANK + QK_ROPE
Q_END = KV_COLS + Q_RANK
POOL_END = Q_END + POOL_DIM
N_EXPERTS = 16
N_GROUPS = 4
EXPERTS_PER_GROUP = 4
D_EXPERT = 512
EPS = 1e-6
N_MOD = 6

LANES = 128
HEAD_W = 2 * LANES
LAT_W = 1024
P_COLS = LAT_W + POOL_DIM + 2 * D_MODEL
POOL_HALO = 16
MOD_ROWS = 8
Q_SCALE = (1.0 / math.sqrt(QK_NOPE + QK_ROPE)) * math.log2(math.e)
VMEM_LIMIT = 56 * 1024 * 1024
SUBLANES = 8
ROW_CHUNKS = D_MODEL // 2 // LANES
ROUTE_ROWS = SUBLANES
MOE_TILE = 256
SC_WINDOW = 128
ATTN_TQ = 256
ATTN_KB = 512


def _sigmoid(x):
    return 1.0 / (1.0 + jnp.exp(-x))


def _bf16_bits(x):
    u = pltpu.bitcast(x, jnp.uint32)
    return (u + jnp.uint32(0x7FFF) + ((u >> 16) & jnp.uint32(1))) >> 16


def _pack_rows(y):
    half = y.shape[1] // 2
    return (_bf16_bits(y[:, half:]) << 16) | _bf16_bits(y[:, :half])


def _store_chunks(ref, w):
    for j in range(ROW_CHUNKS):
        ref[:, j, :] = w[:, j * LANES:(j + 1) * LANES]


def _load_unpack(ref):
    w = jnp.concatenate([ref[:, j, :] for j in range(ROW_CHUNKS)], axis=1)
    lo = pltpu.bitcast(w << 16, F32)
    hi = pltpu.bitcast(w & jnp.uint32(0xFFFF0000), F32)
    return jnp.concatenate([lo, hi], axis=1)


def _rms(xf, g):
    ms = jnp.mean(xf * xf, axis=-1, keepdims=True)
    return xf * lax.rsqrt(ms + EPS) * g


def _params(n_axes):
    return pltpu.CompilerParams(dimension_semantics=("arbitrary",) * n_axes,
                                vmem_limit_bytes=VMEM_LIMIT)


def _ada_kernel(c_ref, w_ref, b_ref, o_ref):
    cf = c_ref[...]
    a = (cf * _sigmoid(cf)).astype(BF16)
    o_ref[0] = jnp.dot(a, w_ref[0].astype(BF16), preferred_element_type=F32) + b_ref[0]


def _ada_mod(c_rows, w_ada, b_ada):
    depth, d, n = w_ada.shape
    tn = 1024
    return pl.pallas_call(
        _ada_kernel,
        grid=(depth, n // tn),
        in_specs=[pl.BlockSpec((MOD_ROWS, d), lambda l, j: (0, 0)),
                  pl.BlockSpec((1, d, tn), lambda l, j: (l, 0, j)),
                  pl.BlockSpec((1, 1, tn), lambda l, j: (l, 0, j))],
        out_specs=pl.BlockSpec((1, MOD_ROWS, tn), lambda l, j: (l, 0, j)),
        out_shape=jax.ShapeDtypeStruct((depth, MOD_ROWS, n), F32),
        compiler_params=_params(2),
        name="ada_mod",
    )(c_rows, w_ada, b_ada.reshape(depth, 1, n))


def _mod_spec(k, brow):
    return pl.BlockSpec((1, 1, D_MODEL), lambda i, *_: (brow(i) * N_MOD + k, 0, 0))


def _inproj_kernel(x_ref, g_ref, sh_ref, sc_ref, w_ref, o_ref, h_scr, *, n_plain):
    j = pl.program_id(1)

    @pl.when(j == 0)
    def _():
        y = _rms(x_ref[...], g_ref[...])
        h_scr[...] = (y * (1.0 + sc_ref[0]) + sh_ref[0]).astype(BF16)

    acc = jnp.dot(h_scr[...], w_ref[...], preferred_element_type=F32)

    @pl.when(j < n_plain)
    def _():
        o_ref[...] = acc.astype(BF16)

    @pl.when(j >= n_plain)
    def _():
        o_ref[...] = _sigmoid(acc).astype(BF16)


def _in_proj(x2, mod3, brow, g, w, n_cols):
    t, d = x2.shape
    tm = min(512, t)
    tn = 512
    return pl.pallas_call(
        functools.partial(_inproj_kernel, n_plain=(LAT_W + POOL_DIM) // tn),
        grid=(t // tm, n_cols // tn),
        in_specs=[pl.BlockSpec((tm, d), lambda i, j: (i, 0)),
                  pl.BlockSpec((1, d), lambda i, j: (0, 0)),
                  _mod_spec(0, lambda i: brow(i * tm)),
                  _mod_spec(1, lambda i: brow(i * tm)),
                  pl.BlockSpec((d, tn), lambda i, j: (0, j))],
        out_specs=pl.BlockSpec((tm, tn), lambda i, j: (i, j)),
        out_shape=jax.ShapeDtypeStruct((t, n_cols), BF16),
        scratch_shapes=[pltpu.VMEM((tm, d), BF16)],
        compiler_params=_params(2),
        name="in_proj",
    )(x2, g.reshape(1, d), mod3, mod3, w)


def _qkv_kernel(lat_ref, cos_ref, sin_ref, gq_ref, gkv_ref, wq_ref, wk_ref, wvt_ref, *out_refs, with_q):
    if with_q:
        q_ref, k_ref, v_ref = out_refs
    else:
        k_ref, v_ref = out_refs
    cos = cos_ref[...]
    sin = sin_ref[...]

    def rope(grp):
        return grp * cos + pltpu.roll(grp, 2 * (QK_ROPE // 2), axis=1) * sin

    kvn = _rms(lat_ref[:, :KV_RANK].astype(F32), gkv_ref[...]).astype(BF16)
    kpe = rope(lat_ref[:, KV_RANK + Q_RANK:].astype(F32)).astype(BF16)
    for h in range(N_HEADS):
        kn = jnp.dot(kvn, wk_ref[:, h * QK_NOPE:(h + 1) * QK_NOPE], preferred_element_type=F32)
        k_ref[0, h, :, :QK_NOPE] = kn.astype(BF16)
        k_ref[0, h, :, QK_NOPE:] = kpe
        vt = lax.dot_general(wvt_ref[h], kvn, (((1,), (1,)), ((), ())), preferred_element_type=F32)
        v_ref[0, h] = vt.astype(BF16)
    if with_q:
        qn = _rms(lat_ref[:, KV_RANK:KV_RANK + Q_RANK].astype(F32), gq_ref[...]).astype(BF16)
        for h in range(N_HEADS):
            qh = jnp.dot(qn, wq_ref[:, h * HEAD_W:(h + 1) * HEAD_W], preferred_element_type=F32)
            q_ref[0, h, :, :QK_NOPE] = (qh[:, :QK_NOPE] * Q_SCALE).astype(BF16)
            q_ref[0, h, :, QK_NOPE:] = (rope(qh[:, QK_NOPE:]) * Q_SCALE).astype(BF16)


def _qkv(p, b, n, cos_t, sin_t, gq, gkv, wq, wk, wvt, with_q):
    tm = min(256, n)
    tpb = n // tm
    rope_tiles = cos_t.shape[0] // tm
    head_spec = pl.BlockSpec((1, N_HEADS, tm, HEAD_W), lambda i: (i // tpb, 0, i % tpb, 0))
    tab_spec = pl.BlockSpec((tm, LANES), lambda i: ((i % tpb) % rope_tiles, 0))
    out_shape = [jax.ShapeDtypeStruct((b, N_HEADS, n, HEAD_W), BF16),
                 jax.ShapeDtypeStruct((b, N_HEADS, V_DIM, n), BF16)]
    out_specs = [head_spec,
                 pl.BlockSpec((1, N_HEADS, V_DIM, tm), lambda i: (i // tpb, 0, 0, i % tpb))]
    if with_q:
        out_shape = [jax.ShapeDtypeStruct((b, N_HEADS, n, HEAD_W), BF16)] + out_shape
        out_specs = [head_spec] + out_specs
    return pl.pallas_call(
        functools.partial(_qkv_kernel, with_q=with_q),
        grid=(b * tpb,),
        in_specs=[pl.BlockSpec((tm, LAT_W), lambda i: (i, 0)),
                  tab_spec, tab_spec,
                  pl.BlockSpec((1, Q_RANK), lambda i: (0, 0)),
                  pl.BlockSpec((1, KV_RANK), lambda i: (0, 0)),
                  pl.BlockSpec(wq.shape, lambda i: (0, 0)),
                  pl.BlockSpec(wk.shape, lambda i: (0, 0)),
                  pl.BlockSpec(wvt.shape, lambda i: (0, 0, 0))],
        out_specs=out_specs,
        out_shape=out_shape,
        compiler_params=_params(1),
        name="qkv",
    )(p, cos_t, sin_t, gq.reshape(1, Q_RANK), gkv.reshape(1, KV_RANK), wq, wk, wvt)


def _row_fold(x, op):
    parts = [x[r:r + SUBLANES] for r in range(0, x.shape[0], SUBLANES)]
    a, b = parts[0], parts[1]
    for i in range(2, len(parts) - 1, 2):
        a, b = op(a, parts[i]), op(b, parts[i + 1])
    if len(parts) % 2:
        a = op(a, parts[-1])
    return op(a, b)


def _attn_kernel(q_ref, *refs, n_seg, kb, tq):
    k_refs, vt_refs = refs[:n_seg], refs[n_seg:2 * n_seg]
    o_ref, s_a, s_b, m_a, m_b = refs[2 * n_seg:]
    chunks = []
    off = 0
    for kr, vr in zip(k_refs, vt_refs):
        nk = kr.shape[2]
        for c0 in range(0, nk, kb):
            n = min(kb, nk - c0)
            chunks.append((kr, vr, c0, n, off))
            off += n

    def scores(t, s_buf, m_buf):
        q = q_ref[0, 0, pl.ds(pl.multiple_of(t * tq, tq), tq), :]
        mp = None
        for kr, vr, c0, n, o in chunks:
            s = lax.dot_general(kr[0, 0, c0:c0 + n, :], q, (((1,), (1,)), ((), ())),
                                preferred_element_type=F32)
            s_buf[o:o + n, :] = s
            m = _row_fold(s, jnp.maximum)
            mp = m if mp is None else jnp.maximum(mp, m)
        m_buf[...] = mp

    def values(t, s_buf, m_buf):
        mrow = jnp.max(m_buf[...], axis=0, keepdims=True)
        lp = None
        acc = None
        for kr, vr, c0, n, o in chunks:
            pr = jnp.exp2(s_buf[o:o + n, :] - mrow)
            ls = _row_fold(pr, jnp.add)
            lp = ls if lp is None else lp + ls
            pv = jnp.dot(vr[0, 0, :, c0:c0 + n], pr.astype(BF16), preferred_element_type=F32)
            acc = pv if acc is None else acc + pv
        l = jnp.sum(lp, axis=0, keepdims=True)
        o_ref[0, pl.ds(pl.multiple_of(t * tq, tq), tq), :] = (acc / l).T.astype(BF16)

    nt = q_ref.shape[2] // tq
    scores(0, s_a, m_a)
    if nt > 1:
        def pair(j, carry):
            scores(2 * j + 1, s_b, m_b)
            values(2 * j, s_a, m_a)
            scores(2 * j + 2, s_a, m_a)
            values(2 * j + 1, s_b, m_b)
            return carry

        lax.fori_loop(0, nt // 2 - 1, pair, 0)
        scores(nt - 1, s_b, m_b)
        values(nt - 2, s_a, m_a)
        values(nt - 1, s_b, m_b)
    else:
        values(0, s_a, m_a)


def _attention(q, ks, vts):
    b, hh, n, _ = q.shape
    tq = min(ATTN_TQ, n)
    nt = n // tq
    assert nt == 1 or nt % 2 == 0
    nk_total = sum(k.shape[2] for k in ks)
    seg_spec = lambda a: pl.BlockSpec((1, 1) + a.shape[2:], lambda bi, h: (bi, h, 0, 0))
    return pl.pallas_call(
        functools.partial(_attn_kernel, n_seg=len(ks), kb=ATTN_KB, tq=tq),
        grid=(b, hh),
        in_specs=[pl.BlockSpec((1, 1, n, HEAD_W), lambda bi, h: (bi, h, 0, 0))]
                 + [seg_spec(a) for a in ks] + [seg_spec(a) for a in vts],
        out_specs=pl.BlockSpec((1, n, V_DIM), lambda bi, h: (bi, 0, h)),
        out_shape=jax.ShapeDtypeStruct((b, n, hh * V_DIM), BF16),
        scratch_shapes=[pltpu.VMEM((nk_total, tq), F32), pltpu.VMEM((nk_total, tq), F32),
                        pltpu.VMEM((SUBLANES, tq), F32), pltpu.VMEM((SUBLANES, tq), F32)],
        compiler_params=_params(2),
        name="attention",
    )(q, *ks, *vts)


def _top2sum(a, b, c, d):
    s1, t1 = jnp.maximum(a, b), jnp.minimum(a, b)
    s2, t2 = jnp.maximum(c, d), jnp.minimum(c, d)
    return jnp.maximum(s1, s2) + jnp.maximum(jnp.minimum(s1, s2), jnp.maximum(t1, t2))


def _route(logits_t, bias):
    sc = _sigmoid(logits_t)
    sel = sc + bias
    sel_r = [sel[e:e + 1, :] for e in range(N_EXPERTS)]
    sc_r = [sc[e:e + 1, :] for e in range(N_EXPERTS)]
    epg = EXPERTS_PER_GROUP
    gs = [_top2sum(*sel_r[g * epg:(g + 1) * epg]) for g in range(N_GROUPS)]
    best, gi = gs[0], jnp.zeros(gs[0].shape, jnp.int32)
    for g in range(1, N_GROUPS):
        upd = gs[g] > best
        best = jnp.where(upd, gs[g], best)
        gi = jnp.where(upd, g, gi)

    def pick_group(rows, k):
        r = rows[k]
        for g in range(1, N_GROUPS):
            r = jnp.where(gi == g, rows[g * epg + k], r)
        return r

    in_sel = [pick_group(sel_r, k) for k in range(epg)]
    in_sc = [pick_group(sc_r, k) for k in range(epg)]
    b1, i1 = in_sel[0], jnp.zeros(gi.shape, jnp.int32)
    for k in range(1, epg):
        upd = in_sel[k] > b1
        b1 = jnp.where(upd, in_sel[k], b1)
        i1 = jnp.where(upd, k, i1)
    b2, i2 = None, None
    for k in range(epg):
        cand = jnp.where(i1 == k, -jnp.inf, in_sel[k])
        if b2 is None:
            b2, i2 = cand, jnp.zeros(gi.shape, jnp.int32)
        else:
            upd = cand > b2
            b2 = jnp.where(upd, cand, b2)
            i2 = jnp.where(upd, k, i2)

    def pick_local(idx):
        r = in_sc[0]
        for k in range(1, epg):
            r = jnp.where(idx == k, in_sc[k], r)
        return r

    s1, s2 = pick_local(i1), pick_local(i2)
    den = s1 + s2
    return gi * epg + i1, gi * epg + i2, s1 / den, s2 / den


def _mixout_kernel(y_ref, u_ref, up_ref, un_ref, gm_ref, gp_ref, wp_ref, ps_ref, wo_ref, x_ref,
                   g1_ref, gf_ref, sh2_ref, sc2_ref, wr_ref, rb_ref,
                   xo_ref, h2_ref, route_ref, *, tm, tpb, n_seq, packed):
    i = pl.program_id(0)
    base = (i % tpb) * tm
    uc = u_ref[...]
    uext = jnp.concatenate([up_ref[...], uc, un_ref[...]], axis=0)
    r = lax.broadcasted_iota(jnp.int32, (tm, tm + 2 * POOL_HALO), 0)
    c = lax.broadcasted_iota(jnp.int32, (tm, tm + 2 * POOL_HALO), 1)
    rel = c - POOL_HALO - r
    jpos = base - POOL_HALO + c
    valid = (jpos >= 0) & (jpos < n_seq)
    tpos = base + lax.broadcasted_iota(jnp.int32, (tm, 1), 0)
    parts = []
    for g, w in enumerate(POOL_WINDOWS):
        sl = slice(g * POOL_GROUP_DIM, (g + 1) * POOL_GROUP_DIM)
        band = jnp.where(valid & (rel >= -(w // 2)) & (rel < w // 2), 1.0, 0.0).astype(BF16)
        cnt = jnp.clip(tpos + w // 2, 0, n_seq) - jnp.clip(tpos - w // 2, 0, n_seq)
        wsum = jnp.dot(band, uext[:, sl], preferred_element_type=F32)
        z = wsum / cnt.astype(F32) - uc[:, sl].astype(F32)
        parts.append(jnp.dot(z.astype(BF16), wp_ref[g], preferred_element_type=F32))
    ypool = jnp.concatenate(parts, axis=1) * ps_ref[...]
    mix = gm_ref[...].astype(F32) * y_ref[...].astype(F32) + gp_ref[...].astype(F32) * ypool
    y = jnp.dot(mix.astype(BF16), wo_ref[...], preferred_element_type=F32)
    xn = x_ref[...] + g1_ref[0] * y
    xo_ref[...] = xn
    h2f = _rms(xn, gf_ref[...]) * (1.0 + sc2_ref[0]) + sh2_ref[0]
    h2 = h2f.astype(BF16)
    if packed:
        _store_chunks(h2_ref, _pack_rows(h2f))
    else:
        h2_ref[...] = h2
    logits_t = lax.dot_general(wr_ref[...], h2, (((1,), (1,)), ((), ())), preferred_element_type=F32)
    e1, e2, w1, w2 = _route(logits_t, rb_ref[...])
    zero = jnp.zeros((ROUTE_ROWS - 4, tm), F32)
    route_ref[...] = jnp.concatenate([e1.astype(F32), e2.astype(F32), w1, w2, zero], axis=0)


def _mixer_out(y, p, x2, mod3, brow, n_seq, wp, ps, wo, gf, wr_t, rb, packed):
    t, d = x2.shape
    tm = min(256, n_seq)
    tpb = n_seq // tm
    hpt = tm // POOL_HALO
    n_halo = t // POOL_HALO
    row = lambda i: (i, 0)
    const2 = lambda i: (0, 0)
    mrow = lambda i: brow(i * tm)
    return pl.pallas_call(
        functools.partial(_mixout_kernel, tm=tm, tpb=tpb, n_seq=n_seq, packed=packed),
        grid=(t // tm,),
        in_specs=[pl.BlockSpec((tm, d), row),
                  pl.BlockSpec((tm, POOL_DIM), lambda i: (i, LAT_W // POOL_DIM)),
                  pl.BlockSpec((POOL_HALO, POOL_DIM),
                               lambda i: (jnp.maximum(i * hpt - 1, 0), LAT_W // POOL_DIM)),
                  pl.BlockSpec((POOL_HALO, POOL_DIM),
                               lambda i: (jnp.minimum((i + 1) * hpt, n_halo - 1), LAT_W // POOL_DIM)),
                  pl.BlockSpec((tm, d), lambda i: (i, (LAT_W + POOL_DIM) // d)),
                  pl.BlockSpec((tm, d), lambda i: (i, (LAT_W + POOL_DIM) // d + 1)),
                  pl.BlockSpec(wp.shape, lambda i: (0, 0, 0)),
                  pl.BlockSpec((1, d), const2),
                  pl.BlockSpec((d, d), const2),
                  pl.BlockSpec((tm, d), row),
                  _mod_spec(2, mrow),
                  pl.BlockSpec((1, d), const2),
                  _mod_spec(3, mrow),
                  _mod_spec(4, mrow),
                  pl.BlockSpec((N_EXPERTS, d), const2),
                  pl.BlockSpec((N_EXPERTS, 1), const2)],
        out_specs=[pl.BlockSpec((tm, d), row),
                   (pl.BlockSpec((tm, ROW_CHUNKS, LANES), lambda i: (i, 0, 0)) if packed
                    else pl.BlockSpec((tm, d), row)),
                   pl.BlockSpec((ROUTE_ROWS, tm), lambda i: (0, i))],
        out_shape=[jax.ShapeDtypeStruct((t, d), F32),
                   (jax.ShapeDtypeStruct((t, ROW_CHUNKS, LANES), jnp.uint32) if packed
                    else jax.ShapeDtypeStruct((t, d), BF16)),
                   jax.ShapeDtypeStruct((ROUTE_ROWS, t), F32)],
        compiler_params=_params(1),
        name="mixer_out",
    )(y, p, p, p, p, p, wp, ps.reshape(1, d), wo, x2, mod3, gf.reshape(1, d), mod3, mod3, wr_t,
      rb.reshape(N_EXPERTS, 1))


def _moe_kernel(h_ref, comb_ref, wg_ref, wu_ref, wd_ref, x_ref, g2_ref, gfin_ref, o_ref, acc_scr,
                *, final):
    e = pl.program_id(1)

    @pl.when(e == 0)
    def _():
        acc_scr[...] = jnp.zeros_like(acc_scr)

    h = h_ref[...]
    comb = comb_ref[...]
    lane = lax.broadcasted_iota(jnp.int32, comb.shape, 1)
    col = jnp.sum(jnp.where(lane == e, comb, 0.0), axis=1, keepdims=True)
    gate = jnp.dot(h, wg_ref[0], preferred_element_type=F32)
    up = jnp.dot(h, wu_ref[0], preferred_element_type=F32)
    a = (gate * _sigmoid(gate)) * up * col
    acc_scr[...] += jnp.dot(a.astype(BF16), wd_ref[0], preferred_element_type=F32)

    @pl.when(e == pl.num_programs(1) - 1)
    def _():
        xn = x_ref[...] + g2_ref[0] * acc_scr[...]
        if final:
            xn = _rms(xn, gfin_ref[...])
        o_ref[...] = xn


def _moe(h2, comb, wg, wu, wd, x2, mod3, brow, gfin, final):
    t, d = x2.shape
    tm = min(512, t)
    ne = wg.shape[0]
    row = lambda i, e: (i, 0)
    return pl.pallas_call(
        functools.partial(_moe_kernel, final=final),
        grid=(t // tm, ne),
        in_specs=[pl.BlockSpec((tm, d), row),
                  pl.BlockSpec((tm, LANES), row),
                  pl.BlockSpec((1, d, D_EXPERT), lambda i, e: (e, 0, 0)),
                  pl.BlockSpec((1, d, D_EXPERT), lambda i, e: (e, 0, 0)),
                  pl.BlockSpec((1, D_EXPERT, d), lambda i, e: (e, 0, 0)),
                  pl.BlockSpec((tm, d), row),
                  _mod_spec(5, lambda i: brow(i * tm)),
                  pl.BlockSpec((1, d), lambda i, e: (0, 0))],
        out_specs=pl.BlockSpec((tm, d), row),
        out_shape=jax.ShapeDtypeStruct((t, d), F32),
        scratch_shapes=[pltpu.VMEM((tm, d), F32)],
        compiler_params=_params(2),
        name="moe",
    )(h2, comb, wg, wu, wd, x2, mod3, gfin.reshape(1, d))


def _sc_index_chunks(idx):
    chunk = jnp.arange(ROW_CHUNKS, dtype=jnp.int32)[None, :]
    return (idx[:, None] * ROW_CHUNKS + chunk).reshape(1, idx.shape[0] * ROW_CHUNKS)


def _sc_mesh():
    return plsc.VectorSubcoreMesh(core_axis_name="core", subcore_axis_name="subcore")


def _sc_gather_rows(table, idx):
    n, m = table.shape[0], idx.shape[0]
    n_idx = m * ROW_CHUNKS
    assert n_idx % SC_WINDOW == 0

    @pl.kernel(out_type=jax.ShapeDtypeStruct((n_idx, LANES), jnp.uint32), mesh=_sc_mesh())
    def gather(t_hbm, i_hbm, o_hbm):
        def body(i_vmem, o_vmem):
            pltpu.sync_copy(t_hbm.at[i_vmem.at[0]], o_vmem)

        pltpu.emit_pipeline(
            body,
            grid=(n_idx // SC_WINDOW,),
            in_specs=[pl.BlockSpec((1, SC_WINDOW), index_map=lambda i: (0, i))],
            out_specs=[pl.BlockSpec((SC_WINDOW, LANES), index_map=lambda i: (i, 0))],
            core_axis_name=("core", "subcore"),
            dimension_semantics=(pltpu.PARALLEL,),
        )(i_hbm, o_hbm)

    return gather(table.reshape(n * ROW_CHUNKS, LANES), _sc_index_chunks(idx)).reshape(m, ROW_CHUNKS, LANES)


def _sc_scatter_rows(rows, idx, n_out):
    n, m = rows.shape[0], idx.shape[0]
    n_idx = m * ROW_CHUNKS
    src_steps = n * ROW_CHUNKS // SC_WINDOW
    assert n_idx % SC_WINDOW == 0 and (n * ROW_CHUNKS) % SC_WINDOW == 0 and m % n == 0

    @pl.kernel(out_type=jax.ShapeDtypeStruct((n_out * ROW_CHUNKS, LANES), jnp.uint32), mesh=_sc_mesh())
    def scatter(x_hbm, i_hbm, o_hbm):
        def body(x_vmem, i_vmem):
            pltpu.sync_copy(x_vmem, o_hbm.at[i_vmem.at[0]])

        pltpu.emit_pipeline(
            body,
            grid=(n_idx // SC_WINDOW,),
            in_specs=[pl.BlockSpec((SC_WINDOW, LANES), index_map=lambda i: (i % src_steps, 0)),
                      pl.BlockSpec((1, SC_WINDOW), index_map=lambda i: (0, i))],
            out_specs=[],
            core_axis_name=("core", "subcore"),
            dimension_semantics=(pltpu.PARALLEL,),
        )(x_hbm, i_hbm)

    return scatter(rows.reshape(n * ROW_CHUNKS, LANES), _sc_index_chunks(idx)).reshape(n_out, ROW_CHUNKS, LANES)


def _dispatch_plan(route):
    t = route.shape[1]
    n_pairs = 2 * t
    n_tiles = n_pairs // MOE_TILE + N_EXPERTS
    blk = LANES
    e = jnp.concatenate([route[0], route[1]]).astype(jnp.int32)
    onehot = (e[:, None] == jnp.arange(N_EXPERTS, dtype=jnp.int32)[None, :]).astype(F32)
    oh3 = onehot.reshape(n_pairs // blk, blk, N_EXPERTS)
    tri = (jnp.arange(blk)[:, None] >= jnp.arange(blk)[None, :]).astype(F32)
    within = jnp.einsum("ij,bjk->bik", tri, oh3)
    totals = within[:, -1, :]
    before = jnp.cumsum(totals, axis=0) - totals
    csum = (within + before[:, None, :]).reshape(n_pairs, N_EXPERTS)
    counts = jnp.sum(totals, axis=0).astype(jnp.int32)
    padded = ((counts + MOE_TILE - 1) // MOE_TILE) * MOE_TILE
    ends = jnp.cumsum(padded)
    starts = ends - padded
    pair_slot = jnp.sum(onehot * (csum - 1.0 + starts.astype(F32)[None, :]), axis=1).astype(jnp.int32)
    tile_start = jnp.arange(n_tiles, dtype=jnp.int32) * MOE_TILE
    tile_e = jnp.minimum(jnp.sum((tile_start[:, None] >= ends[None, :]).astype(jnp.int32), axis=1), N_EXPERTS - 1)
    e_onehot = tile_e[:, None] == jnp.arange(N_EXPERTS, dtype=jnp.int32)[None, :]
    filled = jnp.sum(jnp.where(e_onehot, (starts + counts)[None, :], 0), axis=1)
    tile_rows = jnp.clip(filled - tile_start, 0, MOE_TILE).astype(jnp.int32)
    last_used = jnp.maximum(ends[-1] // MOE_TILE - 1, 0)
    tile_expert = jnp.where(tile_start < ends[-1], tile_e, tile_e[last_used]).astype(jnp.int32)
    return pair_slot, tile_expert, tile_rows


def _swiglu(xb, wg, wu, wd):
    gate = jnp.dot(xb, wg, preferred_element_type=F32)
    up = jnp.dot(xb, wu, preferred_element_type=F32)
    a = ((gate * _sigmoid(gate)) * up).astype(BF16)
    return jnp.dot(a, wd, preferred_element_type=F32)


def _ffn_kernel(te_ref, rows_ref, xs_ref, wg_ref, wu_ref, wd_ref, ys_ref, wg_s, wu_s, wd_s):
    i = pl.program_id(0)
    new_expert = jnp.logical_or(i == 0, te_ref[i] != te_ref[jnp.maximum(i - 1, 0)])

    @pl.when(new_expert)
    def _():
        wg_s[...] = wg_ref[0].astype(BF16)
        wu_s[...] = wu_ref[0].astype(BF16)
        wd_s[...] = wd_ref[0].astype(BF16)

    @pl.when(rows_ref[i] > 0)
    def _():
        live = lax.broadcasted_iota(jnp.int32, (MOE_TILE, 1), 0) < rows_ref[i]
        xb = jnp.where(live, _load_unpack(xs_ref), 0.0).astype(BF16)
        _store_chunks(ys_ref, _pack_rows(_swiglu(xb, wg_s[...], wu_s[...], wd_s[...])))

    @pl.when(rows_ref[i] == 0)
    def _():
        ys_ref[...] = jnp.zeros_like(ys_ref)


def _expert_ffn(xs, tile_expert, tile_rows, wg, wu, wd):
    n_slots = xs.shape[0]
    d = D_MODEL
    blk = pl.BlockSpec((MOE_TILE, ROW_CHUNKS, LANES), lambda i, te, nr: (i, 0, 0))
    return pl.pallas_call(
        _ffn_kernel,
        grid_spec=pltpu.PrefetchScalarGridSpec(
            num_scalar_prefetch=2,
            grid=(n_slots // MOE_TILE,),
            in_specs=[blk,
                      pl.BlockSpec((1, d, D_EXPERT), lambda i, te, nr: (te[i], 0, 0)),
                      pl.BlockSpec((1, d, D_EXPERT), lambda i, te, nr: (te[i], 0, 0)),
                      pl.BlockSpec((1, D_EXPERT, d), lambda i, te, nr: (te[i], 0, 0))],
            out_specs=blk,
            scratch_shapes=[pltpu.VMEM((d, D_EXPERT), BF16), pltpu.VMEM((d, D_EXPERT), BF16),
                            pltpu.VMEM((D_EXPERT, d), BF16)]),
        out_shape=jax.ShapeDtypeStruct(xs.shape, jnp.uint32),
        compiler_params=_params(1),
        name="expert_ffn",
    )(tile_expert, tile_rows, xs, wg, wu, wd)


def _shared_kernel(h_ref, sg_ref, su_ref, sd_ref, o_ref):
    hb = _load_unpack(h_ref).astype(BF16)
    o_ref[...] = _swiglu(hb, sg_ref[...], su_ref[...], sd_ref[...]).astype(BF16)


def _shared_ffn(h2p, sg, su, sd):
    t = h2p.shape[0]
    tm = min(512, t)
    const2 = lambda i: (0, 0)
    return pl.pallas_call(
        _shared_kernel,
        grid=(t // tm,),
        in_specs=[pl.BlockSpec((tm, ROW_CHUNKS, LANES), lambda i: (i, 0, 0)),
                  pl.BlockSpec(sg.shape, const2), pl.BlockSpec(su.shape, const2), pl.BlockSpec(sd.shape, const2)],
        out_specs=pl.BlockSpec((tm, D_MODEL), lambda i: (i, 0)),
        out_shape=jax.ShapeDtypeStruct((t, D_MODEL), BF16),
        compiler_params=_params(1),
        name="shared_ffn",
    )(h2p, sg, su, sd)


def _combine_kernel(ysh_ref, y1_ref, y2_ref, w_ref, x_ref, g2_ref, gfin_ref, o_ref, *, final):
    w = w_ref[...]
    y = ysh_ref[...].astype(F32) + w[:, 0:1] * _load_unpack(y1_ref) + w[:, 1:2] * _load_unpack(y2_ref)
    xn = x_ref[...] + g2_ref[0] * y
    if final:
        xn = _rms(xn, gfin_ref[...])
    o_ref[...] = xn


def _moe_combine(ysh, yg, w12, x2, mod3, brow, gfin, final):
    t, d = x2.shape
    tm = min(512, t)
    nt = t // tm
    row = lambda i: (i, 0)
    pk = lambda off: pl.BlockSpec((tm, ROW_CHUNKS, LANES), lambda i: (i + off, 0, 0))
    return pl.pallas_call(
        functools.partial(_combine_kernel, final=final),
        grid=(nt,),
        in_specs=[pl.BlockSpec((tm, d), row), pk(0), pk(nt),
                  pl.BlockSpec((tm, LANES), row),
                  pl.BlockSpec((tm, d), row),
                  _mod_spec(5, lambda i: brow(i * tm)),
                  pl.BlockSpec((1, d), lambda i: (0, 0))],
        out_specs=pl.BlockSpec((tm, d), row),
        out_shape=jax.ShapeDtypeStruct((t, d), F32),
        compiler_params=_params(1),
        name="moe_combine",
    )(ysh, yg, yg, w12, x2, mod3, gfin.reshape(1, d))


def _moe_routed(h2p, route, wg, wu, wd, sg, su, sd, x2, mod3, brow, gfin, final):
    t = x2.shape[0]
    pair_slot, tile_expert, tile_rows = _dispatch_plan(route)
    n_slots = tile_expert.shape[0] * MOE_TILE
    xs = _sc_scatter_rows(h2p, pair_slot, n_slots)
    ysh = _shared_ffn(h2p, sg, su, sd)
    ys = _expert_ffn(xs, tile_expert, tile_rows, wg, wu, wd)
    yg = _sc_gather_rows(ys, pair_slot)
    w12 = jnp.concatenate([route[2:4].T, jnp.zeros((t, LANES - 2), F32)], axis=1)
    return _moe_combine(ysh, yg, w12, x2, mod3, brow, gfin, final)


def _rope_tables(seq):
    rows = seq // GRID_W
    row = jnp.repeat(jnp.arange(rows), GRID_W).astype(F32)
    col = jnp.tile(jnp.arange(GRID_W), rows).astype(F32)
    axis_dim = QK_ROPE // 2
    inv = ROPE_THETA ** (-jnp.arange(0, axis_dim, 2, dtype=F32) / axis_dim)
    ang = jnp.concatenate([row[:, None] * inv, col[:, None] * inv], axis=-1)
    cos, sin = jnp.cos(ang), jnp.sin(ang)
    zero = jnp.zeros_like(cos)
    return (jnp.concatenate([cos, cos, zero, zero], axis=1),
            jnp.concatenate([-sin, sin, zero, zero], axis=1))


def _rope_group(w, start):
    half = QK_ROPE // 2
    x1, x2 = w[:, start:start + half], w[:, start + half:start + 2 * half]
    return jnp.concatenate([x1, x2, x2, x1], axis=1)


def _relayout_w_in(w):
    return jnp.concatenate([w[:, :KV_RANK], w[:, KV_COLS:Q_END], _rope_group(w, KV_RANK),
                            w[:, Q_END:]], axis=1).astype(BF16)


def _relayout_w_qb(w):
    dq = QK_NOPE + QK_ROPE
    cols = []
    for h in range(N_HEADS):
        cols += [w[:, h * dq:h * dq + QK_NOPE], _rope_group(w, h * dq + QK_NOPE)]
    return jnp.concatenate(cols, axis=1).astype(BF16)


def kernel(x, c, ctx, c_ctx, w_ada, b_ada, norm_mix_g, norm_ffn_g, w_in, q_norm_g, kv_norm_g, w_qb, w_kvb, w_pool, pool_scale, w_out, w_router, router_bias, w_exp_gate, w_exp_up, w_exp_down, w_sh_gate, w_sh_up, w_sh_down, final_norm_g):
    bsz, seq, d = x.shape
    n_ctx = ctx.shape[1]
    depth = w_ada.shape[0]
    assert bsz < MOD_ROWS and d == D_MODEL

    c_rows = jnp.concatenate([c, c_ctx[None], jnp.zeros((MOD_ROWS - bsz - 1, d), F32)], axis=0)
    mod = _ada_mod(c_rows, w_ada, b_ada)

    cos_t, sin_t = _rope_tables(seq)
    ones_t = jnp.concatenate([jnp.ones((n_ctx, 2 * (QK_ROPE // 2)), F32),
                              jnp.zeros((n_ctx, LANES - QK_ROPE), F32)], axis=1)
    zeros_t = jnp.zeros((n_ctx, LANES), F32)
    wr_t = w_router.T.astype(BF16)
    lat_row = lambda r: r // seq
    ctx_row = lambda r: bsz

    x2 = x.reshape(bsz * seq, d)
    xc2 = ctx.reshape(bsz * n_ctx, d)
    for l in range(depth):
        last = l == depth - 1
        mod3 = mod[l].reshape(MOD_ROWS * N_MOD, 1, d)
        w_in_r = _relayout_w_in(w_in[l])
        wq = _relayout_w_qb(w_qb[l])
        w_kv3 = w_kvb[l].reshape(KV_RANK, N_HEADS, QK_NOPE + V_DIM)
        wk = w_kv3[:, :, :QK_NOPE].reshape(KV_RANK, N_HEADS * QK_NOPE).astype(BF16)
        wvt = w_kv3[:, :, QK_NOPE:].transpose(1, 2, 0).astype(BF16)
        wp = w_pool[l].astype(BF16)
        wo = w_out[l].astype(BF16)
        wg, wu, wd = w_exp_gate[l], w_exp_up[l], w_exp_down[l]
        sg, su, sd = w_sh_gate[l].astype(BF16), w_sh_up[l].astype(BF16), w_sh_down[l].astype(BF16)

        def comb_cols(route):
            t = route.shape[1]
            eid = jnp.arange(N_EXPERTS, dtype=F32)[None, :]
            comb = (jnp.where(route[0][:, None] == eid, route[2][:, None], 0.0)
                    + jnp.where(route[1][:, None] == eid, route[3][:, None], 0.0))
            return jnp.concatenate([comb, jnp.ones((t, 1), F32),
                                    jnp.zeros((t, LANES - N_EXPERTS - 1), F32)], axis=1)

        p = _in_proj(x2, mod3, lat_row, norm_mix_g[l], w_in_r, P_COLS)
        pc = _in_proj(xc2, mod3, ctx_row, norm_mix_g[l], w_in_r, LAT_W if last else P_COLS)
        q, k_lat, v_lat = _qkv(p, bsz, seq, cos_t, sin_t, q_norm_g[l], kv_norm_g[l], wq, wk, wvt, True)
        ctx_out = _qkv(pc, bsz, n_ctx, ones_t, zeros_t, q_norm_g[l], kv_norm_g[l], wq, wk, wvt, not last)
        k_ctx, v_ctx = ctx_out[-2:]
        y = _attention(q, [k_ctx, k_lat], [v_ctx, v_lat]).reshape(bsz * seq, d)
        x2, h2p, route = _mixer_out(y, p, x2, mod3, lat_row, seq, wp, pool_scale[l], wo,
                                    norm_ffn_g[l], wr_t, router_bias, True)
        x2 = _moe_routed(h2p, route, wg, wu, wd, sg, su, sd, x2, mod3, lat_row, final_norm_g, last)
        if not last:
            yc = _attention(ctx_out[0], [k_ctx], [v_ctx]).reshape(bsz * n_ctx, d)
            xc2, h2c, route_c = _mixer_out(yc, pc, xc2, mod3, ctx_row, n_ctx, wp, pool_scale[l], wo,
                                           norm_ffn_g[l], wr_t, router_bias, False)
            stack = lambda w, s: jnp.concatenate([w.astype(BF16), s[None]], axis=0)
            xc2 = _moe(h2c, comb_cols(route_c), stack(wg, sg), stack(wu, su), stack(wd, sd),
                       xc2, mod3, ctx_row, final_norm_g, False)
    return x2.reshape(bsz, seq, d)
```

```python
import functools
import math

import jax
import jax.numpy as jnp
from jax import lax
from jax.experimental import pallas as pl
from jax.experimental.pallas import tpu as pltpu
from jax.experimental.pallas import tpu_sc as plsc

F32 = jnp.float32
BF16 = jnp.bfloat16

D_MODEL = 2048
GRID_W = 64
N_HEADS = 16
QK_NOPE = 128
QK_ROPE = 64
V_DIM = 128
Q_RANK = 384
KV_RANK = 512
ROPE_THETA = 10000.0
POOL_DIM = 1024
POOL_GROUPS = 4
POOL_GROUP_DIM = 256
POOL_WINDOWS = (2, 4, 8, 16)
POOL_GROUP_OUT = 512
KV_COLS = KV_RANK + QK_ROPE
Q_END = KV_COLS + Q_RANK
POOL_END = Q_END + POOL_DIM
N_EXPERTS = 16
N_GROUPS = 4
EXPERTS_PER_GROUP = 4
D_EXPERT = 512
EPS = 1e-6
N_MOD = 6

LANES = 128
HEAD_W = 2 * LANES
LAT_W = 1024
P_COLS = LAT_W + POOL_DIM + 2 * D_MODEL
POOL_HALO = 16
MOD_ROWS = 8
Q_SCALE = (1.0 / math.sqrt(QK_NOPE + QK_ROPE)) * math.log2(math.e)
VMEM_LIMIT = 56 * 1024 * 1024
SUBLANES = 8
ROW_CHUNKS = D_MODEL // 2 // LANES
ROUTE_ROWS = SUBLANES
MOE_TILE = 256
SC_WINDOW = 128
ATTN_TQ = 256
ATTN_KB = 512


def _sigmoid(x):
    return 1.0 / (1.0 + jnp.exp(-x))


def _bf16_bits(x):
    u = pltpu.bitcast(x, jnp.uint32)
    return (u + jnp.uint32(0x7FFF) + ((u >> 16) & jnp.uint32(1))) >> 16


def _pack_rows(y):
    half = y.shape[1] // 2
    return (_bf16_bits(y[:, half:]) << 16) | _bf16_bits(y[:, :half])


def _store_chunks(ref, w):
    for j in range(ROW_CHUNKS):
        ref[:, j, :] = w[:, j * LANES:(j + 1) * LANES]


def _load_unpack(ref):
    w = jnp.concatenate([ref[:, j, :] for j in range(ROW_CHUNKS)], axis=1)
    lo = pltpu.bitcast(w << 16, F32)
    hi = pltpu.bitcast(w & jnp.uint32(0xFFFF0000), F32)
    return jnp.concatenate([lo, hi], axis=1)


def _rms(xf, g):
    ms = jnp.mean(xf * xf, axis=-1, keepdims=True)
    return xf * lax.rsqrt(ms + EPS) * g


def _params(n_axes):
    return pltpu.CompilerParams(dimension_semantics=("arbitrary",) * n_axes,
                                vmem_limit_bytes=VMEM_LIMIT)


def _ada_kernel(c_ref, w_ref, b_ref, o_ref):
    cf = c_ref[...]
    a = (cf * _sigmoid(cf)).astype(BF16)
    o_ref[0] = jnp.dot(a, w_ref[0].astype(BF16), preferred_element_type=F32) + b_ref[0]


def _ada_mod(c_rows, w_ada, b_ada):
    depth, d, n = w_ada.shape
    tn = 1024
    return pl.pallas_call(
        _ada_kernel,
        grid=(depth, n // tn),
        in_specs=[pl.BlockSpec((MOD_ROWS, d), lambda l, j: (0, 0)),
                  pl.BlockSpec((1, d, tn), lambda l, j: (l, 0, j)),
                  pl.BlockSpec((1, 1, tn), lambda l, j: (l, 0, j))],
        out_specs=pl.BlockSpec((1, MOD_ROWS, tn), lambda l, j: (l, 0, j)),
        out_shape=jax.ShapeDtypeStruct((depth, MOD_ROWS, n), F32),
        compiler_params=_params(2),
        name="ada_mod",
    )(c_rows, w_ada, b_ada.reshape(depth, 1, n))


def _mod_spec(k, brow):
    return pl.BlockSpec((1, 1, D_MODEL), lambda i, *_: (brow(i) * N_MOD + k, 0, 0))


def _inproj_kernel(x_ref, g_ref, sh_ref, sc_ref, w_ref, o_ref, h_scr, *, tn):
    y = _rms(x_ref[...], g_ref[...])
    h_scr[...] = (y * (1.0 + sc_ref[0]) + sh_ref[0]).astype(BF16)
    for c0 in range(0, o_ref.shape[1], tn):
        acc = jnp.dot(h_scr[...], w_ref[:, c0:c0 + tn], preferred_element_type=F32)
        if c0 >= LAT_W + POOL_DIM:
            acc = _sigmoid(acc)
        o_ref[:, c0:c0 + tn] = acc.astype(BF16)


def _in_proj(x2, mod3, brow, g, w, n_cols):
    t, d = x2.shape
    tm = min(512, t)
    return pl.pallas_call(
        functools.partial(_inproj_kernel, tn=512),
        grid=(t // tm,),
        in_specs=[pl.BlockSpec((tm, d), lambda i: (i, 0)),
                  pl.BlockSpec((1, d), lambda i: (0, 0)),
                  _mod_spec(0, lambda i: brow(i * tm)),
                  _mod_spec(1, lambda i: brow(i * tm)),
                  pl.BlockSpec((d, n_cols), lambda i: (0, 0), pipeline_mode=pl.Buffered(1))],
        out_specs=pl.BlockSpec((tm, n_cols), lambda i: (i, 0)),
        out_shape=jax.ShapeDtypeStruct((t, n_cols), BF16),
        scratch_shapes=[pltpu.VMEM((tm, d), BF16)],
        compiler_params=_params(1),
        name="in_proj",
    )(x2, g.reshape(1, d), mod3, mod3, w)


def _qkv_kernel(lat_ref, cos_ref, sin_ref, gq_ref, gkv_ref, wq_ref, wk_ref, wvt_ref, *out_refs, with_q):
    if with_q:
        q_ref, k_ref, v_ref = out_refs
    else:
        k_ref, v_ref = out_refs
    cos = cos_ref[...]
    sin = sin_ref[...]

    def rope(grp):
        return grp * cos + pltpu.roll(grp, 2 * (QK_ROPE // 2), axis=1) * sin

    kvn = _rms(lat_ref[:, :KV_RANK].astype(F32), gkv_ref[...]).astype(BF16)
    kpe = rope(lat_ref[:, KV_RANK + Q_RANK:].astype(F32)).astype(BF16)
    for h in range(N_HEADS):
        kn = jnp.dot(kvn, wk_ref[:, h * QK_NOPE:(h + 1) * QK_NOPE], preferred_element_type=F32)
        k_ref[0, h, :, :QK_NOPE] = kn.astype(BF16)
        k_ref[0, h, :, QK_NOPE:] = kpe
        vt = lax.dot_general(wvt_ref[h], kvn, (((1,), (1,)), ((), ())), preferred_element_type=F32)
        v_ref[0, h] = vt.astype(BF16)
    if with_q:
        qn = _rms(lat_ref[:, KV_RANK:KV_RANK + Q_RANK].astype(F32), gq_ref[...]).astype(BF16)
        for h in range(N_HEADS):
            qh = jnp.dot(qn, wq_ref[:, h * HEAD_W:(h + 1) * HEAD_W], preferred_element_type=F32)
            q_ref[0, h, :, :QK_NOPE] = (qh[:, :QK_NOPE] * Q_SCALE).astype(BF16)
            q_ref[0, h, :, QK_NOPE:] = (rope(qh[:, QK_NOPE:]) * Q_SCALE).astype(BF16)


def _qkv(p, b, n, cos_t, sin_t, gq, gkv, wq, wk, wvt, with_q):
    tm = min(256, n)
    tpb = n // tm
    rope_tiles = cos_t.shape[0] // tm
    head_spec = pl.BlockSpec((1, N_HEADS, tm, HEAD_W), lambda i: (i // tpb, 0, i % tpb, 0))
    tab_spec = pl.BlockSpec((tm, LANES), lambda i: ((i % tpb) % rope_tiles, 0))
    out_shape = [jax.ShapeDtypeStruct((b, N_HEADS, n, HEAD_W), BF16),
                 jax.ShapeDtypeStruct((b, N_HEADS, V_DIM, n), BF16)]
    out_specs = [head_spec,
                 pl.BlockSpec((1, N_HEADS, V_DIM, tm), lambda i: (i // tpb, 0, 0, i % tpb))]
    if with_q:
        out_shape = [jax.ShapeDtypeStruct((b, N_HEADS, n, HEAD_W), BF16)] + out_shape
        out_specs = [head_spec] + out_specs
    return pl.pallas_call(
        functools.partial(_qkv_kernel, with_q=with_q),
        grid=(b * tpb,),
        in_specs=[pl.BlockSpec((tm, LAT_W), lambda i: (i, 0)),
                  tab_spec, tab_spec,
                  pl.BlockSpec((1, Q_RANK), lambda i: (0, 0)),
                  pl.BlockSpec((1, KV_RANK), lambda i: (0, 0)),
                  pl.BlockSpec(wq.shape, lambda i: (0, 0)),
                  pl.BlockSpec(wk.shape, lambda i: (0, 0)),
                  pl.BlockSpec(wvt.shape, lambda i: (0, 0, 0))],
        out_specs=out_specs,
        out_shape=out_shape,
        compiler_params=_params(1),
        name="qkv",
    )(p, cos_t, sin_t, gq.reshape(1, Q_RANK), gkv.reshape(1, KV_RANK), wq, wk, wvt)


def _row_fold(x, op):
    parts = [x[r:r + SUBLANES] for r in range(0, x.shape[0], SUBLANES)]
    a, b = parts[0], parts[1]
    for i in range(2, len(parts) - 1, 2):
        a, b = op(a, parts[i]), op(b, parts[i + 1])
    if len(parts) % 2:
        a = op(a, parts[-1])
    return op(a, b)


def _attn_kernel(q_ref, *refs, n_seg, kb, tq):
    k_refs, vt_refs = refs[:n_seg], refs[n_seg:2 * n_seg]
    o_ref, s_a, s_b, m_a, m_b = refs[2 * n_seg:]
    chunks = []
    off = 0
    for kr, vr in zip(k_refs, vt_refs):
        nk = kr.shape[2]
        for c0 in range(0, nk, kb):
            n = min(kb, nk - c0)
            chunks.append((kr, vr, c0, n, off))
            off += n

    def scores(t, s_buf, m_buf):
        q = q_ref[0, 0, pl.ds(pl.multiple_of(t * tq, tq), tq), :]
        mp = None
        for kr, vr, c0, n, o in chunks:
            s = lax.dot_general(kr[0, 0, c0:c0 + n, :], q, (((1,), (1,)), ((), ())),
                                preferred_element_type=F32)
            s_buf[o:o + n, :] = s
            m = _row_fold(s, jnp.maximum)
            mp = m if mp is None else jnp.maximum(mp, m)
        m_buf[...] = mp

    def values(t, s_buf, m_buf):
        mrow = jnp.max(m_buf[...], axis=0, keepdims=True)
        lp = None
        acc = None
        for kr, vr, c0, n, o in chunks:
            pr = jnp.exp2(s_buf[o:o + n, :] - mrow)
            ls = _row_fold(pr, jnp.add)
            lp = ls if lp is None else lp + ls
            pv = jnp.dot(vr[0, 0, :, c0:c0 + n], pr.astype(BF16), preferred_element_type=F32)
            acc = pv if acc is None else acc + pv
        l = jnp.sum(lp, axis=0, keepdims=True)
        o_ref[0, pl.ds(pl.multiple_of(t * tq, tq), tq), :] = (acc / l).T.astype(BF16)

    nt = q_ref.shape[2] // tq
    scores(0, s_a, m_a)
    if nt > 1:
        def pair(j, carry):
            scores(2 * j + 1, s_b, m_b)
            values(2 * j, s_a, m_a)
            scores(2 * j + 2, s_a, m_a)
            values(2 * j + 1, s_b, m_b)
            return carry

        lax.fori_loop(0, nt // 2 - 1, pair, 0)
        scores(nt - 1, s_b, m_b)
        values(nt - 2, s_a, m_a)
        values(nt - 1, s_b, m_b)
    else:
        values(0, s_a, m_a)


def _attention(q, ks, vts):
    b, hh, n, _ = q.shape
    tq = min(ATTN_TQ, n)
    nt = n // tq
    assert nt == 1 or nt % 2 == 0
    nk_total = sum(k.shape[2] for k in ks)
    seg_spec = lambda a: pl.BlockSpec((1, 1) + a.shape[2:], lambda bi, h: (bi, h, 0, 0))
    return pl.pallas_call(
        functools.partial(_attn_kernel, n_seg=len(ks), kb=ATTN_KB, tq=tq),
        grid=(b, hh),
        in_specs=[pl.BlockSpec((1, 1, n, HEAD_W), lambda bi, h: (bi, h, 0, 0))]
                 + [seg_spec(a) for a in ks] + [seg_spec(a) for a in vts],
        out_specs=pl.BlockSpec((1, n, V_DIM), lambda bi, h: (bi, 0, h)),
        out_shape=jax.ShapeDtypeStruct((b, n, hh * V_DIM), BF16),
        scratch_shapes=[pltpu.VMEM((nk_total, tq), F32), pltpu.VMEM((nk_total, tq), F32),
                        pltpu.VMEM((SUBLANES, tq), F32), pltpu.VMEM((SUBLANES, tq), F32)],
        compiler_params=_params(2),
        name="attention",
    )(q, *ks, *vts)


def _top2sum(a, b, c, d):
    s1, t1 = jnp.maximum(a, b), jnp.minimum(a, b)
    s2, t2 = jnp.maximum(c, d), jnp.minimum(c, d)
    return jnp.maximum(s1, s2) + jnp.maximum(jnp.minimum(s1, s2), jnp.maximum(t1, t2))


def _route(logits_t, bias):
    sc = _sigmoid(logits_t)
    sel = sc + bias
    sel_r = [sel[e:e + 1, :] for e in range(N_EXPERTS)]
    sc_r = [sc[e:e + 1, :] for e in range(N_EXPERTS)]
    epg = EXPERTS_PER_GROUP
    gs = [_top2sum(*sel_r[g * epg:(g + 1) * epg]) for g in range(N_GROUPS)]
    best, gi = gs[0], jnp.zeros(gs[0].shape, jnp.int32)
    for g in range(1, N_GROUPS):
        upd = gs[g] > best
        best = jnp.where(upd, gs[g], best)
        gi = jnp.where(upd, g, gi)

    def pick_group(rows, k):
        r = rows[k]
        for g in range(1, N_GROUPS):
            r = jnp.where(gi == g, rows[g * epg + k], r)
        return r

    in_sel = [pick_group(sel_r, k) for k in range(epg)]
    in_sc = [pick_group(sc_r, k) for k in range(epg)]
    b1, i1 = in_sel[0], jnp.zeros(gi.shape, jnp.int32)
    for k in range(1, epg):
        upd = in_sel[k] > b1
        b1 = jnp.where(upd, in_sel[k], b1)
        i1 = jnp.where(upd, k, i1)
    b2, i2 = None, None
    for k in range(epg):
        cand = jnp.where(i1 == k, -jnp.inf, in_sel[k])
        if b2 is None:
            b2, i2 = cand, jnp.zeros(gi.shape, jnp.int32)
        else:
            upd = cand > b2
            b2 = jnp.where(upd, cand, b2)
            i2 = jnp.where(upd, k, i2)

    def pick_local(idx):
        r = in_sc[0]
        for k in range(1, epg):
            r = jnp.where(idx == k, in_sc[k], r)
        return r

    s1, s2 = pick_local(i1), pick_local(i2)
    den = s1 + s2
    return gi * epg + i1, gi * epg + i2, s1 / den, s2 / den


def _mixout_kernel(y_ref, u_ref, up_ref, un_ref, gm_ref, gp_ref, wp_ref, ps_ref, wo_ref, x_ref,
                   g1_ref, gf_ref, sh2_ref, sc2_ref, wr_ref, rb_ref,
                   xo_ref, h2_ref, route_ref, *, tm, tpb, n_seq):
    i = pl.program_id(0)
    base = (i % tpb) * tm
    uc = u_ref[...]
    uext = jnp.concatenate([up_ref[...], uc, un_ref[...]], axis=0)
    r = lax.broadcasted_iota(jnp.int32, (tm, tm + 2 * POOL_HALO), 0)
    c = lax.broadcasted_iota(jnp.int32, (tm, tm + 2 * POOL_HALO), 1)
    rel = c - POOL_HALO - r
    jpos = base - POOL_HALO + c
    valid = (jpos >= 0) & (jpos < n_seq)
    tpos = base + lax.broadcasted_iota(jnp.int32, (tm, 1), 0)
    parts = []
    for g, w in enumerate(POOL_WINDOWS):
        sl = slice(g * POOL_GROUP_DIM, (g + 1) * POOL_GROUP_DIM)
        band = jnp.where(valid & (rel >= -(w // 2)) & (rel < w // 2), 1.0, 0.0).astype(BF16)
        cnt = jnp.clip(tpos + w // 2, 0, n_seq) - jnp.clip(tpos - w // 2, 0, n_seq)
        wsum = jnp.dot(band, uext[:, sl], preferred_element_type=F32)
        z = wsum / cnt.astype(F32) - uc[:, sl].astype(F32)
        parts.append(jnp.dot(z.astype(BF16), wp_ref[g], preferred_element_type=F32))
    ypool = jnp.concatenate(parts, axis=1) * ps_ref[...]
    mix = gm_ref[...].astype(F32) * y_ref[...].astype(F32) + gp_ref[...].astype(F32) * ypool
    y = jnp.dot(mix.astype(BF16), wo_ref[...], preferred_element_type=F32)
    xn = x_ref[...] + g1_ref[0] * y
    xo_ref[...] = xn
    h2f = _rms(xn, gf_ref[...]) * (1.0 + sc2_ref[0]) + sh2_ref[0]
    h2 = h2f.astype(BF16)
    _store_chunks(h2_ref, _pack_rows(h2f))
    logits_t = lax.dot_general(wr_ref[...], h2, (((1,), (1,)), ((), ())), preferred_element_type=F32)
    e1, e2, w1, w2 = _route(logits_t, rb_ref[...])
    zero = jnp.zeros((ROUTE_ROWS - 4, tm), F32)
    route_ref[...] = jnp.concatenate([e1.astype(F32), e2.astype(F32), w1, w2, zero], axis=0)


def _mixer_out(y, p, x2, mod3, brow, n_seq, wp, ps, wo, gf, wr_t, rb):
    t, d = x2.shape
    tm = min(256, n_seq)
    tpb = n_seq // tm
    hpt = tm // POOL_HALO
    n_halo = t // POOL_HALO
    row = lambda i: (i, 0)
    const2 = lambda i: (0, 0)
    mrow = lambda i: brow(i * tm)
    return pl.pallas_call(
        functools.partial(_mixout_kernel, tm=tm, tpb=tpb, n_seq=n_seq),
        grid=(t // tm,),
        in_specs=[pl.BlockSpec((tm, d), row),
                  pl.BlockSpec((tm, POOL_DIM), lambda i: (i, LAT_W // POOL_DIM)),
                  pl.BlockSpec((POOL_HALO, POOL_DIM),
                               lambda i: (jnp.maximum(i * hpt - 1, 0), LAT_W // POOL_DIM)),
                  pl.BlockSpec((POOL_HALO, POOL_DIM),
                               lambda i: (jnp.minimum((i + 1) * hpt, n_halo - 1), LAT_W // POOL_DIM)),
                  pl.BlockSpec((tm, d), lambda i: (i, (LAT_W + POOL_DIM) // d)),
                  pl.BlockSpec((tm, d), lambda i: (i, (LAT_W + POOL_DIM) // d + 1)),
                  pl.BlockSpec(wp.shape, lambda i: (0, 0, 0)),
                  pl.BlockSpec((1, d), const2),
                  pl.BlockSpec((d, d), const2),
                  pl.BlockSpec((tm, d), row),
                  _mod_spec(2, mrow),
                  pl.BlockSpec((1, d), const2),
                  _mod_spec(3, mrow),
                  _mod_spec(4, mrow),
                  pl.BlockSpec((N_EXPERTS, d), const2),
                  pl.BlockSpec((N_EXPERTS, 1), const2)],
        out_specs=[pl.BlockSpec((tm, d), row),
                   pl.BlockSpec((tm, ROW_CHUNKS, LANES), lambda i: (i, 0, 0)),
                   pl.BlockSpec((ROUTE_ROWS, tm), lambda i: (0, i))],
        out_shape=[jax.ShapeDtypeStruct((t, d), F32),
                   jax.ShapeDtypeStruct((t, ROW_CHUNKS, LANES), jnp.uint32),
                   jax.ShapeDtypeStruct((ROUTE_ROWS, t), F32)],
        compiler_params=_params(1),
        name="mixer_out",
    )(y, p, p, p, p, p, wp, ps.reshape(1, d), wo, x2, mod3, gf.reshape(1, d), mod3, mod3, wr_t,
      rb.reshape(N_EXPERTS, 1))


def _sc_index_chunks(idx):
    chunk = jnp.arange(ROW_CHUNKS, dtype=jnp.int32)[None, :]
    return (idx[:, None] * ROW_CHUNKS + chunk).reshape(1, idx.shape[0] * ROW_CHUNKS)


def _sc_mesh():
    return plsc.VectorSubcoreMesh(core_axis_name="core", subcore_axis_name="subcore")


def _sc_gather_rows(table, idx):
    n, m = table.shape[0], idx.shape[0]
    n_idx = m * ROW_CHUNKS
    assert n_idx % SC_WINDOW == 0

    @pl.kernel(out_type=jax.ShapeDtypeStruct((n_idx, LANES), jnp.uint32), mesh=_sc_mesh())
    def gather(t_hbm, i_hbm, o_hbm):
        def body(i_vmem, o_vmem):
            pltpu.sync_copy(t_hbm.at[i_vmem.at[0]], o_vmem)

        pltpu.emit_pipeline(
            body,
            grid=(n_idx // SC_WINDOW,),
            in_specs=[pl.BlockSpec((1, SC_WINDOW), index_map=lambda i: (0, i))],
            out_specs=[pl.BlockSpec((SC_WINDOW, LANES), index_map=lambda i: (i, 0))],
            core_axis_name=("core", "subcore"),
            dimension_semantics=(pltpu.PARALLEL,),
        )(i_hbm, o_hbm)

    return gather(table.reshape(n * ROW_CHUNKS, LANES), _sc_index_chunks(idx)).reshape(m, ROW_CHUNKS, LANES)


def _sc_scatter_rows(rows, idx, n_out):
    n, m = rows.shape[0], idx.shape[0]
    n_idx = m * ROW_CHUNKS
    src_steps = n * ROW_CHUNKS // SC_WINDOW
    assert n_idx % SC_WINDOW == 0 and (n * ROW_CHUNKS) % SC_WINDOW == 0 and m % n == 0

    @pl.kernel(out_type=jax.ShapeDtypeStruct((n_out * ROW_CHUNKS, LANES), jnp.uint32), mesh=_sc_mesh())
    def scatter(x_hbm, i_hbm, o_hbm):
        def body(x_vmem, i_vmem):
            pltpu.sync_copy(x_vmem, o_hbm.at[i_vmem.at[0]])

        pltpu.emit_pipeline(
            body,
            grid=(n_idx // SC_WINDOW,),
            in_specs=[pl.BlockSpec((SC_WINDOW, LANES), index_map=lambda i: (i % src_steps, 0)),
                      pl.BlockSpec((1, SC_WINDOW), index_map=lambda i: (0, i))],
            out_specs=[],
            core_axis_name=("core", "subcore"),
            dimension_semantics=(pltpu.PARALLEL,),
        )(x_hbm, i_hbm)

    return scatter(rows.reshape(n * ROW_CHUNKS, LANES), _sc_index_chunks(idx)).reshape(n_out, ROW_CHUNKS, LANES)


def _dispatch_plan(route):
    t = route.shape[1]
    n_pairs = 2 * t
    n_tiles = n_pairs // MOE_TILE + N_EXPERTS
    blk = LANES
    e = jnp.concatenate([route[0], route[1]]).astype(jnp.int32)
    onehot = (e[:, None] == jnp.arange(N_EXPERTS, dtype=jnp.int32)[None, :]).astype(F32)
    oh3 = onehot.reshape(n_pairs // blk, blk, N_EXPERTS)
    tri = (jnp.arange(blk)[:, None] >= jnp.arange(blk)[None, :]).astype(F32)
    within = jnp.einsum("ij,bjk->bik", tri, oh3)
    totals = within[:, -1, :]
    before = jnp.cumsum(totals, axis=0) - totals
    csum = (within + before[:, None, :]).reshape(n_pairs, N_EXPERTS)
    counts = jnp.sum(totals, axis=0).astype(jnp.int32)
    padded = ((counts + MOE_TILE - 1) // MOE_TILE) * MOE_TILE
    ends = jnp.cumsum(padded)
    starts = ends - padded
    pair_slot = jnp.sum(onehot * (csum - 1.0 + starts.astype(F32)[None, :]), axis=1).astype(jnp.int32)
    tile_start = jnp.arange(n_tiles, dtype=jnp.int32) * MOE_TILE
    tile_e = jnp.minimum(jnp.sum((tile_start[:, None] >= ends[None, :]).astype(jnp.int32), axis=1), N_EXPERTS - 1)
    e_onehot = tile_e[:, None] == jnp.arange(N_EXPERTS, dtype=jnp.int32)[None, :]
    filled = jnp.sum(jnp.where(e_onehot, (starts + counts)[None, :], 0), axis=1)
    tile_rows = jnp.clip(filled - tile_start, 0, MOE_TILE).astype(jnp.int32)
    last_used = jnp.maximum(ends[-1] // MOE_TILE - 1, 0)
    tile_expert = jnp.where(tile_start < ends[-1], tile_e, tile_e[last_used]).astype(jnp.int32)
    return pair_slot, tile_expert, tile_rows


def _swiglu(xb, wg, wu, wd):
    gate = jnp.dot(xb, wg, preferred_element_type=F32)
    up = jnp.dot(xb, wu, preferred_element_type=F32)
    a = ((gate * _sigmoid(gate)) * up).astype(BF16)
    return jnp.dot(a, wd, preferred_element_type=F32)


def _ffn_kernel(te_ref, rows_ref, xs_ref, wg_ref, wu_ref, wd_ref, ys_ref, wg_s, wu_s, wd_s):
    i = pl.program_id(0)
    new_expert = jnp.logical_or(i == 0, te_ref[i] != te_ref[jnp.maximum(i - 1, 0)])

    @pl.when(new_expert)
    def _():
        wg_s[...] = wg_ref[0].astype(BF16)
        wu_s[...] = wu_ref[0].astype(BF16)
        wd_s[...] = wd_ref[0].astype(BF16)

    @pl.when(rows_ref[i] > 0)
    def _():
        live = lax.broadcasted_iota(jnp.int32, (MOE_TILE, 1), 0) < rows_ref[i]
        xb = jnp.where(live, _load_unpack(xs_ref), 0.0).astype(BF16)
        _store_chunks(ys_ref, _pack_rows(_swiglu(xb, wg_s[...], wu_s[...], wd_s[...])))

    @pl.when(rows_ref[i] == 0)
    def _():
        ys_ref[...] = jnp.zeros_like(ys_ref)


def _expert_ffn(xs, tile_expert, tile_rows, wg, wu, wd, layer):
    n_slots = xs.shape[0]
    d = D_MODEL
    blk = pl.BlockSpec((MOE_TILE, ROW_CHUNKS, LANES), lambda i, te, nr: (i, 0, 0))
    return pl.pallas_call(
        _ffn_kernel,
        grid_spec=pltpu.PrefetchScalarGridSpec(
            num_scalar_prefetch=2,
            grid=(n_slots // MOE_TILE,),
            in_specs=[blk,
                      pl.BlockSpec((None, 1, d, D_EXPERT), lambda i, te, nr: (layer, te[i], 0, 0)),
                      pl.BlockSpec((None, 1, d, D_EXPERT), lambda i, te, nr: (layer, te[i], 0, 0)),
                      pl.BlockSpec((None, 1, D_EXPERT, d), lambda i, te, nr: (layer, te[i], 0, 0))],
            out_specs=blk,
            scratch_shapes=[pltpu.VMEM((d, D_EXPERT), BF16), pltpu.VMEM((d, D_EXPERT), BF16),
                            pltpu.VMEM((D_EXPERT, d), BF16)]),
        out_shape=jax.ShapeDtypeStruct(xs.shape, jnp.uint32),
        compiler_params=_params(1),
        name="expert_ffn",
    )(tile_expert, tile_rows, xs, wg, wu, wd)


def _shared_kernel(h_ref, sg_ref, su_ref, sd_ref, o_ref):
    hb = _load_unpack(h_ref).astype(BF16)
    o_ref[...] = _swiglu(hb, sg_ref[...], su_ref[...], sd_ref[...]).astype(BF16)


def _shared_ffn(h2p, sg, su, sd):
    t = h2p.shape[0]
    tm = min(512, t)
    const2 = lambda i: (0, 0)
    return pl.pallas_call(
        _shared_kernel,
        grid=(t // tm,),
        in_specs=[pl.BlockSpec((tm, ROW_CHUNKS, LANES), lambda i: (i, 0, 0)),
                  pl.BlockSpec(sg.shape, const2), pl.BlockSpec(su.shape, const2), pl.BlockSpec(sd.shape, const2)],
        out_specs=pl.BlockSpec((tm, D_MODEL), lambda i: (i, 0)),
        out_shape=jax.ShapeDtypeStruct((t, D_MODEL), BF16),
        compiler_params=_params(1),
        name="shared_ffn",
    )(h2p, sg, su, sd)


def _combine_kernel(ysh_ref, y1_ref, y2_ref, w_ref, x_ref, g2_ref, gfin_ref, o_ref, *, final):
    w = w_ref[...]
    y = ysh_ref[...].astype(F32) + w[:, 0:1] * _load_unpack(y1_ref) + w[:, 1:2] * _load_unpack(y2_ref)
    xn = x_ref[...] + g2_ref[0] * y
    if final:
        xn = _rms(xn, gfin_ref[...])
    o_ref[...] = xn


def _moe_combine(ysh, yg, w12, x2, mod3, brow, gfin, final):
    t, d = x2.shape
    tm = min(512, t)
    nt = t // tm
    row = lambda i: (i, 0)
    pk = lambda off: pl.BlockSpec((tm, ROW_CHUNKS, LANES), lambda i: (i + off, 0, 0))
    return pl.pallas_call(
        functools.partial(_combine_kernel, final=final),
        grid=(nt,),
        in_specs=[pl.BlockSpec((tm, d), row), pk(0), pk(nt),
                  pl.BlockSpec((tm, LANES), row),
                  pl.BlockSpec((tm, d), row),
                  _mod_spec(5, lambda i: brow(i * tm)),
                  pl.BlockSpec((1, d), lambda i: (0, 0))],
        out_specs=pl.BlockSpec((tm, d), row),
        out_shape=jax.ShapeDtypeStruct((t, d), F32),
        compiler_params=_params(1),
        name="moe_combine",
    )(ysh, yg, yg, w12, x2, mod3, gfin.reshape(1, d))


def _moe_routed(h2p, route, wg, wu, wd, layer, sg, su, sd, x2, mod3, brow, gfin, final):
    t = x2.shape[0]
    pair_slot, tile_expert, tile_rows = _dispatch_plan(route)
    n_slots = tile_expert.shape[0] * MOE_TILE
    xs = _sc_scatter_rows(h2p, pair_slot, n_slots)
    ysh = _shared_ffn(h2p, sg, su, sd)
    ys = _expert_ffn(xs, tile_expert, tile_rows, wg, wu, wd, layer)
    yg = _sc_gather_rows(ys, pair_slot)
    w12 = jnp.concatenate([route[2:4].T, jnp.zeros((t, LANES - 2), F32)], axis=1)
    return _moe_combine(ysh, yg, w12, x2, mod3, brow, gfin, final)


def _rope_tables(seq):
    rows = seq // GRID_W
    row = jnp.repeat(jnp.arange(rows), GRID_W).astype(F32)
    col = jnp.tile(jnp.arange(GRID_W), rows).astype(F32)
    axis_dim = QK_ROPE // 2
    inv = ROPE_THETA ** (-jnp.arange(0, axis_dim, 2, dtype=F32) / axis_dim)
    ang = jnp.concatenate([row[:, None] * inv, col[:, None] * inv], axis=-1)
    cos, sin = jnp.cos(ang), jnp.sin(ang)
    zero = jnp.zeros_like(cos)
    return (jnp.concatenate([cos, cos, zero, zero], axis=1),
            jnp.concatenate([-sin, sin, zero, zero], axis=1))


def _rope_group(w, start):
    half = QK_ROPE // 2
    x1, x2 = w[:, start:start + half], w[:, start + half:start + 2 * half]
    return jnp.concatenate([x1, x2, x2, x1], axis=1)


def _relayout_w_in(w):
    return jnp.concatenate([w[:, :KV_RANK], w[:, KV_COLS:Q_END], _rope_group(w, KV_RANK),
                            w[:, Q_END:]], axis=1).astype(BF16)


def _relayout_w_qb(w):
    dq = QK_NOPE + QK_ROPE
    cols = []
    for h in range(N_HEADS):
        cols += [w[:, h * dq:h * dq + QK_NOPE], _rope_group(w, h * dq + QK_NOPE)]
    return jnp.concatenate(cols, axis=1).astype(BF16)


def kernel(x, c, ctx, c_ctx, w_ada, b_ada, norm_mix_g, norm_ffn_g, w_in, q_norm_g, kv_norm_g, w_qb, w_kvb, w_pool, pool_scale, w_out, w_router, router_bias, w_exp_gate, w_exp_up, w_exp_down, w_sh_gate, w_sh_up, w_sh_down, final_norm_g):
    bsz, seq, d = x.shape
    n_ctx = ctx.shape[1]
    depth = w_ada.shape[0]
    assert bsz < MOD_ROWS and d == D_MODEL

    c_rows = jnp.concatenate([c, c_ctx[None], jnp.zeros((MOD_ROWS - bsz - 1, d), F32)], axis=0)
    mod = _ada_mod(c_rows, w_ada, b_ada)

    cos_t, sin_t = _rope_tables(seq)
    ones_t = jnp.concatenate([jnp.ones((n_ctx, 2 * (QK_ROPE // 2)), F32),
                              jnp.zeros((n_ctx, LANES - QK_ROPE), F32)], axis=1)
    zeros_t = jnp.zeros((n_ctx, LANES), F32)
    wr_t = w_router.T.astype(BF16)
    lat_row = lambda r: r // seq
    ctx_row = lambda r: bsz

    x2 = x.reshape(bsz * seq, d)
    xc2 = ctx.reshape(bsz * n_ctx, d)
    for l in range(depth):
        last = l == depth - 1
        mod3 = mod[l].reshape(MOD_ROWS * N_MOD, 1, d)
        w_in_r = _relayout_w_in(w_in[l])
        wq = _relayout_w_qb(w_qb[l])
        w_kv3 = w_kvb[l].reshape(KV_RANK, N_HEADS, QK_NOPE + V_DIM)
        wk = w_kv3[:, :, :QK_NOPE].reshape(KV_RANK, N_HEADS * QK_NOPE).astype(BF16)
        wvt = w_kv3[:, :, QK_NOPE:].transpose(1, 2, 0).astype(BF16)
        wp = w_pool[l].astype(BF16)
        wo = w_out[l].astype(BF16)
        sg, su, sd = w_sh_gate[l].astype(BF16), w_sh_up[l].astype(BF16), w_sh_down[l].astype(BF16)
        moe = functools.partial(_moe_routed, wg=w_exp_gate, wu=w_exp_up, wd=w_exp_down, layer=l,
                                sg=sg, su=su, sd=sd, mod3=mod3, gfin=final_norm_g)

        p = _in_proj(x2, mod3, lat_row, norm_mix_g[l], w_in_r, P_COLS)
        pc = _in_proj(xc2, mod3, ctx_row, norm_mix_g[l], w_in_r, LAT_W if last else P_COLS)
        q, k_lat, v_lat = _qkv(p, bsz, seq, cos_t, sin_t, q_norm_g[l], kv_norm_g[l], wq, wk, wvt, True)
        ctx_out = _qkv(pc, bsz, n_ctx, ones_t, zeros_t, q_norm_g[l], kv_norm_g[l], wq, wk, wvt, not last)
        k_ctx, v_ctx = ctx_out[-2:]
        y = _attention(q, [k_ctx, k_lat], [v_ctx, v_lat]).reshape(bsz * seq, d)
        x2, h2p, route = _mixer_out(y, p, x2, mod3, lat_row, seq, wp, pool_scale[l], wo,
                                    norm_ffn_g[l], wr_t, router_bias)
        x2 = moe(h2p, route, x2=x2, brow=lat_row, final=last)
        if not last:
            yc = _attention(ctx_out[0], [k_ctx], [v_ctx]).reshape(bsz * n_ctx, d)
            xc2, h2c, route_c = _mixer_out(yc, pc, xc2, mod3, ctx_row, n_ctx, wp, pool_scale[l], wo,
                                           norm_ffn_g[l], wr_t, router_bias)
            xc2 = moe(h2c, route_c, x2=xc2, brow=ctx_row, final=False)
    return x2.reshape(bsz, seq, d)
```

```python
import functools
import math

import jax
import jax.numpy as jnp
from jax import lax
from jax.experimental import pallas as pl
from jax.experimental.pallas import tpu as pltpu
from jax.experimental.pallas import tpu_sc as plsc

F32 = jnp.float32
BF16 = jnp.bfloat16

D_MODEL = 2048
GRID_W = 64
N_HEADS = 16
QK_NOPE = 128
QK_ROPE = 64
V_DIM = 128
Q_RANK = 384
KV_RANK = 512
ROPE_THETA = 10000.0
POOL_DIM = 1024
POOL_GROUPS = 4
POOL_GROUP_DIM = 256
POOL_WINDOWS = (2, 4, 8, 16)
POOL_GROUP_OUT = 512
KV_COLS = KV_RANK + QK_ROPE
Q_END = KV_COLS + Q_RANK
POOL_END = Q_END + POOL_DIM
N_EXPERTS = 16
N_GROUPS = 4
EXPERTS_PER_GROUP = 4
D_EXPERT = 512
EPS = 1e-6
N_MOD = 6

LANES = 128
HEAD_W = 2 * LANES
LAT_W = 1024
P_COLS = LAT_W + POOL_DIM + 2 * D_MODEL
POOL_HALO = 16
MOD_ROWS = 8
Q_SCALE = (1.0 / math.sqrt(QK_NOPE + QK_ROPE)) * math.log2(math.e)
VMEM_LIMIT = 56 * 1024 * 1024
SUBLANES = 8
PACK_W = D_MODEL // 2
ROW_CHUNKS = PACK_W // LANES
ROUTE_ROWS = SUBLANES
MOE_TILE = 256
SC_WINDOW = 128
ATTN_TQ = 256
ATTN_KB = 512


def _sigmoid(x):
    return 1.0 / (1.0 + jnp.exp(-x))


def _pack_rows(y):
    half = y.shape[1] // 2
    return pltpu.pack_elementwise([y[:, :half], y[:, half:]], packed_dtype=BF16)


def _load_unpack(ref, live=None):
    w = ref[...]
    if live is not None:
        w = jnp.where(live, w, 0)
    lo = pltpu.unpack_elementwise(w, index=0, packed_dtype=BF16, unpacked_dtype=F32)
    hi = pltpu.unpack_elementwise(w, index=1, packed_dtype=BF16, unpacked_dtype=F32)
    return jnp.concatenate([lo, hi], axis=1)


def _rms(xf, g):
    ms = jnp.mean(xf * xf, axis=-1, keepdims=True)
    return xf * lax.rsqrt(ms + EPS) * g


def _params(n_axes):
    return pltpu.CompilerParams(dimension_semantics=("arbitrary",) * n_axes,
                                vmem_limit_bytes=VMEM_LIMIT)


def _ada_kernel(c_ref, w_ref, b_ref, o_ref):
    cf = c_ref[...]
    a = (cf * _sigmoid(cf)).astype(BF16)
    o_ref[0] = jnp.dot(a, w_ref[0].astype(BF16), preferred_element_type=F32) + b_ref[0]


def _ada_mod(c_rows, w_ada, b_ada):
    depth, d, n = w_ada.shape
    tn = 1024
    return pl.pallas_call(
        _ada_kernel,
        grid=(depth, n // tn),
        in_specs=[pl.BlockSpec((MOD_ROWS, d), lambda l, j: (0, 0)),
                  pl.BlockSpec((1, d, tn), lambda l, j: (l, 0, j)),
                  pl.BlockSpec((1, 1, tn), lambda l, j: (l, 0, j))],
        out_specs=pl.BlockSpec((1, MOD_ROWS, tn), lambda l, j: (l, 0, j)),
        out_shape=jax.ShapeDtypeStruct((depth, MOD_ROWS, n), F32),
        compiler_params=_params(2),
        name="ada_mod",
    )(c_rows, w_ada, b_ada.reshape(depth, 1, n))


def _mod_spec(k, brow):
    return pl.BlockSpec((1, 1, D_MODEL), lambda i, *_: (brow(i) * N_MOD + k, 0, 0))


def _inproj_kernel(x_ref, g_ref, sh_ref, sc_ref, w_ref, o_ref, h_scr, *, tn):
    y = _rms(x_ref[...], g_ref[...])
    h_scr[...] = (y * (1.0 + sc_ref[0]) + sh_ref[0]).astype(BF16)
    for c0 in range(0, o_ref.shape[1], tn):
        acc = jnp.dot(h_scr[...], w_ref[:, c0:c0 + tn], preferred_element_type=F32)
        if c0 >= LAT_W + POOL_DIM:
            acc = _sigmoid(acc)
        o_ref[:, c0:c0 + tn] = acc.astype(BF16)


def _in_proj(x2, mod3, brow, g, w, n_cols):
    t, d = x2.shape
    tm = min(512, t)
    return pl.pallas_call(
        functools.partial(_inproj_kernel, tn=512),
        grid=(t // tm,),
        in_specs=[pl.BlockSpec((tm, d), lambda i: (i, 0)),
                  pl.BlockSpec((1, d), lambda i: (0, 0)),
                  _mod_spec(0, lambda i: brow(i * tm)),
                  _mod_spec(1, lambda i: brow(i * tm)),
                  pl.BlockSpec((d, n_cols), lambda i: (0, 0), pipeline_mode=pl.Buffered(1))],
        out_specs=pl.BlockSpec((tm, n_cols), lambda i: (i, 0)),
        out_shape=jax.ShapeDtypeStruct((t, n_cols), BF16),
        scratch_shapes=[pltpu.VMEM((tm, d), BF16)],
        compiler_params=_params(1),
        name="in_proj",
    )(x2, g.reshape(1, d), mod3, mod3, w)


def _qkv_kernel(lat_ref, cos_ref, sin_ref, gq_ref, gkv_ref, wq_ref, wk_ref, wvt_ref, *out_refs, with_q):
    if with_q:
        q_ref, k_ref, v_ref = out_refs
    else:
        k_ref, v_ref = out_refs
    cos = cos_ref[...]
    sin = sin_ref[...]

    def rope(grp):
        return grp * cos + pltpu.roll(grp, 2 * (QK_ROPE // 2), axis=1) * sin

    kvn = _rms(lat_ref[:, :KV_RANK].astype(F32), gkv_ref[...]).astype(BF16)
    kpe = rope(lat_ref[:, KV_RANK + Q_RANK:].astype(F32)).astype(BF16)
    for h in range(N_HEADS):
        kn = jnp.dot(kvn, wk_ref[:, h * QK_NOPE:(h + 1) * QK_NOPE], preferred_element_type=F32)
        k_ref[0, h, :, :QK_NOPE] = kn.astype(BF16)
        k_ref[0, h, :, QK_NOPE:] = kpe
        vt = lax.dot_general(wvt_ref[h], kvn, (((1,), (1,)), ((), ())), preferred_element_type=F32)
        v_ref[0, h] = vt.astype(BF16)
    if with_q:
        qn = _rms(lat_ref[:, KV_RANK:KV_RANK + Q_RANK].astype(F32), gq_ref[...]).astype(BF16)
        for h in range(N_HEADS):
            qh = jnp.dot(qn, wq_ref[:, h * HEAD_W:(h + 1) * HEAD_W], preferred_element_type=F32)
            q_ref[0, h, :, :QK_NOPE] = (qh[:, :QK_NOPE] * Q_SCALE).astype(BF16)
            q_ref[0, h, :, QK_NOPE:] = (rope(qh[:, QK_NOPE:]) * Q_SCALE).astype(BF16)


def _qkv(p, b, n, cos_t, sin_t, gq, gkv, wq, wk, wvt, with_q):
    tm = min(256, n)
    tpb = n // tm
    rope_tiles = cos_t.shape[0] // tm
    head_spec = pl.BlockSpec((1, N_HEADS, tm, HEAD_W), lambda i: (i // tpb, 0, i % tpb, 0))
    tab_spec = pl.BlockSpec((tm, LANES), lambda i: ((i % tpb) % rope_tiles, 0))
    out_shape = [jax.ShapeDtypeStruct((b, N_HEADS, n, HEAD_W), BF16),
                 jax.ShapeDtypeStruct((b, N_HEADS, V_DIM, n), BF16)]
    out_specs = [head_spec,
                 pl.BlockSpec((1, N_HEADS, V_DIM, tm), lambda i: (i // tpb, 0, 0, i % tpb))]
    if with_q:
        out_shape = [jax.ShapeDtypeStruct((b, N_HEADS, n, HEAD_W), BF16)] + out_shape
        out_specs = [head_spec] + out_specs
    return pl.pallas_call(
        functools.partial(_qkv_kernel, with_q=with_q),
        grid=(b * tpb,),
        in_specs=[pl.BlockSpec((tm, LAT_W), lambda i: (i, 0)),
                  tab_spec, tab_spec,
                  pl.BlockSpec((1, Q_RANK), lambda i: (0, 0)),
                  pl.BlockSpec((1, KV_RANK), lambda i: (0, 0)),
                  pl.BlockSpec(wq.shape, lambda i: (0, 0)),
                  pl.BlockSpec(wk.shape, lambda i: (0, 0)),
                  pl.BlockSpec(wvt.shape, lambda i: (0, 0, 0))],
        out_specs=out_specs,
        out_shape=out_shape,
        compiler_params=_params(1),
        name="qkv",
    )(p, cos_t, sin_t, gq.reshape(1, Q_RANK), gkv.reshape(1, KV_RANK), wq, wk, wvt)


def _row_fold(x, op):
    parts = [x[r:r + SUBLANES] for r in range(0, x.shape[0], SUBLANES)]
    a, b = parts[0], parts[1]
    for i in range(2, len(parts) - 1, 2):
        a, b = op(a, parts[i]), op(b, parts[i + 1])
    if len(parts) % 2:
        a = op(a, parts[-1])
    return op(a, b)


def _attn_kernel(q_ref, *refs, n_seg, kb, tq):
    k_refs, vt_refs = refs[:n_seg], refs[n_seg:2 * n_seg]
    o_ref, s_a, s_b, m_a, m_b = refs[2 * n_seg:]
    chunks = []
    off = 0
    for kr, vr in zip(k_refs, vt_refs):
        nk = kr.shape[2]
        for c0 in range(0, nk, kb):
            n = min(kb, nk - c0)
            chunks.append((kr, vr, c0, n, off))
            off += n

    def scores(t, s_buf, m_buf):
        q = q_ref[0, 0, pl.ds(pl.multiple_of(t * tq, tq), tq), :]
        mp = None
        for kr, vr, c0, n, o in chunks:
            s = lax.dot_general(kr[0, 0, c0:c0 + n, :], q, (((1,), (1,)), ((), ())),
                                preferred_element_type=F32)
            s_buf[o:o + n, :] = s
            m = _row_fold(s, jnp.maximum)
            mp = m if mp is None else jnp.maximum(mp, m)
        m_buf[...] = mp

    def values(t, s_buf, m_buf):
        mrow = jnp.max(m_buf[...], axis=0, keepdims=True)
        lp = None
        acc = None
        for kr, vr, c0, n, o in chunks:
            pr = jnp.exp2(s_buf[o:o + n, :] - mrow)
            ls = _row_fold(pr, jnp.add)
            lp = ls if lp is None else lp + ls
            pv = jnp.dot(vr[0, 0, :, c0:c0 + n], pr.astype(BF16), preferred_element_type=F32)
            acc = pv if acc is None else acc + pv
        l = jnp.sum(lp, axis=0, keepdims=True)
        o_ref[0, pl.ds(pl.multiple_of(t * tq, tq), tq), :] = (acc / l).T.astype(BF16)

    nt = q_ref.shape[2] // tq
    scores(0, s_a, m_a)
    if nt > 1:
        def pair(j, carry):
            scores(2 * j + 1, s_b, m_b)
            values(2 * j, s_a, m_a)
            scores(2 * j + 2, s_a, m_a)
            values(2 * j + 1, s_b, m_b)
            return carry

        lax.fori_loop(0, nt // 2 - 1, pair, 0)
        scores(nt - 1, s_b, m_b)
        values(nt - 2, s_a, m_a)
        values(nt - 1, s_b, m_b)
    else:
        values(0, s_a, m_a)


def _attention(q, ks, vts):
    b, hh, n, _ = q.shape
    tq = min(ATTN_TQ, n)
    nt = n // tq
    assert nt == 1 or nt % 2 == 0
    nk_total = sum(k.shape[2] for k in ks)
    seg_spec = lambda a: pl.BlockSpec((1, 1) + a.shape[2:], lambda bi, h: (bi, h, 0, 0))
    return pl.pallas_call(
        functools.partial(_attn_kernel, n_seg=len(ks), kb=ATTN_KB, tq=tq),
        grid=(b, hh),
        in_specs=[pl.BlockSpec((1, 1, n, HEAD_W), lambda bi, h: (bi, h, 0, 0))]
                 + [seg_spec(a) for a in ks] + [seg_spec(a) for a in vts],
        out_specs=pl.BlockSpec((1, n, V_DIM), lambda bi, h: (bi, 0, h)),
        out_shape=jax.ShapeDtypeStruct((b, n, hh * V_DIM), BF16),
        scratch_shapes=[pltpu.VMEM((nk_total, tq), F32), pltpu.VMEM((nk_total, tq), F32),
                        pltpu.VMEM((SUBLANES, tq), F32), pltpu.VMEM((SUBLANES, tq), F32)],
        compiler_params=_params(2),
        name="attention",
    )(q, *ks, *vts)


def _top2sum(a, b, c, d):
    s1, t1 = jnp.maximum(a, b), jnp.minimum(a, b)
    s2, t2 = jnp.maximum(c, d), jnp.minimum(c, d)
    return jnp.maximum(s1, s2) + jnp.maximum(jnp.minimum(s1, s2), jnp.maximum(t1, t2))


def _route(logits_t, bias):
    sc = _sigmoid(logits_t)
    sel = sc + bias
    sel_r = [sel[e:e + 1, :] for e in range(N_EXPERTS)]
    sc_r = [sc[e:e + 1, :] for e in range(N_EXPERTS)]
    epg = EXPERTS_PER_GROUP
    gs = [_top2sum(*sel_r[g * epg:(g + 1) * epg]) for g in range(N_GROUPS)]
    best, gi = gs[0], jnp.zeros(gs[0].shape, jnp.int32)
    for g in range(1, N_GROUPS):
        upd = gs[g] > best
        best = jnp.where(upd, gs[g], best)
        gi = jnp.where(upd, g, gi)

    def pick_group(rows, k):
        r = rows[k]
        for g in range(1, N_GROUPS):
            r = jnp.where(gi == g, rows[g * epg + k], r)
        return r

    in_sel = [pick_group(sel_r, k) for k in range(epg)]
    in_sc = [pick_group(sc_r, k) for k in range(epg)]
    b1, i1 = in_sel[0], jnp.zeros(gi.shape, jnp.int32)
    for k in range(1, epg):
        upd = in_sel[k] > b1
        b1 = jnp.where(upd, in_sel[k], b1)
        i1 = jnp.where(upd, k, i1)
    b2, i2 = None, None
    for k in range(epg):
        cand = jnp.where(i1 == k, -jnp.inf, in_sel[k])
        if b2 is None:
            b2, i2 = cand, jnp.zeros(gi.shape, jnp.int32)
        else:
            upd = cand > b2
            b2 = jnp.where(upd, cand, b2)
            i2 = jnp.where(upd, k, i2)

    def pick_local(idx):
        r = in_sc[0]
        for k in range(1, epg):
            r = jnp.where(idx == k, in_sc[k], r)
        return r

    s1, s2 = pick_local(i1), pick_local(i2)
    den = s1 + s2
    return gi * epg + i1, gi * epg + i2, s1 / den, s2 / den


def _mixout_kernel(y_ref, u_ref, up_ref, un_ref, gm_ref, gp_ref, wp_ref, ps_ref, wo_ref, x_ref,
                   g1_ref, gf_ref, sh2_ref, sc2_ref, wr_ref, rb_ref,
                   xo_ref, h2_ref, route_ref, *, tm, tpb, n_seq):
    i = pl.program_id(0)
    base = (i % tpb) * tm
    uc = u_ref[...]
    uext = jnp.concatenate([up_ref[...], uc, un_ref[...]], axis=0)
    r = lax.broadcasted_iota(jnp.int32, (tm, tm + 2 * POOL_HALO), 0)
    c = lax.broadcasted_iota(jnp.int32, (tm, tm + 2 * POOL_HALO), 1)
    rel = c - POOL_HALO - r
    jpos = base - POOL_HALO + c
    valid = (jpos >= 0) & (jpos < n_seq)
    tpos = base + lax.broadcasted_iota(jnp.int32, (tm, 1), 0)
    parts = []
    for g, w in enumerate(POOL_WINDOWS):
        sl = slice(g * POOL_GROUP_DIM, (g + 1) * POOL_GROUP_DIM)
        band = jnp.where(valid & (rel >= -(w // 2)) & (rel < w // 2), 1.0, 0.0).astype(BF16)
        cnt = jnp.clip(tpos + w // 2, 0, n_seq) - jnp.clip(tpos - w // 2, 0, n_seq)
        wsum = jnp.dot(band, uext[:, sl], preferred_element_type=F32)
        z = wsum / cnt.astype(F32) - uc[:, sl].astype(F32)
        parts.append(jnp.dot(z.astype(BF16), wp_ref[g], preferred_element_type=F32))
    ypool = jnp.concatenate(parts, axis=1) * ps_ref[...]
    mix = gm_ref[...].astype(F32) * y_ref[...].astype(F32) + gp_ref[...].astype(F32) * ypool
    y = jnp.dot(mix.astype(BF16), wo_ref[...], preferred_element_type=F32)
    xn = x_ref[...] + g1_ref[0] * y
    xo_ref[...] = xn
    h2f = _rms(xn, gf_ref[...]) * (1.0 + sc2_ref[0]) + sh2_ref[0]
    h2 = h2f.astype(BF16)
    h2_ref[...] = _pack_rows(h2f)
    logits_t = lax.dot_general(wr_ref[...], h2, (((1,), (1,)), ((), ())), preferred_element_type=F32)
    e1, e2, w1, w2 = _route(logits_t, rb_ref[...])
    zero = jnp.zeros((ROUTE_ROWS - 4, tm), F32)
    route_ref[...] = jnp.concatenate([e1.astype(F32), e2.astype(F32), w1, w2, zero], axis=0)


def _mixer_out(y, p, x2, mod3, brow, n_seq, wp, ps, wo, gf, wr_t, rb):
    t, d = x2.shape
    tm = min(256, n_seq)
    tpb = n_seq // tm
    hpt = tm // POOL_HALO
    n_halo = t // POOL_HALO
    row = lambda i: (i, 0)
    const2 = lambda i: (0, 0)
    mrow = lambda i: brow(i * tm)
    return pl.pallas_call(
        functools.partial(_mixout_kernel, tm=tm, tpb=tpb, n_seq=n_seq),
        grid=(t // tm,),
        in_specs=[pl.BlockSpec((tm, d), row),
                  pl.BlockSpec((tm, POOL_DIM), lambda i: (i, LAT_W // POOL_DIM)),
                  pl.BlockSpec((POOL_HALO, POOL_DIM),
                               lambda i: (jnp.maximum(i * hpt - 1, 0), LAT_W // POOL_DIM)),
                  pl.BlockSpec((POOL_HALO, POOL_DIM),
                               lambda i: (jnp.minimum((i + 1) * hpt, n_halo - 1), LAT_W // POOL_DIM)),
                  pl.BlockSpec((tm, d), lambda i: (i, (LAT_W + POOL_DIM) // d)),
                  pl.BlockSpec((tm, d), lambda i: (i, (LAT_W + POOL_DIM) // d + 1)),
                  pl.BlockSpec(wp.shape, lambda i: (0, 0, 0)),
                  pl.BlockSpec((1, d), const2),
                  pl.BlockSpec((d, d), const2),
                  pl.BlockSpec((tm, d), row),
                  _mod_spec(2, mrow),
                  pl.BlockSpec((1, d), const2),
                  _mod_spec(3, mrow),
                  _mod_spec(4, mrow),
                  pl.BlockSpec((N_EXPERTS, d), const2),
                  pl.BlockSpec((N_EXPERTS, 1), const2)],
        out_specs=[pl.BlockSpec((tm, d), row),
                   pl.BlockSpec((tm, PACK_W), row),
                   pl.BlockSpec((ROUTE_ROWS, tm), lambda i: (0, i))],
        out_shape=[jax.ShapeDtypeStruct((t, d), F32),
                   jax.ShapeDtypeStruct((t, PACK_W), jnp.int32),
                   jax.ShapeDtypeStruct((ROUTE_ROWS, t), F32)],
        compiler_params=_params(1),
        name="mixer_out",
    )(y, p, p, p, p, p, wp, ps.reshape(1, d), wo, x2, mod3, gf.reshape(1, d), mod3, mod3, wr_t,
      rb.reshape(N_EXPERTS, 1))


def _chunk_view(x):
    n = x.shape[0]
    return (x.reshape(n // SUBLANES, SUBLANES, ROW_CHUNKS, LANES).transpose(0, 2, 1, 3)
            .reshape(n * ROW_CHUNKS, LANES))


def _row_view(c):
    n = c.shape[0] // ROW_CHUNKS
    return (c.reshape(n // SUBLANES, ROW_CHUNKS, SUBLANES, LANES).transpose(0, 2, 1, 3)
            .reshape(n, PACK_W))


def _sc_index_chunks(idx):
    r = idx.reshape(-1, 1, SUBLANES)
    j = jnp.arange(ROW_CHUNKS, dtype=jnp.int32)[None, :, None]
    ids = (r // SUBLANES) * (SUBLANES * ROW_CHUNKS) + j * SUBLANES + r % SUBLANES
    return ids.reshape(1, idx.shape[0] * ROW_CHUNKS)


def _sc_mesh():
    return plsc.VectorSubcoreMesh(core_axis_name="core", subcore_axis_name="subcore")


def _sc_gather_rows(table, idx):
    n, m = table.shape[0], idx.shape[0]
    n_idx = m * ROW_CHUNKS
    assert n_idx % SC_WINDOW == 0 and n % SUBLANES == 0 and m % SUBLANES == 0

    @pl.kernel(out_type=jax.ShapeDtypeStruct((n_idx, LANES), jnp.int32), mesh=_sc_mesh())
    def gather(t_hbm, i_hbm, o_hbm):
        def body(i_vmem, o_vmem):
            pltpu.sync_copy(t_hbm.at[i_vmem.at[0]], o_vmem)

        pltpu.emit_pipeline(
            body,
            grid=(n_idx // SC_WINDOW,),
            in_specs=[pl.BlockSpec((1, SC_WINDOW), index_map=lambda i: (0, i))],
            out_specs=[pl.BlockSpec((SC_WINDOW, LANES), index_map=lambda i: (i, 0))],
            core_axis_name=("core", "subcore"),
            dimension_semantics=(pltpu.PARALLEL,),
        )(i_hbm, o_hbm)

    return _row_view(gather(_chunk_view(table), _sc_index_chunks(idx)))


def _sc_scatter_rows(rows, idx, n_out):
    n, m = rows.shape[0], idx.shape[0]
    n_idx = m * ROW_CHUNKS
    src_steps = n * ROW_CHUNKS // SC_WINDOW
    assert n_idx % SC_WINDOW == 0 and (n * ROW_CHUNKS) % SC_WINDOW == 0 and m % n == 0
    assert n % SUBLANES == 0 and n_out % SUBLANES == 0

    @pl.kernel(out_type=jax.ShapeDtypeStruct((n_out * ROW_CHUNKS, LANES), jnp.int32), mesh=_sc_mesh())
    def scatter(x_hbm, i_hbm, o_hbm):
        def body(x_vmem, i_vmem):
            pltpu.sync_copy(x_vmem, o_hbm.at[i_vmem.at[0]])

        pltpu.emit_pipeline(
            body,
            grid=(n_idx // SC_WINDOW,),
            in_specs=[pl.BlockSpec((SC_WINDOW, LANES), index_map=lambda i: (i % src_steps, 0)),
                      pl.BlockSpec((1, SC_WINDOW), index_map=lambda i: (0, i))],
            out_specs=[],
            core_axis_name=("core", "subcore"),
            dimension_semantics=(pltpu.PARALLEL,),
        )(x_hbm, i_hbm)

    return _row_view(scatter(_chunk_view(rows), _sc_index_chunks(idx)))


def _dispatch_plan(route):
    t = route.shape[1]
    n_pairs = 2 * t
    n_tiles = n_pairs // MOE_TILE + N_EXPERTS
    blk = LANES
    e = jnp.concatenate([route[0], route[1]]).astype(jnp.int32)
    onehot = (e[:, None] == jnp.arange(N_EXPERTS, dtype=jnp.int32)[None, :]).astype(F32)
    oh3 = onehot.reshape(n_pairs // blk, blk, N_EXPERTS)
    tri = (jnp.arange(blk)[:, None] >= jnp.arange(blk)[None, :]).astype(F32)
    within = jnp.einsum("ij,bjk->bik", tri, oh3)
    totals = within[:, -1, :]
    before = jnp.cumsum(totals, axis=0) - totals
    csum = (within + before[:, None, :]).reshape(n_pairs, N_EXPERTS)
    counts = jnp.sum(totals, axis=0).astype(jnp.int32)
    padded = ((counts + MOE_TILE - 1) // MOE_TILE) * MOE_TILE
    ends = jnp.cumsum(padded)
    starts = ends - padded
    pair_slot = jnp.sum(onehot * (csum - 1.0 + starts.astype(F32)[None, :]), axis=1).astype(jnp.int32)
    tile_start = jnp.arange(n_tiles, dtype=jnp.int32) * MOE_TILE
    tile_e = jnp.minimum(jnp.sum((tile_start[:, None] >= ends[None, :]).astype(jnp.int32), axis=1), N_EXPERTS - 1)
    e_onehot = tile_e[:, None] == jnp.arange(N_EXPERTS, dtype=jnp.int32)[None, :]
    filled = jnp.sum(jnp.where(e_onehot, (starts + counts)[None, :], 0), axis=1)
    tile_rows = jnp.clip(filled - tile_start, 0, MOE_TILE).astype(jnp.int32)
    last_used = jnp.maximum(ends[-1] // MOE_TILE - 1, 0)
    tile_expert = jnp.where(tile_start < ends[-1], tile_e, tile_e[last_used]).astype(jnp.int32)
    return pair_slot, tile_expert, tile_rows


def _swiglu(xb, wg, wu, wd):
    gate = jnp.dot(xb, wg, preferred_element_type=F32)
    up = jnp.dot(xb, wu, preferred_element_type=F32)
    a = ((gate * _sigmoid(gate)) * up).astype(BF16)
    return jnp.dot(a, wd, preferred_element_type=F32)


def _ffn_kernel(te_ref, rows_ref, xs_ref, wg_ref, wu_ref, wd_ref, ys_ref, wg_s, wu_s, wd_s):
    i = pl.program_id(0)
    new_expert = jnp.logical_or(i == 0, te_ref[i] != te_ref[jnp.maximum(i - 1, 0)])

    @pl.when(new_expert)
    def _():
        wg_s[...] = wg_ref[0].astype(BF16)
        wu_s[...] = wu_ref[0].astype(BF16)
        wd_s[...] = wd_ref[0].astype(BF16)

    @pl.when(rows_ref[i] > 0)
    def _():
        live = lax.broadcasted_iota(jnp.int32, (MOE_TILE, 1), 0) < rows_ref[i]
        xb = _load_unpack(xs_ref, live).astype(BF16)
        ys_ref[...] = _pack_rows(_swiglu(xb, wg_s[...], wu_s[...], wd_s[...]))

    @pl.when(rows_ref[i] == 0)
    def _():
        ys_ref[...] = jnp.zeros_like(ys_ref)


def _expert_ffn(xs, tile_expert, tile_rows, wg, wu, wd, layer):
    n_slots = xs.shape[0]
    d = D_MODEL
    blk = pl.BlockSpec((MOE_TILE, PACK_W), lambda i, te, nr: (i, 0))
    return pl.pallas_call(
        _ffn_kernel,
        grid_spec=pltpu.PrefetchScalarGridSpec(
            num_scalar_prefetch=2,
            grid=(n_slots // MOE_TILE,),
            in_specs=[blk,
                      pl.BlockSpec((None, 1, d, D_EXPERT), lambda i, te, nr: (layer, te[i], 0, 0)),
                      pl.BlockSpec((None, 1, d, D_EXPERT), lambda i, te, nr: (layer, te[i], 0, 0)),
                      pl.BlockSpec((None, 1, D_EXPERT, d), lambda i, te, nr: (layer, te[i], 0, 0))],
            out_specs=blk,
            scratch_shapes=[pltpu.VMEM((d, D_EXPERT), BF16), pltpu.VMEM((d, D_EXPERT), BF16),
                            pltpu.VMEM((D_EXPERT, d), BF16)]),
        out_shape=jax.ShapeDtypeStruct(xs.shape, jnp.int32),
        compiler_params=_params(1),
        name="expert_ffn",
    )(tile_expert, tile_rows, xs, wg, wu, wd)


def _shared_kernel(h_ref, sg_ref, su_ref, sd_ref, o_ref):
    hb = _load_unpack(h_ref).astype(BF16)
    o_ref[...] = _swiglu(hb, sg_ref[...], su_ref[...], sd_ref[...]).astype(BF16)


def _shared_ffn(h2p, sg, su, sd):
    t = h2p.shape[0]
    tm = min(512, t)
    const2 = lambda i: (0, 0)
    return pl.pallas_call(
        _shared_kernel,
        grid=(t // tm,),
        in_specs=[pl.BlockSpec((tm, PACK_W), lambda i: (i, 0)),
                  pl.BlockSpec(sg.shape, const2), pl.BlockSpec(su.shape, const2), pl.BlockSpec(sd.shape, const2)],
        out_specs=pl.BlockSpec((tm, D_MODEL), lambda i: (i, 0)),
        out_shape=jax.ShapeDtypeStruct((t, D_MODEL), BF16),
        compiler_params=_params(1),
        name="shared_ffn",
    )(h2p, sg, su, sd)


def _combine_kernel(ysh_ref, y1_ref, y2_ref, w_ref, x_ref, g2_ref, gfin_ref, o_ref, *, final):
    w = w_ref[...]
    y = ysh_ref[...].astype(F32) + w[:, 0:1] * _load_unpack(y1_ref) + w[:, 1:2] * _load_unpack(y2_ref)
    xn = x_ref[...] + g2_ref[0] * y
    if final:
        xn = _rms(xn, gfin_ref[...])
    o_ref[...] = xn


def _moe_combine(ysh, yg, w12, x2, mod3, brow, gfin, final):
    t, d = x2.shape
    tm = min(512, t)
    nt = t // tm
    row = lambda i: (i, 0)
    pk = lambda off: pl.BlockSpec((tm, PACK_W), lambda i: (i + off, 0))
    return pl.pallas_call(
        functools.partial(_combine_kernel, final=final),
        grid=(nt,),
        in_specs=[pl.BlockSpec((tm, d), row), pk(0), pk(nt),
                  pl.BlockSpec((tm, LANES), row),
                  pl.BlockSpec((tm, d), row),
                  _mod_spec(5, lambda i: brow(i * tm)),
                  pl.BlockSpec((1, d), lambda i: (0, 0))],
        out_specs=pl.BlockSpec((tm, d), row),
        out_shape=jax.ShapeDtypeStruct((t, d), F32),
        compiler_params=_params(1),
        name="moe_combine",
    )(ysh, yg, yg, w12, x2, mod3, gfin.reshape(1, d))


def _moe_routed(h2p, route, wg, wu, wd, layer, sg, su, sd, x2, mod3, brow, gfin, final):
    t = x2.shape[0]
    pair_slot, tile_expert, tile_rows = _dispatch_plan(route)
    n_slots = tile_expert.shape[0] * MOE_TILE
    xs = _sc_scatter_rows(h2p, pair_slot, n_slots)
    ysh = _shared_ffn(h2p, sg, su, sd)
    ys = _expert_ffn(xs, tile_expert, tile_rows, wg, wu, wd, layer)
    yg = _sc_gather_rows(ys, pair_slot)
    w12 = jnp.concatenate([route[2:4].T, jnp.zeros((t, LANES - 2), F32)], axis=1)
    return _moe_combine(ysh, yg, w12, x2, mod3, brow, gfin, final)


def _rope_tables(seq):
    rows = seq // GRID_W
    row = jnp.repeat(jnp.arange(rows), GRID_W).astype(F32)
    col = jnp.tile(jnp.arange(GRID_W), rows).astype(F32)
    axis_dim = QK_ROPE // 2
    inv = ROPE_THETA ** (-jnp.arange(0, axis_dim, 2, dtype=F32) / axis_dim)
    ang = jnp.concatenate([row[:, None] * inv, col[:, None] * inv], axis=-1)
    cos, sin = jnp.cos(ang), jnp.sin(ang)
    zero = jnp.zeros_like(cos)
    return (jnp.concatenate([cos, cos, zero, zero], axis=1),
            jnp.concatenate([-sin, sin, zero, zero], axis=1))


def _rope_group(w, start):
    half = QK_ROPE // 2
    x1, x2 = w[:, start:start + half], w[:, start + half:start + 2 * half]
    return jnp.concatenate([x1, x2, x2, x1], axis=1)


def _relayout_w_in(w):
    return jnp.concatenate([w[:, :KV_RANK], w[:, KV_COLS:Q_END], _rope_group(w, KV_RANK),
                            w[:, Q_END:]], axis=1).astype(BF16)


def _relayout_w_qb(w):
    dq = QK_NOPE + QK_ROPE
    cols = []
    for h in range(N_HEADS):
        cols += [w[:, h * dq:h * dq + QK_NOPE], _rope_group(w, h * dq + QK_NOPE)]
    return jnp.concatenate(cols, axis=1).astype(BF16)


def kernel(x, c, ctx, c_ctx, w_ada, b_ada, norm_mix_g, norm_ffn_g, w_in, q_norm_g, kv_norm_g, w_qb, w_kvb, w_pool, pool_scale, w_out, w_router, router_bias, w_exp_gate, w_exp_up, w_exp_down, w_sh_gate, w_sh_up, w_sh_down, final_norm_g):
    bsz, seq, d = x.shape
    n_ctx = ctx.shape[1]
    depth = w_ada.shape[0]
    assert bsz < MOD_ROWS and d == D_MODEL

    c_rows = jnp.concatenate([c, c_ctx[None], jnp.zeros((MOD_ROWS - bsz - 1, d), F32)], axis=0)
    mod = _ada_mod(c_rows, w_ada, b_ada)

    cos_t, sin_t = _rope_tables(seq)
    ones_t = jnp.concatenate([jnp.ones((n_ctx, 2 * (QK_ROPE // 2)), F32),
                              jnp.zeros((n_ctx, LANES - QK_ROPE), F32)], axis=1)
    zeros_t = jnp.zeros((n_ctx, LANES), F32)
    wr_t = w_router.T.astype(BF16)
    lat_row = lambda r: r // seq
    ctx_row = lambda r: bsz

    x2 = x.reshape(bsz * seq, d)
    xc2 = ctx.reshape(bsz * n_ctx, d)
    for l in range(depth):
        last = l == depth - 1
        mod3 = mod[l].reshape(MOD_ROWS * N_MOD, 1, d)
        w_in_r = _relayout_w_in(w_in[l])
        wq = _relayout_w_qb(w_qb[l])
        w_kv3 = w_kvb[l].reshape(KV_RANK, N_HEADS, QK_NOPE + V_DIM)
        wk = w_kv3[:, :, :QK_NOPE].reshape(KV_RANK, N_HEADS * QK_NOPE).astype(BF16)
        wvt = w_kv3[:, :, QK_NOPE:].transpose(1, 2, 0).astype(BF16)
        wp = w_pool[l].astype(BF16)
        wo = w_out[l].astype(BF16)
        sg, su, sd = w_sh_gate[l].astype(BF16), w_sh_up[l].astype(BF16), w_sh_down[l].astype(BF16)
        moe = functools.partial(_moe_routed, wg=w_exp_gate, wu=w_exp_up, wd=w_exp_down, layer=l,
                                sg=sg, su=su, sd=sd, mod3=mod3, gfin=final_norm_g)

        p = _in_proj(x2, mod3, lat_row, norm_mix_g[l], w_in_r, P_COLS)
        pc = _in_proj(xc2, mod3, ctx_row, norm_mix_g[l], w_in_r, LAT_W if last else P_COLS)
        q, k_lat, v_lat = _qkv(p, bsz, seq, cos_t, sin_t, q_norm_g[l], kv_norm_g[l], wq, wk, wvt, True)
        ctx_out = _qkv(pc, bsz, n_ctx, ones_t, zeros_t, q_norm_g[l], kv_norm_g[l], wq, wk, wvt, not last)
        k_ctx, v_ctx = ctx_out[-2:]
        y = _attention(q, [k_ctx, k_lat], [v_ctx, v_lat]).reshape(bsz * seq, d)
        x2, h2p, route = _mixer_out(y, p, x2, mod3, lat_row, seq, wp, pool_scale[l], wo,
                                    norm_ffn_g[l], wr_t, router_bias)
        x2 = moe(h2p, route, x2=x2, brow=lat_row, final=last)
        if not last:
            yc = _attention(ctx_out[0], [k_ctx], [v_ctx]).reshape(bsz * n_ctx, d)
            xc2, h2c, route_c = _mixer_out(yc, pc, xc2, mod3, ctx_row, n_ctx, wp, pool_scale[l], wo,
                                           norm_ffn_g[l], wr_t, router_bias)
            xc2 = moe(h2c, route_c, x2=xc2, brow=ctx_row, final=False)
    return x2.reshape(bsz, seq, d)
```

```python
import functools
import math

import jax
import jax.numpy as jnp
from jax import lax
from jax.experimental import pallas as pl
from jax.experimental.pallas import tpu as pltpu
from jax.experimental.pallas import tpu_sc as plsc

F32 = jnp.float32
BF16 = jnp.bfloat16

D_MODEL = 2048
GRID_W = 64
N_HEADS = 16
QK_NOPE = 128
QK_ROPE = 64
V_DIM = 128
Q_RANK = 384
KV_RANK = 512
ROPE_THETA = 10000.0
POOL_DIM = 1024
POOL_GROUPS = 4
POOL_GROUP_DIM = 256
POOL_WINDOWS = (2, 4, 8, 16)
POOL_GROUP_OUT = 512
KV_COLS = KV_RANK + QK_ROPE
Q_END = KV_COLS + Q_RANK
POOL_END = Q_END + POOL_DIM
N_EXPERTS = 16
N_GROUPS = 4
EXPERTS_PER_GROUP = 4
D_EXPERT = 512
EPS = 1e-6
N_MOD = 6

LANES = 128
HEAD_W = 2 * LANES
LAT_W = 1024
P_COLS = LAT_W + POOL_DIM + 2 * D_MODEL
POOL_HALO = 16
MOD_ROWS = 8
Q_SCALE = (1.0 / math.sqrt(QK_NOPE + QK_ROPE)) * math.log2(math.e)
VMEM_LIMIT = 56 * 1024 * 1024
SUBLANES = 8
PACK_W = D_MODEL // 2
ROW_CHUNKS = PACK_W // LANES
ROUTE_ROWS = SUBLANES
MOE_TILE = 256
SC_WINDOW = 128
ATTN_TQ = 256
ATTN_KB = 512
V_HEAD_GROUP = 8
ATTN_HEADS_PER_STEP = 2


def _sigmoid(x):
    return 1.0 / (1.0 + jnp.exp(-x))


def _pack_rows(y):
    half = y.shape[1] // 2
    return pltpu.pack_elementwise([y[:, :half], y[:, half:]], packed_dtype=BF16)


def _load_unpack(ref, live=None):
    w = ref[...]
    if live is not None:
        w = jnp.where(live, w, 0)
    lo = pltpu.unpack_elementwise(w, index=0, packed_dtype=BF16, unpacked_dtype=F32)
    hi = pltpu.unpack_elementwise(w, index=1, packed_dtype=BF16, unpacked_dtype=F32)
    return jnp.concatenate([lo, hi], axis=1)


def _rms(xf, g):
    ms = jnp.mean(xf * xf, axis=-1, keepdims=True)
    return xf * lax.rsqrt(ms + EPS) * g


def _params(n_axes):
    return pltpu.CompilerParams(dimension_semantics=("arbitrary",) * n_axes,
                                vmem_limit_bytes=VMEM_LIMIT)


def _ada_kernel(c_ref, w_ref, b_ref, o_ref):
    cf = c_ref[...]
    a = (cf * _sigmoid(cf)).astype(BF16)
    o_ref[0] = jnp.dot(a, w_ref[0].astype(BF16), preferred_element_type=F32) + b_ref[0]


def _ada_mod(c_rows, w_ada, b_ada):
    depth, d, n = w_ada.shape
    tn = 1024
    return pl.pallas_call(
        _ada_kernel,
        grid=(depth, n // tn),
        in_specs=[pl.BlockSpec((MOD_ROWS, d), lambda l, j: (0, 0)),
                  pl.BlockSpec((1, d, tn), lambda l, j: (l, 0, j)),
                  pl.BlockSpec((1, 1, tn), lambda l, j: (l, 0, j))],
        out_specs=pl.BlockSpec((1, MOD_ROWS, tn), lambda l, j: (l, 0, j)),
        out_shape=jax.ShapeDtypeStruct((depth, MOD_ROWS, n), F32),
        compiler_params=_params(2),
        name="ada_mod",
    )(c_rows, w_ada, b_ada.reshape(depth, 1, n))


def _mod_spec(k, brow):
    return pl.BlockSpec((1, 1, D_MODEL), lambda i, *_: (brow(i) * N_MOD + k, 0, 0))


def _inproj_kernel(x_ref, g_ref, sh_ref, sc_ref, w_ref, o_ref, h_scr, *, tn):
    y = _rms(x_ref[...], g_ref[...])
    h_scr[...] = (y * (1.0 + sc_ref[0]) + sh_ref[0]).astype(BF16)
    for c0 in range(0, o_ref.shape[1], tn):
        acc = jnp.dot(h_scr[...], w_ref[:, c0:c0 + tn], preferred_element_type=F32)
        if c0 >= LAT_W + POOL_DIM:
            acc = _sigmoid(acc)
        o_ref[:, c0:c0 + tn] = acc.astype(BF16)


def _in_proj(x2, mod3, brow, g, w, n_cols):
    t, d = x2.shape
    tm = min(512, t)
    return pl.pallas_call(
        functools.partial(_inproj_kernel, tn=512),
        grid=(t // tm,),
        in_specs=[pl.BlockSpec((tm, d), lambda i: (i, 0)),
                  pl.BlockSpec((1, d), lambda i: (0, 0)),
                  _mod_spec(0, lambda i: brow(i * tm)),
                  _mod_spec(1, lambda i: brow(i * tm)),
                  pl.BlockSpec((d, n_cols), lambda i: (0, 0), pipeline_mode=pl.Buffered(1))],
        out_specs=pl.BlockSpec((tm, n_cols), lambda i: (i, 0)),
        out_shape=jax.ShapeDtypeStruct((t, n_cols), BF16),
        scratch_shapes=[pltpu.VMEM((tm, d), BF16)],
        compiler_params=_params(1),
        name="in_proj",
    )(x2, g.reshape(1, d), mod3, mod3, w)


def _qkv_kernel(lat_ref, cos_ref, sin_ref, gq_ref, gkv_ref, wq_ref, wk_ref, wvt_ref, *out_refs, with_q):
    if with_q:
        q_ref, k_ref, v_ref = out_refs
    else:
        k_ref, v_ref = out_refs
    cos = cos_ref[...]
    sin = sin_ref[...]

    def rope(grp):
        return grp * cos + pltpu.roll(grp, 2 * (QK_ROPE // 2), axis=1) * sin

    kvn = _rms(lat_ref[:, :KV_RANK].astype(F32), gkv_ref[...]).astype(BF16)
    kpe = rope(lat_ref[:, KV_RANK + Q_RANK:].astype(F32)).astype(BF16)
    for h in range(0, N_HEADS, 2):
        kn = jnp.dot(kvn, wk_ref[:, h * QK_NOPE:(h + 2) * QK_NOPE], preferred_element_type=F32)
        for hh in range(2):
            k_ref[0, h + hh, :, :QK_NOPE] = kn[:, hh * QK_NOPE:(hh + 1) * QK_NOPE].astype(BF16)
            k_ref[0, h + hh, :, QK_NOPE:] = kpe
    for h in range(0, N_HEADS, V_HEAD_GROUP):
        vt = lax.dot_general(wvt_ref[h * V_DIM:(h + V_HEAD_GROUP) * V_DIM, :], kvn,
                             (((1,), (1,)), ((), ())), preferred_element_type=F32)
        for hh in range(V_HEAD_GROUP):
            v_ref[0, h + hh] = vt[hh * V_DIM:(hh + 1) * V_DIM, :].astype(BF16)
    if with_q:
        qn = _rms(lat_ref[:, KV_RANK:KV_RANK + Q_RANK].astype(F32), gq_ref[...]).astype(BF16)
        for h in range(N_HEADS):
            qh = jnp.dot(qn, wq_ref[:, h * HEAD_W:(h + 1) * HEAD_W], preferred_element_type=F32)
            q_ref[0, h, :, :QK_NOPE] = (qh[:, :QK_NOPE] * Q_SCALE).astype(BF16)
            q_ref[0, h, :, QK_NOPE:] = (rope(qh[:, QK_NOPE:]) * Q_SCALE).astype(BF16)


def _qkv(p, b, n, cos_t, sin_t, gq, gkv, wq, wk, wvt, with_q):
    tm = min(256, n)
    tpb = n // tm
    rope_tiles = cos_t.shape[0] // tm
    head_spec = pl.BlockSpec((1, N_HEADS, tm, HEAD_W), lambda i: (i // tpb, 0, i % tpb, 0))
    tab_spec = pl.BlockSpec((tm, LANES), lambda i: ((i % tpb) % rope_tiles, 0))
    out_shape = [jax.ShapeDtypeStruct((b, N_HEADS, n, HEAD_W), BF16),
                 jax.ShapeDtypeStruct((b, N_HEADS, V_DIM, n), BF16)]
    out_specs = [head_spec,
                 pl.BlockSpec((1, N_HEADS, V_DIM, tm), lambda i: (i // tpb, 0, 0, i % tpb))]
    if with_q:
        out_shape = [jax.ShapeDtypeStruct((b, N_HEADS, n, HEAD_W), BF16)] + out_shape
        out_specs = [head_spec] + out_specs
    return pl.pallas_call(
        functools.partial(_qkv_kernel, with_q=with_q),
        grid=(b * tpb,),
        in_specs=[pl.BlockSpec((tm, LAT_W), lambda i: (i, 0)),
                  tab_spec, tab_spec,
                  pl.BlockSpec((1, Q_RANK), lambda i: (0, 0)),
                  pl.BlockSpec((1, KV_RANK), lambda i: (0, 0)),
                  pl.BlockSpec(wq.shape, lambda i: (0, 0)),
                  pl.BlockSpec(wk.shape, lambda i: (0, 0)),
                  pl.BlockSpec(wvt.shape, lambda i: (0, 0))],
        out_specs=out_specs,
        out_shape=out_shape,
        compiler_params=_params(1),
        name="qkv",
    )(p, cos_t, sin_t, gq.reshape(1, Q_RANK), gkv.reshape(1, KV_RANK), wq, wk, wvt)


def _row_fold(x, op):
    parts = [x[r:r + SUBLANES] for r in range(0, x.shape[0], SUBLANES)]
    a, b = parts[0], parts[1]
    for i in range(2, len(parts) - 1, 2):
        a, b = op(a, parts[i]), op(b, parts[i + 1])
    if len(parts) % 2:
        a = op(a, parts[-1])
    return op(a, b)


def _attn_kernel(q_ref, *refs, n_seg, kb, tq):
    k_refs, vt_refs = refs[:n_seg], refs[n_seg:2 * n_seg]
    o_ref, s_a, s_b, m_a, m_b = refs[2 * n_seg:]
    chunks = []
    off = 0
    for kr, vr in zip(k_refs, vt_refs):
        nk = kr.shape[2]
        for c0 in range(0, nk, kb):
            n = min(kb, nk - c0)
            chunks.append((kr, vr, c0, n, off))
            off += n

    tiles_per_head = q_ref.shape[2] // tq

    def head_rows(t):
        return t // tiles_per_head, pl.ds(pl.multiple_of((t % tiles_per_head) * tq, tq), tq)

    def scores(t, s_buf, m_buf):
        g, rows = head_rows(t)
        q = q_ref[0, g, rows, :]
        mp = None
        for kr, vr, c0, n, o in chunks:
            s = lax.dot_general(kr[0, g, c0:c0 + n, :], q, (((1,), (1,)), ((), ())),
                                preferred_element_type=F32)
            s_buf[o:o + n, :] = s
            m = _row_fold(s, jnp.maximum)
            mp = m if mp is None else jnp.maximum(mp, m)
        m_buf[...] = mp

    def values(t, s_buf, m_buf):
        g, rows = head_rows(t)
        mrow = jnp.max(m_buf[...], axis=0, keepdims=True)
        lp = None
        acc = None
        for kr, vr, c0, n, o in chunks:
            pr = jnp.exp2(s_buf[o:o + n, :] - mrow)
            ls = _row_fold(pr, jnp.add)
            lp = ls if lp is None else lp + ls
            pv = jnp.dot(vr[0, g, :, c0:c0 + n], pr.astype(BF16), preferred_element_type=F32)
            acc = pv if acc is None else acc + pv
        l = jnp.sum(lp, axis=0, keepdims=True)
        o_ref[0, g, rows, :] = (acc / l).T.astype(BF16)

    nt = q_ref.shape[1] * tiles_per_head
    scores(0, s_a, m_a)
    if nt > 1:
        def pair(j, carry):
            scores(2 * j + 1, s_b, m_b)
            values(2 * j, s_a, m_a)
            scores(2 * j + 2, s_a, m_a)
            values(2 * j + 1, s_b, m_b)
            return carry

        lax.fori_loop(0, nt // 2 - 1, pair, 0)
        scores(nt - 1, s_b, m_b)
        values(nt - 2, s_a, m_a)
        values(nt - 1, s_b, m_b)
    else:
        values(0, s_a, m_a)


def _attention(q, ks, vts):
    b, hh, n, _ = q.shape
    tq = min(ATTN_TQ, n)
    hg = ATTN_HEADS_PER_STEP
    assert (hg * (n // tq)) % 2 == 0 and hh % hg == 0
    nk_total = sum(k.shape[2] for k in ks)
    seg_spec = lambda a: pl.BlockSpec((1, hg) + a.shape[2:], lambda bi, h: (bi, h, 0, 0))
    return pl.pallas_call(
        functools.partial(_attn_kernel, n_seg=len(ks), kb=ATTN_KB, tq=tq),
        grid=(b, hh // hg),
        in_specs=[pl.BlockSpec((1, hg, n, HEAD_W), lambda bi, h: (bi, h, 0, 0))]
                 + [seg_spec(a) for a in ks] + [seg_spec(a) for a in vts],
        out_specs=pl.BlockSpec((1, hg, n, V_DIM), lambda bi, h: (bi, h, 0, 0)),
        out_shape=jax.ShapeDtypeStruct((b, hh, n, V_DIM), BF16),
        scratch_shapes=[pltpu.VMEM((nk_total, tq), F32), pltpu.VMEM((nk_total, tq), F32),
                        pltpu.VMEM((SUBLANES, tq), F32), pltpu.VMEM((SUBLANES, tq), F32)],
        compiler_params=_params(2),
        name="attention",
    )(q, *ks, *vts)


def _top2sum(a, b, c, d):
    s1, t1 = jnp.maximum(a, b), jnp.minimum(a, b)
    s2, t2 = jnp.maximum(c, d), jnp.minimum(c, d)
    return jnp.maximum(s1, s2) + jnp.maximum(jnp.minimum(s1, s2), jnp.maximum(t1, t2))


def _route(logits_t, bias):
    sc = _sigmoid(logits_t)
    sel = sc + bias
    sel_r = [sel[e:e + 1, :] for e in range(N_EXPERTS)]
    sc_r = [sc[e:e + 1, :] for e in range(N_EXPERTS)]
    epg = EXPERTS_PER_GROUP
    gs = [_top2sum(*sel_r[g * epg:(g + 1) * epg]) for g in range(N_GROUPS)]
    best, gi = gs[0], jnp.zeros(gs[0].shape, jnp.int32)
    for g in range(1, N_GROUPS):
        upd = gs[g] > best
        best = jnp.where(upd, gs[g], best)
        gi = jnp.where(upd, g, gi)

    def pick_group(rows, k):
        r = rows[k]
        for g in range(1, N_GROUPS):
            r = jnp.where(gi == g, rows[g * epg + k], r)
        return r

    in_sel = [pick_group(sel_r, k) for k in range(epg)]
    in_sc = [pick_group(sc_r, k) for k in range(epg)]
    b1, i1 = in_sel[0], jnp.zeros(gi.shape, jnp.int32)
    for k in range(1, epg):
        upd = in_sel[k] > b1
        b1 = jnp.where(upd, in_sel[k], b1)
        i1 = jnp.where(upd, k, i1)
    b2, i2 = None, None
    for k in range(epg):
        cand = jnp.where(i1 == k, -jnp.inf, in_sel[k])
        if b2 is None:
            b2, i2 = cand, jnp.zeros(gi.shape, jnp.int32)
        else:
            upd = cand > b2
            b2 = jnp.where(upd, cand, b2)
            i2 = jnp.where(upd, k, i2)

    def pick_local(idx):
        r = in_sc[0]
        for k in range(1, epg):
            r = jnp.where(idx == k, in_sc[k], r)
        return r

    s1, s2 = pick_local(i1), pick_local(i2)
    den = s1 + s2
    return gi * epg + i1, gi * epg + i2, s1 / den, s2 / den


def _mixout_kernel(y_ref, u_ref, up_ref, un_ref, gm_ref, gp_ref, wp_ref, ps_ref, wo_ref, x_ref,
                   g1_ref, gf_ref, sh2_ref, sc2_ref, wr_ref, rb_ref,
                   xo_ref, h2_ref, route_ref, *, tm, tpb, n_seq):
    i = pl.program_id(0)
    base = (i % tpb) * tm
    uc = u_ref[...]
    uext = jnp.concatenate([up_ref[...], uc, un_ref[...]], axis=0)
    r = lax.broadcasted_iota(jnp.int32, (tm, tm + 2 * POOL_HALO), 0)
    c = lax.broadcasted_iota(jnp.int32, (tm, tm + 2 * POOL_HALO), 1)
    rel = c - POOL_HALO - r
    jpos = base - POOL_HALO + c
    valid = (jpos >= 0) & (jpos < n_seq)
    tpos = base + lax.broadcasted_iota(jnp.int32, (tm, 1), 0)
    parts = []
    for g, w in enumerate(POOL_WINDOWS):
        sl = slice(g * POOL_GROUP_DIM, (g + 1) * POOL_GROUP_DIM)
        band = jnp.where(valid & (rel >= -(w // 2)) & (rel < w // 2), 1.0, 0.0).astype(BF16)
        cnt = jnp.clip(tpos + w // 2, 0, n_seq) - jnp.clip(tpos - w // 2, 0, n_seq)
        wsum = jnp.dot(band, uext[:, sl], preferred_element_type=F32)
        z = wsum / cnt.astype(F32) - uc[:, sl].astype(F32)
        parts.append(jnp.dot(z.astype(BF16), wp_ref[g], preferred_element_type=F32))
    ypool = jnp.concatenate(parts, axis=1) * ps_ref[...]
    y_mla = jnp.concatenate([y_ref[0, h] for h in range(N_HEADS)], axis=1)
    mix = gm_ref[...].astype(F32) * y_mla.astype(F32) + gp_ref[...].astype(F32) * ypool
    y = jnp.dot(mix.astype(BF16), wo_ref[...], preferred_element_type=F32)
    xn = x_ref[...] + g1_ref[0] * y
    xo_ref[...] = xn
    h2f = _rms(xn, gf_ref[...]) * (1.0 + sc2_ref[0]) + sh2_ref[0]
    h2 = h2f.astype(BF16)
    h2_ref[...] = _pack_rows(h2f)
    logits_t = lax.dot_general(wr_ref[...], h2, (((1,), (1,)), ((), ())), preferred_element_type=F32)
    e1, e2, w1, w2 = _route(logits_t, rb_ref[...])
    zero = jnp.zeros((ROUTE_ROWS - 4, tm), F32)
    route_ref[...] = jnp.concatenate([e1.astype(F32), e2.astype(F32), w1, w2, zero], axis=0)


def _mixer_out(y, p, x2, mod3, brow, n_seq, wp, ps, wo, gf, wr_t, rb):
    t, d = x2.shape
    tm = min(256, n_seq)
    tpb = n_seq // tm
    hpt = tm // POOL_HALO
    n_halo = t // POOL_HALO
    row = lambda i: (i, 0)
    const2 = lambda i: (0, 0)
    mrow = lambda i: brow(i * tm)
    return pl.pallas_call(
        functools.partial(_mixout_kernel, tm=tm, tpb=tpb, n_seq=n_seq),
        grid=(t // tm,),
        in_specs=[pl.BlockSpec((1, N_HEADS, tm, V_DIM), lambda i: (i // tpb, 0, i % tpb, 0)),
                  pl.BlockSpec((tm, POOL_DIM), lambda i: (i, LAT_W // POOL_DIM)),
                  pl.BlockSpec((POOL_HALO, POOL_DIM),
                               lambda i: (jnp.maximum(i * hpt - 1, 0), LAT_W // POOL_DIM)),
                  pl.BlockSpec((POOL_HALO, POOL_DIM),
                               lambda i: (jnp.minimum((i + 1) * hpt, n_halo - 1), LAT_W // POOL_DIM)),
                  pl.BlockSpec((tm, d), lambda i: (i, (LAT_W + POOL_DIM) // d)),
                  pl.BlockSpec((tm, d), lambda i: (i, (LAT_W + POOL_DIM) // d + 1)),
                  pl.BlockSpec(wp.shape, lambda i: (0, 0, 0)),
                  pl.BlockSpec((1, d), const2),
                  pl.BlockSpec((d, d), const2),
                  pl.BlockSpec((tm, d), row),
                  _mod_spec(2, mrow),
                  pl.BlockSpec((1, d), const2),
                  _mod_spec(3, mrow),
                  _mod_spec(4, mrow),
                  pl.BlockSpec((N_EXPERTS, d), const2),
                  pl.BlockSpec((N_EXPERTS, 1), const2)],
        out_specs=[pl.BlockSpec((tm, d), row),
                   pl.BlockSpec((tm, PACK_W), row),
                   pl.BlockSpec((ROUTE_ROWS, tm), lambda i: (0, i))],
        out_shape=[jax.ShapeDtypeStruct((t, d), F32),
                   jax.ShapeDtypeStruct((t, PACK_W), jnp.int32),
                   jax.ShapeDtypeStruct((ROUTE_ROWS, t), F32)],
        compiler_params=_params(1),
        name="mixer_out",
    )(y, p, p, p, p, p, wp, ps.reshape(1, d), wo, x2, mod3, gf.reshape(1, d), mod3, mod3, wr_t,
      rb.reshape(N_EXPERTS, 1))


def _chunk_view(x):
    n = x.shape[0]
    return (x.reshape(n // SUBLANES, SUBLANES, ROW_CHUNKS, LANES).transpose(0, 2, 1, 3)
            .reshape(n * ROW_CHUNKS, LANES))


def _row_view(c):
    n = c.shape[0] // ROW_CHUNKS
    return (c.reshape(n // SUBLANES, ROW_CHUNKS, SUBLANES, LANES).transpose(0, 2, 1, 3)
            .reshape(n, PACK_W))


def _sc_index_chunks(idx):
    r = idx.reshape(-1, 1, SUBLANES)
    j = jnp.arange(ROW_CHUNKS, dtype=jnp.int32)[None, :, None]
    ids = (r // SUBLANES) * (SUBLANES * ROW_CHUNKS) + j * SUBLANES + r % SUBLANES
    return ids.reshape(1, idx.shape[0] * ROW_CHUNKS)


def _sc_mesh():
    return plsc.VectorSubcoreMesh(core_axis_name="core", subcore_axis_name="subcore")


def _sc_gather_rows(table, idx):
    n, m = table.shape[0], idx.shape[0]
    n_idx = m * ROW_CHUNKS
    assert n_idx % SC_WINDOW == 0 and n % SUBLANES == 0 and m % SUBLANES == 0

    @pl.kernel(out_type=jax.ShapeDtypeStruct((n_idx, LANES), jnp.int32), mesh=_sc_mesh())
    def gather(t_hbm, i_hbm, o_hbm):
        def body(i_vmem, o_vmem):
            pltpu.sync_copy(t_hbm.at[i_vmem.at[0]], o_vmem)

        pltpu.emit_pipeline(
            body,
            grid=(n_idx // SC_WINDOW,),
            in_specs=[pl.BlockSpec((1, SC_WINDOW), index_map=lambda i: (0, i))],
            out_specs=[pl.BlockSpec((SC_WINDOW, LANES), index_map=lambda i: (i, 0))],
            core_axis_name=("core", "subcore"),
            dimension_semantics=(pltpu.PARALLEL,),
        )(i_hbm, o_hbm)

    return _row_view(gather(_chunk_view(table), _sc_index_chunks(idx)))


def _sc_scatter_rows(rows, idx, n_out):
    n, m = rows.shape[0], idx.shape[0]
    n_idx = m * ROW_CHUNKS
    src_steps = n * ROW_CHUNKS // SC_WINDOW
    assert n_idx % SC_WINDOW == 0 and (n * ROW_CHUNKS) % SC_WINDOW == 0 and m % n == 0
    assert n % SUBLANES == 0 and n_out % SUBLANES == 0

    @pl.kernel(out_type=jax.ShapeDtypeStruct((n_out * ROW_CHUNKS, LANES), jnp.int32), mesh=_sc_mesh())
    def scatter(x_hbm, i_hbm, o_hbm):
        def body(x_vmem, i_vmem):
            pltpu.sync_copy(x_vmem, o_hbm.at[i_vmem.at[0]])

        pltpu.emit_pipeline(
            body,
            grid=(n_idx // SC_WINDOW,),
            in_specs=[pl.BlockSpec((SC_WINDOW, LANES), index_map=lambda i: (i % src_steps, 0)),
                      pl.BlockSpec((1, SC_WINDOW), index_map=lambda i: (0, i))],
            out_specs=[],
            core_axis_name=("core", "subcore"),
            dimension_semantics=(pltpu.PARALLEL,),
        )(x_hbm, i_hbm)

    return _row_view(scatter(_chunk_view(rows), _sc_index_chunks(idx)))


def _dispatch_plan(route):
    t = route.shape[1]
    n_pairs = 2 * t
    n_tiles = n_pairs // MOE_TILE + N_EXPERTS
    blk = LANES
    e = jnp.concatenate([route[0], route[1]]).astype(jnp.int32)
    onehot = (e[:, None] == jnp.arange(N_EXPERTS, dtype=jnp.int32)[None, :]).astype(F32)
    oh3 = onehot.reshape(n_pairs // blk, blk, N_EXPERTS)
    tri = (jnp.arange(blk)[:, None] >= jnp.arange(blk)[None, :]).astype(F32)
    within = jnp.einsum("ij,bjk->bik", tri, oh3)
    totals = within[:, -1, :]
    before = jnp.cumsum(totals, axis=0) - totals
    csum = (within + before[:, None, :]).reshape(n_pairs, N_EXPERTS)
    counts = jnp.sum(totals, axis=0).astype(jnp.int32)
    padded = ((counts + MOE_TILE - 1) // MOE_TILE) * MOE_TILE
    ends = jnp.cumsum(padded)
    starts = ends - padded
    pair_slot = jnp.sum(onehot * (csum - 1.0 + starts.astype(F32)[None, :]), axis=1).astype(jnp.int32)
    tile_start = jnp.arange(n_tiles, dtype=jnp.int32) * MOE_TILE
    tile_e = jnp.minimum(jnp.sum((tile_start[:, None] >= ends[None, :]).astype(jnp.int32), axis=1), N_EXPERTS - 1)
    e_onehot = tile_e[:, None] == jnp.arange(N_EXPERTS, dtype=jnp.int32)[None, :]
    filled = jnp.sum(jnp.where(e_onehot, (starts + counts)[None, :], 0), axis=1)
    tile_rows = jnp.clip(filled - tile_start, 0, MOE_TILE).astype(jnp.int32)
    last_used = jnp.maximum(ends[-1] // MOE_TILE - 1, 0)
    tile_expert = jnp.where(tile_start < ends[-1], tile_e, tile_e[last_used]).astype(jnp.int32)
    return pair_slot, tile_expert, tile_rows


def _swiglu(xb, wg, wu, wd):
    gate = jnp.dot(xb, wg, preferred_element_type=F32)
    up = jnp.dot(xb, wu, preferred_element_type=F32)
    a = ((gate * _sigmoid(gate)) * up).astype(BF16)
    return jnp.dot(a, wd, preferred_element_type=F32)


def _ffn_kernel(te_ref, rows_ref, xs_ref, wg_ref, wu_ref, wd_ref, ys_ref, wg_s, wu_s, wd_s):
    i = pl.program_id(0)
    new_expert = jnp.logical_or(i == 0, te_ref[i] != te_ref[jnp.maximum(i - 1, 0)])

    @pl.when(new_expert)
    def _():
        wg_s[...] = wg_ref[0].astype(BF16)
        wu_s[...] = wu_ref[0].astype(BF16)
        wd_s[...] = wd_ref[0].astype(BF16)

    @pl.when(rows_ref[i] > 0)
    def _():
        live = lax.broadcasted_iota(jnp.int32, (MOE_TILE, 1), 0) < rows_ref[i]
        xb = _load_unpack(xs_ref, live).astype(BF16)
        ys_ref[...] = _pack_rows(_swiglu(xb, wg_s[...], wu_s[...], wd_s[...]))

    @pl.when(rows_ref[i] == 0)
    def _():
        ys_ref[...] = jnp.zeros_like(ys_ref)


def _expert_ffn(xs, tile_expert, tile_rows, wg, wu, wd, layer):
    n_slots = xs.shape[0]
    d = D_MODEL
    blk = pl.BlockSpec((MOE_TILE, PACK_W), lambda i, te, nr: (i, 0))
    return pl.pallas_call(
        _ffn_kernel,
        grid_spec=pltpu.PrefetchScalarGridSpec(
            num_scalar_prefetch=2,
            grid=(n_slots // MOE_TILE,),
            in_specs=[blk,
                      pl.BlockSpec((None, 1, d, D_EXPERT), lambda i, te, nr: (layer, te[i], 0, 0)),
                      pl.BlockSpec((None, 1, d, D_EXPERT), lambda i, te, nr: (layer, te[i], 0, 0)),
                      pl.BlockSpec((None, 1, D_EXPERT, d), lambda i, te, nr: (layer, te[i], 0, 0))],
            out_specs=blk,
            scratch_shapes=[pltpu.VMEM((d, D_EXPERT), BF16), pltpu.VMEM((d, D_EXPERT), BF16),
                            pltpu.VMEM((D_EXPERT, d), BF16)]),
        out_shape=jax.ShapeDtypeStruct(xs.shape, jnp.int32),
        compiler_params=_params(1),
        name="expert_ffn",
    )(tile_expert, tile_rows, xs, wg, wu, wd)


def _shared_kernel(h_ref, sg_ref, su_ref, sd_ref, o_ref):
    hb = _load_unpack(h_ref).astype(BF16)
    o_ref[...] = _swiglu(hb, sg_ref[...], su_ref[...], sd_ref[...]).astype(BF16)


def _shared_ffn(h2p, sg, su, sd):
    t = h2p.shape[0]
    tm = min(512, t)
    const2 = lambda i: (0, 0)
    return pl.pallas_call(
        _shared_kernel,
        grid=(t // tm,),
        in_specs=[pl.BlockSpec((tm, PACK_W), lambda i: (i, 0)),
                  pl.BlockSpec(sg.shape, const2), pl.BlockSpec(su.shape, const2), pl.BlockSpec(sd.shape, const2)],
        out_specs=pl.BlockSpec((tm, D_MODEL), lambda i: (i, 0)),
        out_shape=jax.ShapeDtypeStruct((t, D_MODEL), BF16),
        compiler_params=_params(1),
        name="shared_ffn",
    )(h2p, sg, su, sd)


def _combine_kernel(ysh_ref, y1_ref, y2_ref, w_ref, x_ref, g2_ref, gfin_ref, o_ref, *, final):
    w = w_ref[...]
    y = ysh_ref[...].astype(F32) + w[:, 0:1] * _load_unpack(y1_ref) + w[:, 1:2] * _load_unpack(y2_ref)
    xn = x_ref[...] + g2_ref[0] * y
    if final:
        xn = _rms(xn, gfin_ref[...])
    o_ref[...] = xn


def _moe_combine(ysh, yg, w12, x2, mod3, brow, gfin, final):
    t, d = x2.shape
    tm = min(512, t)
    nt = t // tm
    row = lambda i: (i, 0)
    pk = lambda off: pl.BlockSpec((tm, PACK_W), lambda i: (i + off, 0))
    return pl.pallas_call(
        functools.partial(_combine_kernel, final=final),
        grid=(nt,),
        in_specs=[pl.BlockSpec((tm, d), row), pk(0), pk(nt),
                  pl.BlockSpec((tm, LANES), row),
                  pl.BlockSpec((tm, d), row),
                  _mod_spec(5, lambda i: brow(i * tm)),
                  pl.BlockSpec((1, d), lambda i: (0, 0))],
        out_specs=pl.BlockSpec((tm, d), row),
        out_shape=jax.ShapeDtypeStruct((t, d), F32),
        compiler_params=_params(1),
        name="moe_combine",
    )(ysh, yg, yg, w12, x2, mod3, gfin.reshape(1, d))


def _moe_routed(h2p, route, wg, wu, wd, layer, sg, su, sd, x2, mod3, brow, gfin, final):
    t = x2.shape[0]
    pair_slot, tile_expert, tile_rows = _dispatch_plan(route)
    n_slots = tile_expert.shape[0] * MOE_TILE
    xs = _sc_scatter_rows(h2p, pair_slot, n_slots)
    ysh = _shared_ffn(h2p, sg, su, sd)
    ys = _expert_ffn(xs, tile_expert, tile_rows, wg, wu, wd, layer)
    yg = _sc_gather_rows(ys, pair_slot)
    w12 = jnp.concatenate([route[2:4].T, jnp.zeros((t, LANES - 2), F32)], axis=1)
    return _moe_combine(ysh, yg, w12, x2, mod3, brow, gfin, final)


def _rope_tables(seq):
    rows = seq // GRID_W
    row = jnp.repeat(jnp.arange(rows), GRID_W).astype(F32)
    col = jnp.tile(jnp.arange(GRID_W), rows).astype(F32)
    axis_dim = QK_ROPE // 2
    inv = ROPE_THETA ** (-jnp.arange(0, axis_dim, 2, dtype=F32) / axis_dim)
    ang = jnp.concatenate([row[:, None] * inv, col[:, None] * inv], axis=-1)
    cos, sin = jnp.cos(ang), jnp.sin(ang)
    zero = jnp.zeros_like(cos)
    return (jnp.concatenate([cos, cos, zero, zero], axis=1),
            jnp.concatenate([-sin, sin, zero, zero], axis=1))


def _rope_group(w, start):
    half = QK_ROPE // 2
    x1, x2 = w[:, start:start + half], w[:, start + half:start + 2 * half]
    return jnp.concatenate([x1, x2, x2, x1], axis=1)


def _relayout_w_in(w):
    return jnp.concatenate([w[:, :KV_RANK], w[:, KV_COLS:Q_END], _rope_group(w, KV_RANK),
                            w[:, Q_END:]], axis=1).astype(BF16)


def _relayout_w_qb(w):
    dq = QK_NOPE + QK_ROPE
    cols = []
    for h in range(N_HEADS):
        cols += [w[:, h * dq:h * dq + QK_NOPE], _rope_group(w, h * dq + QK_NOPE)]
    return jnp.concatenate(cols, axis=1).astype(BF16)


def kernel(x, c, ctx, c_ctx, w_ada, b_ada, norm_mix_g, norm_ffn_g, w_in, q_norm_g, kv_norm_g, w_qb, w_kvb, w_pool, pool_scale, w_out, w_router, router_bias, w_exp_gate, w_exp_up, w_exp_down, w_sh_gate, w_sh_up, w_sh_down, final_norm_g):
    bsz, seq, d = x.shape
    n_ctx = ctx.shape[1]
    depth = w_ada.shape[0]
    assert bsz < MOD_ROWS and d == D_MODEL

    c_rows = jnp.concatenate([c, c_ctx[None], jnp.zeros((MOD_ROWS - bsz - 1, d), F32)], axis=0)
    mod = _ada_mod(c_rows, w_ada, b_ada)

    cos_t, sin_t = _rope_tables(seq)
    ones_t = jnp.concatenate([jnp.ones((n_ctx, 2 * (QK_ROPE // 2)), F32),
                              jnp.zeros((n_ctx, LANES - QK_ROPE), F32)], axis=1)
    zeros_t = jnp.zeros((n_ctx, LANES), F32)
    wr_t = w_router.T.astype(BF16)
    lat_row = lambda r: r // seq
    ctx_row = lambda r: bsz

    x2 = x.reshape(bsz * seq, d)
    xc2 = ctx.reshape(bsz * n_ctx, d)
    for l in range(depth):
        last = l == depth - 1
        mod3 = mod[l].reshape(MOD_ROWS * N_MOD, 1, d)
        w_in_r = _relayout_w_in(w_in[l])
        wq = _relayout_w_qb(w_qb[l])
        w_kv3 = w_kvb[l].reshape(KV_RANK, N_HEADS, QK_NOPE + V_DIM)
        wk = w_kv3[:, :, :QK_NOPE].reshape(KV_RANK, N_HEADS * QK_NOPE).astype(BF16)
        wvt = (w_kv3[:, :, QK_NOPE:].transpose(1, 2, 0).reshape(N_HEADS * V_DIM, KV_RANK)
               .astype(BF16))
        wp = w_pool[l].astype(BF16)
        wo = w_out[l].astype(BF16)
        sg, su, sd = w_sh_gate[l].astype(BF16), w_sh_up[l].astype(BF16), w_sh_down[l].astype(BF16)
        moe = functools.partial(_moe_routed, wg=w_exp_gate, wu=w_exp_up, wd=w_exp_down, layer=l,
                                sg=sg, su=su, sd=sd, mod3=mod3, gfin=final_norm_g)

        p = _in_proj(x2, mod3, lat_row, norm_mix_g[l], w_in_r, P_COLS)
        pc = _in_proj(xc2, mod3, ctx_row, norm_mix_g[l], w_in_r, LAT_W if last else P_COLS)
        q, k_lat, v_lat = _qkv(p, bsz, seq, cos_t, sin_t, q_norm_g[l], kv_norm_g[l], wq, wk, wvt, True)
        ctx_out = _qkv(pc, bsz, n_ctx, ones_t, zeros_t, q_norm_g[l], kv_norm_g[l], wq, wk, wvt, not last)
        k_ctx, v_ctx = ctx_out[-2:]
        y = _attention(q, [k_ctx, k_lat], [v_ctx, v_lat])
        x2, h2p, route = _mixer_out(y, p, x2, mod3, lat_row, seq, wp, pool_scale[l], wo,
                                    norm_ffn_g[l], wr_t, router_bias)
        x2 = moe(h2p, route, x2=x2, brow=lat_row, final=last)
        if not last:
            yc = _attention(ctx_out[0], [k_ctx], [v_ctx])
            xc2, h2c, route_c = _mixer_out(yc, pc, xc2, mod3, ctx_row, n_ctx, wp, pool_scale[l], wo,
                                           norm_ffn_g[l], wr_t, router_bias)
            xc2 = moe(h2c, route_c, x2=xc2, brow=ctx_row, final=False)
    return x2.reshape(bsz, seq, d)
```

```python
import functools
import math

import jax
import jax.numpy as jnp
from jax import lax
from jax.experimental import pallas as pl
from jax.experimental.pallas import tpu as pltpu
from jax.experimental.pallas import tpu_sc as plsc

F32 = jnp.float32
BF16 = jnp.bfloat16

D_MODEL = 2048
GRID_W = 64
N_HEADS = 16
QK_NOPE = 128
QK_ROPE = 64
V_DIM = 128
Q_RANK = 384
KV_RANK = 512
ROPE_THETA = 10000.0
POOL_DIM = 1024
POOL_GROUPS = 4
POOL_GROUP_DIM = 256
POOL_WINDOWS = (2, 4, 8, 16)
POOL_GROUP_OUT = 512
KV_COLS = KV_RANK + QK_ROPE
Q_END = KV_COLS + Q_RANK
POOL_END = Q_END + POOL_DIM
N_EXPERTS = 16
N_GROUPS = 4
EXPERTS_PER_GROUP = 4
D_EXPERT = 512
EPS = 1e-6
N_MOD = 6

LANES = 128
HEAD_W = 2 * LANES
LAT_W = 1024
P_COLS = LAT_W + POOL_DIM + 2 * D_MODEL
POOL_HALO = 16
MOD_ROWS = 8
Q_SCALE = (1.0 / math.sqrt(QK_NOPE + QK_ROPE)) * math.log2(math.e)
VMEM_LIMIT = 56 * 1024 * 1024
SUBLANES = 8
PACK_W = D_MODEL // 2
ROW_CHUNKS = PACK_W // LANES
ROUTE_ROWS = SUBLANES
MOE_TILE = 512
SC_WINDOW = 128
ATTN_TQ = 256
ATTN_KB = 512
V_HEAD_GROUP = 8
ATTN_HEADS_PER_STEP = 1


def _sigmoid(x):
    return 1.0 / (1.0 + jnp.exp(-x))


def _pack_rows(y):
    half = y.shape[1] // 2
    return pltpu.pack_elementwise([y[:, :half], y[:, half:]], packed_dtype=BF16)


def _load_unpack(ref, live=None):
    w = ref[...]
    if live is not None:
        w = jnp.where(live, w, 0)
    lo = pltpu.unpack_elementwise(w, index=0, packed_dtype=BF16, unpacked_dtype=F32)
    hi = pltpu.unpack_elementwise(w, index=1, packed_dtype=BF16, unpacked_dtype=F32)
    return jnp.concatenate([lo, hi], axis=1)


def _rms(xf, g):
    ms = jnp.mean(xf * xf, axis=-1, keepdims=True)
    return xf * lax.rsqrt(ms + EPS) * g


def _params(n_axes):
    return pltpu.CompilerParams(dimension_semantics=("arbitrary",) * n_axes,
                                vmem_limit_bytes=VMEM_LIMIT)


def _ada_kernel(c_ref, w_ref, b_ref, o_ref):
    cf = c_ref[...]
    a = (cf * _sigmoid(cf)).astype(BF16)
    o_ref[0] = jnp.dot(a, w_ref[0].astype(BF16), preferred_element_type=F32) + b_ref[0]


def _ada_mod(c_rows, w_ada, b_ada):
    depth, d, n = w_ada.shape
    tn = 1024
    return pl.pallas_call(
        _ada_kernel,
        grid=(depth, n // tn),
        in_specs=[pl.BlockSpec((MOD_ROWS, d), lambda l, j: (0, 0)),
                  pl.BlockSpec((1, d, tn), lambda l, j: (l, 0, j)),
                  pl.BlockSpec((1, 1, tn), lambda l, j: (l, 0, j))],
        out_specs=pl.BlockSpec((1, MOD_ROWS, tn), lambda l, j: (l, 0, j)),
        out_shape=jax.ShapeDtypeStruct((depth, MOD_ROWS, n), F32),
        compiler_params=_params(2),
        name="ada_mod",
    )(c_rows, w_ada, b_ada.reshape(depth, 1, n))


def _mod_spec(k, brow):
    return pl.BlockSpec((1, 1, D_MODEL), lambda i, *_: (brow(i) * N_MOD + k, 0, 0))


def _inproj_kernel(x_ref, g_ref, sh_ref, sc_ref, w_ref, o_ref, h_scr, *, tn):
    y = _rms(x_ref[...], g_ref[...])
    h_scr[...] = (y * (1.0 + sc_ref[0]) + sh_ref[0]).astype(BF16)
    for c0 in range(0, o_ref.shape[1], tn):
        acc = jnp.dot(h_scr[...], w_ref[:, c0:c0 + tn], preferred_element_type=F32)
        if c0 >= LAT_W + POOL_DIM:
            acc = _sigmoid(acc)
        o_ref[:, c0:c0 + tn] = acc.astype(BF16)


def _in_proj(x2, mod3, brow, g, w, n_cols):
    t, d = x2.shape
    tm = min(512, t)
    return pl.pallas_call(
        functools.partial(_inproj_kernel, tn=512),
        grid=(t // tm,),
        in_specs=[pl.BlockSpec((tm, d), lambda i: (i, 0)),
                  pl.BlockSpec((1, d), lambda i: (0, 0)),
                  _mod_spec(0, lambda i: brow(i * tm)),
                  _mod_spec(1, lambda i: brow(i * tm)),
                  pl.BlockSpec((d, n_cols), lambda i: (0, 0), pipeline_mode=pl.Buffered(1))],
        out_specs=pl.BlockSpec((tm, n_cols), lambda i: (i, 0)),
        out_shape=jax.ShapeDtypeStruct((t, n_cols), BF16),
        scratch_shapes=[pltpu.VMEM((tm, d), BF16)],
        compiler_params=_params(1),
        name="in_proj",
    )(x2, g.reshape(1, d), mod3, mod3, w)


def _qkv_kernel(lat_ref, cos_ref, sin_ref, gq_ref, gkv_ref, wq_ref, wk_ref, wvt_ref, *out_refs, with_q):
    if with_q:
        q_ref, k_ref, v_ref = out_refs
    else:
        k_ref, v_ref = out_refs
    cos = cos_ref[...]
    sin = sin_ref[...]

    def rope(grp):
        return grp * cos + pltpu.roll(grp, 2 * (QK_ROPE // 2), axis=1) * sin

    kvn = _rms(lat_ref[:, :KV_RANK].astype(F32), gkv_ref[...]).astype(BF16)
    kpe = rope(lat_ref[:, KV_RANK + Q_RANK:].astype(F32)).astype(BF16)
    for h in range(0, N_HEADS, 2):
        kn = jnp.dot(kvn, wk_ref[:, h * QK_NOPE:(h + 2) * QK_NOPE], preferred_element_type=F32)
        for hh in range(2):
            k_ref[0, h + hh, :, :QK_NOPE] = kn[:, hh * QK_NOPE:(hh + 1) * QK_NOPE].astype(BF16)
            k_ref[0, h + hh, :, QK_NOPE:] = kpe
    for h in range(0, N_HEADS, V_HEAD_GROUP):
        vt = lax.dot_general(wvt_ref[h * V_DIM:(h + V_HEAD_GROUP) * V_DIM, :], kvn,
                             (((1,), (1,)), ((), ())), preferred_element_type=F32)
        for hh in range(V_HEAD_GROUP):
            v_ref[0, h + hh] = vt[hh * V_DIM:(hh + 1) * V_DIM, :].astype(BF16)
    if with_q:
        qn = _rms(lat_ref[:, KV_RANK:KV_RANK + Q_RANK].astype(F32), gq_ref[...]).astype(BF16)
        for h in range(N_HEADS):
            qh = jnp.dot(qn, wq_ref[:, h * HEAD_W:(h + 1) * HEAD_W], preferred_element_type=F32)
            q_ref[0, h, :, :QK_NOPE] = (qh[:, :QK_NOPE] * Q_SCALE).astype(BF16)
            q_ref[0, h, :, QK_NOPE:] = (rope(qh[:, QK_NOPE:]) * Q_SCALE).astype(BF16)


def _qkv(p, b, n, cos_t, sin_t, gq, gkv, wq, wk, wvt, with_q):
    tm = min(256, n)
    tpb = n // tm
    rope_tiles = cos_t.shape[0] // tm
    head_spec = pl.BlockSpec((1, N_HEADS, tm, HEAD_W), lambda i: (i // tpb, 0, i % tpb, 0))
    tab_spec = pl.BlockSpec((tm, LANES), lambda i: ((i % tpb) % rope_tiles, 0))
    out_shape = [jax.ShapeDtypeStruct((b, N_HEADS, n, HEAD_W), BF16),
                 jax.ShapeDtypeStruct((b, N_HEADS, V_DIM, n), BF16)]
    out_specs = [head_spec,
                 pl.BlockSpec((1, N_HEADS, V_DIM, tm), lambda i: (i // tpb, 0, 0, i % tpb))]
    if with_q:
        out_shape = [jax.ShapeDtypeStruct((b, N_HEADS, n, HEAD_W), BF16)] + out_shape
        out_specs = [head_spec] + out_specs
    return pl.pallas_call(
        functools.partial(_qkv_kernel, with_q=with_q),
        grid=(b * tpb,),
        in_specs=[pl.BlockSpec((tm, LAT_W), lambda i: (i, 0)),
                  tab_spec, tab_spec,
                  pl.BlockSpec((1, Q_RANK), lambda i: (0, 0)),
                  pl.BlockSpec((1, KV_RANK), lambda i: (0, 0)),
                  pl.BlockSpec(wq.shape, lambda i: (0, 0)),
                  pl.BlockSpec(wk.shape, lambda i: (0, 0)),
                  pl.BlockSpec(wvt.shape, lambda i: (0, 0))],
        out_specs=out_specs,
        out_shape=out_shape,
        compiler_params=_params(1),
        name="qkv",
    )(p, cos_t, sin_t, gq.reshape(1, Q_RANK), gkv.reshape(1, KV_RANK), wq, wk, wvt)


def _row_fold(x, op):
    parts = [x[r:r + SUBLANES] for r in range(0, x.shape[0], SUBLANES)]
    a, b = parts[0], parts[1]
    for i in range(2, len(parts) - 1, 2):
        a, b = op(a, parts[i]), op(b, parts[i + 1])
    if len(parts) % 2:
        a = op(a, parts[-1])
    return op(a, b)


def _attn_kernel(q_ref, *refs, n_seg, kb, tq):
    k_refs, vt_refs = refs[:n_seg], refs[n_seg:2 * n_seg]
    o_ref, s_a, s_b, m_a, m_b = refs[2 * n_seg:]
    chunks = []
    off = 0
    for kr, vr in zip(k_refs, vt_refs):
        nk = kr.shape[2]
        for c0 in range(0, nk, kb):
            n = min(kb, nk - c0)
            chunks.append((kr, vr, c0, n, off))
            off += n

    tiles_per_head = q_ref.shape[2] // tq

    def head_rows(t):
        return t // tiles_per_head, pl.ds(pl.multiple_of((t % tiles_per_head) * tq, tq), tq)

    def scores(t, s_buf, m_buf):
        g, rows = head_rows(t)
        q = q_ref[0, g, rows, :]
        mp = None
        for kr, vr, c0, n, o in chunks:
            s = lax.dot_general(kr[0, g, c0:c0 + n, :], q, (((1,), (1,)), ((), ())),
                                preferred_element_type=F32)
            s_buf[o:o + n, :] = s
            m = _row_fold(s, jnp.maximum)
            mp = m if mp is None else jnp.maximum(mp, m)
        m_buf[...] = mp

    def values(t, s_buf, m_buf):
        g, rows = head_rows(t)
        mrow = jnp.max(m_buf[...], axis=0, keepdims=True)
        lp = None
        acc = None
        for kr, vr, c0, n, o in chunks:
            pr = jnp.exp2(s_buf[o:o + n, :] - mrow)
            ls = _row_fold(pr, jnp.add)
            lp = ls if lp is None else lp + ls
            pv = jnp.dot(vr[0, g, :, c0:c0 + n], pr.astype(BF16), preferred_element_type=F32)
            acc = pv if acc is None else acc + pv
        l = jnp.sum(lp, axis=0, keepdims=True)
        o_ref[0, g, rows, :] = (acc / l).T.astype(BF16)

    nt = q_ref.shape[1] * tiles_per_head
    scores(0, s_a, m_a)
    if nt > 1:
        def pair(j, carry):
            scores(2 * j + 1, s_b, m_b)
            values(2 * j, s_a, m_a)
            scores(2 * j + 2, s_a, m_a)
            values(2 * j + 1, s_b, m_b)
            return carry

        lax.fori_loop(0, nt // 2 - 1, pair, 0)
        scores(nt - 1, s_b, m_b)
        values(nt - 2, s_a, m_a)
        values(nt - 1, s_b, m_b)
    else:
        values(0, s_a, m_a)


def _attention(q, ks, vts):
    b, hh, n, _ = q.shape
    tq = min(ATTN_TQ, n)
    hg = ATTN_HEADS_PER_STEP
    n_tiles = hg * (n // tq)
    assert (n_tiles == 1 or n_tiles % 2 == 0) and hh % hg == 0
    nk_total = sum(k.shape[2] for k in ks)
    seg_spec = lambda a: pl.BlockSpec((1, hg) + a.shape[2:], lambda bi, h: (bi, h, 0, 0))
    return pl.pallas_call(
        functools.partial(_attn_kernel, n_seg=len(ks), kb=ATTN_KB, tq=tq),
        grid=(b, hh // hg),
        in_specs=[pl.BlockSpec((1, hg, n, HEAD_W), lambda bi, h: (bi, h, 0, 0))]
                 + [seg_spec(a) for a in ks] + [seg_spec(a) for a in vts],
        out_specs=pl.BlockSpec((1, hg, n, V_DIM), lambda bi, h: (bi, h, 0, 0)),
        out_shape=jax.ShapeDtypeStruct((b, hh, n, V_DIM), BF16),
        scratch_shapes=[pltpu.VMEM((nk_total, tq), F32), pltpu.VMEM((nk_total, tq), F32),
                        pltpu.VMEM((SUBLANES, tq), F32), pltpu.VMEM((SUBLANES, tq), F32)],
        compiler_params=_params(2),
        name="attention",
    )(q, *ks, *vts)


def _top2sum(a, b, c, d):
    s1, t1 = jnp.maximum(a, b), jnp.minimum(a, b)
    s2, t2 = jnp.maximum(c, d), jnp.minimum(c, d)
    return jnp.maximum(s1, s2) + jnp.maximum(jnp.minimum(s1, s2), jnp.maximum(t1, t2))


def _route(logits_t, bias):
    sc = _sigmoid(logits_t)
    sel = sc + bias
    sel_r = [sel[e:e + 1, :] for e in range(N_EXPERTS)]
    sc_r = [sc[e:e + 1, :] for e in range(N_EXPERTS)]
    epg = EXPERTS_PER_GROUP
    gs = [_top2sum(*sel_r[g * epg:(g + 1) * epg]) for g in range(N_GROUPS)]
    best, gi = gs[0], jnp.zeros(gs[0].shape, jnp.int32)
    for g in range(1, N_GROUPS):
        upd = gs[g] > best
        best = jnp.where(upd, gs[g], best)
        gi = jnp.where(upd, g, gi)

    def pick_group(rows, k):
        r = rows[k]
        for g in range(1, N_GROUPS):
            r = jnp.where(gi == g, rows[g * epg + k], r)
        return r

    in_sel = [pick_group(sel_r, k) for k in range(epg)]
    in_sc = [pick_group(sc_r, k) for k in range(epg)]
    b1, i1 = in_sel[0], jnp.zeros(gi.shape, jnp.int32)
    for k in range(1, epg):
        upd = in_sel[k] > b1
        b1 = jnp.where(upd, in_sel[k], b1)
        i1 = jnp.where(upd, k, i1)
    b2, i2 = None, None
    for k in range(epg):
        cand = jnp.where(i1 == k, -jnp.inf, in_sel[k])
        if b2 is None:
            b2, i2 = cand, jnp.zeros(gi.shape, jnp.int32)
        else:
            upd = cand > b2
            b2 = jnp.where(upd, cand, b2)
            i2 = jnp.where(upd, k, i2)

    def pick_local(idx):
        r = in_sc[0]
        for k in range(1, epg):
            r = jnp.where(idx == k, in_sc[k], r)
        return r

    s1, s2 = pick_local(i1), pick_local(i2)
    den = s1 + s2
    return gi * epg + i1, gi * epg + i2, s1 / den, s2 / den


def _mixout_kernel(y_ref, u_ref, up_ref, un_ref, gm_ref, gp_ref, wp_ref, ps_ref, wo_ref, x_ref,
                   g1_ref, gf_ref, sh2_ref, sc2_ref, wr_ref, rb_ref,
                   xo_ref, h2_ref, route_ref, *, tm, tpb, n_seq):
    i = pl.program_id(0)
    base = (i % tpb) * tm
    uc = u_ref[...]
    uext = jnp.concatenate([up_ref[...], uc, un_ref[...]], axis=0)
    r = lax.broadcasted_iota(jnp.int32, (tm, tm + 2 * POOL_HALO), 0)
    c = lax.broadcasted_iota(jnp.int32, (tm, tm + 2 * POOL_HALO), 1)
    rel = c - POOL_HALO - r
    jpos = base - POOL_HALO + c
    valid = (jpos >= 0) & (jpos < n_seq)
    tpos = base + lax.broadcasted_iota(jnp.int32, (tm, 1), 0)
    parts = []
    for g, w in enumerate(POOL_WINDOWS):
        sl = slice(g * POOL_GROUP_DIM, (g + 1) * POOL_GROUP_DIM)
        band = jnp.where(valid & (rel >= -(w // 2)) & (rel < w // 2), 1.0, 0.0).astype(BF16)
        cnt = jnp.clip(tpos + w // 2, 0, n_seq) - jnp.clip(tpos - w // 2, 0, n_seq)
        wsum = jnp.dot(band, uext[:, sl], preferred_element_type=F32)
        z = wsum / cnt.astype(F32) - uc[:, sl].astype(F32)
        parts.append(jnp.dot(z.astype(BF16), wp_ref[g], preferred_element_type=F32))
    ypool = jnp.concatenate(parts, axis=1) * ps_ref[...]
    y_mla = jnp.concatenate([y_ref[0, h] for h in range(N_HEADS)], axis=1)
    mix = gm_ref[...].astype(F32) * y_mla.astype(F32) + gp_ref[...].astype(F32) * ypool
    y = jnp.dot(mix.astype(BF16), wo_ref[...], preferred_element_type=F32)
    xn = x_ref[...] + g1_ref[0] * y
    xo_ref[...] = xn
    h2f = _rms(xn, gf_ref[...]) * (1.0 + sc2_ref[0]) + sh2_ref[0]
    h2 = h2f.astype(BF16)
    h2_ref[...] = _pack_rows(h2f)
    logits_t = lax.dot_general(wr_ref[...], h2, (((1,), (1,)), ((), ())), preferred_element_type=F32)
    e1, e2, w1, w2 = _route(logits_t, rb_ref[...])
    zero = jnp.zeros((ROUTE_ROWS - 4, tm), F32)
    route_ref[...] = jnp.concatenate([e1.astype(F32), e2.astype(F32), w1, w2, zero], axis=0)


def _mixer_out(y, p, x2, mod3, brow, n_seq, wp, ps, wo, gf, wr_t, rb):
    t, d = x2.shape
    tm = min(256, n_seq)
    tpb = n_seq // tm
    hpt = tm // POOL_HALO
    n_halo = t // POOL_HALO
    row = lambda i: (i, 0)
    const2 = lambda i: (0, 0)
    mrow = lambda i: brow(i * tm)
    return pl.pallas_call(
        functools.partial(_mixout_kernel, tm=tm, tpb=tpb, n_seq=n_seq),
        grid=(t // tm,),
        in_specs=[pl.BlockSpec((1, N_HEADS, tm, V_DIM), lambda i: (i // tpb, 0, i % tpb, 0)),
                  pl.BlockSpec((tm, POOL_DIM), lambda i: (i, LAT_W // POOL_DIM)),
                  pl.BlockSpec((POOL_HALO, POOL_DIM),
                               lambda i: (jnp.maximum(i * hpt - 1, 0), LAT_W // POOL_DIM)),
                  pl.BlockSpec((POOL_HALO, POOL_DIM),
                               lambda i: (jnp.minimum((i + 1) * hpt, n_halo - 1), LAT_W // POOL_DIM)),
                  pl.BlockSpec((tm, d), lambda i: (i, (LAT_W + POOL_DIM) // d)),
                  pl.BlockSpec((tm, d), lambda i: (i, (LAT_W + POOL_DIM) // d + 1)),
                  pl.BlockSpec(wp.shape, lambda i: (0, 0, 0)),
                  pl.BlockSpec((1, d), const2),
                  pl.BlockSpec((d, d), const2),
                  pl.BlockSpec((tm, d), row),
                  _mod_spec(2, mrow),
                  pl.BlockSpec((1, d), const2),
                  _mod_spec(3, mrow),
                  _mod_spec(4, mrow),
                  pl.BlockSpec((N_EXPERTS, d), const2),
                  pl.BlockSpec((N_EXPERTS, 1), const2)],
        out_specs=[pl.BlockSpec((tm, d), row),
                   pl.BlockSpec((tm, PACK_W), row),
                   pl.BlockSpec((ROUTE_ROWS, tm), lambda i: (0, i))],
        out_shape=[jax.ShapeDtypeStruct((t, d), F32),
                   jax.ShapeDtypeStruct((t, PACK_W), jnp.int32),
                   jax.ShapeDtypeStruct((ROUTE_ROWS, t), F32)],
        compiler_params=_params(1),
        name="mixer_out",
    )(y, p, p, p, p, p, wp, ps.reshape(1, d), wo, x2, mod3, gf.reshape(1, d), mod3, mod3, wr_t,
      rb.reshape(N_EXPERTS, 1))


def _chunk_view(x):
    n = x.shape[0]
    return (x.reshape(n // SUBLANES, SUBLANES, ROW_CHUNKS, LANES).transpose(0, 2, 1, 3)
            .reshape(n * ROW_CHUNKS, LANES))


def _row_view(c):
    n = c.shape[0] // ROW_CHUNKS
    return (c.reshape(n // SUBLANES, ROW_CHUNKS, SUBLANES, LANES).transpose(0, 2, 1, 3)
            .reshape(n, PACK_W))


def _sc_index_chunks(idx):
    r = idx.reshape(-1, 1, SUBLANES)
    j = jnp.arange(ROW_CHUNKS, dtype=jnp.int32)[None, :, None]
    ids = (r // SUBLANES) * (SUBLANES * ROW_CHUNKS) + j * SUBLANES + r % SUBLANES
    return ids.reshape(1, idx.shape[0] * ROW_CHUNKS)


def _sc_mesh():
    return plsc.VectorSubcoreMesh(core_axis_name="core", subcore_axis_name="subcore")


def _sc_gather_rows(table, idx):
    n, m = table.shape[0], idx.shape[0]
    n_idx = m * ROW_CHUNKS
    assert n_idx % SC_WINDOW == 0 and n % SUBLANES == 0 and m % SUBLANES == 0

    @pl.kernel(out_type=jax.ShapeDtypeStruct((n_idx, LANES), jnp.int32), mesh=_sc_mesh())
    def gather(t_hbm, i_hbm, o_hbm):
        def body(i_vmem, o_vmem):
            pltpu.sync_copy(t_hbm.at[i_vmem.at[0]], o_vmem)

        pltpu.emit_pipeline(
            body,
            grid=(n_idx // SC_WINDOW,),
            in_specs=[pl.BlockSpec((1, SC_WINDOW), index_map=lambda i: (0, i))],
            out_specs=[pl.BlockSpec((SC_WINDOW, LANES), index_map=lambda i: (i, 0))],
            core_axis_name=("core", "subcore"),
            dimension_semantics=(pltpu.PARALLEL,),
        )(i_hbm, o_hbm)

    return _row_view(gather(_chunk_view(table), _sc_index_chunks(idx)))


def _sc_scatter_rows(rows, idx, n_out):
    n, m = rows.shape[0], idx.shape[0]
    n_idx = m * ROW_CHUNKS
    src_steps = n * ROW_CHUNKS // SC_WINDOW
    assert n_idx % SC_WINDOW == 0 and (n * ROW_CHUNKS) % SC_WINDOW == 0 and m % n == 0
    assert n % SUBLANES == 0 and n_out % SUBLANES == 0

    @pl.kernel(out_type=jax.ShapeDtypeStruct((n_out * ROW_CHUNKS, LANES), jnp.int32), mesh=_sc_mesh())
    def scatter(x_hbm, i_hbm, o_hbm):
        def body(x_vmem, i_vmem):
            pltpu.sync_copy(x_vmem, o_hbm.at[i_vmem.at[0]])

        pltpu.emit_pipeline(
            body,
            grid=(n_idx // SC_WINDOW,),
            in_specs=[pl.BlockSpec((SC_WINDOW, LANES), index_map=lambda i: (i % src_steps, 0)),
                      pl.BlockSpec((1, SC_WINDOW), index_map=lambda i: (0, i))],
            out_specs=[],
            core_axis_name=("core", "subcore"),
            dimension_semantics=(pltpu.PARALLEL,),
        )(x_hbm, i_hbm)

    return _row_view(scatter(_chunk_view(rows), _sc_index_chunks(idx)))


def _dispatch_plan(route):
    t = route.shape[1]
    n_pairs = 2 * t
    n_tiles = n_pairs // MOE_TILE + N_EXPERTS
    blk = LANES
    e = jnp.concatenate([route[0], route[1]]).astype(jnp.int32)
    onehot = (e[:, None] == jnp.arange(N_EXPERTS, dtype=jnp.int32)[None, :]).astype(F32)
    oh3 = onehot.reshape(n_pairs // blk, blk, N_EXPERTS)
    tri = (jnp.arange(blk)[:, None] >= jnp.arange(blk)[None, :]).astype(F32)
    within = jnp.einsum("ij,bjk->bik", tri, oh3)
    totals = within[:, -1, :]
    before = jnp.cumsum(totals, axis=0) - totals
    csum = (within + before[:, None, :]).reshape(n_pairs, N_EXPERTS)
    counts = jnp.sum(totals, axis=0).astype(jnp.int32)
    padded = ((counts + MOE_TILE - 1) // MOE_TILE) * MOE_TILE
    ends = jnp.cumsum(padded)
    starts = ends - padded
    pair_slot = jnp.sum(onehot * (csum - 1.0 + starts.astype(F32)[None, :]), axis=1).astype(jnp.int32)
    tile_start = jnp.arange(n_tiles, dtype=jnp.int32) * MOE_TILE
    tile_e = jnp.minimum(jnp.sum((tile_start[:, None] >= ends[None, :]).astype(jnp.int32), axis=1), N_EXPERTS - 1)
    e_onehot = tile_e[:, None] == jnp.arange(N_EXPERTS, dtype=jnp.int32)[None, :]
    filled = jnp.sum(jnp.where(e_onehot, (starts + counts)[None, :], 0), axis=1)
    tile_rows = jnp.clip(filled - tile_start, 0, MOE_TILE).astype(jnp.int32)
    last_used = jnp.maximum(ends[-1] // MOE_TILE - 1, 0)
    tile_expert = jnp.where(tile_start < ends[-1], tile_e, tile_e[last_used]).astype(jnp.int32)
    return pair_slot, tile_expert, tile_rows


def _swiglu(xb, wg, wu, wd):
    gate = jnp.dot(xb, wg, preferred_element_type=F32)
    up = jnp.dot(xb, wu, preferred_element_type=F32)
    a = ((gate * _sigmoid(gate)) * up).astype(BF16)
    return jnp.dot(a, wd, preferred_element_type=F32)


def _ffn_kernel(te_ref, rows_ref, xs_ref, wg_ref, wu_ref, wd_ref, ys_ref, wg_s, wu_s, wd_s):
    i = pl.program_id(0)
    new_expert = jnp.logical_or(i == 0, te_ref[i] != te_ref[jnp.maximum(i - 1, 0)])

    @pl.when(new_expert)
    def _():
        wg_s[...] = wg_ref[0].astype(BF16)
        wu_s[...] = wu_ref[0].astype(BF16)
        wd_s[...] = wd_ref[0].astype(BF16)

    @pl.when(rows_ref[i] > 0)
    def _():
        live = lax.broadcasted_iota(jnp.int32, (MOE_TILE, 1), 0) < rows_ref[i]
        xb = _load_unpack(xs_ref, live).astype(BF16)
        ys_ref[...] = _pack_rows(_swiglu(xb, wg_s[...], wu_s[...], wd_s[...]))

    @pl.when(rows_ref[i] == 0)
    def _():
        ys_ref[...] = jnp.zeros_like(ys_ref)


def _expert_ffn(xs, tile_expert, tile_rows, wg, wu, wd, layer):
    n_slots = xs.shape[0]
    d = D_MODEL
    blk = pl.BlockSpec((MOE_TILE, PACK_W), lambda i, te, nr: (i, 0))
    return pl.pallas_call(
        _ffn_kernel,
        grid_spec=pltpu.PrefetchScalarGridSpec(
            num_scalar_prefetch=2,
            grid=(n_slots // MOE_TILE,),
            in_specs=[blk,
                      pl.BlockSpec((None, 1, d, D_EXPERT), lambda i, te, nr: (layer, te[i], 0, 0)),
                      pl.BlockSpec((None, 1, d, D_EXPERT), lambda i, te, nr: (layer, te[i], 0, 0)),
                      pl.BlockSpec((None, 1, D_EXPERT, d), lambda i, te, nr: (layer, te[i], 0, 0))],
            out_specs=blk,
            scratch_shapes=[pltpu.VMEM((d, D_EXPERT), BF16), pltpu.VMEM((d, D_EXPERT), BF16),
                            pltpu.VMEM((D_EXPERT, d), BF16)]),
        out_shape=jax.ShapeDtypeStruct(xs.shape, jnp.int32),
        compiler_params=_params(1),
        name="expert_ffn",
    )(tile_expert, tile_rows, xs, wg, wu, wd)


def _shared_kernel(h_ref, sg_ref, su_ref, sd_ref, o_ref):
    hb = _load_unpack(h_ref).astype(BF16)
    o_ref[...] = _swiglu(hb, sg_ref[...], su_ref[...], sd_ref[...]).astype(BF16)


def _shared_ffn(h2p, sg, su, sd):
    t = h2p.shape[0]
    tm = min(512, t)
    const2 = lambda i: (0, 0)
    return pl.pallas_call(
        _shared_kernel,
        grid=(t // tm,),
        in_specs=[pl.BlockSpec((tm, PACK_W), lambda i: (i, 0)),
                  pl.BlockSpec(sg.shape, const2), pl.BlockSpec(su.shape, const2), pl.BlockSpec(sd.shape, const2)],
        out_specs=pl.BlockSpec((tm, D_MODEL), lambda i: (i, 0)),
        out_shape=jax.ShapeDtypeStruct((t, D_MODEL), BF16),
        compiler_params=_params(1),
        name="shared_ffn",
    )(h2p, sg, su, sd)


def _combine_kernel(ysh_ref, y1_ref, y2_ref, w_ref, x_ref, g2_ref, gfin_ref, o_ref, *, final):
    w = w_ref[...]
    y = ysh_ref[...].astype(F32) + w[:, 0:1] * _load_unpack(y1_ref) + w[:, 1:2] * _load_unpack(y2_ref)
    xn = x_ref[...] + g2_ref[0] * y
    if final:
        xn = _rms(xn, gfin_ref[...])
    o_ref[...] = xn


def _moe_combine(ysh, yg, w12, x2, row_off, mod3, brow, gfin, final):
    t, d = x2.shape
    tm = min(512, t)
    t_all = ysh.shape[0]
    assert row_off % tm == 0 and t_all % tm == 0
    off = row_off // tm
    row = lambda i: (i, 0)
    pk = lambda o: pl.BlockSpec((tm, PACK_W), lambda i: (i + o, 0))
    return pl.pallas_call(
        functools.partial(_combine_kernel, final=final),
        grid=(t // tm,),
        in_specs=[pl.BlockSpec((tm, d), lambda i: (i + off, 0)), pk(off), pk(off + t_all // tm),
                  pl.BlockSpec((tm, LANES), lambda i: (i + off, 0)),
                  pl.BlockSpec((tm, d), row),
                  _mod_spec(5, lambda i: brow(i * tm)),
                  pl.BlockSpec((1, d), lambda i: (0, 0))],
        out_specs=pl.BlockSpec((tm, d), row),
        out_shape=jax.ShapeDtypeStruct((t, d), F32),
        compiler_params=_params(1),
        name="moe_combine",
    )(ysh, yg, yg, w12, x2, mod3, gfin.reshape(1, d))


def _moe_routed(parts, wg, wu, wd, layer, sg, su, sd, mod3, gfin, final):
    h2p = parts[0][0] if len(parts) == 1 else jnp.concatenate([pt[0] for pt in parts], axis=0)
    route = parts[0][1] if len(parts) == 1 else jnp.concatenate([pt[1] for pt in parts], axis=1)
    t_all = h2p.shape[0]
    pair_slot, tile_expert, tile_rows = _dispatch_plan(route)
    n_slots = tile_expert.shape[0] * MOE_TILE
    xs = _sc_scatter_rows(h2p, pair_slot, n_slots)
    ysh = _shared_ffn(h2p, sg, su, sd)
    ys = _expert_ffn(xs, tile_expert, tile_rows, wg, wu, wd, layer)
    yg = _sc_gather_rows(ys, pair_slot)
    w12 = jnp.concatenate([route[2:4].T, jnp.zeros((t_all, LANES - 2), F32)], axis=1)
    outs, row_off = [], 0
    for _, _, x2, brow in parts:
        outs.append(_moe_combine(ysh, yg, w12, x2, row_off, mod3, brow, gfin, final))
        row_off += x2.shape[0]
    return outs


def _rope_tables(seq):
    rows = seq // GRID_W
    row = jnp.repeat(jnp.arange(rows), GRID_W).astype(F32)
    col = jnp.tile(jnp.arange(GRID_W), rows).astype(F32)
    axis_dim = QK_ROPE // 2
    inv = ROPE_THETA ** (-jnp.arange(0, axis_dim, 2, dtype=F32) / axis_dim)
    ang = jnp.concatenate([row[:, None] * inv, col[:, None] * inv], axis=-1)
    cos, sin = jnp.cos(ang), jnp.sin(ang)
    zero = jnp.zeros_like(cos)
    return (jnp.concatenate([cos, cos, zero, zero], axis=1),
            jnp.concatenate([-sin, sin, zero, zero], axis=1))


def _rope_group(w, start):
    half = QK_ROPE // 2
    x1, x2 = w[:, start:start + half], w[:, start + half:start + 2 * half]
    return jnp.concatenate([x1, x2, x2, x1], axis=1)


def _relayout_w_in(w):
    return jnp.concatenate([w[:, :KV_RANK], w[:, KV_COLS:Q_END], _rope_group(w, KV_RANK),
                            w[:, Q_END:]], axis=1).astype(BF16)


def _relayout_w_qb(w):
    dq = QK_NOPE + QK_ROPE
    cols = []
    for h in range(N_HEADS):
        cols += [w[:, h * dq:h * dq + QK_NOPE], _rope_group(w, h * dq + QK_NOPE)]
    return jnp.concatenate(cols, axis=1).astype(BF16)


def kernel(x, c, ctx, c_ctx, w_ada, b_ada, norm_mix_g, norm_ffn_g, w_in, q_norm_g, kv_norm_g, w_qb, w_kvb, w_pool, pool_scale, w_out, w_router, router_bias, w_exp_gate, w_exp_up, w_exp_down, w_sh_gate, w_sh_up, w_sh_down, final_norm_g):
    bsz, seq, d = x.shape
    n_ctx = ctx.shape[1]
    depth = w_ada.shape[0]
    assert bsz < MOD_ROWS and d == D_MODEL

    c_rows = jnp.concatenate([c, c_ctx[None], jnp.zeros((MOD_ROWS - bsz - 1, d), F32)], axis=0)
    mod = _ada_mod(c_rows, w_ada, b_ada)

    cos_t, sin_t = _rope_tables(seq)
    ones_t = jnp.concatenate([jnp.ones((n_ctx, 2 * (QK_ROPE // 2)), F32),
                              jnp.zeros((n_ctx, LANES - QK_ROPE), F32)], axis=1)
    zeros_t = jnp.zeros((n_ctx, LANES), F32)
    wr_t = w_router.T.astype(BF16)
    lat_row = lambda r: r // seq
    ctx_row = lambda r: bsz

    x2 = x.reshape(bsz * seq, d)
    xc2 = ctx.reshape(bsz * n_ctx, d)
    for l in range(depth):
        last = l == depth - 1
        mod3 = mod[l].reshape(MOD_ROWS * N_MOD, 1, d)
        w_in_r = _relayout_w_in(w_in[l])
        wq = _relayout_w_qb(w_qb[l])
        w_kv3 = w_kvb[l].reshape(KV_RANK, N_HEADS, QK_NOPE + V_DIM)
        wk = w_kv3[:, :, :QK_NOPE].reshape(KV_RANK, N_HEADS * QK_NOPE).astype(BF16)
        wvt = (w_kv3[:, :, QK_NOPE:].transpose(1, 2, 0).reshape(N_HEADS * V_DIM, KV_RANK)
               .astype(BF16))
        wp = w_pool[l].astype(BF16)
        wo = w_out[l].astype(BF16)
        sg, su, sd = w_sh_gate[l].astype(BF16), w_sh_up[l].astype(BF16), w_sh_down[l].astype(BF16)
        moe = functools.partial(_moe_routed, wg=w_exp_gate, wu=w_exp_up, wd=w_exp_down, layer=l,
                                sg=sg, su=su, sd=sd, mod3=mod3, gfin=final_norm_g)

        p = _in_proj(x2, mod3, lat_row, norm_mix_g[l], w_in_r, P_COLS)
        pc = _in_proj(xc2, mod3, ctx_row, norm_mix_g[l], w_in_r, LAT_W if last else P_COLS)
        q, k_lat, v_lat = _qkv(p, bsz, seq, cos_t, sin_t, q_norm_g[l], kv_norm_g[l], wq, wk, wvt, True)
        ctx_out = _qkv(pc, bsz, n_ctx, ones_t, zeros_t, q_norm_g[l], kv_norm_g[l], wq, wk, wvt, not last)
        k_ctx, v_ctx = ctx_out[-2:]
        y = _attention(q, [k_ctx, k_lat], [v_ctx, v_lat])
        x2, h2p, route = _mixer_out(y, p, x2, mod3, lat_row, seq, wp, pool_scale[l], wo,
                                    norm_ffn_g[l], wr_t, router_bias)
        if last:
            x2, = moe([(h2p, route, x2, lat_row)], final=True)
        else:
            yc = _attention(ctx_out[0], [k_ctx], [v_ctx])
            xc2, h2c, route_c = _mixer_out(yc, pc, xc2, mod3, ctx_row, n_ctx, wp, pool_scale[l], wo,
                                           norm_ffn_g[l], wr_t, router_bias)
            x2, xc2 = moe([(h2p, route, x2, lat_row), (h2c, route_c, xc2, ctx_row)], final=False)
    return x2.reshape(bsz, seq, d)
```

```python
import functools
import math

import jax
import jax.numpy as jnp
from jax import lax
from jax.experimental import pallas as pl
from jax.experimental.pallas import tpu as pltpu
from jax.experimental.pallas import tpu_sc as plsc

F32 = jnp.float32
BF16 = jnp.bfloat16

D_MODEL = 2048
GRID_W = 64
N_HEADS = 16
QK_NOPE = 128
QK_ROPE = 64
V_DIM = 128
Q_RANK = 384
KV_RANK = 512
ROPE_THETA = 10000.0
POOL_DIM = 1024
POOL_GROUPS = 4
POOL_GROUP_DIM = 256
POOL_WINDOWS = (2, 4, 8, 16)
POOL_GROUP_OUT = 512
KV_COLS = KV_RANK + QK_ROPE
Q_END = KV_COLS + Q_RANK
POOL_END = Q_END + POOL_DIM
N_EXPERTS = 16
N_GROUPS = 4
EXPERTS_PER_GROUP = 4
D_EXPERT = 512
EPS = 1e-6
N_MOD = 6

LANES = 128
HEAD_W = 2 * LANES
LAT_W = 1024
P_COLS = LAT_W + POOL_DIM + 2 * D_MODEL
POOL_HALO = 16
MOD_ROWS = 8
Q_SCALE = (1.0 / math.sqrt(QK_NOPE + QK_ROPE)) * math.log2(math.e)
VMEM_LIMIT = 56 * 1024 * 1024
SUBLANES = 8
PACK_W = D_MODEL // 2
ROW_CHUNKS = PACK_W // LANES
ROUTE_ROWS = SUBLANES
MOE_TILE = 512
SC_WINDOW = 128
ATTN_TQ = 256
ATTN_KB = 512
V_HEAD_GROUP = 8
ATTN_HEADS_PER_STEP = 1
MIX_CHAIN_ROWS = 256


def _sigmoid(x):
    return 1.0 / (1.0 + jnp.exp(-x))


def _pack_rows(y):
    half = y.shape[1] // 2
    return pltpu.pack_elementwise([y[:, :half], y[:, half:]], packed_dtype=BF16)


def _load_unpack(ref, live=None):
    w = ref[...]
    if live is not None:
        w = jnp.where(live, w, 0)
    lo = pltpu.unpack_elementwise(w, index=0, packed_dtype=BF16, unpacked_dtype=F32)
    hi = pltpu.unpack_elementwise(w, index=1, packed_dtype=BF16, unpacked_dtype=F32)
    return jnp.concatenate([lo, hi], axis=1)


def _rms(xf, g):
    ms = jnp.mean(xf * xf, axis=-1, keepdims=True)
    return xf * lax.rsqrt(ms + EPS) * g


def _params(n_axes):
    return pltpu.CompilerParams(dimension_semantics=("arbitrary",) * n_axes,
                                vmem_limit_bytes=VMEM_LIMIT)


def _ada_kernel(c_ref, w_ref, b_ref, o_ref):
    cf = c_ref[...]
    a = (cf * _sigmoid(cf)).astype(BF16)
    o_ref[0] = jnp.dot(a, w_ref[0].astype(BF16), preferred_element_type=F32) + b_ref[0]


def _ada_mod(c_rows, w_ada, b_ada):
    depth, d, n = w_ada.shape
    tn = 1024
    return pl.pallas_call(
        _ada_kernel,
        grid=(depth, n // tn),
        in_specs=[pl.BlockSpec((MOD_ROWS, d), lambda l, j: (0, 0)),
                  pl.BlockSpec((1, d, tn), lambda l, j: (l, 0, j)),
                  pl.BlockSpec((1, 1, tn), lambda l, j: (l, 0, j))],
        out_specs=pl.BlockSpec((1, MOD_ROWS, tn), lambda l, j: (l, 0, j)),
        out_shape=jax.ShapeDtypeStruct((depth, MOD_ROWS, n), F32),
        compiler_params=_params(2),
        name="ada_mod",
    )(c_rows, w_ada, b_ada.reshape(depth, 1, n))


def _mod_spec(k, brow):
    return pl.BlockSpec((1, 1, D_MODEL), lambda i, *_: (brow(i) * N_MOD + k, 0, 0))


def _inproj_kernel(x_ref, g_ref, sh_ref, sc_ref, w_ref, o_ref, h_scr, *, tn):
    y = _rms(x_ref[...], g_ref[...])
    h_scr[...] = (y * (1.0 + sc_ref[0]) + sh_ref[0]).astype(BF16)
    for c0 in range(0, o_ref.shape[1], tn):
        acc = jnp.dot(h_scr[...], w_ref[:, c0:c0 + tn], preferred_element_type=F32)
        if c0 >= LAT_W + POOL_DIM:
            acc = _sigmoid(acc)
        o_ref[:, c0:c0 + tn] = acc.astype(BF16)


def _in_proj(x2, mod3, brow, g, w, n_cols):
    t, d = x2.shape
    tm = min(512, t)
    return pl.pallas_call(
        functools.partial(_inproj_kernel, tn=512),
        grid=(t // tm,),
        in_specs=[pl.BlockSpec((tm, d), lambda i: (i, 0)),
                  pl.BlockSpec((1, d), lambda i: (0, 0)),
                  _mod_spec(0, lambda i: brow(i * tm)),
                  _mod_spec(1, lambda i: brow(i * tm)),
                  pl.BlockSpec((d, n_cols), lambda i: (0, 0), pipeline_mode=pl.Buffered(1))],
        out_specs=pl.BlockSpec((tm, n_cols), lambda i: (i, 0)),
        out_shape=jax.ShapeDtypeStruct((t, n_cols), BF16),
        scratch_shapes=[pltpu.VMEM((tm, d), BF16)],
        compiler_params=_params(1),
        name="in_proj",
    )(x2, g.reshape(1, d), mod3, mod3, w)


def _qkv_kernel(lat_ref, cos_ref, sin_ref, gq_ref, gkv_ref, wq_ref, wk_ref, wvt_ref, *out_refs, with_q):
    if with_q:
        q_ref, k_ref, v_ref = out_refs
    else:
        k_ref, v_ref = out_refs
    cos = cos_ref[...]
    sin = sin_ref[...]

    def rope(grp):
        return grp * cos + pltpu.roll(grp, 2 * (QK_ROPE // 2), axis=1) * sin

    kvn = _rms(lat_ref[:, :KV_RANK].astype(F32), gkv_ref[...]).astype(BF16)
    kpe = rope(lat_ref[:, KV_RANK + Q_RANK:].astype(F32)).astype(BF16)
    for h in range(0, N_HEADS, 2):
        kn = jnp.dot(kvn, wk_ref[:, h * QK_NOPE:(h + 2) * QK_NOPE], preferred_element_type=F32)
        for hh in range(2):
            k_ref[0, h + hh, :, :QK_NOPE] = kn[:, hh * QK_NOPE:(hh + 1) * QK_NOPE].astype(BF16)
            k_ref[0, h + hh, :, QK_NOPE:] = kpe
    for h in range(0, N_HEADS, V_HEAD_GROUP):
        vt = lax.dot_general(wvt_ref[h * V_DIM:(h + V_HEAD_GROUP) * V_DIM, :], kvn,
                             (((1,), (1,)), ((), ())), preferred_element_type=F32)
        for hh in range(V_HEAD_GROUP):
            v_ref[0, h + hh] = vt[hh * V_DIM:(hh + 1) * V_DIM, :].astype(BF16)
    if with_q:
        qn = _rms(lat_ref[:, KV_RANK:KV_RANK + Q_RANK].astype(F32), gq_ref[...]).astype(BF16)
        for h in range(N_HEADS):
            qh = jnp.dot(qn, wq_ref[:, h * HEAD_W:(h + 1) * HEAD_W], preferred_element_type=F32)
            q_ref[0, h, :, :QK_NOPE] = (qh[:, :QK_NOPE] * Q_SCALE).astype(BF16)
            q_ref[0, h, :, QK_NOPE:] = (rope(qh[:, QK_NOPE:]) * Q_SCALE).astype(BF16)


def _qkv(p, b, n, cos_t, sin_t, gq, gkv, wq, wk, wvt, with_q):
    tm = min(256, n)
    tpb = n // tm
    rope_tiles = cos_t.shape[0] // tm
    head_spec = pl.BlockSpec((1, N_HEADS, tm, HEAD_W), lambda i: (i // tpb, 0, i % tpb, 0))
    tab_spec = pl.BlockSpec((tm, LANES), lambda i: ((i % tpb) % rope_tiles, 0))
    out_shape = [jax.ShapeDtypeStruct((b, N_HEADS, n, HEAD_W), BF16),
                 jax.ShapeDtypeStruct((b, N_HEADS, V_DIM, n), BF16)]
    out_specs = [head_spec,
                 pl.BlockSpec((1, N_HEADS, V_DIM, tm), lambda i: (i // tpb, 0, 0, i % tpb))]
    if with_q:
        out_shape = [jax.ShapeDtypeStruct((b, N_HEADS, n, HEAD_W), BF16)] + out_shape
        out_specs = [head_spec] + out_specs
    return pl.pallas_call(
        functools.partial(_qkv_kernel, with_q=with_q),
        grid=(b * tpb,),
        in_specs=[pl.BlockSpec((tm, LAT_W), lambda i: (i, 0)),
                  tab_spec, tab_spec,
                  pl.BlockSpec((1, Q_RANK), lambda i: (0, 0)),
                  pl.BlockSpec((1, KV_RANK), lambda i: (0, 0)),
                  pl.BlockSpec(wq.shape, lambda i: (0, 0)),
                  pl.BlockSpec(wk.shape, lambda i: (0, 0)),
                  pl.BlockSpec(wvt.shape, lambda i: (0, 0))],
        out_specs=out_specs,
        out_shape=out_shape,
        compiler_params=_params(1),
        name="qkv",
    )(p, cos_t, sin_t, gq.reshape(1, Q_RANK), gkv.reshape(1, KV_RANK), wq, wk, wvt)


def _row_fold(x, op):
    parts = [x[r:r + SUBLANES] for r in range(0, x.shape[0], SUBLANES)]
    a, b = parts[0], parts[1]
    for i in range(2, len(parts) - 1, 2):
        a, b = op(a, parts[i]), op(b, parts[i + 1])
    if len(parts) % 2:
        a = op(a, parts[-1])
    return op(a, b)


def _attn_kernel(q_ref, *refs, n_seg, kb, tq):
    k_refs, vt_refs = refs[:n_seg], refs[n_seg:2 * n_seg]
    o_ref, s_a, s_b, m_a, m_b = refs[2 * n_seg:]
    chunks = []
    off = 0
    for kr, vr in zip(k_refs, vt_refs):
        nk = kr.shape[2]
        for c0 in range(0, nk, kb):
            n = min(kb, nk - c0)
            chunks.append((kr, vr, c0, n, off))
            off += n

    tiles_per_head = q_ref.shape[2] // tq

    def head_rows(t):
        if q_ref.shape[1] == 1:
            return 0, pl.ds(pl.multiple_of(t * tq, tq), tq)
        return t // tiles_per_head, pl.ds(pl.multiple_of((t % tiles_per_head) * tq, tq), tq)

    def scores(t, s_buf, m_buf):
        g, rows = head_rows(t)
        q = q_ref[0, g, rows, :]
        mp = None
        for kr, vr, c0, n, o in chunks:
            s = lax.dot_general(kr[0, g, c0:c0 + n, :], q, (((1,), (1,)), ((), ())),
                                preferred_element_type=F32)
            s_buf[o:o + n, :] = s
            m = _row_fold(s, jnp.maximum)
            mp = m if mp is None else jnp.maximum(mp, m)
        m_buf[...] = mp

    def values(t, s_buf, m_buf):
        g, rows = head_rows(t)
        mrow = jnp.max(m_buf[...], axis=0, keepdims=True)
        lp = None
        acc = None
        for kr, vr, c0, n, o in chunks:
            pr = jnp.exp2(s_buf[o:o + n, :] - mrow)
            ls = _row_fold(pr, jnp.add)
            lp = ls if lp is None else lp + ls
            pv = jnp.dot(vr[0, g, :, c0:c0 + n], pr.astype(BF16), preferred_element_type=F32)
            acc = pv if acc is None else acc + pv
        l = jnp.sum(lp, axis=0, keepdims=True)
        o_ref[0, g, rows, :] = (acc / l).T.astype(BF16)

    nt = q_ref.shape[1] * tiles_per_head
    scores(0, s_a, m_a)
    if nt > 1:
        def pair(j, carry):
            scores(2 * j + 1, s_b, m_b)
            values(2 * j, s_a, m_a)
            scores(2 * j + 2, s_a, m_a)
            values(2 * j + 1, s_b, m_b)
            return carry

        lax.fori_loop(0, nt // 2 - 1, pair, 0)
        scores(nt - 1, s_b, m_b)
        values(nt - 2, s_a, m_a)
        values(nt - 1, s_b, m_b)
    else:
        values(0, s_a, m_a)


def _attention(q, ks, vts):
    b, hh, n, _ = q.shape
    tq = min(ATTN_TQ, n)
    hg = ATTN_HEADS_PER_STEP
    n_tiles = hg * (n // tq)
    assert (n_tiles == 1 or n_tiles % 2 == 0) and hh % hg == 0
    nk_total = sum(k.shape[2] for k in ks)
    seg_spec = lambda a: pl.BlockSpec((1, hg) + a.shape[2:], lambda bi, h: (bi, h, 0, 0))
    return pl.pallas_call(
        functools.partial(_attn_kernel, n_seg=len(ks), kb=ATTN_KB, tq=tq),
        grid=(b, hh // hg),
        in_specs=[pl.BlockSpec((1, hg, n, HEAD_W), lambda bi, h: (bi, h, 0, 0))]
                 + [seg_spec(a) for a in ks] + [seg_spec(a) for a in vts],
        out_specs=pl.BlockSpec((1, hg, n, V_DIM), lambda bi, h: (bi, h, 0, 0)),
        out_shape=jax.ShapeDtypeStruct((b, hh, n, V_DIM), BF16),
        scratch_shapes=[pltpu.VMEM((nk_total, tq), F32), pltpu.VMEM((nk_total, tq), F32),
                        pltpu.VMEM((SUBLANES, tq), F32), pltpu.VMEM((SUBLANES, tq), F32)],
        compiler_params=_params(2),
        name="attention",
    )(q, *ks, *vts)


def _top2sum(a, b, c, d):
    s1, t1 = jnp.maximum(a, b), jnp.minimum(a, b)
    s2, t2 = jnp.maximum(c, d), jnp.minimum(c, d)
    return jnp.maximum(s1, s2) + jnp.maximum(jnp.minimum(s1, s2), jnp.maximum(t1, t2))


def _route(logits_t, bias):
    sc = _sigmoid(logits_t)
    sel = sc + bias
    sel_r = [sel[e:e + 1, :] for e in range(N_EXPERTS)]
    sc_r = [sc[e:e + 1, :] for e in range(N_EXPERTS)]
    epg = EXPERTS_PER_GROUP
    gs = [_top2sum(*sel_r[g * epg:(g + 1) * epg]) for g in range(N_GROUPS)]
    best, gi = gs[0], jnp.zeros(gs[0].shape, jnp.int32)
    for g in range(1, N_GROUPS):
        upd = gs[g] > best
        best = jnp.where(upd, gs[g], best)
        gi = jnp.where(upd, g, gi)

    def pick_group(rows, k):
        r = rows[k]
        for g in range(1, N_GROUPS):
            r = jnp.where(gi == g, rows[g * epg + k], r)
        return r

    in_sel = [pick_group(sel_r, k) for k in range(epg)]
    in_sc = [pick_group(sc_r, k) for k in range(epg)]
    b1, i1 = in_sel[0], jnp.zeros(gi.shape, jnp.int32)
    for k in range(1, epg):
        upd = in_sel[k] > b1
        b1 = jnp.where(upd, in_sel[k], b1)
        i1 = jnp.where(upd, k, i1)
    b2, i2 = None, None
    for k in range(epg):
        cand = jnp.where(i1 == k, -jnp.inf, in_sel[k])
        if b2 is None:
            b2, i2 = cand, jnp.zeros(gi.shape, jnp.int32)
        else:
            upd = cand > b2
            b2 = jnp.where(upd, cand, b2)
            i2 = jnp.where(upd, k, i2)

    def pick_local(idx):
        r = in_sc[0]
        for k in range(1, epg):
            r = jnp.where(idx == k, in_sc[k], r)
        return r

    s1, s2 = pick_local(i1), pick_local(i2)
    den = s1 + s2
    return gi * epg + i1, gi * epg + i2, s1 / den, s2 / den


def _mixout_kernel(y_ref, u_ref, up_ref, un_ref, gm_ref, gp_ref, wp_ref, ps_ref, wo_ref, x_ref,
                   g1_ref, gf_ref, sh2_ref, sc2_ref, wr_ref, rb_ref,
                   xo_ref, h2_ref, route_ref, *, tm, tpb, n_seq):
    i = pl.program_id(0)
    uext = jnp.concatenate([up_ref[...], u_ref[...], un_ref[...]], axis=0)
    sub = min(MIX_CHAIN_ROWS, tm)

    def mix_stage(r0):
        rows = slice(r0, r0 + sub)
        base = (i % tpb) * tm + r0
        r = lax.broadcasted_iota(jnp.int32, (sub, sub + 2 * POOL_HALO), 0)
        c = lax.broadcasted_iota(jnp.int32, (sub, sub + 2 * POOL_HALO), 1)
        rel = c - POOL_HALO - r
        jpos = base - POOL_HALO + c
        valid = (jpos >= 0) & (jpos < n_seq)
        tpos = base + lax.broadcasted_iota(jnp.int32, (sub, 1), 0)
        uwin = uext[r0:r0 + sub + 2 * POOL_HALO]
        uc = uext[r0 + POOL_HALO:r0 + POOL_HALO + sub]
        parts = []
        for g, w in enumerate(POOL_WINDOWS):
            sl = slice(g * POOL_GROUP_DIM, (g + 1) * POOL_GROUP_DIM)
            band = jnp.where(valid & (rel >= -(w // 2)) & (rel < w // 2), 1.0, 0.0).astype(BF16)
            cnt = jnp.clip(tpos + w // 2, 0, n_seq) - jnp.clip(tpos - w // 2, 0, n_seq)
            wsum = jnp.dot(band, uwin[:, sl], preferred_element_type=F32)
            z = wsum / cnt.astype(F32) - uc[:, sl].astype(F32)
            parts.append(jnp.dot(z.astype(BF16), wp_ref[g], preferred_element_type=F32))
        ypool = jnp.concatenate(parts, axis=1) * ps_ref[...]
        y_mla = jnp.concatenate([y_ref[0, h, rows, :] for h in range(N_HEADS)], axis=1)
        mix = gm_ref[rows, :].astype(F32) * y_mla.astype(F32) + gp_ref[rows, :].astype(F32) * ypool
        return jnp.dot(mix.astype(BF16), wo_ref[...], preferred_element_type=F32)

    def norm_route_stage(r0, y):
        rows = slice(r0, r0 + sub)
        xn = x_ref[rows, :] + g1_ref[0] * y
        xo_ref[rows, :] = xn
        h2f = _rms(xn, gf_ref[...]) * (1.0 + sc2_ref[0]) + sh2_ref[0]
        h2_ref[rows, :] = _pack_rows(h2f)
        logits_t = lax.dot_general(wr_ref[...], h2f.astype(BF16), (((1,), (1,)), ((), ())),
                                   preferred_element_type=F32)
        e1, e2, w1, w2 = _route(logits_t, rb_ref[...])
        zero = jnp.zeros((ROUTE_ROWS - 4, sub), F32)
        route_ref[:, rows] = jnp.concatenate([e1.astype(F32), e2.astype(F32), w1, w2, zero], axis=0)

    starts = list(range(0, tm, sub))
    pending = None
    for r0 in starts:
        y = mix_stage(r0)
        if pending is not None:
            norm_route_stage(*pending)
        pending = (r0, y)
    norm_route_stage(*pending)


def _mixer_out(y, p, x2, mod3, brow, n_seq, wp, ps, wo, gf, wr_t, rb):
    t, d = x2.shape
    tm = min(2 * MIX_CHAIN_ROWS, n_seq)
    tpb = n_seq // tm
    hpt = tm // POOL_HALO
    n_halo = t // POOL_HALO
    row = lambda i: (i, 0)
    const2 = lambda i: (0, 0)
    mrow = lambda i: brow(i * tm)
    return pl.pallas_call(
        functools.partial(_mixout_kernel, tm=tm, tpb=tpb, n_seq=n_seq),
        grid=(t // tm,),
        in_specs=[pl.BlockSpec((1, N_HEADS, tm, V_DIM), lambda i: (i // tpb, 0, i % tpb, 0)),
                  pl.BlockSpec((tm, POOL_DIM), lambda i: (i, LAT_W // POOL_DIM)),
                  pl.BlockSpec((POOL_HALO, POOL_DIM),
                               lambda i: (jnp.maximum(i * hpt - 1, 0), LAT_W // POOL_DIM)),
                  pl.BlockSpec((POOL_HALO, POOL_DIM),
                               lambda i: (jnp.minimum((i + 1) * hpt, n_halo - 1), LAT_W // POOL_DIM)),
                  pl.BlockSpec((tm, d), lambda i: (i, (LAT_W + POOL_DIM) // d)),
                  pl.BlockSpec((tm, d), lambda i: (i, (LAT_W + POOL_DIM) // d + 1)),
                  pl.BlockSpec(wp.shape, lambda i: (0, 0, 0)),
                  pl.BlockSpec((1, d), const2),
                  pl.BlockSpec((d, d), const2, pipeline_mode=pl.Buffered(1)),
                  pl.BlockSpec((tm, d), row),
                  _mod_spec(2, mrow),
                  pl.BlockSpec((1, d), const2),
                  _mod_spec(3, mrow),
                  _mod_spec(4, mrow),
                  pl.BlockSpec((N_EXPERTS, d), const2),
                  pl.BlockSpec((N_EXPERTS, 1), const2)],
        out_specs=[pl.BlockSpec((tm, d), row),
                   pl.BlockSpec((tm, PACK_W), row),
                   pl.BlockSpec((ROUTE_ROWS, tm), lambda i: (0, i))],
        out_shape=[jax.ShapeDtypeStruct((t, d), F32),
                   jax.ShapeDtypeStruct((t, PACK_W), jnp.int32),
                   jax.ShapeDtypeStruct((ROUTE_ROWS, t), F32)],
        compiler_params=_params(1),
        name="mixer_out",
    )(y, p, p, p, p, p, wp, ps.reshape(1, d), wo, x2, mod3, gf.reshape(1, d), mod3, mod3, wr_t,
      rb.reshape(N_EXPERTS, 1))


def _chunk_view(x):
    n = x.shape[0]
    return (x.reshape(n // SUBLANES, SUBLANES, ROW_CHUNKS, LANES).transpose(0, 2, 1, 3)
            .reshape(n * ROW_CHUNKS, LANES))


def _row_view(c):
    n = c.shape[0] // ROW_CHUNKS
    return (c.reshape(n // SUBLANES, ROW_CHUNKS, SUBLANES, LANES).transpose(0, 2, 1, 3)
            .reshape(n, PACK_W))


def _sc_index_chunks(idx):
    r = idx.reshape(-1, 1, SUBLANES)
    j = jnp.arange(ROW_CHUNKS, dtype=jnp.int32)[None, :, None]
    ids = (r // SUBLANES) * (SUBLANES * ROW_CHUNKS) + j * SUBLANES + r % SUBLANES
    return ids.reshape(1, idx.shape[0] * ROW_CHUNKS)


def _sc_mesh():
    return plsc.VectorSubcoreMesh(core_axis_name="core", subcore_axis_name="subcore")


def _sc_gather_rows(table, idx):
    n, m = table.shape[0], idx.shape[0]
    n_idx = m * ROW_CHUNKS
    assert n_idx % SC_WINDOW == 0 and n % SUBLANES == 0 and m % SUBLANES == 0

    @pl.kernel(out_type=jax.ShapeDtypeStruct((n_idx, LANES), jnp.int32), mesh=_sc_mesh())
    def gather(t_hbm, i_hbm, o_hbm):
        def body(i_vmem, o_vmem):
            pltpu.sync_copy(t_hbm.at[i_vmem.at[0]], o_vmem)

        pltpu.emit_pipeline(
            body,
            grid=(n_idx // SC_WINDOW,),
            in_specs=[pl.BlockSpec((1, SC_WINDOW), index_map=lambda i: (0, i))],
            out_specs=[pl.BlockSpec((SC_WINDOW, LANES), index_map=lambda i: (i, 0))],
            core_axis_name=("core", "subcore"),
            dimension_semantics=(pltpu.PARALLEL,),
        )(i_hbm, o_hbm)

    return _row_view(gather(_chunk_view(table), _sc_index_chunks(idx)))


def _sc_scatter_rows(rows, idx, n_out):
    n, m = rows.shape[0], idx.shape[0]
    n_idx = m * ROW_CHUNKS
    src_steps = n * ROW_CHUNKS // SC_WINDOW
    assert n_idx % SC_WINDOW == 0 and (n * ROW_CHUNKS) % SC_WINDOW == 0 and m % n == 0
    assert n % SUBLANES == 0 and n_out % SUBLANES == 0

    @pl.kernel(out_type=jax.ShapeDtypeStruct((n_out * ROW_CHUNKS, LANES), jnp.int32), mesh=_sc_mesh())
    def scatter(x_hbm, i_hbm, o_hbm):
        def body(x_vmem, i_vmem):
            pltpu.sync_copy(x_vmem, o_hbm.at[i_vmem.at[0]])

        pltpu.emit_pipeline(
            body,
            grid=(n_idx // SC_WINDOW,),
            in_specs=[pl.BlockSpec((SC_WINDOW, LANES), index_map=lambda i: (i % src_steps, 0)),
                      pl.BlockSpec((1, SC_WINDOW), index_map=lambda i: (0, i))],
            out_specs=[],
            core_axis_name=("core", "subcore"),
            dimension_semantics=(pltpu.PARALLEL,),
        )(x_hbm, i_hbm)

    return _row_view(scatter(_chunk_view(rows), _sc_index_chunks(idx)))


def _dispatch_plan(route):
    t = route.shape[1]
    n_pairs = 2 * t
    n_tiles = n_pairs // MOE_TILE + N_EXPERTS
    blk = LANES
    e = jnp.concatenate([route[0], route[1]]).astype(jnp.int32)
    onehot = (e[:, None] == jnp.arange(N_EXPERTS, dtype=jnp.int32)[None, :]).astype(F32)
    oh3 = onehot.reshape(n_pairs // blk, blk, N_EXPERTS)
    tri = (jnp.arange(blk)[:, None] >= jnp.arange(blk)[None, :]).astype(F32)
    within = jnp.einsum("ij,bjk->bik", tri, oh3)
    totals = within[:, -1, :]
    before = jnp.cumsum(totals, axis=0) - totals
    csum = (within + before[:, None, :]).reshape(n_pairs, N_EXPERTS)
    counts = jnp.sum(totals, axis=0).astype(jnp.int32)
    padded = ((counts + MOE_TILE - 1) // MOE_TILE) * MOE_TILE
    ends = jnp.cumsum(padded)
    starts = ends - padded
    pair_slot = jnp.sum(onehot * (csum - 1.0 + starts.astype(F32)[None, :]), axis=1).astype(jnp.int32)
    tile_start = jnp.arange(n_tiles, dtype=jnp.int32) * MOE_TILE
    tile_e = jnp.minimum(jnp.sum((tile_start[:, None] >= ends[None, :]).astype(jnp.int32), axis=1), N_EXPERTS - 1)
    e_onehot = tile_e[:, None] == jnp.arange(N_EXPERTS, dtype=jnp.int32)[None, :]
    filled = jnp.sum(jnp.where(e_onehot, (starts + counts)[None, :], 0), axis=1)
    tile_rows = jnp.clip(filled - tile_start, 0, MOE_TILE).astype(jnp.int32)
    last_used = jnp.maximum(ends[-1] // MOE_TILE - 1, 0)
    tile_expert = jnp.where(tile_start < ends[-1], tile_e, tile_e[last_used]).astype(jnp.int32)
    return pair_slot, tile_expert, tile_rows


def _swiglu(xb, wg, wu, wd):
    gate = jnp.dot(xb, wg, preferred_element_type=F32)
    up = jnp.dot(xb, wu, preferred_element_type=F32)
    a = ((gate * _sigmoid(gate)) * up).astype(BF16)
    return jnp.dot(a, wd, preferred_element_type=F32)


def _ffn_kernel(te_ref, rows_ref, xs_ref, wg_ref, wu_ref, wd_ref, ys_ref, wg_s, wu_s, wd_s):
    i = pl.program_id(0)
    new_expert = jnp.logical_or(i == 0, te_ref[i] != te_ref[jnp.maximum(i - 1, 0)])

    @pl.when(new_expert)
    def _():
        wg_s[...] = wg_ref[0].astype(BF16)
        wu_s[...] = wu_ref[0].astype(BF16)
        wd_s[...] = wd_ref[0].astype(BF16)

    @pl.when(rows_ref[i] > 0)
    def _():
        live = lax.broadcasted_iota(jnp.int32, (MOE_TILE, 1), 0) < rows_ref[i]
        xb = _load_unpack(xs_ref, live).astype(BF16)
        ys_ref[...] = _pack_rows(_swiglu(xb, wg_s[...], wu_s[...], wd_s[...]))

    @pl.when(rows_ref[i] == 0)
    def _():
        ys_ref[...] = jnp.zeros_like(ys_ref)


def _expert_ffn(xs, tile_expert, tile_rows, wg, wu, wd, layer):
    n_slots = xs.shape[0]
    d = D_MODEL
    blk = pl.BlockSpec((MOE_TILE, PACK_W), lambda i, te, nr: (i, 0))
    return pl.pallas_call(
        _ffn_kernel,
        grid_spec=pltpu.PrefetchScalarGridSpec(
            num_scalar_prefetch=2,
            grid=(n_slots // MOE_TILE,),
            in_specs=[blk,
                      pl.BlockSpec((None, 1, d, D_EXPERT), lambda i, te, nr: (layer, te[i], 0, 0)),
                      pl.BlockSpec((None, 1, d, D_EXPERT), lambda i, te, nr: (layer, te[i], 0, 0)),
                      pl.BlockSpec((None, 1, D_EXPERT, d), lambda i, te, nr: (layer, te[i], 0, 0))],
            out_specs=blk,
            scratch_shapes=[pltpu.VMEM((d, D_EXPERT), BF16), pltpu.VMEM((d, D_EXPERT), BF16),
                            pltpu.VMEM((D_EXPERT, d), BF16)]),
        out_shape=jax.ShapeDtypeStruct(xs.shape, jnp.int32),
        compiler_params=_params(1),
        name="expert_ffn",
    )(tile_expert, tile_rows, xs, wg, wu, wd)


def _shared_kernel(h_ref, sg_ref, su_ref, sd_ref, o_ref):
    hb = _load_unpack(h_ref).astype(BF16)
    o_ref[...] = _swiglu(hb, sg_ref[...], su_ref[...], sd_ref[...]).astype(BF16)


def _shared_ffn(h2p, sg, su, sd):
    t = h2p.shape[0]
    tm = min(512, t)
    const2 = lambda i: (0, 0)
    return pl.pallas_call(
        _shared_kernel,
        grid=(t // tm,),
        in_specs=[pl.BlockSpec((tm, PACK_W), lambda i: (i, 0)),
                  pl.BlockSpec(sg.shape, const2), pl.BlockSpec(su.shape, const2), pl.BlockSpec(sd.shape, const2)],
        out_specs=pl.BlockSpec((tm, D_MODEL), lambda i: (i, 0)),
        out_shape=jax.ShapeDtypeStruct((t, D_MODEL), BF16),
        compiler_params=_params(1),
        name="shared_ffn",
    )(h2p, sg, su, sd)


def _combine_kernel(ysh_ref, y1_ref, y2_ref, w_ref, x_ref, g2_ref, gfin_ref, o_ref, *, final):
    w = w_ref[...]
    y = ysh_ref[...].astype(F32) + w[:, 0:1] * _load_unpack(y1_ref) + w[:, 1:2] * _load_unpack(y2_ref)
    xn = x_ref[...] + g2_ref[0] * y
    if final:
        xn = _rms(xn, gfin_ref[...])
    o_ref[...] = xn


def _moe_combine(ysh, yg, w12, x2, row_off, mod3, brow, gfin, final):
    t, d = x2.shape
    tm = min(512, t)
    t_all = ysh.shape[0]
    assert row_off % tm == 0 and t_all % tm == 0
    off = row_off // tm
    row = lambda i: (i, 0)
    pk = lambda o: pl.BlockSpec((tm, PACK_W), lambda i: (i + o, 0))
    return pl.pallas_call(
        functools.partial(_combine_kernel, final=final),
        grid=(t // tm,),
        in_specs=[pl.BlockSpec((tm, d), lambda i: (i + off, 0)), pk(off), pk(off + t_all // tm),
                  pl.BlockSpec((tm, LANES), lambda i: (i + off, 0)),
                  pl.BlockSpec((tm, d), row),
                  _mod_spec(5, lambda i: brow(i * tm)),
                  pl.BlockSpec((1, d), lambda i: (0, 0))],
        out_specs=pl.BlockSpec((tm, d), row),
        out_shape=jax.ShapeDtypeStruct((t, d), F32),
        compiler_params=_params(1),
        name="moe_combine",
    )(ysh, yg, yg, w12, x2, mod3, gfin.reshape(1, d))


def _moe_routed(parts, wg, wu, wd, layer, sg, su, sd, mod3, gfin, final):
    h2p = parts[0][0] if len(parts) == 1 else jnp.concatenate([pt[0] for pt in parts], axis=0)
    route = parts[0][1] if len(parts) == 1 else jnp.concatenate([pt[1] for pt in parts], axis=1)
    t_all = h2p.shape[0]
    pair_slot, tile_expert, tile_rows = _dispatch_plan(route)
    n_slots = tile_expert.shape[0] * MOE_TILE
    xs = _sc_scatter_rows(h2p, pair_slot, n_slots)
    ysh = _shared_ffn(h2p, sg, su, sd)
    ys = _expert_ffn(xs, tile_expert, tile_rows, wg, wu, wd, layer)
    yg = _sc_gather_rows(ys, pair_slot)
    w12 = jnp.concatenate([route[2:4].T, jnp.zeros((t_all, LANES - 2), F32)], axis=1)
    outs, row_off = [], 0
    for _, _, x2, brow in parts:
        outs.append(_moe_combine(ysh, yg, w12, x2, row_off, mod3, brow, gfin, final))
        row_off += x2.shape[0]
    return outs


def _rope_tables(seq):
    rows = seq // GRID_W
    row = jnp.repeat(jnp.arange(rows), GRID_W).astype(F32)
    col = jnp.tile(jnp.arange(GRID_W), rows).astype(F32)
    axis_dim = QK_ROPE // 2
    inv = ROPE_THETA ** (-jnp.arange(0, axis_dim, 2, dtype=F32) / axis_dim)
    ang = jnp.concatenate([row[:, None] * inv, col[:, None] * inv], axis=-1)
    cos, sin = jnp.cos(ang), jnp.sin(ang)
    zero = jnp.zeros_like(cos)
    return (jnp.concatenate([cos, cos, zero, zero], axis=1),
            jnp.concatenate([-sin, sin, zero, zero], axis=1))


def _rope_group(w, start):
    half = QK_ROPE // 2
    x1, x2 = w[:, start:start + half], w[:, start + half:start + 2 * half]
    return jnp.concatenate([x1, x2, x2, x1], axis=1)


def _relayout_w_in(w):
    return jnp.concatenate([w[:, :KV_RANK], w[:, KV_COLS:Q_END], _rope_group(w, KV_RANK),
                            w[:, Q_END:]], axis=1).astype(BF16)


def _relayout_w_qb(w):
    dq = QK_NOPE + QK_ROPE
    cols = []
    for h in range(N_HEADS):
        cols += [w[:, h * dq:h * dq + QK_NOPE], _rope_group(w, h * dq + QK_NOPE)]
    return jnp.concatenate(cols, axis=1).astype(BF16)


def kernel(x, c, ctx, c_ctx, w_ada, b_ada, norm_mix_g, norm_ffn_g, w_in, q_norm_g, kv_norm_g, w_qb, w_kvb, w_pool, pool_scale, w_out, w_router, router_bias, w_exp_gate, w_exp_up, w_exp_down, w_sh_gate, w_sh_up, w_sh_down, final_norm_g):
    bsz, seq, d = x.shape
    n_ctx = ctx.shape[1]
    depth = w_ada.shape[0]
    assert bsz < MOD_ROWS and d == D_MODEL

    c_rows = jnp.concatenate([c, c_ctx[None], jnp.zeros((MOD_ROWS - bsz - 1, d), F32)], axis=0)
    mod = _ada_mod(c_rows, w_ada, b_ada)

    cos_t, sin_t = _rope_tables(seq)
    ones_t = jnp.concatenate([jnp.ones((n_ctx, 2 * (QK_ROPE // 2)), F32),
                              jnp.zeros((n_ctx, LANES - QK_ROPE), F32)], axis=1)
    zeros_t = jnp.zeros((n_ctx, LANES), F32)
    wr_t = w_router.T.astype(BF16)
    lat_row = lambda r: r // seq
    ctx_row = lambda r: bsz

    x2 = x.reshape(bsz * seq, d)
    xc2 = ctx.reshape(bsz * n_ctx, d)
    for l in range(depth):
        last = l == depth - 1
        mod3 = mod[l].reshape(MOD_ROWS * N_MOD, 1, d)
        w_in_r = _relayout_w_in(w_in[l])
        wq = _relayout_w_qb(w_qb[l])
        w_kv3 = w_kvb[l].reshape(KV_RANK, N_HEADS, QK_NOPE + V_DIM)
        wk = w_kv3[:, :, :QK_NOPE].reshape(KV_RANK, N_HEADS * QK_NOPE).astype(BF16)
        wvt = (w_kv3[:, :, QK_NOPE:].transpose(1, 2, 0).reshape(N_HEADS * V_DIM, KV_RANK)
               .astype(BF16))
        wp = w_pool[l].astype(BF16)
        wo = w_out[l].astype(BF16)
        sg, su, sd = w_sh_gate[l].astype(BF16), w_sh_up[l].astype(BF16), w_sh_down[l].astype(BF16)
        moe = functools.partial(_moe_routed, wg=w_exp_gate, wu=w_exp_up, wd=w_exp_down, layer=l,
                                sg=sg, su=su, sd=sd, mod3=mod3, gfin=final_norm_g)

        p = _in_proj(x2, mod3, lat_row, norm_mix_g[l], w_in_r, P_COLS)
        pc = _in_proj(xc2, mod3, ctx_row, norm_mix_g[l], w_in_r, LAT_W if last else P_COLS)
        q, k_lat, v_lat = _qkv(p, bsz, seq, cos_t, sin_t, q_norm_g[l], kv_norm_g[l], wq, wk, wvt, True)
        ctx_out = _qkv(pc, bsz, n_ctx, ones_t, zeros_t, q_norm_g[l], kv_norm_g[l], wq, wk, wvt, not last)
        k_ctx, v_ctx = ctx_out[-2:]
        y = _attention(q, [k_ctx, k_lat], [v_ctx, v_lat])
        x2, h2p, route = _mixer_out(y, p, x2, mod3, lat_row, seq, wp, pool_scale[l], wo,
                                    norm_ffn_g[l], wr_t, router_bias)
        if last:
            x2, = moe([(h2p, route, x2, lat_row)], final=True)
        else:
            yc = _attention(ctx_out[0], [k_ctx], [v_ctx])
            xc2, h2c, route_c = _mixer_out(yc, pc, xc2, mod3, ctx_row, n_ctx, wp, pool_scale[l], wo,
                                           norm_ffn_g[l], wr_t, router_bias)
            x2, xc2 = moe([(h2p, route, x2, lat_row), (h2c, route_c, xc2, ctx_row)], final=False)
    return x2.reshape(bsz, seq, d)
```

```python
import functools
import math

import jax
import jax.numpy as jnp
from jax import lax
from jax.experimental import pallas as pl
from jax.experimental.pallas import tpu as pltpu
from jax.experimental.pallas import tpu_sc as plsc

F32 = jnp.float32
BF16 = jnp.bfloat16

D_MODEL = 2048
GRID_W = 64
N_HEADS = 16
QK_NOPE = 128
QK_ROPE = 64
V_DIM = 128
Q_RANK = 384
KV_RANK = 512
ROPE_THETA = 10000.0
POOL_DIM = 1024
POOL_GROUPS = 4
POOL_GROUP_DIM = 256
POOL_WINDOWS = (2, 4, 8, 16)
POOL_GROUP_OUT = 512
KV_COLS = KV_RANK + QK_ROPE
Q_END = KV_COLS + Q_RANK
POOL_END = Q_END + POOL_DIM
N_EXPERTS = 16
N_GROUPS = 4
EXPERTS_PER_GROUP = 4
D_EXPERT = 512
EPS = 1e-6
N_MOD = 6

LANES = 128
HEAD_W = 2 * LANES
LAT_W = 1024
P_COLS = LAT_W + POOL_DIM + 2 * D_MODEL
POOL_HALO = 16
MOD_ROWS = 8
Q_SCALE = (1.0 / math.sqrt(QK_NOPE + QK_ROPE)) * math.log2(math.e)
VMEM_LIMIT = 56 * 1024 * 1024
SUBLANES = 8
PACK_W = D_MODEL // 2
ROW_CHUNKS = PACK_W // LANES
ROUTE_ROWS = SUBLANES
MOE_TILE = 512
SC_WINDOW = 128
ATTN_TQ = 256
ATTN_KB = 512
V_HEAD_GROUP = 8
ATTN_HEADS_PER_STEP = 1
MIX_CHAIN_ROWS = 256
COMBINE_CHUNKS = 2


def _sigmoid(x):
    return 1.0 / (1.0 + jnp.exp(-x))


def _pack_rows(y):
    half = y.shape[1] // 2
    return pltpu.pack_elementwise([y[:, :half], y[:, half:]], packed_dtype=BF16)


def _load_unpack(ref, live=None):
    w = ref[...]
    if live is not None:
        w = jnp.where(live, w, 0)
    lo = pltpu.unpack_elementwise(w, index=0, packed_dtype=BF16, unpacked_dtype=F32)
    hi = pltpu.unpack_elementwise(w, index=1, packed_dtype=BF16, unpacked_dtype=F32)
    return jnp.concatenate([lo, hi], axis=1)


def _rms(xf, g):
    ms = jnp.mean(xf * xf, axis=-1, keepdims=True)
    return xf * lax.rsqrt(ms + EPS) * g


def _params(n_axes):
    return pltpu.CompilerParams(dimension_semantics=("arbitrary",) * n_axes,
                                vmem_limit_bytes=VMEM_LIMIT)


def _ada_kernel(c_ref, w_ref, b_ref, o_ref):
    cf = c_ref[...]
    a = (cf * _sigmoid(cf)).astype(BF16)
    o_ref[0] = jnp.dot(a, w_ref[0].astype(BF16), preferred_element_type=F32) + b_ref[0]


def _ada_mod(c_rows, w_ada, b_ada):
    depth, d, n = w_ada.shape
    tn = 1024
    return pl.pallas_call(
        _ada_kernel,
        grid=(depth, n // tn),
        in_specs=[pl.BlockSpec((MOD_ROWS, d), lambda l, j: (0, 0)),
                  pl.BlockSpec((1, d, tn), lambda l, j: (l, 0, j)),
                  pl.BlockSpec((1, 1, tn), lambda l, j: (l, 0, j))],
        out_specs=pl.BlockSpec((1, MOD_ROWS, tn), lambda l, j: (l, 0, j)),
        out_shape=jax.ShapeDtypeStruct((depth, MOD_ROWS, n), F32),
        compiler_params=_params(2),
        name="ada_mod",
    )(c_rows, w_ada, b_ada.reshape(depth, 1, n))


def _mod_spec(k, brow):
    return pl.BlockSpec((1, 1, D_MODEL), lambda i, *_: (brow(i) * N_MOD + k, 0, 0))


def _inproj_kernel(x_ref, g_ref, sh_ref, sc_ref, w_ref, o_ref, h_scr, *, tn):
    y = _rms(x_ref[...], g_ref[...])
    h_scr[...] = (y * (1.0 + sc_ref[0]) + sh_ref[0]).astype(BF16)
    for c0 in range(0, o_ref.shape[1], tn):
        acc = jnp.dot(h_scr[...], w_ref[:, c0:c0 + tn], preferred_element_type=F32)
        if c0 >= LAT_W + POOL_DIM:
            acc = _sigmoid(acc)
        o_ref[:, c0:c0 + tn] = acc.astype(BF16)


def _in_proj(x2, mod3, brow, g, w, n_cols):
    t, d = x2.shape
    tm = min(512, t)
    return pl.pallas_call(
        functools.partial(_inproj_kernel, tn=512),
        grid=(t // tm,),
        in_specs=[pl.BlockSpec((tm, d), lambda i: (i, 0)),
                  pl.BlockSpec((1, d), lambda i: (0, 0)),
                  _mod_spec(0, lambda i: brow(i * tm)),
                  _mod_spec(1, lambda i: brow(i * tm)),
                  pl.BlockSpec((d, n_cols), lambda i: (0, 0), pipeline_mode=pl.Buffered(1))],
        out_specs=pl.BlockSpec((tm, n_cols), lambda i: (i, 0)),
        out_shape=jax.ShapeDtypeStruct((t, n_cols), BF16),
        scratch_shapes=[pltpu.VMEM((tm, d), BF16)],
        compiler_params=_params(1),
        name="in_proj",
    )(x2, g.reshape(1, d), mod3, mod3, w)


def _qkv_kernel(lat_ref, cos_ref, sin_ref, gq_ref, gkv_ref, wq_ref, wk_ref, wvt_ref, *out_refs, with_q):
    if with_q:
        q_ref, k_ref, v_ref = out_refs
    else:
        k_ref, v_ref = out_refs
    cos = cos_ref[...]
    sin = sin_ref[...]

    def rope(grp):
        return grp * cos + pltpu.roll(grp, 2 * (QK_ROPE // 2), axis=1) * sin

    kvn = _rms(lat_ref[:, :KV_RANK].astype(F32), gkv_ref[...]).astype(BF16)
    kpe = rope(lat_ref[:, KV_RANK + Q_RANK:].astype(F32)).astype(BF16)
    for h in range(0, N_HEADS, 2):
        kn = jnp.dot(kvn, wk_ref[:, h * QK_NOPE:(h + 2) * QK_NOPE], preferred_element_type=F32)
        for hh in range(2):
            k_ref[0, h + hh, :, :QK_NOPE] = kn[:, hh * QK_NOPE:(hh + 1) * QK_NOPE].astype(BF16)
            k_ref[0, h + hh, :, QK_NOPE:] = kpe
    for h in range(0, N_HEADS, V_HEAD_GROUP):
        vt = lax.dot_general(wvt_ref[h * V_DIM:(h + V_HEAD_GROUP) * V_DIM, :], kvn,
                             (((1,), (1,)), ((), ())), preferred_element_type=F32)
        for hh in range(V_HEAD_GROUP):
            v_ref[0, h + hh] = vt[hh * V_DIM:(hh + 1) * V_DIM, :].astype(BF16)
    if with_q:
        qn = _rms(lat_ref[:, KV_RANK:KV_RANK + Q_RANK].astype(F32), gq_ref[...]).astype(BF16)
        for h in range(N_HEADS):
            qh = jnp.dot(qn, wq_ref[:, h * HEAD_W:(h + 1) * HEAD_W], preferred_element_type=F32)
            q_ref[0, h, :, :QK_NOPE] = (qh[:, :QK_NOPE] * Q_SCALE).astype(BF16)
            q_ref[0, h, :, QK_NOPE:] = (rope(qh[:, QK_NOPE:]) * Q_SCALE).astype(BF16)


def _qkv(p, b, n, cos_t, sin_t, gq, gkv, wq, wk, wvt, with_q):
    tm = min(256, n)
    tpb = n // tm
    rope_tiles = cos_t.shape[0] // tm
    head_spec = pl.BlockSpec((1, N_HEADS, tm, HEAD_W), lambda i: (i // tpb, 0, i % tpb, 0))
    tab_spec = pl.BlockSpec((tm, LANES), lambda i: ((i % tpb) % rope_tiles, 0))
    out_shape = [jax.ShapeDtypeStruct((b, N_HEADS, n, HEAD_W), BF16),
                 jax.ShapeDtypeStruct((b, N_HEADS, V_DIM, n), BF16)]
    out_specs = [head_spec,
                 pl.BlockSpec((1, N_HEADS, V_DIM, tm), lambda i: (i // tpb, 0, 0, i % tpb))]
    if with_q:
        out_shape = [jax.ShapeDtypeStruct((b, N_HEADS, n, HEAD_W), BF16)] + out_shape
        out_specs = [head_spec] + out_specs
    return pl.pallas_call(
        functools.partial(_qkv_kernel, with_q=with_q),
        grid=(b * tpb,),
        in_specs=[pl.BlockSpec((tm, LAT_W), lambda i: (i, 0)),
                  tab_spec, tab_spec,
                  pl.BlockSpec((1, Q_RANK), lambda i: (0, 0)),
                  pl.BlockSpec((1, KV_RANK), lambda i: (0, 0)),
                  pl.BlockSpec(wq.shape, lambda i: (0, 0)),
                  pl.BlockSpec(wk.shape, lambda i: (0, 0)),
                  pl.BlockSpec(wvt.shape, lambda i: (0, 0))],
        out_specs=out_specs,
        out_shape=out_shape,
        compiler_params=_params(1),
        name="qkv",
    )(p, cos_t, sin_t, gq.reshape(1, Q_RANK), gkv.reshape(1, KV_RANK), wq, wk, wvt)


def _row_fold(x, op):
    parts = [x[r:r + SUBLANES] for r in range(0, x.shape[0], SUBLANES)]
    a, b = parts[0], parts[1]
    for i in range(2, len(parts) - 1, 2):
        a, b = op(a, parts[i]), op(b, parts[i + 1])
    if len(parts) % 2:
        a = op(a, parts[-1])
    return op(a, b)


def _attn_kernel(q_ref, *refs, n_seg, kb, tq):
    k_refs, vt_refs = refs[:n_seg], refs[n_seg:2 * n_seg]
    o_ref, s_a, s_b, m_a, m_b = refs[2 * n_seg:]
    chunks = []
    off = 0
    for kr, vr in zip(k_refs, vt_refs):
        nk = kr.shape[2]
        for c0 in range(0, nk, kb):
            n = min(kb, nk - c0)
            chunks.append((kr, vr, c0, n, off))
            off += n

    tiles_per_head = q_ref.shape[2] // tq

    def head_rows(t):
        if q_ref.shape[1] == 1:
            return 0, pl.ds(pl.multiple_of(t * tq, tq), tq)
        return t // tiles_per_head, pl.ds(pl.multiple_of((t % tiles_per_head) * tq, tq), tq)

    def scores(t, s_buf, m_buf):
        g, rows = head_rows(t)
        q = q_ref[0, g, rows, :]
        mp = None
        for kr, vr, c0, n, o in chunks:
            s = lax.dot_general(kr[0, g, c0:c0 + n, :], q, (((1,), (1,)), ((), ())),
                                preferred_element_type=F32)
            s_buf[o:o + n, :] = s
            m = _row_fold(s, jnp.maximum)
            mp = m if mp is None else jnp.maximum(mp, m)
        m_buf[...] = mp

    def values(t, s_buf, m_buf):
        g, rows = head_rows(t)
        mrow = jnp.max(m_buf[...], axis=0, keepdims=True)
        lp = None
        acc = None
        for kr, vr, c0, n, o in chunks:
            pr = jnp.exp2(s_buf[o:o + n, :] - mrow)
            ls = _row_fold(pr, jnp.add)
            lp = ls if lp is None else lp + ls
            pv = jnp.dot(vr[0, g, :, c0:c0 + n], pr.astype(BF16), preferred_element_type=F32)
            acc = pv if acc is None else acc + pv
        l = jnp.sum(lp, axis=0, keepdims=True)
        o_ref[0, g, rows, :] = (acc / l).T.astype(BF16)

    nt = q_ref.shape[1] * tiles_per_head
    scores(0, s_a, m_a)
    if nt > 1:
        def pair(j, carry):
            scores(2 * j + 1, s_b, m_b)
            values(2 * j, s_a, m_a)
            scores(2 * j + 2, s_a, m_a)
            values(2 * j + 1, s_b, m_b)
            return carry

        lax.fori_loop(0, nt // 2 - 1, pair, 0)
        scores(nt - 1, s_b, m_b)
        values(nt - 2, s_a, m_a)
        values(nt - 1, s_b, m_b)
    else:
        values(0, s_a, m_a)


def _attention(q, ks, vts):
    b, hh, n, _ = q.shape
    tq = min(ATTN_TQ, n)
    hg = ATTN_HEADS_PER_STEP
    n_tiles = hg * (n // tq)
    assert (n_tiles == 1 or n_tiles % 2 == 0) and hh % hg == 0
    nk_total = sum(k.shape[2] for k in ks)
    seg_spec = lambda a: pl.BlockSpec((1, hg) + a.shape[2:], lambda bi, h: (bi, h, 0, 0))
    return pl.pallas_call(
        functools.partial(_attn_kernel, n_seg=len(ks), kb=ATTN_KB, tq=tq),
        grid=(b, hh // hg),
        in_specs=[pl.BlockSpec((1, hg, n, HEAD_W), lambda bi, h: (bi, h, 0, 0))]
                 + [seg_spec(a) for a in ks] + [seg_spec(a) for a in vts],
        out_specs=pl.BlockSpec((1, hg, n, V_DIM), lambda bi, h: (bi, h, 0, 0)),
        out_shape=jax.ShapeDtypeStruct((b, hh, n, V_DIM), BF16),
        scratch_shapes=[pltpu.VMEM((nk_total, tq), F32), pltpu.VMEM((nk_total, tq), F32),
                        pltpu.VMEM((SUBLANES, tq), F32), pltpu.VMEM((SUBLANES, tq), F32)],
        compiler_params=_params(2),
        name="attention",
    )(q, *ks, *vts)


def _top2sum(a, b, c, d):
    s1, t1 = jnp.maximum(a, b), jnp.minimum(a, b)
    s2, t2 = jnp.maximum(c, d), jnp.minimum(c, d)
    return jnp.maximum(s1, s2) + jnp.maximum(jnp.minimum(s1, s2), jnp.maximum(t1, t2))


def _route(logits_t, bias):
    sc = _sigmoid(logits_t)
    sel = sc + bias
    sel_r = [sel[e:e + 1, :] for e in range(N_EXPERTS)]
    sc_r = [sc[e:e + 1, :] for e in range(N_EXPERTS)]
    epg = EXPERTS_PER_GROUP
    gs = [_top2sum(*sel_r[g * epg:(g + 1) * epg]) for g in range(N_GROUPS)]
    best, gi = gs[0], jnp.zeros(gs[0].shape, jnp.int32)
    for g in range(1, N_GROUPS):
        upd = gs[g] > best
        best = jnp.where(upd, gs[g], best)
        gi = jnp.where(upd, g, gi)

    def pick_group(rows, k):
        r = rows[k]
        for g in range(1, N_GROUPS):
            r = jnp.where(gi == g, rows[g * epg + k], r)
        return r

    in_sel = [pick_group(sel_r, k) for k in range(epg)]
    in_sc = [pick_group(sc_r, k) for k in range(epg)]
    b1, i1 = in_sel[0], jnp.zeros(gi.shape, jnp.int32)
    for k in range(1, epg):
        upd = in_sel[k] > b1
        b1 = jnp.where(upd, in_sel[k], b1)
        i1 = jnp.where(upd, k, i1)
    b2, i2 = None, None
    for k in range(epg):
        cand = jnp.where(i1 == k, -jnp.inf, in_sel[k])
        if b2 is None:
            b2, i2 = cand, jnp.zeros(gi.shape, jnp.int32)
        else:
            upd = cand > b2
            b2 = jnp.where(upd, cand, b2)
            i2 = jnp.where(upd, k, i2)

    def pick_local(idx):
        r = in_sc[0]
        for k in range(1, epg):
            r = jnp.where(idx == k, in_sc[k], r)
        return r

    s1, s2 = pick_local(i1), pick_local(i2)
    den = s1 + s2
    return gi * epg + i1, gi * epg + i2, s1 / den, s2 / den


def _mixout_kernel(y_ref, u_ref, up_ref, un_ref, gm_ref, gp_ref, wp_ref, ps_ref, wo_ref, x_ref,
                   g1_ref, gf_ref, sh2_ref, sc2_ref, wr_ref, rb_ref,
                   xo_ref, h2_ref, route_ref, *, tm, tpb, n_seq):
    i = pl.program_id(0)
    uext = jnp.concatenate([up_ref[...], u_ref[...], un_ref[...]], axis=0)
    sub = min(MIX_CHAIN_ROWS, tm)

    def mix_stage(r0):
        rows = slice(r0, r0 + sub)
        base = (i % tpb) * tm + r0
        r = lax.broadcasted_iota(jnp.int32, (sub, sub + 2 * POOL_HALO), 0)
        c = lax.broadcasted_iota(jnp.int32, (sub, sub + 2 * POOL_HALO), 1)
        rel = c - POOL_HALO - r
        jpos = base - POOL_HALO + c
        valid = (jpos >= 0) & (jpos < n_seq)
        tpos = base + lax.broadcasted_iota(jnp.int32, (sub, 1), 0)
        uwin = uext[r0:r0 + sub + 2 * POOL_HALO]
        uc = uext[r0 + POOL_HALO:r0 + POOL_HALO + sub]
        parts = []
        for g, w in enumerate(POOL_WINDOWS):
            sl = slice(g * POOL_GROUP_DIM, (g + 1) * POOL_GROUP_DIM)
            band = jnp.where(valid & (rel >= -(w // 2)) & (rel < w // 2), 1.0, 0.0).astype(BF16)
            cnt = jnp.clip(tpos + w // 2, 0, n_seq) - jnp.clip(tpos - w // 2, 0, n_seq)
            wsum = jnp.dot(band, uwin[:, sl], preferred_element_type=F32)
            z = wsum / cnt.astype(F32) - uc[:, sl].astype(F32)
            parts.append(jnp.dot(z.astype(BF16), wp_ref[g], preferred_element_type=F32))
        ypool = jnp.concatenate(parts, axis=1) * ps_ref[...]
        y_mla = jnp.concatenate([y_ref[0, h, rows, :] for h in range(N_HEADS)], axis=1)
        mix = gm_ref[rows, :].astype(F32) * y_mla.astype(F32) + gp_ref[rows, :].astype(F32) * ypool
        return jnp.dot(mix.astype(BF16), wo_ref[...], preferred_element_type=F32)

    def norm_route_stage(r0, y):
        rows = slice(r0, r0 + sub)
        xn = x_ref[rows, :] + g1_ref[0] * y
        xo_ref[rows, :] = xn
        h2f = _rms(xn, gf_ref[...]) * (1.0 + sc2_ref[0]) + sh2_ref[0]
        h2_ref[rows, :] = _pack_rows(h2f)
        logits_t = lax.dot_general(wr_ref[...], h2f.astype(BF16), (((1,), (1,)), ((), ())),
                                   preferred_element_type=F32)
        e1, e2, w1, w2 = _route(logits_t, rb_ref[...])
        zero = jnp.zeros((ROUTE_ROWS - 4, sub), F32)
        route_ref[:, rows] = jnp.concatenate([e1.astype(F32), e2.astype(F32), w1, w2, zero], axis=0)

    starts = list(range(0, tm, sub))
    pending = None
    for r0 in starts:
        y = mix_stage(r0)
        if pending is not None:
            norm_route_stage(*pending)
        pending = (r0, y)
    norm_route_stage(*pending)


def _mixer_out(y, p, x2, mod3, brow, n_seq, wp, ps, wo, gf, wr_t, rb):
    t, d = x2.shape
    tm = min(2 * MIX_CHAIN_ROWS, n_seq)
    tpb = n_seq // tm
    hpt = tm // POOL_HALO
    n_halo = t // POOL_HALO
    row = lambda i: (i, 0)
    const2 = lambda i: (0, 0)
    mrow = lambda i: brow(i * tm)
    return pl.pallas_call(
        functools.partial(_mixout_kernel, tm=tm, tpb=tpb, n_seq=n_seq),
        grid=(t // tm,),
        in_specs=[pl.BlockSpec((1, N_HEADS, tm, V_DIM), lambda i: (i // tpb, 0, i % tpb, 0)),
                  pl.BlockSpec((tm, POOL_DIM), lambda i: (i, LAT_W // POOL_DIM)),
                  pl.BlockSpec((POOL_HALO, POOL_DIM),
                               lambda i: (jnp.maximum(i * hpt - 1, 0), LAT_W // POOL_DIM)),
                  pl.BlockSpec((POOL_HALO, POOL_DIM),
                               lambda i: (jnp.minimum((i + 1) * hpt, n_halo - 1), LAT_W // POOL_DIM)),
                  pl.BlockSpec((tm, d), lambda i: (i, (LAT_W + POOL_DIM) // d)),
                  pl.BlockSpec((tm, d), lambda i: (i, (LAT_W + POOL_DIM) // d + 1)),
                  pl.BlockSpec(wp.shape, lambda i: (0, 0, 0)),
                  pl.BlockSpec((1, d), const2),
                  pl.BlockSpec((d, d), const2, pipeline_mode=pl.Buffered(1)),
                  pl.BlockSpec((tm, d), row),
                  _mod_spec(2, mrow),
                  pl.BlockSpec((1, d), const2),
                  _mod_spec(3, mrow),
                  _mod_spec(4, mrow),
                  pl.BlockSpec((N_EXPERTS, d), const2),
                  pl.BlockSpec((N_EXPERTS, 1), const2)],
        out_specs=[pl.BlockSpec((tm, d), row),
                   pl.BlockSpec((tm, PACK_W), row),
                   pl.BlockSpec((ROUTE_ROWS, tm), lambda i: (0, i))],
        out_shape=[jax.ShapeDtypeStruct((t, d), F32),
                   jax.ShapeDtypeStruct((t, PACK_W), jnp.int32),
                   jax.ShapeDtypeStruct((ROUTE_ROWS, t), F32)],
        compiler_params=_params(1),
        name="mixer_out",
    )(y, p, p, p, p, p, wp, ps.reshape(1, d), wo, x2, mod3, gf.reshape(1, d), mod3, mod3, wr_t,
      rb.reshape(N_EXPERTS, 1))


def _chunk_view(x):
    n = x.shape[0]
    return (x.reshape(n // SUBLANES, SUBLANES, ROW_CHUNKS, LANES).transpose(0, 2, 1, 3)
            .reshape(n * ROW_CHUNKS, LANES))


def _row_view(c):
    n = c.shape[0] // ROW_CHUNKS
    return (c.reshape(n // SUBLANES, ROW_CHUNKS, SUBLANES, LANES).transpose(0, 2, 1, 3)
            .reshape(n, PACK_W))


def _sc_index_chunks(idx):
    r = idx.reshape(-1, 1, SUBLANES)
    j = jnp.arange(ROW_CHUNKS, dtype=jnp.int32)[None, :, None]
    ids = (r // SUBLANES) * (SUBLANES * ROW_CHUNKS) + j * SUBLANES + r % SUBLANES
    return ids.reshape(1, idx.shape[0] * ROW_CHUNKS)


def _sc_mesh():
    return plsc.VectorSubcoreMesh(core_axis_name="core", subcore_axis_name="subcore")


def _sc_gather_rows(table, idx):
    n, m = table.shape[0], idx.shape[0]
    n_idx = m * ROW_CHUNKS
    assert n_idx % SC_WINDOW == 0 and n % SUBLANES == 0 and m % SUBLANES == 0

    @pl.kernel(out_type=jax.ShapeDtypeStruct((n_idx, LANES), jnp.int32), mesh=_sc_mesh())
    def gather(t_hbm, i_hbm, o_hbm):
        def body(i_vmem, o_vmem):
            pltpu.sync_copy(t_hbm.at[i_vmem.at[0]], o_vmem)

        pltpu.emit_pipeline(
            body,
            grid=(n_idx // SC_WINDOW,),
            in_specs=[pl.BlockSpec((1, SC_WINDOW), index_map=lambda i: (0, i))],
            out_specs=[pl.BlockSpec((SC_WINDOW, LANES), index_map=lambda i: (i, 0))],
            core_axis_name=("core", "subcore"),
            dimension_semantics=(pltpu.PARALLEL,),
        )(i_hbm, o_hbm)

    return _row_view(gather(_chunk_view(table), _sc_index_chunks(idx)))


def _sc_scatter_rows(rows, idx, n_out):
    n, m = rows.shape[0], idx.shape[0]
    n_idx = m * ROW_CHUNKS
    src_steps = n * ROW_CHUNKS // SC_WINDOW
    assert n_idx % SC_WINDOW == 0 and (n * ROW_CHUNKS) % SC_WINDOW == 0 and m % n == 0
    assert n % SUBLANES == 0 and n_out % SUBLANES == 0

    @pl.kernel(out_type=jax.ShapeDtypeStruct((n_out * ROW_CHUNKS, LANES), jnp.int32), mesh=_sc_mesh())
    def scatter(x_hbm, i_hbm, o_hbm):
        def body(x_vmem, i_vmem):
            pltpu.sync_copy(x_vmem, o_hbm.at[i_vmem.at[0]])

        pltpu.emit_pipeline(
            body,
            grid=(n_idx // SC_WINDOW,),
            in_specs=[pl.BlockSpec((SC_WINDOW, LANES), index_map=lambda i: (i % src_steps, 0)),
                      pl.BlockSpec((1, SC_WINDOW), index_map=lambda i: (0, i))],
            out_specs=[],
            core_axis_name=("core", "subcore"),
            dimension_semantics=(pltpu.PARALLEL,),
        )(x_hbm, i_hbm)

    return _row_view(scatter(_chunk_view(rows), _sc_index_chunks(idx)))


def _dispatch_plan(route):
    t = route.shape[1]
    n_pairs = 2 * t
    n_tiles = n_pairs // MOE_TILE + N_EXPERTS
    blk = LANES
    e = jnp.concatenate([route[0], route[1]]).astype(jnp.int32)
    onehot = (e[:, None] == jnp.arange(N_EXPERTS, dtype=jnp.int32)[None, :]).astype(F32)
    oh3 = onehot.reshape(n_pairs // blk, blk, N_EXPERTS)
    tri = (jnp.arange(blk)[:, None] >= jnp.arange(blk)[None, :]).astype(F32)
    within = jnp.einsum("ij,bjk->bik", tri, oh3)
    totals = within[:, -1, :]
    before = jnp.cumsum(totals, axis=0) - totals
    csum = (within + before[:, None, :]).reshape(n_pairs, N_EXPERTS)
    counts = jnp.sum(totals, axis=0).astype(jnp.int32)
    padded = ((counts + MOE_TILE - 1) // MOE_TILE) * MOE_TILE
    ends = jnp.cumsum(padded)
    starts = ends - padded
    pair_slot = jnp.sum(onehot * (csum - 1.0 + starts.astype(F32)[None, :]), axis=1).astype(jnp.int32)
    tile_start = jnp.arange(n_tiles, dtype=jnp.int32) * MOE_TILE
    tile_e = jnp.minimum(jnp.sum((tile_start[:, None] >= ends[None, :]).astype(jnp.int32), axis=1), N_EXPERTS - 1)
    e_onehot = tile_e[:, None] == jnp.arange(N_EXPERTS, dtype=jnp.int32)[None, :]
    filled = jnp.sum(jnp.where(e_onehot, (starts + counts)[None, :], 0), axis=1)
    tile_rows = jnp.clip(filled - tile_start, 0, MOE_TILE).astype(jnp.int32)
    last_used = jnp.maximum(ends[-1] // MOE_TILE - 1, 0)
    tile_expert = jnp.where(tile_start < ends[-1], tile_e, tile_e[last_used]).astype(jnp.int32)
    return pair_slot, tile_expert, tile_rows


def _swiglu(xb, wg, wu, wd):
    gate = jnp.dot(xb, wg, preferred_element_type=F32)
    up = jnp.dot(xb, wu, preferred_element_type=F32)
    a = ((gate * _sigmoid(gate)) * up).astype(BF16)
    return jnp.dot(a, wd, preferred_element_type=F32)


def _ffn_kernel(te_ref, rows_ref, xs_ref, wg_ref, wu_ref, wd_ref, ys_ref, wg_s, wu_s, wd_s):
    i = pl.program_id(0)
    new_expert = jnp.logical_or(i == 0, te_ref[i] != te_ref[jnp.maximum(i - 1, 0)])

    @pl.when(new_expert)
    def _():
        wg_s[...] = wg_ref[0].astype(BF16)
        wu_s[...] = wu_ref[0].astype(BF16)
        wd_s[...] = wd_ref[0].astype(BF16)

    @pl.when(rows_ref[i] > 0)
    def _():
        live = lax.broadcasted_iota(jnp.int32, (MOE_TILE, 1), 0) < rows_ref[i]
        xb = _load_unpack(xs_ref, live).astype(BF16)
        ys_ref[...] = _pack_rows(_swiglu(xb, wg_s[...], wu_s[...], wd_s[...]))

    @pl.when(rows_ref[i] == 0)
    def _():
        ys_ref[...] = jnp.zeros_like(ys_ref)


def _expert_ffn(xs, tile_expert, tile_rows, wg, wu, wd, layer):
    n_slots = xs.shape[0]
    d = D_MODEL
    blk = pl.BlockSpec((MOE_TILE, PACK_W), lambda i, te, nr: (i, 0))
    return pl.pallas_call(
        _ffn_kernel,
        grid_spec=pltpu.PrefetchScalarGridSpec(
            num_scalar_prefetch=2,
            grid=(n_slots // MOE_TILE,),
            in_specs=[blk,
                      pl.BlockSpec((None, 1, d, D_EXPERT), lambda i, te, nr: (layer, te[i], 0, 0)),
                      pl.BlockSpec((None, 1, d, D_EXPERT), lambda i, te, nr: (layer, te[i], 0, 0)),
                      pl.BlockSpec((None, 1, D_EXPERT, d), lambda i, te, nr: (layer, te[i], 0, 0))],
            out_specs=blk,
            scratch_shapes=[pltpu.VMEM((d, D_EXPERT), BF16), pltpu.VMEM((d, D_EXPERT), BF16),
                            pltpu.VMEM((D_EXPERT, d), BF16)]),
        out_shape=jax.ShapeDtypeStruct(xs.shape, jnp.int32),
        compiler_params=_params(1),
        name="expert_ffn",
    )(tile_expert, tile_rows, xs, wg, wu, wd)


def _combine_kernel(h_ref, y1_ref, y2_ref, w_ref, sg_ref, su_ref, sd_ref, x_ref, g2_ref, gfin_ref, o_ref,
                    *, final):
    y = _swiglu(_load_unpack(h_ref).astype(BF16), sg_ref[...], su_ref[...], sd_ref[...])
    w = w_ref[...]
    y = y + w[:, 0:1] * _load_unpack(y1_ref) + w[:, 1:2] * _load_unpack(y2_ref)
    xn = x_ref[...] + g2_ref[0] * y
    if final:
        xn = _rms(xn, gfin_ref[...])
    o_ref[...] = xn


def _moe_combine(h2p, yg, w12, sg, su, sd, x2, set_off, x_off, n_rows, mod3, brow, gfin, final):
    d = x2.shape[1]
    tm = min(512, n_rows)
    assert set_off % tm == 0 and x_off % tm == 0 and n_rows % tm == 0
    s_blk, x_blk, n_blk = set_off // tm, x_off // tm, n_rows // tm
    const2 = lambda i: (0, 0)
    pk = lambda o: pl.BlockSpec((tm, PACK_W), lambda i: (i + o, 0))
    x_spec = pl.BlockSpec((tm, d), lambda i: (i + x_blk, 0))
    return pl.pallas_call(
        functools.partial(_combine_kernel, final=final),
        grid=(n_blk,),
        in_specs=[pk(s_blk), pk(0), pk(n_blk),
                  pl.BlockSpec((tm, LANES), lambda i: (i + s_blk, 0)),
                  pl.BlockSpec(sg.shape, const2), pl.BlockSpec(su.shape, const2), pl.BlockSpec(sd.shape, const2),
                  x_spec,
                  _mod_spec(5, lambda i: brow((i + x_blk) * tm)),
                  pl.BlockSpec((1, d), const2)],
        out_specs=x_spec,
        out_shape=jax.ShapeDtypeStruct(x2.shape, F32),
        input_output_aliases={7: 0},
        compiler_params=_params(1),
        name="moe_combine",
    )(h2p, yg, yg, w12, sg, su, sd, x2, mod3, gfin.reshape(1, d))


def _moe_routed(parts, wg, wu, wd, layer, sg, su, sd, mod3, gfin, final):
    h2p = parts[0][0] if len(parts) == 1 else jnp.concatenate([pt[0] for pt in parts], axis=0)
    route = parts[0][1] if len(parts) == 1 else jnp.concatenate([pt[1] for pt in parts], axis=1)
    t_all = h2p.shape[0]
    pair_slot, tile_expert, tile_rows = _dispatch_plan(route)
    n_slots = tile_expert.shape[0] * MOE_TILE
    xs = _sc_scatter_rows(h2p, pair_slot, n_slots)
    ys = _expert_ffn(xs, tile_expert, tile_rows, wg, wu, wd, layer)
    w12 = jnp.concatenate([route[2:4].T, jnp.zeros((t_all, LANES - 2), F32)], axis=1)
    outs, set_off = [], 0
    for _, _, x2, brow in parts:
        t = x2.shape[0]
        n_rows = t // COMBINE_CHUNKS if t % (COMBINE_CHUNKS * 512) == 0 else t
        for x_off in range(0, t, n_rows):
            lo = set_off + x_off
            idx = jnp.concatenate([pair_slot[lo:lo + n_rows], pair_slot[t_all + lo:t_all + lo + n_rows]])
            yg = _sc_gather_rows(ys, idx)
            x2 = _moe_combine(h2p, yg, w12, sg, su, sd, x2, lo, x_off, n_rows, mod3, brow, gfin, final)
        outs.append(x2)
        set_off += t
    return outs


def _rope_tables(seq):
    rows = seq // GRID_W
    row = jnp.repeat(jnp.arange(rows), GRID_W).astype(F32)
    col = jnp.tile(jnp.arange(GRID_W), rows).astype(F32)
    axis_dim = QK_ROPE // 2
    inv = ROPE_THETA ** (-jnp.arange(0, axis_dim, 2, dtype=F32) / axis_dim)
    ang = jnp.concatenate([row[:, None] * inv, col[:, None] * inv], axis=-1)
    cos, sin = jnp.cos(ang), jnp.sin(ang)
    zero = jnp.zeros_like(cos)
    return (jnp.concatenate([cos, cos, zero, zero], axis=1),
            jnp.concatenate([-sin, sin, zero, zero], axis=1))


def _rope_group(w, start):
    half = QK_ROPE // 2
    x1, x2 = w[:, start:start + half], w[:, start + half:start + 2 * half]
    return jnp.concatenate([x1, x2, x2, x1], axis=1)


def _relayout_w_in(w):
    return jnp.concatenate([w[:, :KV_RANK], w[:, KV_COLS:Q_END], _rope_group(w, KV_RANK),
                            w[:, Q_END:]], axis=1).astype(BF16)


def _relayout_w_qb(w):
    dq = QK_NOPE + QK_ROPE
    cols = []
    for h in range(N_HEADS):
        cols += [w[:, h * dq:h * dq + QK_NOPE], _rope_group(w, h * dq + QK_NOPE)]
    return jnp.concatenate(cols, axis=1).astype(BF16)


def kernel(x, c, ctx, c_ctx, w_ada, b_ada, norm_mix_g, norm_ffn_g, w_in, q_norm_g, kv_norm_g, w_qb, w_kvb, w_pool, pool_scale, w_out, w_router, router_bias, w_exp_gate, w_exp_up, w_exp_down, w_sh_gate, w_sh_up, w_sh_down, final_norm_g):
    bsz, seq, d = x.shape
    n_ctx = ctx.shape[1]
    depth = w_ada.shape[0]
    assert bsz < MOD_ROWS and d == D_MODEL

    c_rows = jnp.concatenate([c, c_ctx[None], jnp.zeros((MOD_ROWS - bsz - 1, d), F32)], axis=0)
    mod = _ada_mod(c_rows, w_ada, b_ada)

    cos_t, sin_t = _rope_tables(seq)
    ones_t = jnp.concatenate([jnp.ones((n_ctx, 2 * (QK_ROPE // 2)), F32),
                              jnp.zeros((n_ctx, LANES - QK_ROPE), F32)], axis=1)
    zeros_t = jnp.zeros((n_ctx, LANES), F32)
    wr_t = w_router.T.astype(BF16)
    lat_row = lambda r: r // seq
    ctx_row = lambda r: bsz

    x2 = x.reshape(bsz * seq, d)
    xc2 = ctx.reshape(bsz * n_ctx, d)
    for l in range(depth):
        last = l == depth - 1
        mod3 = mod[l].reshape(MOD_ROWS * N_MOD, 1, d)
        w_in_r = _relayout_w_in(w_in[l])
        wq = _relayout_w_qb(w_qb[l])
        w_kv3 = w_kvb[l].reshape(KV_RANK, N_HEADS, QK_NOPE + V_DIM)
        wk = w_kv3[:, :, :QK_NOPE].reshape(KV_RANK, N_HEADS * QK_NOPE).astype(BF16)
        wvt = (w_kv3[:, :, QK_NOPE:].transpose(1, 2, 0).reshape(N_HEADS * V_DIM, KV_RANK)
               .astype(BF16))
        wp = w_pool[l].astype(BF16)
        wo = w_out[l].astype(BF16)
        sg, su, sd = w_sh_gate[l].astype(BF16), w_sh_up[l].astype(BF16), w_sh_down[l].astype(BF16)
        moe = functools.partial(_moe_routed, wg=w_exp_gate, wu=w_exp_up, wd=w_exp_down, layer=l,
                                sg=sg, su=su, sd=sd, mod3=mod3, gfin=final_norm_g)

        p = _in_proj(x2, mod3, lat_row, norm_mix_g[l], w_in_r, P_COLS)
        pc = _in_proj(xc2, mod3, ctx_row, norm_mix_g[l], w_in_r, LAT_W if last else P_COLS)
        q, k_lat, v_lat = _qkv(p, bsz, seq, cos_t, sin_t, q_norm_g[l], kv_norm_g[l], wq, wk, wvt, True)
        ctx_out = _qkv(pc, bsz, n_ctx, ones_t, zeros_t, q_norm_g[l], kv_norm_g[l], wq, wk, wvt, not last)
        k_ctx, v_ctx = ctx_out[-2:]
        y = _attention(q, [k_ctx, k_lat], [v_ctx, v_lat])
        x2, h2p, route = _mixer_out(y, p, x2, mod3, lat_row, seq, wp, pool_scale[l], wo,
                                    norm_ffn_g[l], wr_t, router_bias)
        if last:
            x2, = moe([(h2p, route, x2, lat_row)], final=True)
        else:
            yc = _attention(ctx_out[0], [k_ctx], [v_ctx])
            xc2, h2c, route_c = _mixer_out(yc, pc, xc2, mod3, ctx_row, n_ctx, wp, pool_scale[l], wo,
                                           norm_ffn_g[l], wr_t, router_bias)
            x2, xc2 = moe([(h2p, route, x2, lat_row), (h2c, route_c, xc2, ctx_row)], final=False)
    return x2.reshape(bsz, seq, d)
```

```python
import functools
import math

import jax
import jax.numpy as jnp
import numpy as np
from jax import lax
from jax.experimental import pallas as pl
from jax.experimental.pallas import tpu as pltpu
from jax.experimental.pallas import tpu_sc as plsc

F32 = jnp.float32
BF16 = jnp.bfloat16

D_MODEL = 2048
GRID_W = 64
N_HEADS = 16
QK_NOPE = 128
QK_ROPE = 64
V_DIM = 128
Q_RANK = 384
KV_RANK = 512
ROPE_THETA = 10000.0
POOL_DIM = 1024
POOL_GROUPS = 4
POOL_GROUP_DIM = 256
POOL_WINDOWS = (2, 4, 8, 16)
POOL_GROUP_OUT = 512
KV_COLS = KV_RANK + QK_ROPE
Q_END = KV_COLS + Q_RANK
POOL_END = Q_END + POOL_DIM
N_EXPERTS = 16
N_GROUPS = 4
EXPERTS_PER_GROUP = 4
D_EXPERT = 512
EPS = 1e-6
N_MOD = 6

LANES = 128
HEAD_W = 2 * LANES
LAT_W = 1024
P_COLS = LAT_W + POOL_DIM + 2 * D_MODEL
POOL_HALO = 16
MOD_ROWS = 8
Q_SCALE = (1.0 / math.sqrt(QK_NOPE + QK_ROPE)) * math.log2(math.e)
VMEM_LIMIT = 56 * 1024 * 1024
SUBLANES = 8
PACK_W = D_MODEL // 2
ROW_CHUNKS = PACK_W // LANES
ROUTE_ROWS = SUBLANES
MOE_TILE = 512
SC_WINDOW = 128
ATTN_TQ = 256
ATTN_KB = 512
V_HEAD_GROUP = 8
ATTN_HEADS_PER_STEP = 1
ATTN_PAIRS_PER_ITER = 2
MIX_CHAIN_ROWS = 256
COMBINE_CHUNKS = 2


def _sigmoid(x):
    return 1.0 / (1.0 + jnp.exp(-x))


def _pack_rows(y):
    half = y.shape[1] // 2
    return pltpu.pack_elementwise([y[:, :half], y[:, half:]], packed_dtype=BF16)


def _load_unpack(ref, live=None):
    w = ref[...]
    if live is not None:
        w = jnp.where(live, w, 0)
    lo = pltpu.unpack_elementwise(w, index=0, packed_dtype=BF16, unpacked_dtype=F32)
    hi = pltpu.unpack_elementwise(w, index=1, packed_dtype=BF16, unpacked_dtype=F32)
    return jnp.concatenate([lo, hi], axis=1)


def _rms(xf, g):
    ms = jnp.mean(xf * xf, axis=-1, keepdims=True)
    return xf * lax.rsqrt(ms + EPS) * g


def _params(n_axes):
    return pltpu.CompilerParams(dimension_semantics=("arbitrary",) * n_axes,
                                vmem_limit_bytes=VMEM_LIMIT)


def _ada_kernel(c_ref, w_ref, b_ref, o_ref):
    cf = c_ref[...]
    a = (cf * _sigmoid(cf)).astype(BF16)
    o_ref[0] = jnp.dot(a, w_ref[0].astype(BF16), preferred_element_type=F32) + b_ref[0]


def _ada_mod(c_rows, w_ada, b_ada):
    depth, d, n = w_ada.shape
    tn = 1024
    return pl.pallas_call(
        _ada_kernel,
        grid=(depth, n // tn),
        in_specs=[pl.BlockSpec((MOD_ROWS, d), lambda l, j: (0, 0)),
                  pl.BlockSpec((1, d, tn), lambda l, j: (l, 0, j)),
                  pl.BlockSpec((1, 1, tn), lambda l, j: (l, 0, j))],
        out_specs=pl.BlockSpec((1, MOD_ROWS, tn), lambda l, j: (l, 0, j)),
        out_shape=jax.ShapeDtypeStruct((depth, MOD_ROWS, n), F32),
        compiler_params=_params(2),
        name="ada_mod",
    )(c_rows, w_ada, b_ada.reshape(depth, 1, n))


def _mod_spec(k, brow):
    return pl.BlockSpec((1, 1, D_MODEL), lambda i, *_: (brow(i) * N_MOD + k, 0, 0))


def _inproj_kernel(x_ref, g_ref, sh_ref, sc_ref, w_ref, o_ref, h_scr, *, tn):
    y = _rms(x_ref[...], g_ref[...])
    h_scr[...] = (y * (1.0 + sc_ref[0]) + sh_ref[0]).astype(BF16)
    for c0 in range(0, o_ref.shape[1], tn):
        acc = jnp.dot(h_scr[...], w_ref[:, c0:c0 + tn], preferred_element_type=F32)
        if c0 >= LAT_W + POOL_DIM:
            acc = _sigmoid(acc)
        o_ref[:, c0:c0 + tn] = acc.astype(BF16)


def _in_proj(x2, mod3, brow, g, w, layer, n_cols):
    t, d = x2.shape
    tm = min(512, t)
    return pl.pallas_call(
        functools.partial(_inproj_kernel, tn=512),
        grid=(t // tm,),
        in_specs=[pl.BlockSpec((tm, d), lambda i: (i, 0)),
                  pl.BlockSpec((1, d), lambda i: (0, 0)),
                  _mod_spec(0, lambda i: brow(i * tm)),
                  _mod_spec(1, lambda i: brow(i * tm)),
                  pl.BlockSpec((None, d, n_cols), lambda i: (layer, 0, 0), pipeline_mode=pl.Buffered(1))],
        out_specs=pl.BlockSpec((tm, n_cols), lambda i: (i, 0)),
        out_shape=jax.ShapeDtypeStruct((t, n_cols), BF16),
        scratch_shapes=[pltpu.VMEM((tm, d), BF16)],
        compiler_params=_params(1),
        name="in_proj",
    )(x2, g.reshape(1, d), mod3, mod3, w)


def _qkv_kernel(lat_ref, cos_ref, sin_ref, gq_ref, gkv_ref, wq_ref, wk_ref, wvt_ref, *out_refs, with_q):
    if with_q:
        q_ref, k_ref, v_ref = out_refs
    else:
        k_ref, v_ref = out_refs
    cos = cos_ref[...]
    sin = sin_ref[...]

    def rope(grp):
        return grp * cos + pltpu.roll(grp, 2 * (QK_ROPE // 2), axis=1) * sin

    kvn = _rms(lat_ref[:, :KV_RANK].astype(F32), gkv_ref[...]).astype(BF16)
    kpe = rope(lat_ref[:, KV_RANK + Q_RANK:].astype(F32)).astype(BF16)
    for h in range(0, N_HEADS, 2):
        kn = jnp.dot(kvn, wk_ref[:, h * QK_NOPE:(h + 2) * QK_NOPE], preferred_element_type=F32)
        for hh in range(2):
            k_ref[0, h + hh, :, :QK_NOPE] = kn[:, hh * QK_NOPE:(hh + 1) * QK_NOPE].astype(BF16)
            k_ref[0, h + hh, :, QK_NOPE:] = kpe
    for h in range(0, N_HEADS, V_HEAD_GROUP):
        vt = lax.dot_general(wvt_ref[h * V_DIM:(h + V_HEAD_GROUP) * V_DIM, :], kvn,
                             (((1,), (1,)), ((), ())), preferred_element_type=F32)
        for hh in range(V_HEAD_GROUP):
            v_ref[0, h + hh] = vt[hh * V_DIM:(hh + 1) * V_DIM, :].astype(BF16)
    if with_q:
        qn = _rms(lat_ref[:, KV_RANK:KV_RANK + Q_RANK].astype(F32), gq_ref[...]).astype(BF16)
        for h in range(N_HEADS):
            qh = jnp.dot(qn, wq_ref[:, h * HEAD_W:(h + 1) * HEAD_W], preferred_element_type=F32)
            q_ref[0, h, :, :QK_NOPE] = (qh[:, :QK_NOPE] * Q_SCALE).astype(BF16)
            q_ref[0, h, :, QK_NOPE:] = (rope(qh[:, QK_NOPE:]) * Q_SCALE).astype(BF16)


def _qkv(p, b, n, cos_t, sin_t, gq, gkv, wq, wk, wvt, with_q):
    tm = min(256, n)
    tpb = n // tm
    rope_tiles = cos_t.shape[0] // tm
    head_spec = pl.BlockSpec((1, N_HEADS, tm, HEAD_W), lambda i: (i // tpb, 0, i % tpb, 0))
    tab_spec = pl.BlockSpec((tm, LANES), lambda i: ((i % tpb) % rope_tiles, 0))
    out_shape = [jax.ShapeDtypeStruct((b, N_HEADS, n, HEAD_W), BF16),
                 jax.ShapeDtypeStruct((b, N_HEADS, V_DIM, n), BF16)]
    out_specs = [head_spec,
                 pl.BlockSpec((1, N_HEADS, V_DIM, tm), lambda i: (i // tpb, 0, 0, i % tpb))]
    if with_q:
        out_shape = [jax.ShapeDtypeStruct((b, N_HEADS, n, HEAD_W), BF16)] + out_shape
        out_specs = [head_spec] + out_specs
    return pl.pallas_call(
        functools.partial(_qkv_kernel, with_q=with_q),
        grid=(b * tpb,),
        in_specs=[pl.BlockSpec((tm, LAT_W), lambda i: (i, 0)),
                  tab_spec, tab_spec,
                  pl.BlockSpec((1, Q_RANK), lambda i: (0, 0)),
                  pl.BlockSpec((1, KV_RANK), lambda i: (0, 0)),
                  pl.BlockSpec(wq.shape, lambda i: (0, 0)),
                  pl.BlockSpec(wk.shape, lambda i: (0, 0)),
                  pl.BlockSpec(wvt.shape, lambda i: (0, 0))],
        out_specs=out_specs,
        out_shape=out_shape,
        compiler_params=_params(1),
        name="qkv",
    )(p, cos_t, sin_t, gq.reshape(1, Q_RANK), gkv.reshape(1, KV_RANK), wq, wk, wvt)


def _row_fold(x, op):
    parts = [x[r:r + SUBLANES] for r in range(0, x.shape[0], SUBLANES)]
    a, b = parts[0], parts[1]
    for i in range(2, len(parts) - 1, 2):
        a, b = op(a, parts[i]), op(b, parts[i + 1])
    if len(parts) % 2:
        a = op(a, parts[-1])
    return op(a, b)


def _attn_kernel(q_ref, *refs, n_seg, kb, tq):
    k_refs, vt_refs = refs[:n_seg], refs[n_seg:2 * n_seg]
    o_ref, s_a, s_b, m_a, m_b = refs[2 * n_seg:]
    chunks = []
    off = 0
    for kr, vr in zip(k_refs, vt_refs):
        nk = kr.shape[2]
        for c0 in range(0, nk, kb):
            n = min(kb, nk - c0)
            chunks.append((kr, vr, c0, n, off))
            off += n

    tiles_per_head = q_ref.shape[2] // tq

    def head_rows(t):
        if q_ref.shape[1] == 1:
            return 0, pl.ds(pl.multiple_of(t * tq, tq), tq)
        return t // tiles_per_head, pl.ds(pl.multiple_of((t % tiles_per_head) * tq, tq), tq)

    def scores(t, s_buf, m_buf):
        g, rows = head_rows(t)
        q = q_ref[0, g, rows, :]
        mp = None
        for kr, vr, c0, n, o in chunks:
            s = lax.dot_general(kr[0, g, c0:c0 + n, :], q, (((1,), (1,)), ((), ())),
                                preferred_element_type=F32)
            s_buf[o:o + n, :] = s
            m = _row_fold(s, jnp.maximum)
            mp = m if mp is None else jnp.maximum(mp, m)
        m_buf[...] = mp

    def values(t, s_buf, m_buf):
        g, rows = head_rows(t)
        mrow = jnp.max(m_buf[...], axis=0, keepdims=True)
        lp = None
        acc = None
        for kr, vr, c0, n, o in chunks:
            pr = jnp.exp2(s_buf[o:o + n, :] - mrow)
            ls = _row_fold(pr, jnp.add)
            lp = ls if lp is None else lp + ls
            pv = jnp.dot(vr[0, g, :, c0:c0 + n], pr.astype(BF16), preferred_element_type=F32)
            acc = pv if acc is None else acc + pv
        l = jnp.sum(lp, axis=0, keepdims=True)
        o_ref[0, g, rows, :] = (acc / l).T.astype(BF16)

    nt = q_ref.shape[1] * tiles_per_head
    scores(0, s_a, m_a)
    if nt > 1:
        def pair(j):
            scores(2 * j + 1, s_b, m_b)
            values(2 * j, s_a, m_a)
            scores(2 * j + 2, s_a, m_a)
            values(2 * j + 1, s_b, m_b)

        def pairs(jj, carry):
            for k in range(ATTN_PAIRS_PER_ITER):
                pair(jj * ATTN_PAIRS_PER_ITER + k)
            return carry

        n_pairs = nt // 2 - 1
        n_iter = n_pairs // ATTN_PAIRS_PER_ITER
        lax.fori_loop(0, n_iter, pairs, 0)
        for j in range(n_iter * ATTN_PAIRS_PER_ITER, n_pairs):
            pair(j)
        scores(nt - 1, s_b, m_b)
        values(nt - 2, s_a, m_a)
        values(nt - 1, s_b, m_b)
    else:
        values(0, s_a, m_a)


def _attention(q, ks, vts):
    b, hh, n, _ = q.shape
    tq = min(ATTN_TQ, n)
    hg = ATTN_HEADS_PER_STEP
    n_tiles = hg * (n // tq)
    assert (n_tiles == 1 or n_tiles % 2 == 0) and hh % hg == 0
    nk_total = sum(k.shape[2] for k in ks)
    seg_spec = lambda a: pl.BlockSpec((1, hg) + a.shape[2:], lambda bi, h: (bi, h, 0, 0))
    return pl.pallas_call(
        functools.partial(_attn_kernel, n_seg=len(ks), kb=ATTN_KB, tq=tq),
        grid=(b, hh // hg),
        in_specs=[pl.BlockSpec((1, hg, n, HEAD_W), lambda bi, h: (bi, h, 0, 0))]
                 + [seg_spec(a) for a in ks] + [seg_spec(a) for a in vts],
        out_specs=pl.BlockSpec((1, hg, n, V_DIM), lambda bi, h: (bi, h, 0, 0)),
        out_shape=jax.ShapeDtypeStruct((b, hh, n, V_DIM), BF16),
        scratch_shapes=[pltpu.VMEM((nk_total, tq), F32), pltpu.VMEM((nk_total, tq), F32),
                        pltpu.VMEM((SUBLANES, tq), F32), pltpu.VMEM((SUBLANES, tq), F32)],
        compiler_params=_params(2),
        name="attention",
    )(q, *ks, *vts)


def _top2sum(a, b, c, d):
    s1, t1 = jnp.maximum(a, b), jnp.minimum(a, b)
    s2, t2 = jnp.maximum(c, d), jnp.minimum(c, d)
    return jnp.maximum(s1, s2) + jnp.maximum(jnp.minimum(s1, s2), jnp.maximum(t1, t2))


def _route(logits_t, bias):
    sc = _sigmoid(logits_t)
    sel = sc + bias
    sel_r = [sel[e:e + 1, :] for e in range(N_EXPERTS)]
    sc_r = [sc[e:e + 1, :] for e in range(N_EXPERTS)]
    epg = EXPERTS_PER_GROUP
    gs = [_top2sum(*sel_r[g * epg:(g + 1) * epg]) for g in range(N_GROUPS)]
    best, gi = gs[0], jnp.zeros(gs[0].shape, jnp.int32)
    for g in range(1, N_GROUPS):
        upd = gs[g] > best
        best = jnp.where(upd, gs[g], best)
        gi = jnp.where(upd, g, gi)

    def pick_group(rows, k):
        r = rows[k]
        for g in range(1, N_GROUPS):
            r = jnp.where(gi == g, rows[g * epg + k], r)
        return r

    in_sel = [pick_group(sel_r, k) for k in range(epg)]
    in_sc = [pick_group(sc_r, k) for k in range(epg)]
    b1, i1 = in_sel[0], jnp.zeros(gi.shape, jnp.int32)
    for k in range(1, epg):
        upd = in_sel[k] > b1
        b1 = jnp.where(upd, in_sel[k], b1)
        i1 = jnp.where(upd, k, i1)
    b2, i2 = None, None
    for k in range(epg):
        cand = jnp.where(i1 == k, -jnp.inf, in_sel[k])
        if b2 is None:
            b2, i2 = cand, jnp.zeros(gi.shape, jnp.int32)
        else:
            upd = cand > b2
            b2 = jnp.where(upd, cand, b2)
            i2 = jnp.where(upd, k, i2)

    def pick_local(idx):
        r = in_sc[0]
        for k in range(1, epg):
            r = jnp.where(idx == k, in_sc[k], r)
        return r

    s1, s2 = pick_local(i1), pick_local(i2)
    den = s1 + s2
    return gi * epg + i1, gi * epg + i2, s1 / den, s2 / den


def _mixout_kernel(y_ref, u_ref, up_ref, un_ref, gm_ref, gp_ref, wp_ref, ps_ref, wo_ref, x_ref,
                   g1_ref, gf_ref, sh2_ref, sc2_ref, wr_ref, rb_ref,
                   xo_ref, h2_ref, route_ref, *, tm, tpb, n_seq):
    i = pl.program_id(0)
    uext = jnp.concatenate([up_ref[...], u_ref[...], un_ref[...]], axis=0)
    sub = min(MIX_CHAIN_ROWS, tm)

    def mix_stage(r0):
        rows = slice(r0, r0 + sub)
        base = (i % tpb) * tm + r0
        r = lax.broadcasted_iota(jnp.int32, (sub, sub + 2 * POOL_HALO), 0)
        c = lax.broadcasted_iota(jnp.int32, (sub, sub + 2 * POOL_HALO), 1)
        rel = c - POOL_HALO - r
        jpos = base - POOL_HALO + c
        valid = (jpos >= 0) & (jpos < n_seq)
        tpos = base + lax.broadcasted_iota(jnp.int32, (sub, 1), 0)
        uwin = uext[r0:r0 + sub + 2 * POOL_HALO]
        uc = uext[r0 + POOL_HALO:r0 + POOL_HALO + sub]
        parts = []
        for g, w in enumerate(POOL_WINDOWS):
            sl = slice(g * POOL_GROUP_DIM, (g + 1) * POOL_GROUP_DIM)
            band = jnp.where(valid & (rel >= -(w // 2)) & (rel < w // 2), 1.0, 0.0).astype(BF16)
            cnt = jnp.clip(tpos + w // 2, 0, n_seq) - jnp.clip(tpos - w // 2, 0, n_seq)
            wsum = jnp.dot(band, uwin[:, sl], preferred_element_type=F32)
            z = wsum / cnt.astype(F32) - uc[:, sl].astype(F32)
            parts.append(jnp.dot(z.astype(BF16), wp_ref[g], preferred_element_type=F32))
        ypool = jnp.concatenate(parts, axis=1) * ps_ref[...]
        y_mla = jnp.concatenate([y_ref[0, h, rows, :] for h in range(N_HEADS)], axis=1)
        mix = gm_ref[rows, :].astype(F32) * y_mla.astype(F32) + gp_ref[rows, :].astype(F32) * ypool
        return jnp.dot(mix.astype(BF16), wo_ref[...], preferred_element_type=F32)

    def norm_route_stage(r0, y):
        rows = slice(r0, r0 + sub)
        xn = x_ref[rows, :] + g1_ref[0] * y
        xo_ref[rows, :] = xn
        h2f = _rms(xn, gf_ref[...]) * (1.0 + sc2_ref[0]) + sh2_ref[0]
        h2_ref[rows, :] = _pack_rows(h2f)
        logits_t = lax.dot_general(wr_ref[...], h2f.astype(BF16), (((1,), (1,)), ((), ())),
                                   preferred_element_type=F32)
        e1, e2, w1, w2 = _route(logits_t, rb_ref[...])
        zero = jnp.zeros((ROUTE_ROWS - 4, sub), F32)
        route_ref[:, rows] = jnp.concatenate([e1.astype(F32), e2.astype(F32), w1, w2, zero], axis=0)

    starts = list(range(0, tm, sub))
    pending = None
    for r0 in starts:
        y = mix_stage(r0)
        if pending is not None:
            norm_route_stage(*pending)
        pending = (r0, y)
    norm_route_stage(*pending)


def _mixer_out(y, p, x2, mod3, brow, n_seq, wp, ps, wo, gf, wr_t, rb):
    t, d = x2.shape
    tm = min(2 * MIX_CHAIN_ROWS, n_seq)
    tpb = n_seq // tm
    hpt = tm // POOL_HALO
    n_halo = t // POOL_HALO
    row = lambda i: (i, 0)
    const2 = lambda i: (0, 0)
    mrow = lambda i: brow(i * tm)
    return pl.pallas_call(
        functools.partial(_mixout_kernel, tm=tm, tpb=tpb, n_seq=n_seq),
        grid=(t // tm,),
        in_specs=[pl.BlockSpec((1, N_HEADS, tm, V_DIM), lambda i: (i // tpb, 0, i % tpb, 0)),
                  pl.BlockSpec((tm, POOL_DIM), lambda i: (i, LAT_W // POOL_DIM)),
                  pl.BlockSpec((POOL_HALO, POOL_DIM),
                               lambda i: (jnp.maximum(i * hpt - 1, 0), LAT_W // POOL_DIM)),
                  pl.BlockSpec((POOL_HALO, POOL_DIM),
                               lambda i: (jnp.minimum((i + 1) * hpt, n_halo - 1), LAT_W // POOL_DIM)),
                  pl.BlockSpec((tm, d), lambda i: (i, (LAT_W + POOL_DIM) // d)),
                  pl.BlockSpec((tm, d), lambda i: (i, (LAT_W + POOL_DIM) // d + 1)),
                  pl.BlockSpec(wp.shape, lambda i: (0, 0, 0)),
                  pl.BlockSpec((1, d), const2),
                  pl.BlockSpec((d, d), const2, pipeline_mode=pl.Buffered(1)),
                  pl.BlockSpec((tm, d), row),
                  _mod_spec(2, mrow),
                  pl.BlockSpec((1, d), const2),
                  _mod_spec(3, mrow),
                  _mod_spec(4, mrow),
                  pl.BlockSpec((N_EXPERTS, d), const2),
                  pl.BlockSpec((N_EXPERTS, 1), const2)],
        out_specs=[pl.BlockSpec((tm, d), row),
                   pl.BlockSpec((tm, PACK_W), row),
                   pl.BlockSpec((ROUTE_ROWS, tm), lambda i: (0, i))],
        out_shape=[jax.ShapeDtypeStruct((t, d), F32),
                   jax.ShapeDtypeStruct((t, PACK_W), jnp.int32),
                   jax.ShapeDtypeStruct((ROUTE_ROWS, t), F32)],
        compiler_params=_params(1),
        name="mixer_out",
    )(y, p, p, p, p, p, wp, ps.reshape(1, d), wo, x2, mod3, gf.reshape(1, d), mod3, mod3, wr_t,
      rb.reshape(N_EXPERTS, 1))


def _chunk_view(x):
    n = x.shape[0]
    return (x.reshape(n // SUBLANES, SUBLANES, ROW_CHUNKS, LANES).transpose(0, 2, 1, 3)
            .reshape(n * ROW_CHUNKS, LANES))


def _row_view(c):
    n = c.shape[0] // ROW_CHUNKS
    return (c.reshape(n // SUBLANES, ROW_CHUNKS, SUBLANES, LANES).transpose(0, 2, 1, 3)
            .reshape(n, PACK_W))


def _sc_index_chunks(idx):
    r = idx.reshape(-1, 1, SUBLANES)
    j = jnp.arange(ROW_CHUNKS, dtype=jnp.int32)[None, :, None]
    ids = (r // SUBLANES) * (SUBLANES * ROW_CHUNKS) + j * SUBLANES + r % SUBLANES
    return ids.reshape(1, idx.shape[0] * ROW_CHUNKS)


def _sc_mesh():
    return plsc.VectorSubcoreMesh(core_axis_name="core", subcore_axis_name="subcore")


def _sc_gather_rows(table, idx):
    n, m = table.shape[0], idx.shape[0]
    n_idx = m * ROW_CHUNKS
    assert n_idx % SC_WINDOW == 0 and n % SUBLANES == 0 and m % SUBLANES == 0

    @pl.kernel(out_type=jax.ShapeDtypeStruct((n_idx, LANES), jnp.int32), mesh=_sc_mesh())
    def gather(t_hbm, i_hbm, o_hbm):
        def body(i_vmem, o_vmem):
            pltpu.sync_copy(t_hbm.at[i_vmem.at[0]], o_vmem)

        pltpu.emit_pipeline(
            body,
            grid=(n_idx // SC_WINDOW,),
            in_specs=[pl.BlockSpec((1, SC_WINDOW), index_map=lambda i: (0, i))],
            out_specs=[pl.BlockSpec((SC_WINDOW, LANES), index_map=lambda i: (i, 0))],
            core_axis_name=("core", "subcore"),
            dimension_semantics=(pltpu.PARALLEL,),
        )(i_hbm, o_hbm)

    return _row_view(gather(_chunk_view(table), _sc_index_chunks(idx)))


def _sc_scatter_rows(rows, idx, n_out):
    n, m = rows.shape[0], idx.shape[0]
    n_idx = m * ROW_CHUNKS
    src_steps = n * ROW_CHUNKS // SC_WINDOW
    assert n_idx % SC_WINDOW == 0 and (n * ROW_CHUNKS) % SC_WINDOW == 0 and m % n == 0
    assert n % SUBLANES == 0 and n_out % SUBLANES == 0

    @pl.kernel(out_type=jax.ShapeDtypeStruct((n_out * ROW_CHUNKS, LANES), jnp.int32), mesh=_sc_mesh())
    def scatter(x_hbm, i_hbm, o_hbm):
        def body(x_vmem, i_vmem):
            pltpu.sync_copy(x_vmem, o_hbm.at[i_vmem.at[0]])

        pltpu.emit_pipeline(
            body,
            grid=(n_idx // SC_WINDOW,),
            in_specs=[pl.BlockSpec((SC_WINDOW, LANES), index_map=lambda i: (i % src_steps, 0)),
                      pl.BlockSpec((1, SC_WINDOW), index_map=lambda i: (0, i))],
            out_specs=[],
            core_axis_name=("core", "subcore"),
            dimension_semantics=(pltpu.PARALLEL,),
        )(x_hbm, i_hbm)

    return _row_view(scatter(_chunk_view(rows), _sc_index_chunks(idx)))


def _dispatch_plan(route):
    t = route.shape[1]
    n_pairs = 2 * t
    n_tiles = n_pairs // MOE_TILE + N_EXPERTS
    blk = LANES
    e = jnp.concatenate([route[0], route[1]]).astype(jnp.int32)
    onehot = (e[:, None] == jnp.arange(N_EXPERTS, dtype=jnp.int32)[None, :]).astype(F32)
    oh3 = onehot.reshape(n_pairs // blk, blk, N_EXPERTS)
    tri = (jnp.arange(blk)[:, None] >= jnp.arange(blk)[None, :]).astype(F32)
    within = jnp.einsum("ij,bjk->bik", tri, oh3)
    totals = within[:, -1, :]
    before = jnp.cumsum(totals, axis=0) - totals
    csum = (within + before[:, None, :]).reshape(n_pairs, N_EXPERTS)
    counts = jnp.sum(totals, axis=0).astype(jnp.int32)
    padded = ((counts + MOE_TILE - 1) // MOE_TILE) * MOE_TILE
    ends = jnp.cumsum(padded)
    starts = ends - padded
    pair_slot = jnp.sum(onehot * (csum - 1.0 + starts.astype(F32)[None, :]), axis=1).astype(jnp.int32)
    tile_start = jnp.arange(n_tiles, dtype=jnp.int32) * MOE_TILE
    tile_e = jnp.minimum(jnp.sum((tile_start[:, None] >= ends[None, :]).astype(jnp.int32), axis=1), N_EXPERTS - 1)
    e_onehot = tile_e[:, None] == jnp.arange(N_EXPERTS, dtype=jnp.int32)[None, :]
    filled = jnp.sum(jnp.where(e_onehot, (starts + counts)[None, :], 0), axis=1)
    tile_rows = jnp.clip(filled - tile_start, 0, MOE_TILE).astype(jnp.int32)
    last_used = jnp.maximum(ends[-1] // MOE_TILE - 1, 0)
    tile_expert = jnp.where(tile_start < ends[-1], tile_e, tile_e[last_used]).astype(jnp.int32)
    return pair_slot, tile_expert, tile_rows


def _swiglu(xb, wg, wu, wd):
    gate = jnp.dot(xb, wg, preferred_element_type=F32)
    up = jnp.dot(xb, wu, preferred_element_type=F32)
    a = ((gate * _sigmoid(gate)) * up).astype(BF16)
    return jnp.dot(a, wd, preferred_element_type=F32)


def _ffn_kernel(te_ref, rows_ref, xs_ref, wg_ref, wu_ref, wd_ref, ys_ref, wg_s, wu_s, wd_s):
    i = pl.program_id(0)
    new_expert = jnp.logical_or(i == 0, te_ref[i] != te_ref[jnp.maximum(i - 1, 0)])

    @pl.when(new_expert)
    def _():
        wg_s[...] = wg_ref[0].astype(BF16)
        wu_s[...] = wu_ref[0].astype(BF16)
        wd_s[...] = wd_ref[0].astype(BF16)

    @pl.when(rows_ref[i] > 0)
    def _():
        live = lax.broadcasted_iota(jnp.int32, (MOE_TILE, 1), 0) < rows_ref[i]
        xb = _load_unpack(xs_ref, live).astype(BF16)
        ys_ref[...] = _pack_rows(_swiglu(xb, wg_s[...], wu_s[...], wd_s[...]))

    @pl.when(rows_ref[i] == 0)
    def _():
        ys_ref[...] = jnp.zeros_like(ys_ref)


def _expert_ffn(xs, tile_expert, tile_rows, wg, wu, wd, layer):
    n_slots = xs.shape[0]
    d = D_MODEL
    blk = pl.BlockSpec((MOE_TILE, PACK_W), lambda i, te, nr: (i, 0))
    return pl.pallas_call(
        _ffn_kernel,
        grid_spec=pltpu.PrefetchScalarGridSpec(
            num_scalar_prefetch=2,
            grid=(n_slots // MOE_TILE,),
            in_specs=[blk,
                      pl.BlockSpec((None, 1, d, D_EXPERT), lambda i, te, nr: (layer, te[i], 0, 0)),
                      pl.BlockSpec((None, 1, d, D_EXPERT), lambda i, te, nr: (layer, te[i], 0, 0)),
                      pl.BlockSpec((None, 1, D_EXPERT, d), lambda i, te, nr: (layer, te[i], 0, 0))],
            out_specs=blk,
            scratch_shapes=[pltpu.VMEM((d, D_EXPERT), BF16), pltpu.VMEM((d, D_EXPERT), BF16),
                            pltpu.VMEM((D_EXPERT, d), BF16)]),
        out_shape=jax.ShapeDtypeStruct(xs.shape, jnp.int32),
        compiler_params=_params(1),
        name="expert_ffn",
    )(tile_expert, tile_rows, xs, wg, wu, wd)


def _combine_kernel(h_ref, y1_ref, y2_ref, w_ref, sg_ref, su_ref, sd_ref, x_ref, g2_ref, gfin_ref, o_ref,
                    *, final):
    y = _swiglu(_load_unpack(h_ref).astype(BF16), sg_ref[...], su_ref[...], sd_ref[...])
    w = w_ref[...]
    y = y + w[:, 0:1] * _load_unpack(y1_ref) + w[:, 1:2] * _load_unpack(y2_ref)
    xn = x_ref[...] + g2_ref[0] * y
    if final:
        xn = _rms(xn, gfin_ref[...])
    o_ref[...] = xn


def _moe_combine(h2p, yg, w12, sg, su, sd, x2, set_off, x_off, n_rows, mod3, brow, gfin, final):
    d = x2.shape[1]
    tm = min(512, n_rows)
    assert set_off % tm == 0 and x_off % tm == 0 and n_rows % tm == 0
    s_blk, x_blk, n_blk = set_off // tm, x_off // tm, n_rows // tm
    const2 = lambda i: (0, 0)
    pk = lambda o: pl.BlockSpec((tm, PACK_W), lambda i: (i + o, 0))
    x_spec = pl.BlockSpec((tm, d), lambda i: (i + x_blk, 0))
    return pl.pallas_call(
        functools.partial(_combine_kernel, final=final),
        grid=(n_blk,),
        in_specs=[pk(s_blk), pk(0), pk(n_blk),
                  pl.BlockSpec((tm, LANES), lambda i: (i + s_blk, 0)),
                  pl.BlockSpec(sg.shape, const2), pl.BlockSpec(su.shape, const2), pl.BlockSpec(sd.shape, const2),
                  x_spec,
                  _mod_spec(5, lambda i: brow((i + x_blk) * tm)),
                  pl.BlockSpec((1, d), const2)],
        out_specs=x_spec,
        out_shape=jax.ShapeDtypeStruct(x2.shape, F32),
        input_output_aliases={7: 0},
        compiler_params=_params(1),
        name="moe_combine",
    )(h2p, yg, yg, w12, sg, su, sd, x2, mod3, gfin.reshape(1, d))


def _moe_routed(parts, wg, wu, wd, layer, sg, su, sd, mod3, gfin, final):
    h2p = parts[0][0] if len(parts) == 1 else jnp.concatenate([pt[0] for pt in parts], axis=0)
    route = parts[0][1] if len(parts) == 1 else jnp.concatenate([pt[1] for pt in parts], axis=1)
    t_all = h2p.shape[0]
    pair_slot, tile_expert, tile_rows = _dispatch_plan(route)
    n_slots = tile_expert.shape[0] * MOE_TILE
    xs = _sc_scatter_rows(h2p, pair_slot, n_slots)
    ys = _expert_ffn(xs, tile_expert, tile_rows, wg, wu, wd, layer)
    w12 = jnp.concatenate([route[2:4].T, jnp.zeros((t_all, LANES - 2), F32)], axis=1)
    outs, set_off = [], 0
    for _, _, x2, brow in parts:
        t = x2.shape[0]
        n_rows = t // COMBINE_CHUNKS if t % (COMBINE_CHUNKS * 512) == 0 else t
        for x_off in range(0, t, n_rows):
            lo = set_off + x_off
            idx = jnp.concatenate([pair_slot[lo:lo + n_rows], pair_slot[t_all + lo:t_all + lo + n_rows]])
            yg = _sc_gather_rows(ys, idx)
            x2 = _moe_combine(h2p, yg, w12, sg, su, sd, x2, lo, x_off, n_rows, mod3, brow, gfin, final)
        outs.append(x2)
        set_off += t
    return outs


def _rope_tables(seq):
    rows = seq // GRID_W
    row = np.repeat(np.arange(rows), GRID_W).astype(np.float32)
    col = np.tile(np.arange(GRID_W), rows).astype(np.float32)
    axis_dim = QK_ROPE // 2
    inv = (ROPE_THETA ** (-np.arange(0, axis_dim, 2, dtype=np.float32) / axis_dim)).astype(np.float32)
    ang = np.concatenate([row[:, None] * inv, col[:, None] * inv], axis=-1).astype(np.float32)
    cos, sin = np.cos(ang), np.sin(ang)
    zero = np.zeros_like(cos)
    return (jnp.asarray(np.concatenate([cos, cos, zero, zero], axis=1)),
            jnp.asarray(np.concatenate([-sin, sin, zero, zero], axis=1)))


def _rope_group(w, start):
    half = QK_ROPE // 2
    x1, x2 = w[..., start:start + half], w[..., start + half:start + 2 * half]
    return jnp.concatenate([x1, x2, x2, x1], axis=-1)


def _relayout_w_in(w):
    return jnp.concatenate([w[..., :KV_RANK], w[..., KV_COLS:Q_END], _rope_group(w, KV_RANK),
                            w[..., Q_END:]], axis=-1).astype(BF16)


def _relayout_w_qb(w):
    dq = QK_NOPE + QK_ROPE
    cols = []
    for h in range(N_HEADS):
        cols += [w[:, h * dq:h * dq + QK_NOPE], _rope_group(w, h * dq + QK_NOPE)]
    return jnp.concatenate(cols, axis=1).astype(BF16)


def kernel(x, c, ctx, c_ctx, w_ada, b_ada, norm_mix_g, norm_ffn_g, w_in, q_norm_g, kv_norm_g, w_qb, w_kvb, w_pool, pool_scale, w_out, w_router, router_bias, w_exp_gate, w_exp_up, w_exp_down, w_sh_gate, w_sh_up, w_sh_down, final_norm_g):
    bsz, seq, d = x.shape
    n_ctx = ctx.shape[1]
    depth = w_ada.shape[0]
    assert bsz < MOD_ROWS and d == D_MODEL

    c_rows = jnp.concatenate([c, c_ctx[None], jnp.zeros((MOD_ROWS - bsz - 1, d), F32)], axis=0)
    mod = _ada_mod(c_rows, w_ada, b_ada)

    cos_t, sin_t = _rope_tables(seq)
    ones_t = jnp.concatenate([jnp.ones((n_ctx, 2 * (QK_ROPE // 2)), F32),
                              jnp.zeros((n_ctx, LANES - QK_ROPE), F32)], axis=1)
    zeros_t = jnp.zeros((n_ctx, LANES), F32)
    wr_t = w_router.T.astype(BF16)
    w_in_r = _relayout_w_in(w_in)
    lat_row = lambda r: r // seq
    ctx_row = lambda r: bsz

    x2 = x.reshape(bsz * seq, d)
    xc2 = ctx.reshape(bsz * n_ctx, d)
    for l in range(depth):
        last = l == depth - 1
        mod3 = mod[l].reshape(MOD_ROWS * N_MOD, 1, d)
        wq = _relayout_w_qb(w_qb[l])
        w_kv3 = w_kvb[l].reshape(KV_RANK, N_HEADS, QK_NOPE + V_DIM)
        wk = w_kv3[:, :, :QK_NOPE].reshape(KV_RANK, N_HEADS * QK_NOPE).astype(BF16)
        wvt = (w_kv3[:, :, QK_NOPE:].transpose(1, 2, 0).reshape(N_HEADS * V_DIM, KV_RANK)
               .astype(BF16))
        wp = w_pool[l].astype(BF16)
        wo = w_out[l].astype(BF16)
        sg, su, sd = w_sh_gate[l].astype(BF16), w_sh_up[l].astype(BF16), w_sh_down[l].astype(BF16)
        moe = functools.partial(_moe_routed, wg=w_exp_gate, wu=w_exp_up, wd=w_exp_down, layer=l,
                                sg=sg, su=su, sd=sd, mod3=mod3, gfin=final_norm_g)

        p = _in_proj(x2, mod3, lat_row, norm_mix_g[l], w_in_r, l, P_COLS)
        pc = _in_proj(xc2, mod3, ctx_row, norm_mix_g[l], w_in_r, l, LAT_W if last else P_COLS)
        q, k_lat, v_lat = _qkv(p, bsz, seq, cos_t, sin_t, q_norm_g[l], kv_norm_g[l], wq, wk, wvt, True)
        ctx_out = _qkv(pc, bsz, n_ctx, ones_t, zeros_t, q_norm_g[l], kv_norm_g[l], wq, wk, wvt, not last)
        k_ctx, v_ctx = ctx_out[-2:]
        y = _attention(q, [k_ctx, k_lat], [v_ctx, v_lat])
        x2, h2p, route = _mixer_out(y, p, x2, mod3, lat_row, seq, wp, pool_scale[l], wo,
                                    norm_ffn_g[l], wr_t, router_bias)
        if last:
            x2, = moe([(h2p, route, x2, lat_row)], final=True)
        else:
            yc = _attention(ctx_out[0], [k_ctx], [v_ctx])
            xc2, h2c, route_c = _mixer_out(yc, pc, xc2, mod3, ctx_row, n_ctx, wp, pool_scale[l], wo,
                                           norm_ffn_g[l], wr_t, router_bias)
            x2, xc2 = moe([(h2p, route, x2, lat_row), (h2c, route_c, xc2, ctx_row)], final=False)
    return x2.reshape(bsz, seq, d)
```

```python
import functools
import math

import jax
import jax.numpy as jnp
import numpy as np
from jax import lax
from jax.experimental import pallas as pl
from jax.experimental.pallas import tpu as pltpu
from jax.experimental.pallas import tpu_sc as plsc

F32 = jnp.float32
BF16 = jnp.bfloat16

D_MODEL = 2048
GRID_W = 64
N_HEADS = 16
QK_NOPE = 128
QK_ROPE = 64
V_DIM = 128
Q_RANK = 384
KV_RANK = 512
ROPE_THETA = 10000.0
POOL_DIM = 1024
POOL_GROUPS = 4
POOL_GROUP_DIM = 256
POOL_WINDOWS = (2, 4, 8, 16)
POOL_GROUP_OUT = 512
KV_COLS = KV_RANK + QK_ROPE
Q_END = KV_COLS + Q_RANK
POOL_END = Q_END + POOL_DIM
N_EXPERTS = 16
N_GROUPS = 4
EXPERTS_PER_GROUP = 4
D_EXPERT = 512
EPS = 1e-6
N_MOD = 6

LANES = 128
HEAD_W = 2 * LANES
LAT_W = 1024
P_COLS = LAT_W + POOL_DIM + 2 * D_MODEL
POOL_HALO = 16
MOD_ROWS = 8
Q_SCALE = (1.0 / math.sqrt(QK_NOPE + QK_ROPE)) * math.log2(math.e)
VMEM_LIMIT = 56 * 1024 * 1024
SUBLANES = 8
PACK_W = D_MODEL // 2
ROW_CHUNKS = PACK_W // LANES
ROUTE_ROWS = SUBLANES
MOE_TILE = 512
SC_WINDOW = 128
ATTN_TQ = 256
ATTN_KB = 512
V_HEAD_GROUP = 8
ATTN_HEADS_PER_STEP = 1
ATTN_PAIRS_PER_ITER = 3
MIX_CHAIN_ROWS = 256
COMBINE_CHUNKS = 2


def _sigmoid(x):
    return 1.0 / (1.0 + jnp.exp(-x))


def _pack_rows(y):
    half = y.shape[1] // 2
    return pltpu.pack_elementwise([y[:, :half], y[:, half:]], packed_dtype=BF16)


def _load_unpack(ref, live=None):
    w = ref[...]
    if live is not None:
        w = jnp.where(live, w, 0)
    lo = pltpu.unpack_elementwise(w, index=0, packed_dtype=BF16, unpacked_dtype=F32)
    hi = pltpu.unpack_elementwise(w, index=1, packed_dtype=BF16, unpacked_dtype=F32)
    return jnp.concatenate([lo, hi], axis=1)


def _rms(xf, g):
    ms = jnp.mean(xf * xf, axis=-1, keepdims=True)
    return xf * lax.rsqrt(ms + EPS) * g


def _params(n_axes):
    return pltpu.CompilerParams(dimension_semantics=("arbitrary",) * n_axes,
                                vmem_limit_bytes=VMEM_LIMIT)


def _ada_kernel(c_ref, w_ref, b_ref, o_ref):
    cf = c_ref[...]
    a = (cf * _sigmoid(cf)).astype(BF16)
    o_ref[0] = jnp.dot(a, w_ref[0].astype(BF16), preferred_element_type=F32) + b_ref[0]


def _ada_mod(c_rows, w_ada, b_ada):
    depth, d, n = w_ada.shape
    tn = 1024
    return pl.pallas_call(
        _ada_kernel,
        grid=(depth, n // tn),
        in_specs=[pl.BlockSpec((MOD_ROWS, d), lambda l, j: (0, 0)),
                  pl.BlockSpec((1, d, tn), lambda l, j: (l, 0, j)),
                  pl.BlockSpec((1, 1, tn), lambda l, j: (l, 0, j))],
        out_specs=pl.BlockSpec((1, MOD_ROWS, tn), lambda l, j: (l, 0, j)),
        out_shape=jax.ShapeDtypeStruct((depth, MOD_ROWS, n), F32),
        compiler_params=_params(2),
        name="ada_mod",
    )(c_rows, w_ada, b_ada.reshape(depth, 1, n))


def _mod_spec(k, brow):
    return pl.BlockSpec((1, 1, D_MODEL), lambda i, *_: (brow(i) * N_MOD + k, 0, 0))


def _inproj_kernel(x_ref, g_ref, sh_ref, sc_ref, w_ref, o_ref, h_scr, *, tn):
    y = _rms(x_ref[...], g_ref[...])
    h_scr[...] = (y * (1.0 + sc_ref[0]) + sh_ref[0]).astype(BF16)
    for c0 in range(0, o_ref.shape[1], tn):
        acc = jnp.dot(h_scr[...], w_ref[:, c0:c0 + tn], preferred_element_type=F32)
        if c0 >= LAT_W + POOL_DIM:
            acc = _sigmoid(acc)
        o_ref[:, c0:c0 + tn] = acc.astype(BF16)


def _in_proj(x2, mod3, brow, g, w, layer, n_cols):
    t, d = x2.shape
    tm = min(512, t)
    return pl.pallas_call(
        functools.partial(_inproj_kernel, tn=512),
        grid=(t // tm,),
        in_specs=[pl.BlockSpec((tm, d), lambda i: (i, 0)),
                  pl.BlockSpec((1, d), lambda i: (0, 0)),
                  _mod_spec(0, lambda i: brow(i * tm)),
                  _mod_spec(1, lambda i: brow(i * tm)),
                  pl.BlockSpec((None, d, n_cols), lambda i: (layer, 0, 0), pipeline_mode=pl.Buffered(1))],
        out_specs=pl.BlockSpec((tm, n_cols), lambda i: (i, 0)),
        out_shape=jax.ShapeDtypeStruct((t, n_cols), BF16),
        scratch_shapes=[pltpu.VMEM((tm, d), BF16)],
        compiler_params=_params(1),
        name="in_proj",
    )(x2, g.reshape(1, d), mod3, mod3, w)


def _qkv_kernel(lat_ref, cos_ref, sin_ref, gq_ref, gkv_ref, wq_ref, wk_ref, wvt_ref, *out_refs, with_q):
    if with_q:
        q_ref, k_ref, v_ref = out_refs
    else:
        k_ref, v_ref = out_refs
    cos = cos_ref[...]
    sin = sin_ref[...]

    def rope(grp):
        return grp * cos + pltpu.roll(grp, 2 * (QK_ROPE // 2), axis=1) * sin

    kvn = _rms(lat_ref[:, :KV_RANK].astype(F32), gkv_ref[...]).astype(BF16)
    kpe = rope(lat_ref[:, KV_RANK + Q_RANK:].astype(F32)).astype(BF16)
    for h in range(0, N_HEADS, 2):
        kn = jnp.dot(kvn, wk_ref[:, h * QK_NOPE:(h + 2) * QK_NOPE], preferred_element_type=F32)
        for hh in range(2):
            k_ref[0, h + hh, :, :QK_NOPE] = kn[:, hh * QK_NOPE:(hh + 1) * QK_NOPE].astype(BF16)
            k_ref[0, h + hh, :, QK_NOPE:] = kpe
    for h in range(0, N_HEADS, V_HEAD_GROUP):
        vt = lax.dot_general(wvt_ref[h * V_DIM:(h + V_HEAD_GROUP) * V_DIM, :], kvn,
                             (((1,), (1,)), ((), ())), preferred_element_type=F32)
        for hh in range(V_HEAD_GROUP):
            v_ref[0, h + hh] = vt[hh * V_DIM:(hh + 1) * V_DIM, :].astype(BF16)
    if with_q:
        qn = _rms(lat_ref[:, KV_RANK:KV_RANK + Q_RANK].astype(F32), gq_ref[...]).astype(BF16)
        for h in range(N_HEADS):
            qh = jnp.dot(qn, wq_ref[:, h * HEAD_W:(h + 1) * HEAD_W], preferred_element_type=F32)
            q_ref[0, h, :, :QK_NOPE] = (qh[:, :QK_NOPE] * Q_SCALE).astype(BF16)
            q_ref[0, h, :, QK_NOPE:] = (rope(qh[:, QK_NOPE:]) * Q_SCALE).astype(BF16)


def _qkv(p, b, n, cos_t, sin_t, gq, gkv, wq, wk, wvt, with_q):
    tm = min(256, n)
    tpb = n // tm
    rope_tiles = cos_t.shape[0] // tm
    head_spec = pl.BlockSpec((1, N_HEADS, tm, HEAD_W), lambda i: (i // tpb, 0, i % tpb, 0))
    tab_spec = pl.BlockSpec((tm, LANES), lambda i: ((i % tpb) % rope_tiles, 0))
    out_shape = [jax.ShapeDtypeStruct((b, N_HEADS, n, HEAD_W), BF16),
                 jax.ShapeDtypeStruct((b, N_HEADS, V_DIM, n), BF16)]
    out_specs = [head_spec,
                 pl.BlockSpec((1, N_HEADS, V_DIM, tm), lambda i: (i // tpb, 0, 0, i % tpb))]
    if with_q:
        out_shape = [jax.ShapeDtypeStruct((b, N_HEADS, n, HEAD_W), BF16)] + out_shape
        out_specs = [head_spec] + out_specs
    return pl.pallas_call(
        functools.partial(_qkv_kernel, with_q=with_q),
        grid=(b * tpb,),
        in_specs=[pl.BlockSpec((tm, LAT_W), lambda i: (i, 0)),
                  tab_spec, tab_spec,
                  pl.BlockSpec((1, Q_RANK), lambda i: (0, 0)),
                  pl.BlockSpec((1, KV_RANK), lambda i: (0, 0)),
                  pl.BlockSpec(wq.shape, lambda i: (0, 0)),
                  pl.BlockSpec(wk.shape, lambda i: (0, 0)),
                  pl.BlockSpec(wvt.shape, lambda i: (0, 0))],
        out_specs=out_specs,
        out_shape=out_shape,
        compiler_params=_params(1),
        name="qkv",
    )(p, cos_t, sin_t, gq.reshape(1, Q_RANK), gkv.reshape(1, KV_RANK), wq, wk, wvt)


def _row_fold(x, op):
    parts = [x[r:r + SUBLANES] for r in range(0, x.shape[0], SUBLANES)]
    a, b = parts[0], parts[1]
    for i in range(2, len(parts) - 1, 2):
        a, b = op(a, parts[i]), op(b, parts[i + 1])
    if len(parts) % 2:
        a = op(a, parts[-1])
    return op(a, b)


def _attn_kernel(q_ref, *refs, n_seg, kb, tq):
    k_refs, vt_refs = refs[:n_seg], refs[n_seg:2 * n_seg]
    o_ref, s_a, s_b, m_a, m_b = refs[2 * n_seg:]
    chunks = []
    off = 0
    for kr, vr in zip(k_refs, vt_refs):
        nk = kr.shape[2]
        for c0 in range(0, nk, kb):
            n = min(kb, nk - c0)
            chunks.append((kr, vr, c0, n, off))
            off += n

    n_heads, nt = q_ref.shape[1], q_ref.shape[2] // tq
    bufs = ((s_a, m_a), (s_b, m_b))

    def tile_rows(t):
        return pl.ds(pl.multiple_of(t * tq, tq), tq)

    def scores(g, t, s_buf, m_buf):
        rows = tile_rows(t)
        q = q_ref[0, g, rows, :]
        mp = None
        for kr, vr, c0, n, o in chunks:
            s = lax.dot_general(kr[0, g, c0:c0 + n, :], q, (((1,), (1,)), ((), ())),
                                preferred_element_type=F32)
            s_buf[o:o + n, :] = s
            m = _row_fold(s, jnp.maximum)
            mp = m if mp is None else jnp.maximum(mp, m)
        m_buf[...] = mp

    def values(g, t, s_buf, m_buf):
        rows = tile_rows(t)
        mrow = jnp.max(m_buf[...], axis=0, keepdims=True)
        lp = None
        acc = None
        for kr, vr, c0, n, o in chunks:
            pr = jnp.exp2(s_buf[o:o + n, :] - mrow)
            ls = _row_fold(pr, jnp.add)
            lp = ls if lp is None else lp + ls
            pv = jnp.dot(vr[0, g, :, c0:c0 + n], pr.astype(BF16), preferred_element_type=F32)
            acc = pv if acc is None else acc + pv
        l = jnp.sum(lp, axis=0, keepdims=True)
        o_ref[0, g, rows, :] = (acc / l).T.astype(BF16)

    def phase(g, t, parity):
        scores(g, t + 1, *bufs[1 - parity])
        values(g, t, *bufs[parity])

    scores(0, 0, *bufs[0])
    for g in range(n_heads):
        par0 = (g * nt) % 2
        n_pairs = (nt - 1) // 2

        def pair(j, g=g, par0=par0):
            phase(g, 2 * j, par0)
            phase(g, 2 * j + 1, 1 - par0)

        def pairs(jj, carry, pair=pair):
            for k in range(ATTN_PAIRS_PER_ITER):
                pair(jj * ATTN_PAIRS_PER_ITER + k)
            return carry

        n_iter = n_pairs // ATTN_PAIRS_PER_ITER
        if n_iter:
            lax.fori_loop(0, n_iter, pairs, 0)
        for j in range(n_iter * ATTN_PAIRS_PER_ITER, n_pairs):
            pair(j)
        if (nt - 1) % 2:
            phase(g, nt - 2, (par0 + nt - 2) % 2)
        last = bufs[(par0 + nt - 1) % 2]
        if g + 1 < n_heads:
            scores(g + 1, 0, *bufs[((g + 1) * nt) % 2])
        values(g, nt - 1, *last)


def _attention(q, ks, vts):
    b, hh, n, _ = q.shape
    tq = min(ATTN_TQ, n)
    hg = ATTN_HEADS_PER_STEP if n > tq else hh
    assert hh % hg == 0
    nk_total = sum(k.shape[2] for k in ks)
    seg_spec = lambda a: pl.BlockSpec((1, hg) + a.shape[2:], lambda bi, h: (bi, h, 0, 0))
    return pl.pallas_call(
        functools.partial(_attn_kernel, n_seg=len(ks), kb=ATTN_KB, tq=tq),
        grid=(b, hh // hg),
        in_specs=[pl.BlockSpec((1, hg, n, HEAD_W), lambda bi, h: (bi, h, 0, 0))]
                 + [seg_spec(a) for a in ks] + [seg_spec(a) for a in vts],
        out_specs=pl.BlockSpec((1, hg, n, V_DIM), lambda bi, h: (bi, h, 0, 0)),
        out_shape=jax.ShapeDtypeStruct((b, hh, n, V_DIM), BF16),
        scratch_shapes=[pltpu.VMEM((nk_total, tq), F32), pltpu.VMEM((nk_total, tq), F32),
                        pltpu.VMEM((SUBLANES, tq), F32), pltpu.VMEM((SUBLANES, tq), F32)],
        compiler_params=_params(2),
        name="attention",
    )(q, *ks, *vts)


def _top2sum(a, b, c, d):
    s1, t1 = jnp.maximum(a, b), jnp.minimum(a, b)
    s2, t2 = jnp.maximum(c, d), jnp.minimum(c, d)
    return jnp.maximum(s1, s2) + jnp.maximum(jnp.minimum(s1, s2), jnp.maximum(t1, t2))


def _route(logits_t, bias):
    sc = _sigmoid(logits_t)
    sel = sc + bias
    sel_r = [sel[e:e + 1, :] for e in range(N_EXPERTS)]
    sc_r = [sc[e:e + 1, :] for e in range(N_EXPERTS)]
    epg = EXPERTS_PER_GROUP
    gs = [_top2sum(*sel_r[g * epg:(g + 1) * epg]) for g in range(N_GROUPS)]
    best, gi = gs[0], jnp.zeros(gs[0].shape, jnp.int32)
    for g in range(1, N_GROUPS):
        upd = gs[g] > best
        best = jnp.where(upd, gs[g], best)
        gi = jnp.where(upd, g, gi)

    def pick_group(rows, k):
        r = rows[k]
        for g in range(1, N_GROUPS):
            r = jnp.where(gi == g, rows[g * epg + k], r)
        return r

    in_sel = [pick_group(sel_r, k) for k in range(epg)]
    in_sc = [pick_group(sc_r, k) for k in range(epg)]
    b1, i1 = in_sel[0], jnp.zeros(gi.shape, jnp.int32)
    for k in range(1, epg):
        upd = in_sel[k] > b1
        b1 = jnp.where(upd, in_sel[k], b1)
        i1 = jnp.where(upd, k, i1)
    b2, i2 = None, None
    for k in range(epg):
        cand = jnp.where(i1 == k, -jnp.inf, in_sel[k])
        if b2 is None:
            b2, i2 = cand, jnp.zeros(gi.shape, jnp.int32)
        else:
            upd = cand > b2
            b2 = jnp.where(upd, cand, b2)
            i2 = jnp.where(upd, k, i2)

    def pick_local(idx):
        r = in_sc[0]
        for k in range(1, epg):
            r = jnp.where(idx == k, in_sc[k], r)
        return r

    s1, s2 = pick_local(i1), pick_local(i2)
    den = s1 + s2
    return gi * epg + i1, gi * epg + i2, s1 / den, s2 / den


def _mixout_kernel(y_ref, u_ref, up_ref, un_ref, gm_ref, gp_ref, wp_ref, ps_ref, wo_ref, x_ref,
                   g1_ref, gf_ref, sh2_ref, sc2_ref, wr_ref, rb_ref,
                   xo_ref, h2_ref, route_ref, w12_ref, *, tm, tpb, n_seq):
    i = pl.program_id(0)
    uext = jnp.concatenate([up_ref[...], u_ref[...], un_ref[...]], axis=0)
    sub = min(MIX_CHAIN_ROWS, tm)

    def mix_stage(r0):
        rows = slice(r0, r0 + sub)
        base = (i % tpb) * tm + r0
        r = lax.broadcasted_iota(jnp.int32, (sub, sub + 2 * POOL_HALO), 0)
        c = lax.broadcasted_iota(jnp.int32, (sub, sub + 2 * POOL_HALO), 1)
        rel = c - POOL_HALO - r
        jpos = base - POOL_HALO + c
        valid = (jpos >= 0) & (jpos < n_seq)
        tpos = base + lax.broadcasted_iota(jnp.int32, (sub, 1), 0)
        uwin = uext[r0:r0 + sub + 2 * POOL_HALO]
        uc = uext[r0 + POOL_HALO:r0 + POOL_HALO + sub]
        parts = []
        for g, w in enumerate(POOL_WINDOWS):
            sl = slice(g * POOL_GROUP_DIM, (g + 1) * POOL_GROUP_DIM)
            band = jnp.where(valid & (rel >= -(w // 2)) & (rel < w // 2), 1.0, 0.0).astype(BF16)
            cnt = jnp.clip(tpos + w // 2, 0, n_seq) - jnp.clip(tpos - w // 2, 0, n_seq)
            wsum = jnp.dot(band, uwin[:, sl], preferred_element_type=F32)
            z = wsum / cnt.astype(F32) - uc[:, sl].astype(F32)
            parts.append(jnp.dot(z.astype(BF16), wp_ref[g], preferred_element_type=F32))
        ypool = jnp.concatenate(parts, axis=1) * ps_ref[...]
        y_mla = jnp.concatenate([y_ref[0, h, rows, :] for h in range(N_HEADS)], axis=1)
        mix = gm_ref[rows, :].astype(F32) * y_mla.astype(F32) + gp_ref[rows, :].astype(F32) * ypool
        return jnp.dot(mix.astype(BF16), wo_ref[...], preferred_element_type=F32)

    def norm_route_stage(r0, y):
        rows = slice(r0, r0 + sub)
        xn = x_ref[rows, :] + g1_ref[0] * y
        xo_ref[rows, :] = xn
        h2f = _rms(xn, gf_ref[...]) * (1.0 + sc2_ref[0]) + sh2_ref[0]
        h2_ref[rows, :] = _pack_rows(h2f)
        logits_t = lax.dot_general(wr_ref[...], h2f.astype(BF16), (((1,), (1,)), ((), ())),
                                   preferred_element_type=F32)
        e1, e2, w1, w2 = _route(logits_t, rb_ref[...])
        zero = jnp.zeros((ROUTE_ROWS - 4, sub), F32)
        route_ref[:, rows] = jnp.concatenate([e1.astype(F32), e2.astype(F32), w1, w2, zero], axis=0)
        w12_ref[rows, :] = jnp.concatenate([w1, w2, jnp.zeros((LANES - 2, sub), F32)], axis=0).T

    starts = list(range(0, tm, sub))
    pending = None
    for r0 in starts:
        y = mix_stage(r0)
        if pending is not None:
            norm_route_stage(*pending)
        pending = (r0, y)
    norm_route_stage(*pending)


def _mixer_out(y, p, x2, mod3, brow, n_seq, wp, ps, wo, gf, wr_t, rb):
    t, d = x2.shape
    tm = min(2 * MIX_CHAIN_ROWS, n_seq)
    tpb = n_seq // tm
    hpt = tm // POOL_HALO
    n_halo = t // POOL_HALO
    row = lambda i: (i, 0)
    const2 = lambda i: (0, 0)
    mrow = lambda i: brow(i * tm)
    return pl.pallas_call(
        functools.partial(_mixout_kernel, tm=tm, tpb=tpb, n_seq=n_seq),
        grid=(t // tm,),
        in_specs=[pl.BlockSpec((1, N_HEADS, tm, V_DIM), lambda i: (i // tpb, 0, i % tpb, 0)),
                  pl.BlockSpec((tm, POOL_DIM), lambda i: (i, LAT_W // POOL_DIM)),
                  pl.BlockSpec((POOL_HALO, POOL_DIM),
                               lambda i: (jnp.maximum(i * hpt - 1, 0), LAT_W // POOL_DIM)),
                  pl.BlockSpec((POOL_HALO, POOL_DIM),
                               lambda i: (jnp.minimum((i + 1) * hpt, n_halo - 1), LAT_W // POOL_DIM)),
                  pl.BlockSpec((tm, d), lambda i: (i, (LAT_W + POOL_DIM) // d)),
                  pl.BlockSpec((tm, d), lambda i: (i, (LAT_W + POOL_DIM) // d + 1)),
                  pl.BlockSpec(wp.shape, lambda i: (0, 0, 0)),
                  pl.BlockSpec((1, d), const2),
                  pl.BlockSpec((d, d), const2, pipeline_mode=pl.Buffered(1)),
                  pl.BlockSpec((tm, d), row),
                  _mod_spec(2, mrow),
                  pl.BlockSpec((1, d), const2),
                  _mod_spec(3, mrow),
                  _mod_spec(4, mrow),
                  pl.BlockSpec((N_EXPERTS, d), const2),
                  pl.BlockSpec((N_EXPERTS, 1), const2)],
        out_specs=[pl.BlockSpec((tm, d), row),
                   pl.BlockSpec((tm, PACK_W), row),
                   pl.BlockSpec((ROUTE_ROWS, tm), lambda i: (0, i)),
                   pl.BlockSpec((tm, LANES), row)],
        out_shape=[jax.ShapeDtypeStruct((t, d), F32),
                   jax.ShapeDtypeStruct((t, PACK_W), jnp.int32),
                   jax.ShapeDtypeStruct((ROUTE_ROWS, t), F32),
                   jax.ShapeDtypeStruct((t, LANES), F32)],
        compiler_params=_params(1),
        name="mixer_out",
    )(y, p, p, p, p, p, wp, ps.reshape(1, d), wo, x2, mod3, gf.reshape(1, d), mod3, mod3, wr_t,
      rb.reshape(N_EXPERTS, 1))


def _chunk_view(x):
    n = x.shape[0]
    return (x.reshape(n // SUBLANES, SUBLANES, ROW_CHUNKS, LANES).transpose(0, 2, 1, 3)
            .reshape(n * ROW_CHUNKS, LANES))


def _row_view(c):
    n = c.shape[0] // ROW_CHUNKS
    return (c.reshape(n // SUBLANES, ROW_CHUNKS, SUBLANES, LANES).transpose(0, 2, 1, 3)
            .reshape(n, PACK_W))


def _sc_index_chunks(idx):
    r = idx.reshape(-1, 1, SUBLANES)
    j = jnp.arange(ROW_CHUNKS, dtype=jnp.int32)[None, :, None]
    ids = (r // SUBLANES) * (SUBLANES * ROW_CHUNKS) + j * SUBLANES + r % SUBLANES
    return ids.reshape(1, idx.shape[0] * ROW_CHUNKS)


def _sc_mesh():
    return plsc.VectorSubcoreMesh(core_axis_name="core", subcore_axis_name="subcore")


def _sc_gather_rows(table, idx):
    n, m = table.shape[0], idx.shape[0]
    n_idx = m * ROW_CHUNKS
    assert n_idx % SC_WINDOW == 0 and n % SUBLANES == 0 and m % SUBLANES == 0

    @pl.kernel(out_type=jax.ShapeDtypeStruct((n_idx, LANES), jnp.int32), mesh=_sc_mesh())
    def gather(t_hbm, i_hbm, o_hbm):
        def body(i_vmem, o_vmem):
            pltpu.sync_copy(t_hbm.at[i_vmem.at[0]], o_vmem)

        pltpu.emit_pipeline(
            body,
            grid=(n_idx // SC_WINDOW,),
            in_specs=[pl.BlockSpec((1, SC_WINDOW), index_map=lambda i: (0, i))],
            out_specs=[pl.BlockSpec((SC_WINDOW, LANES), index_map=lambda i: (i, 0))],
            core_axis_name=("core", "subcore"),
            dimension_semantics=(pltpu.PARALLEL,),
        )(i_hbm, o_hbm)

    return _row_view(gather(_chunk_view(table), _sc_index_chunks(idx)))


def _sc_scatter_rows(rows, idx, n_out):
    n, m = rows.shape[0], idx.shape[0]
    n_idx = m * ROW_CHUNKS
    src_steps = n * ROW_CHUNKS // SC_WINDOW
    assert n_idx % SC_WINDOW == 0 and (n * ROW_CHUNKS) % SC_WINDOW == 0 and m % n == 0
    assert n % SUBLANES == 0 and n_out % SUBLANES == 0

    @pl.kernel(out_type=jax.ShapeDtypeStruct((n_out * ROW_CHUNKS, LANES), jnp.int32), mesh=_sc_mesh())
    def scatter(x_hbm, i_hbm, o_hbm):
        def body(x_vmem, i_vmem):
            pltpu.sync_copy(x_vmem, o_hbm.at[i_vmem.at[0]])

        pltpu.emit_pipeline(
            body,
            grid=(n_idx // SC_WINDOW,),
            in_specs=[pl.BlockSpec((SC_WINDOW, LANES), index_map=lambda i: (i % src_steps, 0)),
                      pl.BlockSpec((1, SC_WINDOW), index_map=lambda i: (0, i))],
            out_specs=[],
            core_axis_name=("core", "subcore"),
            dimension_semantics=(pltpu.PARALLEL,),
        )(x_hbm, i_hbm)

    return _row_view(scatter(_chunk_view(rows), _sc_index_chunks(idx)))


def _dispatch_plan(route):
    t = route.shape[1]
    n_pairs = 2 * t
    n_tiles = n_pairs // MOE_TILE + N_EXPERTS
    blk = LANES
    e = jnp.concatenate([route[0], route[1]]).astype(jnp.int32)
    onehot = (e[:, None] == jnp.arange(N_EXPERTS, dtype=jnp.int32)[None, :]).astype(F32)
    oh3 = onehot.reshape(n_pairs // blk, blk, N_EXPERTS)
    tri = (jnp.arange(blk)[:, None] >= jnp.arange(blk)[None, :]).astype(F32)
    within = jnp.einsum("ij,bjk->bik", tri, oh3)
    totals = within[:, -1, :]
    before = jnp.cumsum(totals, axis=0) - totals
    csum = (within + before[:, None, :]).reshape(n_pairs, N_EXPERTS)
    counts = jnp.sum(totals, axis=0).astype(jnp.int32)
    padded = ((counts + MOE_TILE - 1) // MOE_TILE) * MOE_TILE
    ends = jnp.cumsum(padded)
    starts = ends - padded
    pair_slot = jnp.sum(onehot * (csum - 1.0 + starts.astype(F32)[None, :]), axis=1).astype(jnp.int32)
    tile_start = jnp.arange(n_tiles, dtype=jnp.int32) * MOE_TILE
    tile_e = jnp.minimum(jnp.sum((tile_start[:, None] >= ends[None, :]).astype(jnp.int32), axis=1), N_EXPERTS - 1)
    e_onehot = tile_e[:, None] == jnp.arange(N_EXPERTS, dtype=jnp.int32)[None, :]
    filled = jnp.sum(jnp.where(e_onehot, (starts + counts)[None, :], 0), axis=1)
    tile_rows = jnp.clip(filled - tile_start, 0, MOE_TILE).astype(jnp.int32)
    last_used = jnp.maximum(ends[-1] // MOE_TILE - 1, 0)
    tile_expert = jnp.where(tile_start < ends[-1], tile_e, tile_e[last_used]).astype(jnp.int32)
    return pair_slot, tile_expert, tile_rows


def _swiglu(xb, wg, wu, wd):
    gate = jnp.dot(xb, wg, preferred_element_type=F32)
    up = jnp.dot(xb, wu, preferred_element_type=F32)
    a = ((gate * _sigmoid(gate)) * up).astype(BF16)
    return jnp.dot(a, wd, preferred_element_type=F32)


def _ffn_kernel(te_ref, rows_ref, xs_ref, wg_ref, wu_ref, wd_ref, ys_ref, wg_s, wu_s, wd_s):
    i = pl.program_id(0)
    new_expert = jnp.logical_or(i == 0, te_ref[i] != te_ref[jnp.maximum(i - 1, 0)])

    @pl.when(new_expert)
    def _():
        wg_s[...] = wg_ref[0].astype(BF16)
        wu_s[...] = wu_ref[0].astype(BF16)
        wd_s[...] = wd_ref[0].astype(BF16)

    @pl.when(rows_ref[i] > 0)
    def _():
        live = lax.broadcasted_iota(jnp.int32, (MOE_TILE, 1), 0) < rows_ref[i]
        xb = _load_unpack(xs_ref, live).astype(BF16)
        ys_ref[...] = _pack_rows(_swiglu(xb, wg_s[...], wu_s[...], wd_s[...]))

    @pl.when(rows_ref[i] == 0)
    def _():
        ys_ref[...] = jnp.zeros_like(ys_ref)


def _expert_ffn(xs, tile_expert, tile_rows, wg, wu, wd, layer):
    n_slots = xs.shape[0]
    d = D_MODEL
    blk = pl.BlockSpec((MOE_TILE, PACK_W), lambda i, te, nr: (i, 0))
    return pl.pallas_call(
        _ffn_kernel,
        grid_spec=pltpu.PrefetchScalarGridSpec(
            num_scalar_prefetch=2,
            grid=(n_slots // MOE_TILE,),
            in_specs=[blk,
                      pl.BlockSpec((None, 1, d, D_EXPERT), lambda i, te, nr: (layer, te[i], 0, 0)),
                      pl.BlockSpec((None, 1, d, D_EXPERT), lambda i, te, nr: (layer, te[i], 0, 0)),
                      pl.BlockSpec((None, 1, D_EXPERT, d), lambda i, te, nr: (layer, te[i], 0, 0))],
            out_specs=blk,
            scratch_shapes=[pltpu.VMEM((d, D_EXPERT), BF16), pltpu.VMEM((d, D_EXPERT), BF16),
                            pltpu.VMEM((D_EXPERT, d), BF16)]),
        out_shape=jax.ShapeDtypeStruct(xs.shape, jnp.int32),
        compiler_params=_params(1),
        name="expert_ffn",
    )(tile_expert, tile_rows, xs, wg, wu, wd)


def _combine_kernel(h_ref, y1_ref, y2_ref, w_ref, sg_ref, su_ref, sd_ref, x_ref, g2_ref, gfin_ref, o_ref,
                    *, final):
    y = _swiglu(_load_unpack(h_ref).astype(BF16), sg_ref[...], su_ref[...], sd_ref[...])
    w = w_ref[...]
    y = y + w[:, 0:1] * _load_unpack(y1_ref) + w[:, 1:2] * _load_unpack(y2_ref)
    xn = x_ref[...] + g2_ref[0] * y
    if final:
        xn = _rms(xn, gfin_ref[...])
    o_ref[...] = xn


def _moe_combine(h2p, yg, w12, sg, su, sd, x2, set_off, x_off, n_rows, mod3, brow, gfin, final):
    d = x2.shape[1]
    tm = min(512, n_rows)
    assert set_off % tm == 0 and x_off % tm == 0 and n_rows % tm == 0
    s_blk, x_blk, n_blk = set_off // tm, x_off // tm, n_rows // tm
    const2 = lambda i: (0, 0)
    pk = lambda o: pl.BlockSpec((tm, PACK_W), lambda i: (i + o, 0))
    x_spec = pl.BlockSpec((tm, d), lambda i: (i + x_blk, 0))
    return pl.pallas_call(
        functools.partial(_combine_kernel, final=final),
        grid=(n_blk,),
        in_specs=[pk(s_blk), pk(0), pk(n_blk),
                  pl.BlockSpec((tm, LANES), lambda i: (i + x_blk, 0)),
                  pl.BlockSpec(sg.shape, const2), pl.BlockSpec(su.shape, const2), pl.BlockSpec(sd.shape, const2),
                  x_spec,
                  _mod_spec(5, lambda i: brow((i + x_blk) * tm)),
                  pl.BlockSpec((1, d), const2)],
        out_specs=x_spec,
        out_shape=jax.ShapeDtypeStruct(x2.shape, F32),
        input_output_aliases={7: 0},
        compiler_params=_params(1),
        name="moe_combine",
    )(h2p, yg, yg, w12, sg, su, sd, x2, mod3, gfin.reshape(1, d))


def _moe_routed(parts, wg, wu, wd, layer, sg, su, sd, mod3, gfin, final):
    h2p = parts[0][0] if len(parts) == 1 else jnp.concatenate([pt[0] for pt in parts], axis=0)
    route = parts[0][1] if len(parts) == 1 else jnp.concatenate([pt[1] for pt in parts], axis=1)
    t_all = h2p.shape[0]
    pair_slot, tile_expert, tile_rows = _dispatch_plan(route)
    n_slots = tile_expert.shape[0] * MOE_TILE
    xs = _sc_scatter_rows(h2p, pair_slot, n_slots)
    ys = _expert_ffn(xs, tile_expert, tile_rows, wg, wu, wd, layer)
    outs, set_off = [], 0
    for _, _, w12, x2, brow in parts:
        t = x2.shape[0]
        n_rows = t // COMBINE_CHUNKS if t % (COMBINE_CHUNKS * 512) == 0 else t
        for x_off in range(0, t, n_rows):
            lo = set_off + x_off
            idx = jnp.concatenate([pair_slot[lo:lo + n_rows], pair_slot[t_all + lo:t_all + lo + n_rows]])
            yg = _sc_gather_rows(ys, idx)
            x2 = _moe_combine(h2p, yg, w12, sg, su, sd, x2, lo, x_off, n_rows, mod3, brow, gfin, final)
        outs.append(x2)
        set_off += t
    return outs


def _rope_tables(seq):
    rows = seq // GRID_W
    row = np.repeat(np.arange(rows), GRID_W).astype(np.float32)
    col = np.tile(np.arange(GRID_W), rows).astype(np.float32)
    axis_dim = QK_ROPE // 2
    inv = (ROPE_THETA ** (-np.arange(0, axis_dim, 2, dtype=np.float32) / axis_dim)).astype(np.float32)
    ang = np.concatenate([row[:, None] * inv, col[:, None] * inv], axis=-1).astype(np.float32)
    cos, sin = np.cos(ang), np.sin(ang)
    zero = np.zeros_like(cos)
    return (jnp.asarray(np.concatenate([cos, cos, zero, zero], axis=1)),
            jnp.asarray(np.concatenate([-sin, sin, zero, zero], axis=1)))


def _rope_group(w, start):
    half = QK_ROPE // 2
    x1, x2 = w[..., start:start + half], w[..., start + half:start + 2 * half]
    return jnp.concatenate([x1, x2, x2, x1], axis=-1)


def _relayout_w_in(w):
    return jnp.concatenate([w[..., :KV_RANK], w[..., KV_COLS:Q_END], _rope_group(w, KV_RANK),
                            w[..., Q_END:]], axis=-1).astype(BF16)


def _relayout_w_qb(w):
    dq = QK_NOPE + QK_ROPE
    cols = []
    for h in range(N_HEADS):
        cols += [w[:, h * dq:h * dq + QK_NOPE], _rope_group(w, h * dq + QK_NOPE)]
    return jnp.concatenate(cols, axis=1).astype(BF16)


def kernel(x, c, ctx, c_ctx, w_ada, b_ada, norm_mix_g, norm_ffn_g, w_in, q_norm_g, kv_norm_g, w_qb, w_kvb, w_pool, pool_scale, w_out, w_router, router_bias, w_exp_gate, w_exp_up, w_exp_down, w_sh_gate, w_sh_up, w_sh_down, final_norm_g):
    bsz, seq, d = x.shape
    n_ctx = ctx.shape[1]
    depth = w_ada.shape[0]
    assert bsz < MOD_ROWS and d == D_MODEL

    c_rows = jnp.concatenate([c, c_ctx[None], jnp.zeros((MOD_ROWS - bsz - 1, d), F32)], axis=0)
    mod = _ada_mod(c_rows, w_ada, b_ada)

    cos_t, sin_t = _rope_tables(seq)
    ones_t = jnp.concatenate([jnp.ones((n_ctx, 2 * (QK_ROPE // 2)), F32),
                              jnp.zeros((n_ctx, LANES - QK_ROPE), F32)], axis=1)
    zeros_t = jnp.zeros((n_ctx, LANES), F32)
    wr_t = w_router.T.astype(BF16)
    w_in_r = _relayout_w_in(w_in)
    lat_row = lambda r: r // seq
    ctx_row = lambda r: bsz

    x2 = x.reshape(bsz * seq, d)
    xc2 = ctx.reshape(bsz * n_ctx, d)
    for l in range(depth):
        last = l == depth - 1
        mod3 = mod[l].reshape(MOD_ROWS * N_MOD, 1, d)
        wq = _relayout_w_qb(w_qb[l])
        w_kv3 = w_kvb[l].reshape(KV_RANK, N_HEADS, QK_NOPE + V_DIM)
        wk = w_kv3[:, :, :QK_NOPE].reshape(KV_RANK, N_HEADS * QK_NOPE).astype(BF16)
        wvt = (w_kv3[:, :, QK_NOPE:].transpose(1, 2, 0).reshape(N_HEADS * V_DIM, KV_RANK)
               .astype(BF16))
        wp = w_pool[l].astype(BF16)
        wo = w_out[l].astype(BF16)
        sg, su, sd = w_sh_gate[l].astype(BF16), w_sh_up[l].astype(BF16), w_sh_down[l].astype(BF16)
        moe = functools.partial(_moe_routed, wg=w_exp_gate, wu=w_exp_up, wd=w_exp_down, layer=l,
                                sg=sg, su=su, sd=sd, mod3=mod3, gfin=final_norm_g)

        p = _in_proj(x2, mod3, lat_row, norm_mix_g[l], w_in_r, l, P_COLS)
        pc = _in_proj(xc2, mod3, ctx_row, norm_mix_g[l], w_in_r, l, LAT_W if last else P_COLS)
        q, k_lat, v_lat = _qkv(p, bsz, seq, cos_t, sin_t, q_norm_g[l], kv_norm_g[l], wq, wk, wvt, True)
        ctx_out = _qkv(pc, bsz, n_ctx, ones_t, zeros_t, q_norm_g[l], kv_norm_g[l], wq, wk, wvt, not last)
        k_ctx, v_ctx = ctx_out[-2:]
        y = _attention(q, [k_ctx, k_lat], [v_ctx, v_lat])
        x2, h2p, route, w12 = _mixer_out(y, p, x2, mod3, lat_row, seq, wp, pool_scale[l], wo,
                                         norm_ffn_g[l], wr_t, router_bias)
        if last:
            x2, = moe([(h2p, route, w12, x2, lat_row)], final=True)
        else:
            yc = _attention(ctx_out[0], [k_ctx], [v_ctx])
            xc2, h2c, route_c, w12c = _mixer_out(yc, pc, xc2, mod3, ctx_row, n_ctx, wp, pool_scale[l], wo,
                                                 norm_ffn_g[l], wr_t, router_bias)
            x2, xc2 = moe([(h2p, route, w12, x2, lat_row), (h2c, route_c, w12c, xc2, ctx_row)], final=False)
    return x2.reshape(bsz, seq, d)
```

```python
import functools
import math

import jax
import jax.numpy as jnp
import numpy as np
from jax import lax
from jax.experimental import pallas as pl
from jax.experimental.pallas import tpu as pltpu
from jax.experimental.pallas import tpu_sc as plsc

F32 = jnp.float32
BF16 = jnp.bfloat16

D_MODEL = 2048
GRID_W = 64
N_HEADS = 16
QK_NOPE = 128
QK_ROPE = 64
V_DIM = 128
Q_RANK = 384
KV_RANK = 512
ROPE_THETA = 10000.0
POOL_DIM = 1024
POOL_GROUPS = 4
POOL_GROUP_DIM = 256
POOL_WINDOWS = (2, 4, 8, 16)
POOL_GROUP_OUT = 512
KV_COLS = KV_RANK + QK_ROPE
Q_END = KV_COLS + Q_RANK
POOL_END = Q_END + POOL_DIM
N_EXPERTS = 16
N_GROUPS = 4
EXPERTS_PER_GROUP = 4
D_EXPERT = 512
EPS = 1e-6
N_MOD = 6

LANES = 128
HEAD_W = 2 * LANES
LAT_W = 1024
P_COLS = LAT_W + POOL_DIM + 2 * D_MODEL
POOL_HALO = 16
MOD_ROWS = 8
Q_SCALE = (1.0 / math.sqrt(QK_NOPE + QK_ROPE)) * math.log2(math.e)
VMEM_LIMIT = 56 * 1024 * 1024
SUBLANES = 8
PACK_W = D_MODEL // 2
ROW_CHUNKS = PACK_W // LANES
ROUTE_ROWS = SUBLANES
MOE_TILE = 512
SC_WINDOW = 128
ATTN_TQ = 256
ATTN_KB = 512
V_HEAD_GROUP = 8
ATTN_HEADS_PER_STEP = 2
ATTN_PAIRS_PER_ITER = 3
MIX_CHAIN_ROWS = 256
COMBINE_CHUNKS = 2


def _sigmoid(x):
    return 1.0 / (1.0 + jnp.exp(-x))


def _pack_rows(y):
    half = y.shape[1] // 2
    return pltpu.pack_elementwise([y[:, :half], y[:, half:]], packed_dtype=BF16)


def _load_unpack(ref, live=None):
    w = ref[...]
    if live is not None:
        w = jnp.where(live, w, 0)
    lo = pltpu.unpack_elementwise(w, index=0, packed_dtype=BF16, unpacked_dtype=F32)
    hi = pltpu.unpack_elementwise(w, index=1, packed_dtype=BF16, unpacked_dtype=F32)
    return jnp.concatenate([lo, hi], axis=1)


def _rms(xf, g):
    ms = jnp.mean(xf * xf, axis=-1, keepdims=True)
    return xf * lax.rsqrt(ms + EPS) * g


def _params(n_axes):
    return pltpu.CompilerParams(dimension_semantics=("arbitrary",) * n_axes,
                                vmem_limit_bytes=VMEM_LIMIT)


def _ada_kernel(c_ref, w_ref, b_ref, o_ref):
    cf = c_ref[...]
    a = (cf * _sigmoid(cf)).astype(BF16)
    o_ref[0] = jnp.dot(a, w_ref[0].astype(BF16), preferred_element_type=F32) + b_ref[0]


def _ada_mod(c_rows, w_ada, b_ada):
    depth, d, n = w_ada.shape
    tn = 1024
    return pl.pallas_call(
        _ada_kernel,
        grid=(depth, n // tn),
        in_specs=[pl.BlockSpec((MOD_ROWS, d), lambda l, j: (0, 0)),
                  pl.BlockSpec((1, d, tn), lambda l, j: (l, 0, j)),
                  pl.BlockSpec((1, 1, tn), lambda l, j: (l, 0, j))],
        out_specs=pl.BlockSpec((1, MOD_ROWS, tn), lambda l, j: (l, 0, j)),
        out_shape=jax.ShapeDtypeStruct((depth, MOD_ROWS, n), F32),
        compiler_params=_params(2),
        name="ada_mod",
    )(c_rows, w_ada, b_ada.reshape(depth, 1, n))


def _mod_spec(k, brow):
    return pl.BlockSpec((1, 1, D_MODEL), lambda i, *_: (brow(i) * N_MOD + k, 0, 0))


def _inproj_kernel(x_ref, g_ref, sh_ref, sc_ref, w_ref, o_ref, h_scr, *, tn):
    y = _rms(x_ref[...], g_ref[...])
    h_scr[...] = (y * (1.0 + sc_ref[0]) + sh_ref[0]).astype(BF16)
    for c0 in range(0, o_ref.shape[1], tn):
        acc = jnp.dot(h_scr[...], w_ref[:, c0:c0 + tn], preferred_element_type=F32)
        if c0 >= LAT_W + POOL_DIM:
            acc = _sigmoid(acc)
        o_ref[:, c0:c0 + tn] = acc.astype(BF16)


def _in_proj(x2, mod3, brow, g, w, layer, n_cols):
    t, d = x2.shape
    tm = min(512, t)
    return pl.pallas_call(
        functools.partial(_inproj_kernel, tn=512),
        grid=(t // tm,),
        in_specs=[pl.BlockSpec((tm, d), lambda i: (i, 0)),
                  pl.BlockSpec((1, d), lambda i: (0, 0)),
                  _mod_spec(0, lambda i: brow(i * tm)),
                  _mod_spec(1, lambda i: brow(i * tm)),
                  pl.BlockSpec((None, d, n_cols), lambda i: (layer, 0, 0), pipeline_mode=pl.Buffered(1))],
        out_specs=pl.BlockSpec((tm, n_cols), lambda i: (i, 0)),
        out_shape=jax.ShapeDtypeStruct((t, n_cols), BF16),
        scratch_shapes=[pltpu.VMEM((tm, d), BF16)],
        compiler_params=_params(1),
        name="in_proj",
    )(x2, g.reshape(1, d), mod3, mod3, w)


def _qkv_kernel(lat_ref, cos_ref, sin_ref, gq_ref, gkv_ref, wq_ref, wk_ref, wvt_ref, *out_refs, with_q):
    if with_q:
        q_ref, k_ref, v_ref = out_refs
    else:
        k_ref, v_ref = out_refs
    cos = cos_ref[...]
    sin = sin_ref[...]

    def rope(grp):
        return grp * cos + pltpu.roll(grp, 2 * (QK_ROPE // 2), axis=1) * sin

    kvn = _rms(lat_ref[:, :KV_RANK].astype(F32), gkv_ref[...]).astype(BF16)
    kpe = rope(lat_ref[:, KV_RANK + Q_RANK:].astype(F32)).astype(BF16)
    for h in range(0, N_HEADS, 2):
        kn = jnp.dot(kvn, wk_ref[:, h * QK_NOPE:(h + 2) * QK_NOPE], preferred_element_type=F32)
        for hh in range(2):
            k_ref[0, h + hh, :, :QK_NOPE] = kn[:, hh * QK_NOPE:(hh + 1) * QK_NOPE].astype(BF16)
            k_ref[0, h + hh, :, QK_NOPE:] = kpe
    for h in range(0, N_HEADS, V_HEAD_GROUP):
        vt = lax.dot_general(wvt_ref[h * V_DIM:(h + V_HEAD_GROUP) * V_DIM, :], kvn,
                             (((1,), (1,)), ((), ())), preferred_element_type=F32)
        for hh in range(V_HEAD_GROUP):
            v_ref[0, h + hh] = vt[hh * V_DIM:(hh + 1) * V_DIM, :].astype(BF16)
    if with_q:
        qn = _rms(lat_ref[:, KV_RANK:KV_RANK + Q_RANK].astype(F32), gq_ref[...]).astype(BF16)
        for h in range(N_HEADS):
            qh = jnp.dot(qn, wq_ref[:, h * HEAD_W:(h + 1) * HEAD_W], preferred_element_type=F32)
            q_ref[0, h, :, :QK_NOPE] = (qh[:, :QK_NOPE] * Q_SCALE).astype(BF16)
            q_ref[0, h, :, QK_NOPE:] = (rope(qh[:, QK_NOPE:]) * Q_SCALE).astype(BF16)


def _qkv(p, b, n, cos_t, sin_t, gq, gkv, wq, wk, wvt, with_q):
    tm = min(256, n)
    tpb = n // tm
    rope_tiles = cos_t.shape[0] // tm
    head_spec = pl.BlockSpec((1, N_HEADS, tm, HEAD_W), lambda i: (i // tpb, 0, i % tpb, 0))
    tab_spec = pl.BlockSpec((tm, LANES), lambda i: ((i % tpb) % rope_tiles, 0))
    out_shape = [jax.ShapeDtypeStruct((b, N_HEADS, n, HEAD_W), BF16),
                 jax.ShapeDtypeStruct((b, N_HEADS, V_DIM, n), BF16)]
    out_specs = [head_spec,
                 pl.BlockSpec((1, N_HEADS, V_DIM, tm), lambda i: (i // tpb, 0, 0, i % tpb))]
    if with_q:
        out_shape = [jax.ShapeDtypeStruct((b, N_HEADS, n, HEAD_W), BF16)] + out_shape
        out_specs = [head_spec] + out_specs
    return pl.pallas_call(
        functools.partial(_qkv_kernel, with_q=with_q),
        grid=(b * tpb,),
        in_specs=[pl.BlockSpec((tm, LAT_W), lambda i: (i, 0)),
                  tab_spec, tab_spec,
                  pl.BlockSpec((1, Q_RANK), lambda i: (0, 0)),
                  pl.BlockSpec((1, KV_RANK), lambda i: (0, 0)),
                  pl.BlockSpec(wq.shape, lambda i: (0, 0)),
                  pl.BlockSpec(wk.shape, lambda i: (0, 0)),
                  pl.BlockSpec(wvt.shape, lambda i: (0, 0))],
        out_specs=out_specs,
        out_shape=out_shape,
        compiler_params=_params(1),
        name="qkv",
    )(p, cos_t, sin_t, gq.reshape(1, Q_RANK), gkv.reshape(1, KV_RANK), wq, wk, wvt)


def _row_fold(x, op):
    parts = [x[r:r + SUBLANES] for r in range(0, x.shape[0], SUBLANES)]
    a, b = parts[0], parts[1]
    for i in range(2, len(parts) - 1, 2):
        a, b = op(a, parts[i]), op(b, parts[i + 1])
    if len(parts) % 2:
        a = op(a, parts[-1])
    return op(a, b)


def _attn_kernel(q_ref, *refs, n_seg, kb, tq):
    k_refs, vt_refs = refs[:n_seg], refs[n_seg:2 * n_seg]
    o_ref, s_a, s_b, m_a, m_b = refs[2 * n_seg:]
    chunks = []
    off = 0
    for kr, vr in zip(k_refs, vt_refs):
        nk = kr.shape[2]
        for c0 in range(0, nk, kb):
            n = min(kb, nk - c0)
            chunks.append((kr, vr, c0, n, off))
            off += n

    n_heads, nt = q_ref.shape[1], q_ref.shape[2] // tq
    bufs = ((s_a, m_a), (s_b, m_b))

    def tile_rows(t):
        return pl.ds(pl.multiple_of(t * tq, tq), tq)

    def scores(g, t, s_buf, m_buf):
        rows = tile_rows(t)
        q = q_ref[0, g, rows, :]
        mp = None
        for kr, vr, c0, n, o in chunks:
            s = lax.dot_general(kr[0, g, c0:c0 + n, :], q, (((1,), (1,)), ((), ())),
                                preferred_element_type=F32)
            s_buf[o:o + n, :] = s
            m = _row_fold(s, jnp.maximum)
            mp = m if mp is None else jnp.maximum(mp, m)
        m_buf[...] = mp

    def values(g, t, s_buf, m_buf):
        rows = tile_rows(t)
        mrow = jnp.max(m_buf[...], axis=0, keepdims=True)
        lp = None
        acc = None
        for kr, vr, c0, n, o in chunks:
            pr = jnp.exp2(s_buf[o:o + n, :] - mrow)
            ls = _row_fold(pr, jnp.add)
            lp = ls if lp is None else lp + ls
            pv = jnp.dot(vr[0, g, :, c0:c0 + n], pr.astype(BF16), preferred_element_type=F32)
            acc = pv if acc is None else acc + pv
        l = jnp.sum(lp, axis=0, keepdims=True)
        o_ref[0, g, rows, :] = (acc / l).T.astype(BF16)

    def phase(g, t, parity):
        scores(g, t + 1, *bufs[1 - parity])
        values(g, t, *bufs[parity])

    scores(0, 0, *bufs[0])
    for g in range(n_heads):
        par0 = (g * nt) % 2
        n_pairs = (nt - 1) // 2

        def pair(j, g=g, par0=par0):
            phase(g, 2 * j, par0)
            phase(g, 2 * j + 1, 1 - par0)

        def pairs(jj, carry, pair=pair):
            for k in range(ATTN_PAIRS_PER_ITER):
                pair(jj * ATTN_PAIRS_PER_ITER + k)
            return carry

        n_iter = n_pairs // ATTN_PAIRS_PER_ITER
        if n_iter:
            lax.fori_loop(0, n_iter, pairs, 0)
        for j in range(n_iter * ATTN_PAIRS_PER_ITER, n_pairs):
            pair(j)
        if (nt - 1) % 2:
            phase(g, nt - 2, (par0 + nt - 2) % 2)
        last = bufs[(par0 + nt - 1) % 2]
        if g + 1 < n_heads:
            scores(g + 1, 0, *bufs[((g + 1) * nt) % 2])
        values(g, nt - 1, *last)


def _attention(q, ks, vts):
    b, hh, n, _ = q.shape
    tq = min(ATTN_TQ, n)
    hg = ATTN_HEADS_PER_STEP if n > tq else hh
    assert hh % hg == 0
    nk_total = sum(k.shape[2] for k in ks)
    seg_spec = lambda a: pl.BlockSpec((1, hg) + a.shape[2:], lambda bi, h: (bi, h, 0, 0))
    return pl.pallas_call(
        functools.partial(_attn_kernel, n_seg=len(ks), kb=ATTN_KB, tq=tq),
        grid=(b, hh // hg),
        in_specs=[pl.BlockSpec((1, hg, n, HEAD_W), lambda bi, h: (bi, h, 0, 0))]
                 + [seg_spec(a) for a in ks] + [seg_spec(a) for a in vts],
        out_specs=pl.BlockSpec((1, hg, n, V_DIM), lambda bi, h: (bi, h, 0, 0)),
        out_shape=jax.ShapeDtypeStruct((b, hh, n, V_DIM), BF16),
        scratch_shapes=[pltpu.VMEM((nk_total, tq), F32), pltpu.VMEM((nk_total, tq), F32),
                        pltpu.VMEM((SUBLANES, tq), F32), pltpu.VMEM((SUBLANES, tq), F32)],
        compiler_params=_params(2),
        name="attention",
    )(q, *ks, *vts)


def _top2sum(a, b, c, d):
    s1, t1 = jnp.maximum(a, b), jnp.minimum(a, b)
    s2, t2 = jnp.maximum(c, d), jnp.minimum(c, d)
    return jnp.maximum(s1, s2) + jnp.maximum(jnp.minimum(s1, s2), jnp.maximum(t1, t2))


def _route(logits_t, bias):
    sc = _sigmoid(logits_t)
    sel = sc + bias
    sel_r = [sel[e:e + 1, :] for e in range(N_EXPERTS)]
    sc_r = [sc[e:e + 1, :] for e in range(N_EXPERTS)]
    epg = EXPERTS_PER_GROUP
    gs = [_top2sum(*sel_r[g * epg:(g + 1) * epg]) for g in range(N_GROUPS)]
    best, gi = gs[0], jnp.zeros(gs[0].shape, jnp.int32)
    for g in range(1, N_GROUPS):
        upd = gs[g] > best
        best = jnp.where(upd, gs[g], best)
        gi = jnp.where(upd, g, gi)

    def pick_group(rows, k):
        r = rows[k]
        for g in range(1, N_GROUPS):
            r = jnp.where(gi == g, rows[g * epg + k], r)
        return r

    in_sel = [pick_group(sel_r, k) for k in range(epg)]
    in_sc = [pick_group(sc_r, k) for k in range(epg)]
    b1, i1 = in_sel[0], jnp.zeros(gi.shape, jnp.int32)
    for k in range(1, epg):
        upd = in_sel[k] > b1
        b1 = jnp.where(upd, in_sel[k], b1)
        i1 = jnp.where(upd, k, i1)
    b2, i2 = None, None
    for k in range(epg):
        cand = jnp.where(i1 == k, -jnp.inf, in_sel[k])
        if b2 is None:
            b2, i2 = cand, jnp.zeros(gi.shape, jnp.int32)
        else:
            upd = cand > b2
            b2 = jnp.where(upd, cand, b2)
            i2 = jnp.where(upd, k, i2)

    def pick_local(idx):
        r = in_sc[0]
        for k in range(1, epg):
            r = jnp.where(idx == k, in_sc[k], r)
        return r

    s1, s2 = pick_local(i1), pick_local(i2)
    den = s1 + s2
    return gi * epg + i1, gi * epg + i2, s1 / den, s2 / den


def _mixout_kernel(y_ref, u_ref, up_ref, un_ref, gm_ref, gp_ref, wp_ref, ps_ref, wo_ref, x_ref,
                   g1_ref, gf_ref, sh2_ref, sc2_ref, wr_ref, rb_ref,
                   xo_ref, h2_ref, route_ref, w12_ref, *, tm, tpb, n_seq):
    i = pl.program_id(0)
    uext = jnp.concatenate([up_ref[...], u_ref[...], un_ref[...]], axis=0)
    sub = min(MIX_CHAIN_ROWS, tm)

    def mix_stage(r0):
        rows = slice(r0, r0 + sub)
        base = (i % tpb) * tm + r0
        r = lax.broadcasted_iota(jnp.int32, (sub, sub + 2 * POOL_HALO), 0)
        c = lax.broadcasted_iota(jnp.int32, (sub, sub + 2 * POOL_HALO), 1)
        rel = c - POOL_HALO - r
        jpos = base - POOL_HALO + c
        valid = (jpos >= 0) & (jpos < n_seq)
        tpos = base + lax.broadcasted_iota(jnp.int32, (sub, 1), 0)
        uwin = uext[r0:r0 + sub + 2 * POOL_HALO]
        uc = uext[r0 + POOL_HALO:r0 + POOL_HALO + sub]
        parts = []
        for g, w in enumerate(POOL_WINDOWS):
            sl = slice(g * POOL_GROUP_DIM, (g + 1) * POOL_GROUP_DIM)
            band = jnp.where(valid & (rel >= -(w // 2)) & (rel < w // 2), 1.0, 0.0).astype(BF16)
            cnt = jnp.clip(tpos + w // 2, 0, n_seq) - jnp.clip(tpos - w // 2, 0, n_seq)
            wsum = jnp.dot(band, uwin[:, sl], preferred_element_type=F32)
            z = wsum / cnt.astype(F32) - uc[:, sl].astype(F32)
            parts.append(jnp.dot(z.astype(BF16), wp_ref[g], preferred_element_type=F32))
        ypool = jnp.concatenate(parts, axis=1) * ps_ref[...]
        y_mla = jnp.concatenate([y_ref[0, h, rows, :] for h in range(N_HEADS)], axis=1)
        mix = gm_ref[rows, :].astype(F32) * y_mla.astype(F32) + gp_ref[rows, :].astype(F32) * ypool
        return jnp.dot(mix.astype(BF16), wo_ref[...], preferred_element_type=F32)

    def norm_route_stage(r0, y):
        rows = slice(r0, r0 + sub)
        xn = x_ref[rows, :] + g1_ref[0] * y
        xo_ref[rows, :] = xn
        h2f = _rms(xn, gf_ref[...]) * (1.0 + sc2_ref[0]) + sh2_ref[0]
        h2_ref[rows, :] = _pack_rows(h2f)
        logits_t = lax.dot_general(wr_ref[...], h2f.astype(BF16), (((1,), (1,)), ((), ())),
                                   preferred_element_type=F32)
        e1, e2, w1, w2 = _route(logits_t, rb_ref[...])
        zero = jnp.zeros((ROUTE_ROWS - 4, sub), F32)
        route_ref[:, rows] = jnp.concatenate([e1.astype(F32), e2.astype(F32), w1, w2, zero], axis=0)
        w12_ref[rows, :] = jnp.concatenate([w1, w2, jnp.zeros((LANES - 2, sub), F32)], axis=0).T

    starts = list(range(0, tm, sub))
    pending = None
    for r0 in starts:
        y = mix_stage(r0)
        if pending is not None:
            norm_route_stage(*pending)
        pending = (r0, y)
    norm_route_stage(*pending)


def _mixer_out(y, p, x2, mod3, brow, n_seq, wp, ps, wo, gf, wr_t, rb):
    t, d = x2.shape
    tm = min(2 * MIX_CHAIN_ROWS, n_seq)
    tpb = n_seq // tm
    hpt = tm // POOL_HALO
    n_halo = t // POOL_HALO
    row = lambda i: (i, 0)
    const2 = lambda i: (0, 0)
    mrow = lambda i: brow(i * tm)
    return pl.pallas_call(
        functools.partial(_mixout_kernel, tm=tm, tpb=tpb, n_seq=n_seq),
        grid=(t // tm,),
        in_specs=[pl.BlockSpec((1, N_HEADS, tm, V_DIM), lambda i: (i // tpb, 0, i % tpb, 0)),
                  pl.BlockSpec((tm, POOL_DIM), lambda i: (i, LAT_W // POOL_DIM)),
                  pl.BlockSpec((POOL_HALO, POOL_DIM),
                               lambda i: (jnp.maximum(i * hpt - 1, 0), LAT_W // POOL_DIM)),
                  pl.BlockSpec((POOL_HALO, POOL_DIM),
                               lambda i: (jnp.minimum((i + 1) * hpt, n_halo - 1), LAT_W // POOL_DIM)),
                  pl.BlockSpec((tm, d), lambda i: (i, (LAT_W + POOL_DIM) // d)),
                  pl.BlockSpec((tm, d), lambda i: (i, (LAT_W + POOL_DIM) // d + 1)),
                  pl.BlockSpec(wp.shape, lambda i: (0, 0, 0)),
                  pl.BlockSpec((1, d), const2),
                  pl.BlockSpec((d, d), const2, pipeline_mode=pl.Buffered(1)),
                  pl.BlockSpec((tm, d), row),
                  _mod_spec(2, mrow),
                  pl.BlockSpec((1, d), const2),
                  _mod_spec(3, mrow),
                  _mod_spec(4, mrow),
                  pl.BlockSpec((N_EXPERTS, d), const2),
                  pl.BlockSpec((N_EXPERTS, 1), const2)],
        out_specs=[pl.BlockSpec((tm, d), row),
                   pl.BlockSpec((tm, PACK_W), row),
                   pl.BlockSpec((ROUTE_ROWS, tm), lambda i: (0, i)),
                   pl.BlockSpec((tm, LANES), row)],
        out_shape=[jax.ShapeDtypeStruct((t, d), F32),
                   jax.ShapeDtypeStruct((t, PACK_W), jnp.int32),
                   jax.ShapeDtypeStruct((ROUTE_ROWS, t), F32),
                   jax.ShapeDtypeStruct((t, LANES), F32)],
        compiler_params=_params(1),
        name="mixer_out",
    )(y, p, p, p, p, p, wp, ps.reshape(1, d), wo, x2, mod3, gf.reshape(1, d), mod3, mod3, wr_t,
      rb.reshape(N_EXPERTS, 1))


def _chunk_view(x):
    n = x.shape[0]
    return (x.reshape(n // SUBLANES, SUBLANES, ROW_CHUNKS, LANES).transpose(0, 2, 1, 3)
            .reshape(n * ROW_CHUNKS, LANES))


def _row_view(c):
    n = c.shape[0] // ROW_CHUNKS
    return (c.reshape(n // SUBLANES, ROW_CHUNKS, SUBLANES, LANES).transpose(0, 2, 1, 3)
            .reshape(n, PACK_W))


def _sc_index_chunks(idx):
    r = idx.reshape(-1, 1, SUBLANES)
    j = jnp.arange(ROW_CHUNKS, dtype=jnp.int32)[None, :, None]
    ids = (r // SUBLANES) * (SUBLANES * ROW_CHUNKS) + j * SUBLANES + r % SUBLANES
    return ids.reshape(1, idx.shape[0] * ROW_CHUNKS)


def _sc_mesh():
    return plsc.VectorSubcoreMesh(core_axis_name="core", subcore_axis_name="subcore")


def _sc_gather_rows(table, idx):
    n, m = table.shape[0], idx.shape[0]
    n_idx = m * ROW_CHUNKS
    assert n_idx % SC_WINDOW == 0 and n % SUBLANES == 0 and m % SUBLANES == 0

    @pl.kernel(out_type=jax.ShapeDtypeStruct((n_idx, LANES), jnp.int32), mesh=_sc_mesh())
    def gather(t_hbm, i_hbm, o_hbm):
        def body(i_vmem, o_vmem):
            pltpu.sync_copy(t_hbm.at[i_vmem.at[0]], o_vmem)

        pltpu.emit_pipeline(
            body,
            grid=(n_idx // SC_WINDOW,),
            in_specs=[pl.BlockSpec((1, SC_WINDOW), index_map=lambda i: (0, i))],
            out_specs=[pl.BlockSpec((SC_WINDOW, LANES), index_map=lambda i: (i, 0))],
            core_axis_name=("core", "subcore"),
            dimension_semantics=(pltpu.PARALLEL,),
        )(i_hbm, o_hbm)

    return _row_view(gather(_chunk_view(table), _sc_index_chunks(idx)))


def _sc_scatter_rows(rows, idx, n_out):
    n, m = rows.shape[0], idx.shape[0]
    n_idx = m * ROW_CHUNKS
    src_steps = n * ROW_CHUNKS // SC_WINDOW
    assert n_idx % SC_WINDOW == 0 and (n * ROW_CHUNKS) % SC_WINDOW == 0 and m % n == 0
    assert n % SUBLANES == 0 and n_out % SUBLANES == 0

    @pl.kernel(out_type=jax.ShapeDtypeStruct((n_out * ROW_CHUNKS, LANES), jnp.int32), mesh=_sc_mesh())
    def scatter(x_hbm, i_hbm, o_hbm):
        def body(x_vmem, i_vmem):
            pltpu.sync_copy(x_vmem, o_hbm.at[i_vmem.at[0]])

        pltpu.emit_pipeline(
            body,
            grid=(n_idx // SC_WINDOW,),
            in_specs=[pl.BlockSpec((SC_WINDOW, LANES), index_map=lambda i: (i % src_steps, 0)),
                      pl.BlockSpec((1, SC_WINDOW), index_map=lambda i: (0, i))],
            out_specs=[],
            core_axis_name=("core", "subcore"),
            dimension_semantics=(pltpu.PARALLEL,),
        )(x_hbm, i_hbm)

    return _row_view(scatter(_chunk_view(rows), _sc_index_chunks(idx)))


def _dispatch_plan(route):
    t = route.shape[1]
    n_pairs = 2 * t
    n_tiles = n_pairs // MOE_TILE + N_EXPERTS
    blk = LANES
    e = jnp.concatenate([route[0], route[1]]).astype(jnp.int32)
    onehot = (e[:, None] == jnp.arange(N_EXPERTS, dtype=jnp.int32)[None, :]).astype(F32)
    oh3 = onehot.reshape(n_pairs // blk, blk, N_EXPERTS)
    tri = (jnp.arange(blk)[:, None] >= jnp.arange(blk)[None, :]).astype(F32)
    within = jnp.einsum("ij,bjk->bik", tri, oh3)
    totals = within[:, -1, :]
    before = jnp.cumsum(totals, axis=0) - totals
    csum = (within + before[:, None, :]).reshape(n_pairs, N_EXPERTS)
    counts = jnp.sum(totals, axis=0).astype(jnp.int32)
    padded = ((counts + MOE_TILE - 1) // MOE_TILE) * MOE_TILE
    ends = jnp.cumsum(padded)
    starts = ends - padded
    pair_slot = jnp.sum(onehot * (csum - 1.0 + starts.astype(F32)[None, :]), axis=1).astype(jnp.int32)
    tile_start = jnp.arange(n_tiles, dtype=jnp.int32) * MOE_TILE
    tile_e = jnp.minimum(jnp.sum((tile_start[:, None] >= ends[None, :]).astype(jnp.int32), axis=1), N_EXPERTS - 1)
    e_onehot = tile_e[:, None] == jnp.arange(N_EXPERTS, dtype=jnp.int32)[None, :]
    filled = jnp.sum(jnp.where(e_onehot, (starts + counts)[None, :], 0), axis=1)
    tile_rows = jnp.clip(filled - tile_start, 0, MOE_TILE).astype(jnp.int32)
    last_used = jnp.maximum(ends[-1] // MOE_TILE - 1, 0)
    tile_expert = jnp.where(tile_start < ends[-1], tile_e, tile_e[last_used]).astype(jnp.int32)
    return pair_slot, tile_expert, tile_rows


def _swiglu(xb, wg, wu, wd):
    gate = jnp.dot(xb, wg, preferred_element_type=F32)
    up = jnp.dot(xb, wu, preferred_element_type=F32)
    a = ((gate * _sigmoid(gate)) * up).astype(BF16)
    return jnp.dot(a, wd, preferred_element_type=F32)


def _ffn_kernel(te_ref, rows_ref, xs_ref, wg_ref, wu_ref, wd_ref, ys_ref, wg_s, wu_s, wd_s):
    i = pl.program_id(0)
    new_expert = jnp.logical_or(i == 0, te_ref[i] != te_ref[jnp.maximum(i - 1, 0)])

    @pl.when(new_expert)
    def _():
        wg_s[...] = wg_ref[0].astype(BF16)
        wu_s[...] = wu_ref[0].astype(BF16)
        wd_s[...] = wd_ref[0].astype(BF16)

    @pl.when(rows_ref[i] > 0)
    def _():
        live = lax.broadcasted_iota(jnp.int32, (MOE_TILE, 1), 0) < rows_ref[i]
        xb = _load_unpack(xs_ref, live).astype(BF16)
        ys_ref[...] = _pack_rows(_swiglu(xb, wg_s[...], wu_s[...], wd_s[...]))

    @pl.when(rows_ref[i] == 0)
    def _():
        ys_ref[...] = jnp.zeros_like(ys_ref)


def _expert_ffn(xs, tile_expert, tile_rows, wg, wu, wd, layer):
    n_slots = xs.shape[0]
    d = D_MODEL
    blk = pl.BlockSpec((MOE_TILE, PACK_W), lambda i, te, nr: (i, 0))
    return pl.pallas_call(
        _ffn_kernel,
        grid_spec=pltpu.PrefetchScalarGridSpec(
            num_scalar_prefetch=2,
            grid=(n_slots // MOE_TILE,),
            in_specs=[blk,
                      pl.BlockSpec((None, 1, d, D_EXPERT), lambda i, te, nr: (layer, te[i], 0, 0)),
                      pl.BlockSpec((None, 1, d, D_EXPERT), lambda i, te, nr: (layer, te[i], 0, 0)),
                      pl.BlockSpec((None, 1, D_EXPERT, d), lambda i, te, nr: (layer, te[i], 0, 0))],
            out_specs=blk,
            scratch_shapes=[pltpu.VMEM((d, D_EXPERT), BF16), pltpu.VMEM((d, D_EXPERT), BF16),
                            pltpu.VMEM((D_EXPERT, d), BF16)]),
        out_shape=jax.ShapeDtypeStruct(xs.shape, jnp.int32),
        compiler_params=_params(1),
        name="expert_ffn",
    )(tile_expert, tile_rows, xs, wg, wu, wd)


def _combine_kernel(h_ref, y1_ref, y2_ref, w_ref, sg_ref, su_ref, sd_ref, x_ref, g2_ref, gfin_ref, o_ref,
                    *, final):
    y = _swiglu(_load_unpack(h_ref).astype(BF16), sg_ref[...], su_ref[...], sd_ref[...])
    w = w_ref[...]
    y = y + w[:, 0:1] * _load_unpack(y1_ref) + w[:, 1:2] * _load_unpack(y2_ref)
    xn = x_ref[...] + g2_ref[0] * y
    if final:
        xn = _rms(xn, gfin_ref[...])
    o_ref[...] = xn


def _moe_combine(h2p, yg, w12, sg, su, sd, x2, set_off, x_off, n_rows, mod3, brow, gfin, final):
    d = x2.shape[1]
    tm = min(512, n_rows)
    assert set_off % tm == 0 and x_off % tm == 0 and n_rows % tm == 0
    s_blk, x_blk, n_blk = set_off // tm, x_off // tm, n_rows // tm
    const2 = lambda i: (0, 0)
    pk = lambda o: pl.BlockSpec((tm, PACK_W), lambda i: (i + o, 0))
    x_spec = pl.BlockSpec((tm, d), lambda i: (i + x_blk, 0))
    return pl.pallas_call(
        functools.partial(_combine_kernel, final=final),
        grid=(n_blk,),
        in_specs=[pk(s_blk), pk(0), pk(n_blk),
                  pl.BlockSpec((tm, LANES), lambda i: (i + x_blk, 0)),
                  pl.BlockSpec(sg.shape, const2), pl.BlockSpec(su.shape, const2), pl.BlockSpec(sd.shape, const2),
                  x_spec,
                  _mod_spec(5, lambda i: brow((i + x_blk) * tm)),
                  pl.BlockSpec((1, d), const2)],
        out_specs=x_spec,
        out_shape=jax.ShapeDtypeStruct(x2.shape, F32),
        input_output_aliases={7: 0},
        compiler_params=_params(1),
        name="moe_combine",
    )(h2p, yg, yg, w12, sg, su, sd, x2, mod3, gfin.reshape(1, d))


def _moe_routed(parts, wg, wu, wd, layer, sg, su, sd, mod3, gfin, final):
    h2p = parts[0][0] if len(parts) == 1 else jnp.concatenate([pt[0] for pt in parts], axis=0)
    route = parts[0][1] if len(parts) == 1 else jnp.concatenate([pt[1] for pt in parts], axis=1)
    t_all = h2p.shape[0]
    pair_slot, tile_expert, tile_rows = _dispatch_plan(route)
    n_slots = tile_expert.shape[0] * MOE_TILE
    xs = _sc_scatter_rows(h2p, pair_slot, n_slots)
    ys = _expert_ffn(xs, tile_expert, tile_rows, wg, wu, wd, layer)
    outs, set_off = [], 0
    for _, _, w12, x2, brow in parts:
        t = x2.shape[0]
        n_rows = t // COMBINE_CHUNKS if t % (COMBINE_CHUNKS * 512) == 0 else t
        for x_off in range(0, t, n_rows):
            lo = set_off + x_off
            idx = jnp.concatenate([pair_slot[lo:lo + n_rows], pair_slot[t_all + lo:t_all + lo + n_rows]])
            yg = _sc_gather_rows(ys, idx)
            x2 = _moe_combine(h2p, yg, w12, sg, su, sd, x2, lo, x_off, n_rows, mod3, brow, gfin, final)
        outs.append(x2)
        set_off += t
    return outs


def _rope_tables(seq):
    rows = seq // GRID_W
    row = np.repeat(np.arange(rows), GRID_W).astype(np.float32)
    col = np.tile(np.arange(GRID_W), rows).astype(np.float32)
    axis_dim = QK_ROPE // 2
    inv = (ROPE_THETA ** (-np.arange(0, axis_dim, 2, dtype=np.float32) / axis_dim)).astype(np.float32)
    ang = np.concatenate([row[:, None] * inv, col[:, None] * inv], axis=-1).astype(np.float32)
    cos, sin = np.cos(ang), np.sin(ang)
    zero = np.zeros_like(cos)
    return (jnp.asarray(np.concatenate([cos, cos, zero, zero], axis=1)),
            jnp.asarray(np.concatenate([-sin, sin, zero, zero], axis=1)))


def _rope_group(w, start):
    half = QK_ROPE // 2
    x1, x2 = w[..., start:start + half], w[..., start + half:start + 2 * half]
    return jnp.concatenate([x1, x2, x2, x1], axis=-1)


def _w_in_relayout_kernel(w_ref, o_ref):
    w = w_ref[...]
    o_ref[...] = jnp.concatenate([w[:, :KV_RANK], w[:, KV_COLS:Q_END], _rope_group(w, KV_RANK),
                                  w[:, Q_END:]], axis=1).astype(BF16)


def _relayout_w_in(w):
    depth, d, n_in = w.shape
    tr = 256
    return pl.pallas_call(
        _w_in_relayout_kernel,
        grid=(depth, d // tr),
        in_specs=[pl.BlockSpec((None, tr, n_in), lambda l, i: (l, i, 0))],
        out_specs=pl.BlockSpec((None, tr, P_COLS), lambda l, i: (l, i, 0)),
        out_shape=jax.ShapeDtypeStruct((depth, d, P_COLS), BF16),
        compiler_params=_params(2),
        name="w_in_relayout",
    )(w)


def _relayout_w_qb(w):
    dq = QK_NOPE + QK_ROPE
    cols = []
    for h in range(N_HEADS):
        cols += [w[:, h * dq:h * dq + QK_NOPE], _rope_group(w, h * dq + QK_NOPE)]
    return jnp.concatenate(cols, axis=1).astype(BF16)


def kernel(x, c, ctx, c_ctx, w_ada, b_ada, norm_mix_g, norm_ffn_g, w_in, q_norm_g, kv_norm_g, w_qb, w_kvb, w_pool, pool_scale, w_out, w_router, router_bias, w_exp_gate, w_exp_up, w_exp_down, w_sh_gate, w_sh_up, w_sh_down, final_norm_g):
    bsz, seq, d = x.shape
    n_ctx = ctx.shape[1]
    depth = w_ada.shape[0]
    assert bsz < MOD_ROWS and d == D_MODEL

    c_rows = jnp.concatenate([c, c_ctx[None], jnp.zeros((MOD_ROWS - bsz - 1, d), F32)], axis=0)
    mod = _ada_mod(c_rows, w_ada, b_ada)

    cos_t, sin_t = _rope_tables(seq)
    ones_t = jnp.concatenate([jnp.ones((n_ctx, 2 * (QK_ROPE // 2)), F32),
                              jnp.zeros((n_ctx, LANES - QK_ROPE), F32)], axis=1)
    zeros_t = jnp.zeros((n_ctx, LANES), F32)
    wr_t = w_router.T.astype(BF16)
    w_in_r = _relayout_w_in(w_in)
    lat_row = lambda r: r // seq
    ctx_row = lambda r: bsz

    x2 = x.reshape(bsz * seq, d)
    xc2 = ctx.reshape(bsz * n_ctx, d)
    for l in range(depth):
        last = l == depth - 1
        mod3 = mod[l].reshape(MOD_ROWS * N_MOD, 1, d)
        wq = _relayout_w_qb(w_qb[l])
        w_kv3 = w_kvb[l].reshape(KV_RANK, N_HEADS, QK_NOPE + V_DIM)
        wk = w_kv3[:, :, :QK_NOPE].reshape(KV_RANK, N_HEADS * QK_NOPE).astype(BF16)
        wvt = (w_kv3[:, :, QK_NOPE:].transpose(1, 2, 0).reshape(N_HEADS * V_DIM, KV_RANK)
               .astype(BF16))
        wp = w_pool[l].astype(BF16)
        wo = w_out[l].astype(BF16)
        sg, su, sd = w_sh_gate[l].astype(BF16), w_sh_up[l].astype(BF16), w_sh_down[l].astype(BF16)
        moe = functools.partial(_moe_routed, wg=w_exp_gate, wu=w_exp_up, wd=w_exp_down, layer=l,
                                sg=sg, su=su, sd=sd, mod3=mod3, gfin=final_norm_g)

        p = _in_proj(x2, mod3, lat_row, norm_mix_g[l], w_in_r, l, P_COLS)
        pc = _in_proj(xc2, mod3, ctx_row, norm_mix_g[l], w_in_r, l, LAT_W if last else P_COLS)
        q, k_lat, v_lat = _qkv(p, bsz, seq, cos_t, sin_t, q_norm_g[l], kv_norm_g[l], wq, wk, wvt, True)
        ctx_out = _qkv(pc, bsz, n_ctx, ones_t, zeros_t, q_norm_g[l], kv_norm_g[l], wq, wk, wvt, not last)
        k_ctx, v_ctx = ctx_out[-2:]
        y = _attention(q, [k_ctx, k_lat], [v_ctx, v_lat])
        x2, h2p, route, w12 = _mixer_out(y, p, x2, mod3, lat_row, seq, wp, pool_scale[l], wo,
                                         norm_ffn_g[l], wr_t, router_bias)
        if last:
            x2, = moe([(h2p, route, w12, x2, lat_row)], final=True)
        else:
            yc = _attention(ctx_out[0], [k_ctx], [v_ctx])
            xc2, h2c, route_c, w12c = _mixer_out(yc, pc, xc2, mod3, ctx_row, n_ctx, wp, pool_scale[l], wo,
                                                 norm_ffn_g[l], wr_t, router_bias)
            x2, xc2 = moe([(h2p, route, w12, x2, lat_row), (h2c, route_c, w12c, xc2, ctx_row)], final=False)
    return x2.reshape(bsz, seq, d)
```

```python
import functools
import math

import jax
import jax.numpy as jnp
import numpy as np
from jax import lax
from jax.experimental import pallas as pl
from jax.experimental.pallas import tpu as pltpu
from jax.experimental.pallas import tpu_sc as plsc

F32 = jnp.float32
BF16 = jnp.bfloat16

D_MODEL = 2048
GRID_W = 64
N_HEADS = 16
QK_NOPE = 128
QK_ROPE = 64
V_DIM = 128
Q_RANK = 384
KV_RANK = 512
ROPE_THETA = 10000.0
POOL_DIM = 1024
POOL_GROUPS = 4
POOL_GROUP_DIM = 256
POOL_WINDOWS = (2, 4, 8, 16)
POOL_GROUP_OUT = 512
KV_COLS = KV_RANK + QK_ROPE
Q_END = KV_COLS + Q_RANK
POOL_END = Q_END + POOL_DIM
N_EXPERTS = 16
N_GROUPS = 4
EXPERTS_PER_GROUP = 4
D_EXPERT = 512
EPS = 1e-6
N_MOD = 6

LANES = 128
HEAD_W = 2 * LANES
LAT_W = 1024
P_COLS = LAT_W + POOL_DIM + 2 * D_MODEL
POOL_HALO = 16
MOD_ROWS = 8
Q_SCALE = (1.0 / math.sqrt(QK_NOPE + QK_ROPE)) * math.log2(math.e)
VMEM_LIMIT = 56 * 1024 * 1024
SUBLANES = 8
PACK_W = D_MODEL // 2
ROW_CHUNKS = PACK_W // LANES
ROUTE_ROWS = SUBLANES
MOE_TILE = 512
SC_WINDOW = 128
ATTN_TQ = 256
ATTN_KB = 512
V_HEAD_GROUP = 8
ATTN_HEADS_PER_STEP = 2
ATTN_PAIRS_PER_ITER = 3
MIX_CHAIN_ROWS = 256
COMBINE_CHUNKS = 2


def _sigmoid(x):
    return 1.0 / (1.0 + jnp.exp(-x))


def _pack_rows(y):
    half = y.shape[1] // 2
    return pltpu.pack_elementwise([y[:, :half], y[:, half:]], packed_dtype=BF16)


def _load_unpack(ref, live=None):
    w = ref[...]
    if live is not None:
        w = jnp.where(live, w, 0)
    lo = pltpu.unpack_elementwise(w, index=0, packed_dtype=BF16, unpacked_dtype=F32)
    hi = pltpu.unpack_elementwise(w, index=1, packed_dtype=BF16, unpacked_dtype=F32)
    return jnp.concatenate([lo, hi], axis=1)


def _rms(xf, g):
    ms = jnp.mean(xf * xf, axis=-1, keepdims=True)
    return xf * lax.rsqrt(ms + EPS) * g


def _params(n_axes):
    return pltpu.CompilerParams(dimension_semantics=("arbitrary",) * n_axes,
                                vmem_limit_bytes=VMEM_LIMIT)


def _ada_kernel(c_ref, w_ref, b_ref, o_ref):
    cf = c_ref[...]
    a = (cf * _sigmoid(cf)).astype(BF16)
    o_ref[0] = jnp.dot(a, w_ref[0].astype(BF16), preferred_element_type=F32) + b_ref[0]


def _ada_mod(c_rows, w_ada, b_ada):
    depth, d, n = w_ada.shape
    tn = 1024
    return pl.pallas_call(
        _ada_kernel,
        grid=(depth, n // tn),
        in_specs=[pl.BlockSpec((MOD_ROWS, d), lambda l, j: (0, 0)),
                  pl.BlockSpec((1, d, tn), lambda l, j: (l, 0, j)),
                  pl.BlockSpec((1, 1, tn), lambda l, j: (l, 0, j))],
        out_specs=pl.BlockSpec((1, MOD_ROWS, tn), lambda l, j: (l, 0, j)),
        out_shape=jax.ShapeDtypeStruct((depth, MOD_ROWS, n), F32),
        compiler_params=_params(2),
        name="ada_mod",
    )(c_rows, w_ada, b_ada.reshape(depth, 1, n))


def _mod_spec(k, brow):
    return pl.BlockSpec((1, 1, D_MODEL), lambda i, *_: (brow(i) * N_MOD + k, 0, 0))


def _inproj_kernel(x_ref, g_ref, sh_ref, sc_ref, w_ref, o_ref, h_scr, *, tn):
    y = _rms(x_ref[...], g_ref[...])
    h_scr[...] = (y * (1.0 + sc_ref[0]) + sh_ref[0]).astype(BF16)
    for c0 in range(0, o_ref.shape[1], tn):
        acc = lax.dot_general(h_scr[...], w_ref[c0:c0 + tn, :], (((1,), (1,)), ((), ())),
                              preferred_element_type=F32)
        if c0 >= LAT_W + POOL_DIM:
            acc = _sigmoid(acc)
        o_ref[:, c0:c0 + tn] = acc.astype(BF16)


def _in_proj(x2, mod3, brow, g, w, layer, n_cols):
    t, d = x2.shape
    tm = min(512, t)
    return pl.pallas_call(
        functools.partial(_inproj_kernel, tn=512),
        grid=(t // tm,),
        in_specs=[pl.BlockSpec((tm, d), lambda i: (i, 0)),
                  pl.BlockSpec((1, d), lambda i: (0, 0)),
                  _mod_spec(0, lambda i: brow(i * tm)),
                  _mod_spec(1, lambda i: brow(i * tm)),
                  pl.BlockSpec((None, n_cols, d), lambda i: (layer, 0, 0), pipeline_mode=pl.Buffered(1))],
        out_specs=pl.BlockSpec((tm, n_cols), lambda i: (i, 0)),
        out_shape=jax.ShapeDtypeStruct((t, n_cols), BF16),
        scratch_shapes=[pltpu.VMEM((tm, d), BF16)],
        compiler_params=_params(1),
        name="in_proj",
    )(x2, g.reshape(1, d), mod3, mod3, w)


def _qkv_kernel(lat_ref, cos_ref, sin_ref, gq_ref, gkv_ref, wq_ref, wk_ref, wvt_ref, *out_refs, with_q):
    if with_q:
        q_ref, k_ref, v_ref = out_refs
    else:
        k_ref, v_ref = out_refs
    cos = cos_ref[...]
    sin = sin_ref[...]

    def rope(grp):
        return grp * cos + pltpu.roll(grp, 2 * (QK_ROPE // 2), axis=1) * sin

    kvn = _rms(lat_ref[:, :KV_RANK].astype(F32), gkv_ref[...]).astype(BF16)
    kpe = rope(lat_ref[:, KV_RANK + Q_RANK:].astype(F32)).astype(BF16)
    for h in range(0, N_HEADS, 2):
        kn = jnp.dot(kvn, wk_ref[:, h * QK_NOPE:(h + 2) * QK_NOPE], preferred_element_type=F32)
        for hh in range(2):
            k_ref[0, h + hh, :, :QK_NOPE] = kn[:, hh * QK_NOPE:(hh + 1) * QK_NOPE].astype(BF16)
            k_ref[0, h + hh, :, QK_NOPE:] = kpe
    for h in range(0, N_HEADS, V_HEAD_GROUP):
        vt = lax.dot_general(wvt_ref[h * V_DIM:(h + V_HEAD_GROUP) * V_DIM, :], kvn,
                             (((1,), (1,)), ((), ())), preferred_element_type=F32)
        for hh in range(V_HEAD_GROUP):
            v_ref[0, h + hh] = vt[hh * V_DIM:(hh + 1) * V_DIM, :].astype(BF16)
    if with_q:
        qn = _rms(lat_ref[:, KV_RANK:KV_RANK + Q_RANK].astype(F32), gq_ref[...]).astype(BF16)
        for h in range(N_HEADS):
            qh = jnp.dot(qn, wq_ref[:, h * HEAD_W:(h + 1) * HEAD_W], preferred_element_type=F32)
            q_ref[0, h, :, :QK_NOPE] = (qh[:, :QK_NOPE] * Q_SCALE).astype(BF16)
            q_ref[0, h, :, QK_NOPE:] = (rope(qh[:, QK_NOPE:]) * Q_SCALE).astype(BF16)


def _qkv(p, b, n, cos_t, sin_t, gq, gkv, wq, wk, wvt, with_q):
    tm = min(256, n)
    tpb = n // tm
    rope_tiles = cos_t.shape[0] // tm
    head_spec = pl.BlockSpec((1, N_HEADS, tm, HEAD_W), lambda i: (i // tpb, 0, i % tpb, 0))
    tab_spec = pl.BlockSpec((tm, LANES), lambda i: ((i % tpb) % rope_tiles, 0))
    out_shape = [jax.ShapeDtypeStruct((b, N_HEADS, n, HEAD_W), BF16),
                 jax.ShapeDtypeStruct((b, N_HEADS, V_DIM, n), BF16)]
    out_specs = [head_spec,
                 pl.BlockSpec((1, N_HEADS, V_DIM, tm), lambda i: (i // tpb, 0, 0, i % tpb))]
    if with_q:
        out_shape = [jax.ShapeDtypeStruct((b, N_HEADS, n, HEAD_W), BF16)] + out_shape
        out_specs = [head_spec] + out_specs
    return pl.pallas_call(
        functools.partial(_qkv_kernel, with_q=with_q),
        grid=(b * tpb,),
        in_specs=[pl.BlockSpec((tm, LAT_W), lambda i: (i, 0)),
                  tab_spec, tab_spec,
                  pl.BlockSpec((1, Q_RANK), lambda i: (0, 0)),
                  pl.BlockSpec((1, KV_RANK), lambda i: (0, 0)),
                  pl.BlockSpec(wq.shape, lambda i: (0, 0)),
                  pl.BlockSpec(wk.shape, lambda i: (0, 0)),
                  pl.BlockSpec(wvt.shape, lambda i: (0, 0))],
        out_specs=out_specs,
        out_shape=out_shape,
        compiler_params=_params(1),
        name="qkv",
    )(p, cos_t, sin_t, gq.reshape(1, Q_RANK), gkv.reshape(1, KV_RANK), wq, wk, wvt)


def _row_fold(x, op):
    parts = [x[r:r + SUBLANES] for r in range(0, x.shape[0], SUBLANES)]
    a, b = parts[0], parts[1]
    for i in range(2, len(parts) - 1, 2):
        a, b = op(a, parts[i]), op(b, parts[i + 1])
    if len(parts) % 2:
        a = op(a, parts[-1])
    return op(a, b)


def _attn_kernel(q_ref, *refs, n_seg, kb, tq):
    k_refs, vt_refs = refs[:n_seg], refs[n_seg:2 * n_seg]
    o_ref, s_a, s_b, m_a, m_b = refs[2 * n_seg:]
    chunks = []
    off = 0
    for kr, vr in zip(k_refs, vt_refs):
        nk = kr.shape[2]
        for c0 in range(0, nk, kb):
            n = min(kb, nk - c0)
            chunks.append((kr, vr, c0, n, off))
            off += n

    n_heads, nt = q_ref.shape[1], q_ref.shape[2] // tq
    bufs = ((s_a, m_a), (s_b, m_b))

    def tile_rows(t):
        return pl.ds(pl.multiple_of(t * tq, tq), tq)

    def scores(g, t, s_buf, m_buf):
        rows = tile_rows(t)
        q = q_ref[0, g, rows, :]
        mp = None
        for kr, vr, c0, n, o in chunks:
            s = lax.dot_general(kr[0, g, c0:c0 + n, :], q, (((1,), (1,)), ((), ())),
                                preferred_element_type=F32)
            s_buf[o:o + n, :] = s
            m = _row_fold(s, jnp.maximum)
            mp = m if mp is None else jnp.maximum(mp, m)
        m_buf[...] = mp

    def values(g, t, s_buf, m_buf):
        rows = tile_rows(t)
        mrow = jnp.max(m_buf[...], axis=0, keepdims=True)
        lp = None
        acc = None
        for kr, vr, c0, n, o in chunks:
            pr = jnp.exp2(s_buf[o:o + n, :] - mrow)
            ls = _row_fold(pr, jnp.add)
            lp = ls if lp is None else lp + ls
            pv = jnp.dot(vr[0, g, :, c0:c0 + n], pr.astype(BF16), preferred_element_type=F32)
            acc = pv if acc is None else acc + pv
        l = jnp.sum(lp, axis=0, keepdims=True)
        o_ref[0, g, rows, :] = (acc / l).T.astype(BF16)

    def phase(g, t, parity):
        scores(g, t + 1, *bufs[1 - parity])
        values(g, t, *bufs[parity])

    scores(0, 0, *bufs[0])
    for g in range(n_heads):
        par0 = (g * nt) % 2
        n_pairs = (nt - 1) // 2

        def pair(j, g=g, par0=par0):
            phase(g, 2 * j, par0)
            phase(g, 2 * j + 1, 1 - par0)

        def pairs(jj, carry, pair=pair):
            for k in range(ATTN_PAIRS_PER_ITER):
                pair(jj * ATTN_PAIRS_PER_ITER + k)
            return carry

        n_iter = n_pairs // ATTN_PAIRS_PER_ITER
        if n_iter:
            lax.fori_loop(0, n_iter, pairs, 0)
        for j in range(n_iter * ATTN_PAIRS_PER_ITER, n_pairs):
            pair(j)
        if (nt - 1) % 2:
            phase(g, nt - 2, (par0 + nt - 2) % 2)
        last = bufs[(par0 + nt - 1) % 2]
        if g + 1 < n_heads:
            scores(g + 1, 0, *bufs[((g + 1) * nt) % 2])
        values(g, nt - 1, *last)


def _attention(q, ks, vts):
    b, hh, n, _ = q.shape
    tq = min(ATTN_TQ, n)
    hg = ATTN_HEADS_PER_STEP if n > tq else hh
    assert hh % hg == 0
    nk_total = sum(k.shape[2] for k in ks)
    seg_spec = lambda a: pl.BlockSpec((1, hg) + a.shape[2:], lambda bi, h: (bi, h, 0, 0))
    return pl.pallas_call(
        functools.partial(_attn_kernel, n_seg=len(ks), kb=ATTN_KB, tq=tq),
        grid=(b, hh // hg),
        in_specs=[pl.BlockSpec((1, hg, n, HEAD_W), lambda bi, h: (bi, h, 0, 0))]
                 + [seg_spec(a) for a in ks] + [seg_spec(a) for a in vts],
        out_specs=pl.BlockSpec((1, hg, n, V_DIM), lambda bi, h: (bi, h, 0, 0)),
        out_shape=jax.ShapeDtypeStruct((b, hh, n, V_DIM), BF16),
        scratch_shapes=[pltpu.VMEM((nk_total, tq), F32), pltpu.VMEM((nk_total, tq), F32),
                        pltpu.VMEM((SUBLANES, tq), F32), pltpu.VMEM((SUBLANES, tq), F32)],
        compiler_params=_params(2),
        name="attention",
    )(q, *ks, *vts)


def _top2sum(a, b, c, d):
    s1, t1 = jnp.maximum(a, b), jnp.minimum(a, b)
    s2, t2 = jnp.maximum(c, d), jnp.minimum(c, d)
    return jnp.maximum(s1, s2) + jnp.maximum(jnp.minimum(s1, s2), jnp.maximum(t1, t2))


def _route(logits_t, bias):
    sc = _sigmoid(logits_t)
    sel = sc + bias
    sel_r = [sel[e:e + 1, :] for e in range(N_EXPERTS)]
    sc_r = [sc[e:e + 1, :] for e in range(N_EXPERTS)]
    epg = EXPERTS_PER_GROUP
    gs = [_top2sum(*sel_r[g * epg:(g + 1) * epg]) for g in range(N_GROUPS)]
    best, gi = gs[0], jnp.zeros(gs[0].shape, jnp.int32)
    for g in range(1, N_GROUPS):
        upd = gs[g] > best
        best = jnp.where(upd, gs[g], best)
        gi = jnp.where(upd, g, gi)

    def pick_group(rows, k):
        r = rows[k]
        for g in range(1, N_GROUPS):
            r = jnp.where(gi == g, rows[g * epg + k], r)
        return r

    in_sel = [pick_group(sel_r, k) for k in range(epg)]
    in_sc = [pick_group(sc_r, k) for k in range(epg)]
    b1, i1 = in_sel[0], jnp.zeros(gi.shape, jnp.int32)
    for k in range(1, epg):
        upd = in_sel[k] > b1
        b1 = jnp.where(upd, in_sel[k], b1)
        i1 = jnp.where(upd, k, i1)
    b2, i2 = None, None
    for k in range(epg):
        cand = jnp.where(i1 == k, -jnp.inf, in_sel[k])
        if b2 is None:
            b2, i2 = cand, jnp.zeros(gi.shape, jnp.int32)
        else:
            upd = cand > b2
            b2 = jnp.where(upd, cand, b2)
            i2 = jnp.where(upd, k, i2)

    def pick_local(idx):
        r = in_sc[0]
        for k in range(1, epg):
            r = jnp.where(idx == k, in_sc[k], r)
        return r

    s1, s2 = pick_local(i1), pick_local(i2)
    den = s1 + s2
    return gi * epg + i1, gi * epg + i2, s1 / den, s2 / den


def _mixout_kernel(y_ref, u_ref, up_ref, un_ref, gm_ref, gp_ref, wp_ref, ps_ref, wo_ref, x_ref,
                   g1_ref, gf_ref, sh2_ref, sc2_ref, wr_ref, rb_ref,
                   xo_ref, h2_ref, route_ref, w12_ref, *, tm, tpb, n_seq):
    i = pl.program_id(0)
    uext = jnp.concatenate([up_ref[...], u_ref[...], un_ref[...]], axis=0)
    sub = min(MIX_CHAIN_ROWS, tm)

    def mix_stage(r0):
        rows = slice(r0, r0 + sub)
        base = (i % tpb) * tm + r0
        r = lax.broadcasted_iota(jnp.int32, (sub, sub + 2 * POOL_HALO), 0)
        c = lax.broadcasted_iota(jnp.int32, (sub, sub + 2 * POOL_HALO), 1)
        rel = c - POOL_HALO - r
        jpos = base - POOL_HALO + c
        valid = (jpos >= 0) & (jpos < n_seq)
        tpos = base + lax.broadcasted_iota(jnp.int32, (sub, 1), 0)
        uwin = uext[r0:r0 + sub + 2 * POOL_HALO]
        uc = uext[r0 + POOL_HALO:r0 + POOL_HALO + sub]
        parts = []
        for g, w in enumerate(POOL_WINDOWS):
            sl = slice(g * POOL_GROUP_DIM, (g + 1) * POOL_GROUP_DIM)
            band = jnp.where(valid & (rel >= -(w // 2)) & (rel < w // 2), 1.0, 0.0).astype(BF16)
            cnt = jnp.clip(tpos + w // 2, 0, n_seq) - jnp.clip(tpos - w // 2, 0, n_seq)
            wsum = jnp.dot(band, uwin[:, sl], preferred_element_type=F32)
            z = wsum / cnt.astype(F32) - uc[:, sl].astype(F32)
            parts.append(jnp.dot(z.astype(BF16), wp_ref[g], preferred_element_type=F32))
        ypool = jnp.concatenate(parts, axis=1) * ps_ref[...]
        y_mla = jnp.concatenate([y_ref[0, h, rows, :] for h in range(N_HEADS)], axis=1)
        mix = gm_ref[rows, :].astype(F32) * y_mla.astype(F32) + gp_ref[rows, :].astype(F32) * ypool
        return jnp.dot(mix.astype(BF16), wo_ref[...], preferred_element_type=F32)

    def norm_route_stage(r0, y):
        rows = slice(r0, r0 + sub)
        xn = x_ref[rows, :] + g1_ref[0] * y
        xo_ref[rows, :] = xn
        h2f = _rms(xn, gf_ref[...]) * (1.0 + sc2_ref[0]) + sh2_ref[0]
        h2_ref[rows, :] = _pack_rows(h2f)
        logits_t = lax.dot_general(wr_ref[...], h2f.astype(BF16), (((1,), (1,)), ((), ())),
                                   preferred_element_type=F32)
        e1, e2, w1, w2 = _route(logits_t, rb_ref[...])
        zero = jnp.zeros((ROUTE_ROWS - 4, sub), F32)
        route_ref[:, rows] = jnp.concatenate([e1.astype(F32), e2.astype(F32), w1, w2, zero], axis=0)
        w12_ref[rows, :] = jnp.concatenate([w1, w2, jnp.zeros((LANES - 2, sub), F32)], axis=0).T

    starts = list(range(0, tm, sub))
    pending = None
    for r0 in starts:
        y = mix_stage(r0)
        if pending is not None:
            norm_route_stage(*pending)
        pending = (r0, y)
    norm_route_stage(*pending)


def _mixer_out(y, p, x2, mod3, brow, n_seq, wp, ps, wo, gf, wr_t, rb):
    t, d = x2.shape
    tm = min(2 * MIX_CHAIN_ROWS, n_seq)
    tpb = n_seq // tm
    hpt = tm // POOL_HALO
    n_halo = t // POOL_HALO
    row = lambda i: (i, 0)
    const2 = lambda i: (0, 0)
    mrow = lambda i: brow(i * tm)
    return pl.pallas_call(
        functools.partial(_mixout_kernel, tm=tm, tpb=tpb, n_seq=n_seq),
        grid=(t // tm,),
        in_specs=[pl.BlockSpec((1, N_HEADS, tm, V_DIM), lambda i: (i // tpb, 0, i % tpb, 0)),
                  pl.BlockSpec((tm, POOL_DIM), lambda i: (i, LAT_W // POOL_DIM)),
                  pl.BlockSpec((POOL_HALO, POOL_DIM),
                               lambda i: (jnp.maximum(i * hpt - 1, 0), LAT_W // POOL_DIM)),
                  pl.BlockSpec((POOL_HALO, POOL_DIM),
                               lambda i: (jnp.minimum((i + 1) * hpt, n_halo - 1), LAT_W // POOL_DIM)),
                  pl.BlockSpec((tm, d), lambda i: (i, (LAT_W + POOL_DIM) // d)),
                  pl.BlockSpec((tm, d), lambda i: (i, (LAT_W + POOL_DIM) // d + 1)),
                  pl.BlockSpec(wp.shape, lambda i: (0, 0, 0)),
                  pl.BlockSpec((1, d), const2),
                  pl.BlockSpec((d, d), const2, pipeline_mode=pl.Buffered(1)),
                  pl.BlockSpec((tm, d), row),
                  _mod_spec(2, mrow),
                  pl.BlockSpec((1, d), const2),
                  _mod_spec(3, mrow),
                  _mod_spec(4, mrow),
                  pl.BlockSpec((N_EXPERTS, d), const2),
                  pl.BlockSpec((N_EXPERTS, 1), const2)],
        out_specs=[pl.BlockSpec((tm, d), row),
                   pl.BlockSpec((tm, PACK_W), row),
                   pl.BlockSpec((ROUTE_ROWS, tm), lambda i: (0, i)),
                   pl.BlockSpec((tm, LANES), row)],
        out_shape=[jax.ShapeDtypeStruct((t, d), F32),
                   jax.ShapeDtypeStruct((t, PACK_W), jnp.int32),
                   jax.ShapeDtypeStruct((ROUTE_ROWS, t), F32),
                   jax.ShapeDtypeStruct((t, LANES), F32)],
        compiler_params=_params(1),
        name="mixer_out",
    )(y, p, p, p, p, p, wp, ps.reshape(1, d), wo, x2, mod3, gf.reshape(1, d), mod3, mod3, wr_t,
      rb.reshape(N_EXPERTS, 1))


def _chunk_view(x):
    n = x.shape[0]
    return (x.reshape(n // SUBLANES, SUBLANES, ROW_CHUNKS, LANES).transpose(0, 2, 1, 3)
            .reshape(n * ROW_CHUNKS, LANES))


def _row_view(c):
    n = c.shape[0] // ROW_CHUNKS
    return (c.reshape(n // SUBLANES, ROW_CHUNKS, SUBLANES, LANES).transpose(0, 2, 1, 3)
            .reshape(n, PACK_W))


def _sc_index_chunks(idx):
    r = idx.reshape(-1, 1, SUBLANES)
    j = jnp.arange(ROW_CHUNKS, dtype=jnp.int32)[None, :, None]
    ids = (r // SUBLANES) * (SUBLANES * ROW_CHUNKS) + j * SUBLANES + r % SUBLANES
    return ids.reshape(1, idx.shape[0] * ROW_CHUNKS)


def _sc_mesh():
    return plsc.VectorSubcoreMesh(core_axis_name="core", subcore_axis_name="subcore")


def _sc_gather_rows(table, idx):
    n, m = table.shape[0], idx.shape[0]
    n_idx = m * ROW_CHUNKS
    assert n_idx % SC_WINDOW == 0 and n % SUBLANES == 0 and m % SUBLANES == 0

    @pl.kernel(out_type=jax.ShapeDtypeStruct((n_idx, LANES), jnp.int32), mesh=_sc_mesh())
    def gather(t_hbm, i_hbm, o_hbm):
        def body(i_vmem, o_vmem):
            pltpu.sync_copy(t_hbm.at[i_vmem.at[0]], o_vmem)

        pltpu.emit_pipeline(
            body,
            grid=(n_idx // SC_WINDOW,),
            in_specs=[pl.BlockSpec((1, SC_WINDOW), index_map=lambda i: (0, i))],
            out_specs=[pl.BlockSpec((SC_WINDOW, LANES), index_map=lambda i: (i, 0))],
            core_axis_name=("core", "subcore"),
            dimension_semantics=(pltpu.PARALLEL,),
        )(i_hbm, o_hbm)

    return _row_view(gather(_chunk_view(table), _sc_index_chunks(idx)))


def _sc_scatter_rows(rows, idx, n_out):
    n, m = rows.shape[0], idx.shape[0]
    n_idx = m * ROW_CHUNKS
    src_steps = n * ROW_CHUNKS // SC_WINDOW
    assert n_idx % SC_WINDOW == 0 and (n * ROW_CHUNKS) % SC_WINDOW == 0 and m % n == 0
    assert n % SUBLANES == 0 and n_out % SUBLANES == 0

    @pl.kernel(out_type=jax.ShapeDtypeStruct((n_out * ROW_CHUNKS, LANES), jnp.int32), mesh=_sc_mesh())
    def scatter(x_hbm, i_hbm, o_hbm):
        def body(x_vmem, i_vmem):
            pltpu.sync_copy(x_vmem, o_hbm.at[i_vmem.at[0]])

        pltpu.emit_pipeline(
            body,
            grid=(n_idx // SC_WINDOW,),
            in_specs=[pl.BlockSpec((SC_WINDOW, LANES), index_map=lambda i: (i % src_steps, 0)),
                      pl.BlockSpec((1, SC_WINDOW), index_map=lambda i: (0, i))],
            out_specs=[],
            core_axis_name=("core", "subcore"),
            dimension_semantics=(pltpu.PARALLEL,),
        )(x_hbm, i_hbm)

    return _row_view(scatter(_chunk_view(rows), _sc_index_chunks(idx)))


def _dispatch_plan(route):
    t = route.shape[1]
    n_pairs = 2 * t
    n_tiles = n_pairs // MOE_TILE + N_EXPERTS
    blk = LANES
    e = jnp.concatenate([route[0], route[1]]).astype(jnp.int32)
    onehot = (e[:, None] == jnp.arange(N_EXPERTS, dtype=jnp.int32)[None, :]).astype(F32)
    oh3 = onehot.reshape(n_pairs // blk, blk, N_EXPERTS)
    tri = (jnp.arange(blk)[:, None] >= jnp.arange(blk)[None, :]).astype(F32)
    within = jnp.einsum("ij,bjk->bik", tri, oh3)
    totals = within[:, -1, :]
    before = jnp.cumsum(totals, axis=0) - totals
    csum = (within + before[:, None, :]).reshape(n_pairs, N_EXPERTS)
    counts = jnp.sum(totals, axis=0).astype(jnp.int32)
    padded = ((counts + MOE_TILE - 1) // MOE_TILE) * MOE_TILE
    ends = jnp.cumsum(padded)
    starts = ends - padded
    pair_slot = jnp.sum(onehot * (csum - 1.0 + starts.astype(F32)[None, :]), axis=1).astype(jnp.int32)
    tile_start = jnp.arange(n_tiles, dtype=jnp.int32) * MOE_TILE
    tile_e = jnp.minimum(jnp.sum((tile_start[:, None] >= ends[None, :]).astype(jnp.int32), axis=1), N_EXPERTS - 1)
    e_onehot = tile_e[:, None] == jnp.arange(N_EXPERTS, dtype=jnp.int32)[None, :]
    filled = jnp.sum(jnp.where(e_onehot, (starts + counts)[None, :], 0), axis=1)
    tile_rows = jnp.clip(filled - tile_start, 0, MOE_TILE).astype(jnp.int32)
    last_used = jnp.maximum(ends[-1] // MOE_TILE - 1, 0)
    tile_expert = jnp.where(tile_start < ends[-1], tile_e, tile_e[last_used]).astype(jnp.int32)
    return pair_slot, tile_expert, tile_rows


def _swiglu(xb, wg, wu, wd):
    gate = jnp.dot(xb, wg, preferred_element_type=F32)
    up = jnp.dot(xb, wu, preferred_element_type=F32)
    a = ((gate * _sigmoid(gate)) * up).astype(BF16)
    return jnp.dot(a, wd, preferred_element_type=F32)


def _ffn_kernel(te_ref, rows_ref, xs_ref, wg_ref, wu_ref, wd_ref, ys_ref, wg_s, wu_s, wd_s):
    i = pl.program_id(0)
    new_expert = jnp.logical_or(i == 0, te_ref[i] != te_ref[jnp.maximum(i - 1, 0)])

    @pl.when(new_expert)
    def _():
        wg_s[...] = wg_ref[0].astype(BF16)
        wu_s[...] = wu_ref[0].astype(BF16)
        wd_s[...] = wd_ref[0].astype(BF16)

    @pl.when(rows_ref[i] > 0)
    def _():
        live = lax.broadcasted_iota(jnp.int32, (MOE_TILE, 1), 0) < rows_ref[i]
        xb = _load_unpack(xs_ref, live).astype(BF16)
        ys_ref[...] = _pack_rows(_swiglu(xb, wg_s[...], wu_s[...], wd_s[...]))

    @pl.when(rows_ref[i] == 0)
    def _():
        ys_ref[...] = jnp.zeros_like(ys_ref)


def _expert_ffn(xs, tile_expert, tile_rows, wg, wu, wd, layer):
    n_slots = xs.shape[0]
    d = D_MODEL
    blk = pl.BlockSpec((MOE_TILE, PACK_W), lambda i, te, nr: (i, 0))
    return pl.pallas_call(
        _ffn_kernel,
        grid_spec=pltpu.PrefetchScalarGridSpec(
            num_scalar_prefetch=2,
            grid=(n_slots // MOE_TILE,),
            in_specs=[blk,
                      pl.BlockSpec((None, 1, d, D_EXPERT), lambda i, te, nr: (layer, te[i], 0, 0)),
                      pl.BlockSpec((None, 1, d, D_EXPERT), lambda i, te, nr: (layer, te[i], 0, 0)),
                      pl.BlockSpec((None, 1, D_EXPERT, d), lambda i, te, nr: (layer, te[i], 0, 0))],
            out_specs=blk,
            scratch_shapes=[pltpu.VMEM((d, D_EXPERT), BF16), pltpu.VMEM((d, D_EXPERT), BF16),
                            pltpu.VMEM((D_EXPERT, d), BF16)]),
        out_shape=jax.ShapeDtypeStruct(xs.shape, jnp.int32),
        compiler_params=_params(1),
        name="expert_ffn",
    )(tile_expert, tile_rows, xs, wg, wu, wd)


def _combine_kernel(h_ref, y1_ref, y2_ref, w_ref, sg_ref, su_ref, sd_ref, x_ref, g2_ref, gfin_ref, o_ref,
                    *, final):
    y = _swiglu(_load_unpack(h_ref).astype(BF16), sg_ref[...], su_ref[...], sd_ref[...])
    w = w_ref[...]
    y = y + w[:, 0:1] * _load_unpack(y1_ref) + w[:, 1:2] * _load_unpack(y2_ref)
    xn = x_ref[...] + g2_ref[0] * y
    if final:
        xn = _rms(xn, gfin_ref[...])
    o_ref[...] = xn


def _moe_combine(h2p, yg, w12, sg, su, sd, x2, set_off, x_off, n_rows, mod3, brow, gfin, final):
    d = x2.shape[1]
    tm = min(512, n_rows)
    assert set_off % tm == 0 and x_off % tm == 0 and n_rows % tm == 0
    s_blk, x_blk, n_blk = set_off // tm, x_off // tm, n_rows // tm
    const2 = lambda i: (0, 0)
    pk = lambda o: pl.BlockSpec((tm, PACK_W), lambda i: (i + o, 0))
    x_spec = pl.BlockSpec((tm, d), lambda i: (i + x_blk, 0))
    return pl.pallas_call(
        functools.partial(_combine_kernel, final=final),
        grid=(n_blk,),
        in_specs=[pk(s_blk), pk(0), pk(n_blk),
                  pl.BlockSpec((tm, LANES), lambda i: (i + x_blk, 0)),
                  pl.BlockSpec(sg.shape, const2), pl.BlockSpec(su.shape, const2), pl.BlockSpec(sd.shape, const2),
                  x_spec,
                  _mod_spec(5, lambda i: brow((i + x_blk) * tm)),
                  pl.BlockSpec((1, d), const2)],
        out_specs=x_spec,
        out_shape=jax.ShapeDtypeStruct(x2.shape, F32),
        input_output_aliases={7: 0},
        compiler_params=_params(1),
        name="moe_combine",
    )(h2p, yg, yg, w12, sg, su, sd, x2, mod3, gfin.reshape(1, d))


def _moe_routed(parts, wg, wu, wd, layer, sg, su, sd, mod3, gfin, final):
    h2p = parts[0][0] if len(parts) == 1 else jnp.concatenate([pt[0] for pt in parts], axis=0)
    route = parts[0][1] if len(parts) == 1 else jnp.concatenate([pt[1] for pt in parts], axis=1)
    t_all = h2p.shape[0]
    pair_slot, tile_expert, tile_rows = _dispatch_plan(route)
    n_slots = tile_expert.shape[0] * MOE_TILE
    xs = _sc_scatter_rows(h2p, pair_slot, n_slots)
    ys = _expert_ffn(xs, tile_expert, tile_rows, wg, wu, wd, layer)
    outs, set_off = [], 0
    for _, _, w12, x2, brow in parts:
        t = x2.shape[0]
        n_rows = t // COMBINE_CHUNKS if t % (COMBINE_CHUNKS * 512) == 0 else t
        for x_off in range(0, t, n_rows):
            lo = set_off + x_off
            idx = jnp.concatenate([pair_slot[lo:lo + n_rows], pair_slot[t_all + lo:t_all + lo + n_rows]])
            yg = _sc_gather_rows(ys, idx)
            x2 = _moe_combine(h2p, yg, w12, sg, su, sd, x2, lo, x_off, n_rows, mod3, brow, gfin, final)
        outs.append(x2)
        set_off += t
    return outs


def _rope_tables(seq):
    rows = seq // GRID_W
    row = np.repeat(np.arange(rows), GRID_W).astype(np.float32)
    col = np.tile(np.arange(GRID_W), rows).astype(np.float32)
    axis_dim = QK_ROPE // 2
    inv = (ROPE_THETA ** (-np.arange(0, axis_dim, 2, dtype=np.float32) / axis_dim)).astype(np.float32)
    ang = np.concatenate([row[:, None] * inv, col[:, None] * inv], axis=-1).astype(np.float32)
    cos, sin = np.cos(ang), np.sin(ang)
    zero = np.zeros_like(cos)
    return (jnp.asarray(np.concatenate([cos, cos, zero, zero], axis=1)),
            jnp.asarray(np.concatenate([-sin, sin, zero, zero], axis=1)))


def _rope_group(w, start):
    half = QK_ROPE // 2
    x1, x2 = w[..., start:start + half], w[..., start + half:start + 2 * half]
    return jnp.concatenate([x1, x2, x2, x1], axis=-1)


def _w_in_relayout_kernel(w_ref, o_ref):
    half = QK_ROPE // 2
    x1, x2 = KV_RANK, KV_RANK + half
    pieces = [(0, KV_RANK), (KV_COLS, Q_RANK), (x1, half), (x2, half), (x2, half), (x1, half),
              (Q_END, o_ref.shape[0] - LAT_W)]
    dst = 0
    for src, n in pieces:
        o_ref[dst:dst + n, :] = w_ref[src:src + n, :].astype(BF16)
        dst += n


def _relayout_w_in(w):
    depth, d, n_in = w.shape
    tc = 256
    return pl.pallas_call(
        _w_in_relayout_kernel,
        grid=(depth, d // tc),
        in_specs=[pl.BlockSpec((None, n_in, tc), lambda l, i: (l, 0, i))],
        out_specs=pl.BlockSpec((None, P_COLS, tc), lambda l, i: (l, 0, i)),
        out_shape=jax.ShapeDtypeStruct((depth, P_COLS, d), BF16),
        compiler_params=_params(2),
        name="w_in_relayout",
    )(jnp.swapaxes(w, 1, 2))


def _relayout_w_qb(w):
    dq = QK_NOPE + QK_ROPE
    cols = []
    for h in range(N_HEADS):
        cols += [w[:, h * dq:h * dq + QK_NOPE], _rope_group(w, h * dq + QK_NOPE)]
    return jnp.concatenate(cols, axis=1).astype(BF16)


def kernel(x, c, ctx, c_ctx, w_ada, b_ada, norm_mix_g, norm_ffn_g, w_in, q_norm_g, kv_norm_g, w_qb, w_kvb, w_pool, pool_scale, w_out, w_router, router_bias, w_exp_gate, w_exp_up, w_exp_down, w_sh_gate, w_sh_up, w_sh_down, final_norm_g):
    bsz, seq, d = x.shape
    n_ctx = ctx.shape[1]
    depth = w_ada.shape[0]
    assert bsz < MOD_ROWS and d == D_MODEL

    c_rows = jnp.concatenate([c, c_ctx[None], jnp.zeros((MOD_ROWS - bsz - 1, d), F32)], axis=0)
    mod = _ada_mod(c_rows, w_ada, b_ada)

    cos_t, sin_t = _rope_tables(seq)
    ones_t = jnp.concatenate([jnp.ones((n_ctx, 2 * (QK_ROPE // 2)), F32),
                              jnp.zeros((n_ctx, LANES - QK_ROPE), F32)], axis=1)
    zeros_t = jnp.zeros((n_ctx, LANES), F32)
    wr_t = w_router.T.astype(BF16)
    w_in_r = _relayout_w_in(w_in)
    lat_row = lambda r: r // seq
    ctx_row = lambda r: bsz

    x2 = x.reshape(bsz * seq, d)
    xc2 = ctx.reshape(bsz * n_ctx, d)
    for l in range(depth):
        last = l == depth - 1
        mod3 = mod[l].reshape(MOD_ROWS * N_MOD, 1, d)
        wq = _relayout_w_qb(w_qb[l])
        w_kv3 = w_kvb[l].reshape(KV_RANK, N_HEADS, QK_NOPE + V_DIM)
        wk = w_kv3[:, :, :QK_NOPE].reshape(KV_RANK, N_HEADS * QK_NOPE).astype(BF16)
        wvt = (w_kv3[:, :, QK_NOPE:].transpose(1, 2, 0).reshape(N_HEADS * V_DIM, KV_RANK)
               .astype(BF16))
        wp = w_pool[l].astype(BF16)
        wo = w_out[l].astype(BF16)
        sg, su, sd = w_sh_gate[l].astype(BF16), w_sh_up[l].astype(BF16), w_sh_down[l].astype(BF16)
        moe = functools.partial(_moe_routed, wg=w_exp_gate, wu=w_exp_up, wd=w_exp_down, layer=l,
                                sg=sg, su=su, sd=sd, mod3=mod3, gfin=final_norm_g)

        p = _in_proj(x2, mod3, lat_row, norm_mix_g[l], w_in_r, l, P_COLS)
        pc = _in_proj(xc2, mod3, ctx_row, norm_mix_g[l], w_in_r, l, LAT_W if last else P_COLS)
        q, k_lat, v_lat = _qkv(p, bsz, seq, cos_t, sin_t, q_norm_g[l], kv_norm_g[l], wq, wk, wvt, True)
        ctx_out = _qkv(pc, bsz, n_ctx, ones_t, zeros_t, q_norm_g[l], kv_norm_g[l], wq, wk, wvt, not last)
        k_ctx, v_ctx = ctx_out[-2:]
        y = _attention(q, [k_ctx, k_lat], [v_ctx, v_lat])
        x2, h2p, route, w12 = _mixer_out(y, p, x2, mod3, lat_row, seq, wp, pool_scale[l], wo,
                                         norm_ffn_g[l], wr_t, router_bias)
        if last:
            x2, = moe([(h2p, route, w12, x2, lat_row)], final=True)
        else:
            yc = _attention(ctx_out[0], [k_ctx], [v_ctx])
            xc2, h2c, route_c, w12c = _mixer_out(yc, pc, xc2, mod3, ctx_row, n_ctx, wp, pool_scale[l], wo,
                                                 norm_ffn_g[l], wr_t, router_bias)
            x2, xc2 = moe([(h2p, route, w12, x2, lat_row), (h2c, route_c, w12c, xc2, ctx_row)], final=False)
    return x2.reshape(bsz, seq, d)
```

```python
import functools
import math

import jax
import jax.numpy as jnp
import numpy as np
from jax import lax
from jax.experimental import pallas as pl
from jax.experimental.pallas import tpu as pltpu
from jax.experimental.pallas import tpu_sc as plsc

F32 = jnp.float32
BF16 = jnp.bfloat16

D_MODEL = 2048
GRID_W = 64
N_HEADS = 16
QK_NOPE = 128
QK_ROPE = 64
V_DIM = 128
Q_RANK = 384
KV_RANK = 512
ROPE_THETA = 10000.0
POOL_DIM = 1024
POOL_GROUPS = 4
POOL_GROUP_DIM = 256
POOL_WINDOWS = (2, 4, 8, 16)
POOL_GROUP_OUT = 512
KV_COLS = KV_RANK + QK_ROPE
Q_END = KV_COLS + Q_RANK
POOL_END = Q_END + POOL_DIM
N_EXPERTS = 16
N_GROUPS = 4
EXPERTS_PER_GROUP = 4
D_EXPERT = 512
EPS = 1e-6
N_MOD = 6

LANES = 128
HEAD_W = 2 * LANES
LAT_W = 1024
P_COLS = LAT_W + POOL_DIM + 2 * D_MODEL
POOL_HALO = 16
MOD_ROWS = 8
Q_SCALE = (1.0 / math.sqrt(QK_NOPE + QK_ROPE)) * math.log2(math.e)
VMEM_LIMIT = 56 * 1024 * 1024
SUBLANES = 8
PACK_W = D_MODEL // 2
ROW_CHUNKS = PACK_W // LANES
ROUTE_ROWS = SUBLANES
MOE_TILE = 512
SC_WINDOW = 128
ATTN_TQ = 256
ATTN_KB = 512
V_HEAD_GROUP = 8
ATTN_HEADS_PER_STEP = 2
ATTN_PAIRS_PER_ITER = 3
ROW_TILE = 512
IN_PROJ_COL_CHUNK = 512
MIX_CHAIN_ROWS = 256
COMBINE_CHUNKS = 2


def _sigmoid(x):
    return 1.0 / (1.0 + jnp.exp(-x))


def _pack_rows(y):
    half = y.shape[1] // 2
    return pltpu.pack_elementwise([y[:, :half], y[:, half:]], packed_dtype=BF16)


def _load_unpack(ref, live=None):
    w = ref[...]
    if live is not None:
        w = jnp.where(live, w, 0)
    lo = pltpu.unpack_elementwise(w, index=0, packed_dtype=BF16, unpacked_dtype=F32)
    hi = pltpu.unpack_elementwise(w, index=1, packed_dtype=BF16, unpacked_dtype=F32)
    return jnp.concatenate([lo, hi], axis=1)


def _rms(xf, g):
    ms = jnp.mean(xf * xf, axis=-1, keepdims=True)
    return xf * lax.rsqrt(ms + EPS) * g


def _params(n_axes):
    return pltpu.CompilerParams(dimension_semantics=("arbitrary",) * n_axes,
                                vmem_limit_bytes=VMEM_LIMIT)


def _ada_kernel(c_ref, w_ref, b_ref, o_ref):
    cf = c_ref[...]
    a = (cf * _sigmoid(cf)).astype(BF16)
    o_ref[0] = jnp.dot(a, w_ref[0].astype(BF16), preferred_element_type=F32) + b_ref[0]


def _ada_mod(c_rows, w_ada, b_ada):
    depth, d, n = w_ada.shape
    tn = 1024
    return pl.pallas_call(
        _ada_kernel,
        grid=(depth, n // tn),
        in_specs=[pl.BlockSpec((MOD_ROWS, d), lambda l, j: (0, 0)),
                  pl.BlockSpec((1, d, tn), lambda l, j: (l, 0, j)),
                  pl.BlockSpec((1, 1, tn), lambda l, j: (l, 0, j))],
        out_specs=pl.BlockSpec((1, MOD_ROWS, tn), lambda l, j: (l, 0, j)),
        out_shape=jax.ShapeDtypeStruct((depth, MOD_ROWS, n), F32),
        compiler_params=_params(2),
        name="ada_mod",
    )(c_rows, w_ada, b_ada.reshape(depth, 1, n))


def _mod_spec(k, brow):
    return pl.BlockSpec((1, 1, D_MODEL), lambda i, *_: (brow(i) * N_MOD + k, 0, 0))


def _inproj_kernel(x_ref, g_ref, sh_ref, sc_ref, w_ref, o_ref, h_scr, *, tn):
    y = _rms(x_ref[...], g_ref[...])
    h_scr[...] = (y * (1.0 + sc_ref[0]) + sh_ref[0]).astype(BF16)
    for c0 in range(0, o_ref.shape[1], tn):
        acc = lax.dot_general(h_scr[...], w_ref[c0:c0 + tn, :], (((1,), (1,)), ((), ())),
                              preferred_element_type=F32)
        if c0 >= LAT_W + POOL_DIM:
            acc = _sigmoid(acc)
        o_ref[:, c0:c0 + tn] = acc.astype(BF16)


def _in_proj(x2, mod3, brow, g, w, layer, n_cols):
    t, d = x2.shape
    tm = min(ROW_TILE, t)
    return pl.pallas_call(
        functools.partial(_inproj_kernel, tn=IN_PROJ_COL_CHUNK),
        grid=(t // tm,),
        in_specs=[pl.BlockSpec((tm, d), lambda i: (i, 0)),
                  pl.BlockSpec((1, d), lambda i: (0, 0)),
                  _mod_spec(0, lambda i: brow(i * tm)),
                  _mod_spec(1, lambda i: brow(i * tm)),
                  pl.BlockSpec((None, n_cols, d), lambda i: (layer, 0, 0), pipeline_mode=pl.Buffered(1))],
        out_specs=pl.BlockSpec((tm, n_cols), lambda i: (i, 0)),
        out_shape=jax.ShapeDtypeStruct((t, n_cols), BF16),
        scratch_shapes=[pltpu.VMEM((tm, d), BF16)],
        compiler_params=_params(1),
        name="in_proj",
    )(x2, g.reshape(1, d), mod3, mod3, w)


def _qkv_kernel(lat_ref, cos_ref, sin_ref, gq_ref, gkv_ref, wq_ref, wk_ref, wvt_ref, *out_refs, with_q):
    if with_q:
        q_ref, k_ref, v_ref = out_refs
    else:
        k_ref, v_ref = out_refs
    cos = cos_ref[...]
    sin = sin_ref[...]

    def rope(grp):
        return grp * cos + pltpu.roll(grp, 2 * (QK_ROPE // 2), axis=1) * sin

    kvn = _rms(lat_ref[:, :KV_RANK].astype(F32), gkv_ref[...]).astype(BF16)
    kpe = rope(lat_ref[:, KV_RANK + Q_RANK:].astype(F32)).astype(BF16)
    for h in range(0, N_HEADS, 2):
        kn = jnp.dot(kvn, wk_ref[:, h * QK_NOPE:(h + 2) * QK_NOPE], preferred_element_type=F32)
        for hh in range(2):
            k_ref[0, h + hh, :, :QK_NOPE] = kn[:, hh * QK_NOPE:(hh + 1) * QK_NOPE].astype(BF16)
            k_ref[0, h + hh, :, QK_NOPE:] = kpe
    for h in range(0, N_HEADS, V_HEAD_GROUP):
        vt = lax.dot_general(wvt_ref[h * V_DIM:(h + V_HEAD_GROUP) * V_DIM, :], kvn,
                             (((1,), (1,)), ((), ())), preferred_element_type=F32)
        for hh in range(V_HEAD_GROUP):
            v_ref[0, h + hh] = vt[hh * V_DIM:(hh + 1) * V_DIM, :].astype(BF16)
    if with_q:
        qn = _rms(lat_ref[:, KV_RANK:KV_RANK + Q_RANK].astype(F32), gq_ref[...]).astype(BF16)
        for h in range(N_HEADS):
            qh = jnp.dot(qn, wq_ref[:, h * HEAD_W:(h + 1) * HEAD_W], preferred_element_type=F32)
            q_ref[0, h, :, :QK_NOPE] = (qh[:, :QK_NOPE] * Q_SCALE).astype(BF16)
            q_ref[0, h, :, QK_NOPE:] = (rope(qh[:, QK_NOPE:]) * Q_SCALE).astype(BF16)


def _qkv(p, b, n, cos_t, sin_t, gq, gkv, wq, wk, wvt, with_q):
    tm = min(ROW_TILE, n)
    tpb = n // tm
    rope_tiles = cos_t.shape[0] // tm
    head_spec = pl.BlockSpec((1, N_HEADS, tm, HEAD_W), lambda i: (i // tpb, 0, i % tpb, 0))
    tab_spec = pl.BlockSpec((tm, LANES), lambda i: ((i % tpb) % rope_tiles, 0))
    out_shape = [jax.ShapeDtypeStruct((b, N_HEADS, n, HEAD_W), BF16),
                 jax.ShapeDtypeStruct((b, N_HEADS, V_DIM, n), BF16)]
    out_specs = [head_spec,
                 pl.BlockSpec((1, N_HEADS, V_DIM, tm), lambda i: (i // tpb, 0, 0, i % tpb))]
    if with_q:
        out_shape = [jax.ShapeDtypeStruct((b, N_HEADS, n, HEAD_W), BF16)] + out_shape
        out_specs = [head_spec] + out_specs
    return pl.pallas_call(
        functools.partial(_qkv_kernel, with_q=with_q),
        grid=(b * tpb,),
        in_specs=[pl.BlockSpec((tm, LAT_W), lambda i: (i, 0)),
                  tab_spec, tab_spec,
                  pl.BlockSpec((1, Q_RANK), lambda i: (0, 0)),
                  pl.BlockSpec((1, KV_RANK), lambda i: (0, 0)),
                  pl.BlockSpec(wq.shape, lambda i: (0, 0)),
                  pl.BlockSpec(wk.shape, lambda i: (0, 0)),
                  pl.BlockSpec(wvt.shape, lambda i: (0, 0))],
        out_specs=out_specs,
        out_shape=out_shape,
        compiler_params=_params(1),
        name="qkv",
    )(p, cos_t, sin_t, gq.reshape(1, Q_RANK), gkv.reshape(1, KV_RANK), wq, wk, wvt)


def _row_fold(x, op):
    parts = [x[r:r + SUBLANES] for r in range(0, x.shape[0], SUBLANES)]
    a, b = parts[0], parts[1]
    for i in range(2, len(parts) - 1, 2):
        a, b = op(a, parts[i]), op(b, parts[i + 1])
    if len(parts) % 2:
        a = op(a, parts[-1])
    return op(a, b)


def _attn_kernel(q_ref, *refs, n_seg, kb, tq):
    k_refs, vt_refs = refs[:n_seg], refs[n_seg:2 * n_seg]
    o_ref, s_a, s_b, m_a, m_b = refs[2 * n_seg:]
    chunks = []
    off = 0
    for kr, vr in zip(k_refs, vt_refs):
        nk = kr.shape[2]
        for c0 in range(0, nk, kb):
            n = min(kb, nk - c0)
            chunks.append((kr, vr, c0, n, off))
            off += n

    n_heads, nt = q_ref.shape[1], q_ref.shape[2] // tq
    bufs = ((s_a, m_a), (s_b, m_b))

    def tile_rows(t):
        return pl.ds(pl.multiple_of(t * tq, tq), tq)

    def scores(g, t, s_buf, m_buf):
        rows = tile_rows(t)
        q = q_ref[0, g, rows, :]
        mp = None
        for kr, vr, c0, n, o in chunks:
            s = lax.dot_general(kr[0, g, c0:c0 + n, :], q, (((1,), (1,)), ((), ())),
                                preferred_element_type=F32)
            s_buf[o:o + n, :] = s
            m = _row_fold(s, jnp.maximum)
            mp = m if mp is None else jnp.maximum(mp, m)
        m_buf[...] = mp

    def values(g, t, s_buf, m_buf):
        rows = tile_rows(t)
        mrow = jnp.max(m_buf[...], axis=0, keepdims=True)
        lp = None
        acc = None
        for kr, vr, c0, n, o in chunks:
            pr = jnp.exp2(s_buf[o:o + n, :] - mrow)
            ls = _row_fold(pr, jnp.add)
            lp = ls if lp is None else lp + ls
            pv = jnp.dot(vr[0, g, :, c0:c0 + n], pr.astype(BF16), preferred_element_type=F32)
            acc = pv if acc is None else acc + pv
        l = jnp.sum(lp, axis=0, keepdims=True)
        o_ref[0, g, rows, :] = (acc / l).T.astype(BF16)

    def phase(g, t, parity):
        scores(g, t + 1, *bufs[1 - parity])
        values(g, t, *bufs[parity])

    scores(0, 0, *bufs[0])
    for g in range(n_heads):
        par0 = (g * nt) % 2
        n_pairs = (nt - 1) // 2

        def pair(j, g=g, par0=par0):
            phase(g, 2 * j, par0)
            phase(g, 2 * j + 1, 1 - par0)

        def pairs(jj, carry, pair=pair):
            for k in range(ATTN_PAIRS_PER_ITER):
                pair(jj * ATTN_PAIRS_PER_ITER + k)
            return carry

        n_iter = n_pairs // ATTN_PAIRS_PER_ITER
        if n_iter:
            lax.fori_loop(0, n_iter, pairs, 0)
        for j in range(n_iter * ATTN_PAIRS_PER_ITER, n_pairs):
            pair(j)
        if (nt - 1) % 2:
            phase(g, nt - 2, (par0 + nt - 2) % 2)
        last = bufs[(par0 + nt - 1) % 2]
        if g + 1 < n_heads:
            scores(g + 1, 0, *bufs[((g + 1) * nt) % 2])
        values(g, nt - 1, *last)


def _attention(q, ks, vts):
    b, hh, n, _ = q.shape
    tq = min(ATTN_TQ, n)
    hg = ATTN_HEADS_PER_STEP if n > tq else hh
    assert hh % hg == 0
    nk_total = sum(k.shape[2] for k in ks)
    seg_spec = lambda a: pl.BlockSpec((1, hg) + a.shape[2:], lambda bi, h: (bi, h, 0, 0))
    return pl.pallas_call(
        functools.partial(_attn_kernel, n_seg=len(ks), kb=ATTN_KB, tq=tq),
        grid=(b, hh // hg),
        in_specs=[pl.BlockSpec((1, hg, n, HEAD_W), lambda bi, h: (bi, h, 0, 0))]
                 + [seg_spec(a) for a in ks] + [seg_spec(a) for a in vts],
        out_specs=pl.BlockSpec((1, hg, n, V_DIM), lambda bi, h: (bi, h, 0, 0)),
        out_shape=jax.ShapeDtypeStruct((b, hh, n, V_DIM), BF16),
        scratch_shapes=[pltpu.VMEM((nk_total, tq), F32), pltpu.VMEM((nk_total, tq), F32),
                        pltpu.VMEM((SUBLANES, tq), F32), pltpu.VMEM((SUBLANES, tq), F32)],
        compiler_params=_params(2),
        name="attention",
    )(q, *ks, *vts)


def _top2sum(a, b, c, d):
    s1, t1 = jnp.maximum(a, b), jnp.minimum(a, b)
    s2, t2 = jnp.maximum(c, d), jnp.minimum(c, d)
    return jnp.maximum(s1, s2) + jnp.maximum(jnp.minimum(s1, s2), jnp.maximum(t1, t2))


def _route(logits_t, bias):
    sc = _sigmoid(logits_t)
    sel = sc + bias
    sel_r = [sel[e:e + 1, :] for e in range(N_EXPERTS)]
    sc_r = [sc[e:e + 1, :] for e in range(N_EXPERTS)]
    epg = EXPERTS_PER_GROUP
    gs = [_top2sum(*sel_r[g * epg:(g + 1) * epg]) for g in range(N_GROUPS)]
    best, gi = gs[0], jnp.zeros(gs[0].shape, jnp.int32)
    for g in range(1, N_GROUPS):
        upd = gs[g] > best
        best = jnp.where(upd, gs[g], best)
        gi = jnp.where(upd, g, gi)

    def pick_group(rows, k):
        r = rows[k]
        for g in range(1, N_GROUPS):
            r = jnp.where(gi == g, rows[g * epg + k], r)
        return r

    in_sel = [pick_group(sel_r, k) for k in range(epg)]
    in_sc = [pick_group(sc_r, k) for k in range(epg)]
    b1, i1 = in_sel[0], jnp.zeros(gi.shape, jnp.int32)
    for k in range(1, epg):
        upd = in_sel[k] > b1
        b1 = jnp.where(upd, in_sel[k], b1)
        i1 = jnp.where(upd, k, i1)
    b2, i2 = None, None
    for k in range(epg):
        cand = jnp.where(i1 == k, -jnp.inf, in_sel[k])
        if b2 is None:
            b2, i2 = cand, jnp.zeros(gi.shape, jnp.int32)
        else:
            upd = cand > b2
            b2 = jnp.where(upd, cand, b2)
            i2 = jnp.where(upd, k, i2)

    def pick_local(idx):
        r = in_sc[0]
        for k in range(1, epg):
            r = jnp.where(idx == k, in_sc[k], r)
        return r

    s1, s2 = pick_local(i1), pick_local(i2)
    den = s1 + s2
    return gi * epg + i1, gi * epg + i2, s1 / den, s2 / den


def _mixout_kernel(y_ref, u_ref, up_ref, un_ref, gm_ref, gp_ref, wp_ref, ps_ref, wo_ref, x_ref,
                   g1_ref, gf_ref, sh2_ref, sc2_ref, wr_ref, rb_ref,
                   xo_ref, h2_ref, route_ref, w12_ref, *, tm, tpb, n_seq):
    i = pl.program_id(0)
    uext = jnp.concatenate([up_ref[...], u_ref[...], un_ref[...]], axis=0)
    sub = min(MIX_CHAIN_ROWS, tm)

    def mix_stage(r0):
        rows = slice(r0, r0 + sub)
        base = (i % tpb) * tm + r0
        r = lax.broadcasted_iota(jnp.int32, (sub, sub + 2 * POOL_HALO), 0)
        c = lax.broadcasted_iota(jnp.int32, (sub, sub + 2 * POOL_HALO), 1)
        rel = c - POOL_HALO - r
        jpos = base - POOL_HALO + c
        valid = (jpos >= 0) & (jpos < n_seq)
        tpos = base + lax.broadcasted_iota(jnp.int32, (sub, 1), 0)
        uwin = uext[r0:r0 + sub + 2 * POOL_HALO]
        uc = uext[r0 + POOL_HALO:r0 + POOL_HALO + sub]
        parts = []
        for g, w in enumerate(POOL_WINDOWS):
            sl = slice(g * POOL_GROUP_DIM, (g + 1) * POOL_GROUP_DIM)
            band = jnp.where(valid & (rel >= -(w // 2)) & (rel < w // 2), 1.0, 0.0).astype(BF16)
            cnt = jnp.clip(tpos + w // 2, 0, n_seq) - jnp.clip(tpos - w // 2, 0, n_seq)
            wsum = jnp.dot(band, uwin[:, sl], preferred_element_type=F32)
            z = wsum / cnt.astype(F32) - uc[:, sl].astype(F32)
            parts.append(jnp.dot(z.astype(BF16), wp_ref[g], preferred_element_type=F32))
        ypool = jnp.concatenate(parts, axis=1) * ps_ref[...]
        y_mla = jnp.concatenate([y_ref[0, h, rows, :] for h in range(N_HEADS)], axis=1)
        mix = gm_ref[rows, :].astype(F32) * y_mla.astype(F32) + gp_ref[rows, :].astype(F32) * ypool
        return jnp.dot(mix.astype(BF16), wo_ref[...], preferred_element_type=F32)

    def norm_route_stage(r0, y):
        rows = slice(r0, r0 + sub)
        xn = x_ref[rows, :] + g1_ref[0] * y
        xo_ref[rows, :] = xn
        h2f = _rms(xn, gf_ref[...]) * (1.0 + sc2_ref[0]) + sh2_ref[0]
        h2_ref[rows, :] = _pack_rows(h2f)
        logits_t = lax.dot_general(wr_ref[...], h2f.astype(BF16), (((1,), (1,)), ((), ())),
                                   preferred_element_type=F32)
        e1, e2, w1, w2 = _route(logits_t, rb_ref[...])
        zero = jnp.zeros((ROUTE_ROWS - 4, sub), F32)
        route_ref[:, rows] = jnp.concatenate([e1.astype(F32), e2.astype(F32), w1, w2, zero], axis=0)
        w12_ref[rows, :] = jnp.concatenate([w1, w2, jnp.zeros((LANES - 2, sub), F32)], axis=0).T

    starts = list(range(0, tm, sub))
    pending = None
    for r0 in starts:
        y = mix_stage(r0)
        if pending is not None:
            norm_route_stage(*pending)
        pending = (r0, y)
    norm_route_stage(*pending)


def _mixer_out(y, p, x2, mod3, brow, n_seq, wp, ps, wo, gf, wr_t, rb):
    t, d = x2.shape
    tm = min(2 * MIX_CHAIN_ROWS, n_seq)
    tpb = n_seq // tm
    hpt = tm // POOL_HALO
    n_halo = t // POOL_HALO
    row = lambda i: (i, 0)
    const2 = lambda i: (0, 0)
    mrow = lambda i: brow(i * tm)
    return pl.pallas_call(
        functools.partial(_mixout_kernel, tm=tm, tpb=tpb, n_seq=n_seq),
        grid=(t // tm,),
        in_specs=[pl.BlockSpec((1, N_HEADS, tm, V_DIM), lambda i: (i // tpb, 0, i % tpb, 0)),
                  pl.BlockSpec((tm, POOL_DIM), lambda i: (i, LAT_W // POOL_DIM)),
                  pl.BlockSpec((POOL_HALO, POOL_DIM),
                               lambda i: (jnp.maximum(i * hpt - 1, 0), LAT_W // POOL_DIM)),
                  pl.BlockSpec((POOL_HALO, POOL_DIM),
                               lambda i: (jnp.minimum((i + 1) * hpt, n_halo - 1), LAT_W // POOL_DIM)),
                  pl.BlockSpec((tm, d), lambda i: (i, (LAT_W + POOL_DIM) // d)),
                  pl.BlockSpec((tm, d), lambda i: (i, (LAT_W + POOL_DIM) // d + 1)),
                  pl.BlockSpec(wp.shape, lambda i: (0, 0, 0)),
                  pl.BlockSpec((1, d), const2),
                  pl.BlockSpec((d, d), const2, pipeline_mode=pl.Buffered(1)),
                  pl.BlockSpec((tm, d), row),
                  _mod_spec(2, mrow),
                  pl.BlockSpec((1, d), const2),
                  _mod_spec(3, mrow),
                  _mod_spec(4, mrow),
                  pl.BlockSpec((N_EXPERTS, d), const2),
                  pl.BlockSpec((N_EXPERTS, 1), const2)],
        out_specs=[pl.BlockSpec((tm, d), row),
                   pl.BlockSpec((tm, PACK_W), row),
                   pl.BlockSpec((ROUTE_ROWS, tm), lambda i: (0, i)),
                   pl.BlockSpec((tm, LANES), row)],
        out_shape=[jax.ShapeDtypeStruct((t, d), F32),
                   jax.ShapeDtypeStruct((t, PACK_W), jnp.int32),
                   jax.ShapeDtypeStruct((ROUTE_ROWS, t), F32),
                   jax.ShapeDtypeStruct((t, LANES), F32)],
        compiler_params=_params(1),
        name="mixer_out",
    )(y, p, p, p, p, p, wp, ps.reshape(1, d), wo, x2, mod3, gf.reshape(1, d), mod3, mod3, wr_t,
      rb.reshape(N_EXPERTS, 1))


def _chunk_view(x):
    n = x.shape[0]
    return (x.reshape(n // SUBLANES, SUBLANES, ROW_CHUNKS, LANES).transpose(0, 2, 1, 3)
            .reshape(n * ROW_CHUNKS, LANES))


def _row_view(c):
    n = c.shape[0] // ROW_CHUNKS
    return (c.reshape(n // SUBLANES, ROW_CHUNKS, SUBLANES, LANES).transpose(0, 2, 1, 3)
            .reshape(n, PACK_W))


def _sc_index_chunks(idx):
    r = idx.reshape(-1, 1, SUBLANES)
    j = jnp.arange(ROW_CHUNKS, dtype=jnp.int32)[None, :, None]
    ids = (r // SUBLANES) * (SUBLANES * ROW_CHUNKS) + j * SUBLANES + r % SUBLANES
    return ids.reshape(1, idx.shape[0] * ROW_CHUNKS)


def _sc_mesh():
    return plsc.VectorSubcoreMesh(core_axis_name="core", subcore_axis_name="subcore")


def _sc_gather_rows(table, idx):
    n, m = table.shape[0], idx.shape[0]
    n_idx = m * ROW_CHUNKS
    assert n_idx % SC_WINDOW == 0 and n % SUBLANES == 0 and m % SUBLANES == 0

    @pl.kernel(out_type=jax.ShapeDtypeStruct((n_idx, LANES), jnp.int32), mesh=_sc_mesh())
    def gather(t_hbm, i_hbm, o_hbm):
        def body(i_vmem, o_vmem):
            pltpu.sync_copy(t_hbm.at[i_vmem.at[0]], o_vmem)

        pltpu.emit_pipeline(
            body,
            grid=(n_idx // SC_WINDOW,),
            in_specs=[pl.BlockSpec((1, SC_WINDOW), index_map=lambda i: (0, i))],
            out_specs=[pl.BlockSpec((SC_WINDOW, LANES), index_map=lambda i: (i, 0))],
            core_axis_name=("core", "subcore"),
            dimension_semantics=(pltpu.PARALLEL,),
        )(i_hbm, o_hbm)

    return _row_view(gather(_chunk_view(table), _sc_index_chunks(idx)))


def _sc_scatter_rows(rows, idx, n_out):
    n, m = rows.shape[0], idx.shape[0]
    n_idx = m * ROW_CHUNKS
    src_steps = n * ROW_CHUNKS // SC_WINDOW
    assert n_idx % SC_WINDOW == 0 and (n * ROW_CHUNKS) % SC_WINDOW == 0 and m % n == 0
    assert n % SUBLANES == 0 and n_out % SUBLANES == 0

    @pl.kernel(out_type=jax.ShapeDtypeStruct((n_out * ROW_CHUNKS, LANES), jnp.int32), mesh=_sc_mesh())
    def scatter(x_hbm, i_hbm, o_hbm):
        def body(x_vmem, i_vmem):
            pltpu.sync_copy(x_vmem, o_hbm.at[i_vmem.at[0]])

        pltpu.emit_pipeline(
            body,
            grid=(n_idx // SC_WINDOW,),
            in_specs=[pl.BlockSpec((SC_WINDOW, LANES), index_map=lambda i: (i % src_steps, 0)),
                      pl.BlockSpec((1, SC_WINDOW), index_map=lambda i: (0, i))],
            out_specs=[],
            core_axis_name=("core", "subcore"),
            dimension_semantics=(pltpu.PARALLEL,),
        )(x_hbm, i_hbm)

    return _row_view(scatter(_chunk_view(rows), _sc_index_chunks(idx)))


def _dispatch_plan(route):
    t = route.shape[1]
    n_pairs = 2 * t
    n_tiles = n_pairs // MOE_TILE + N_EXPERTS
    blk = LANES
    e = jnp.concatenate([route[0], route[1]]).astype(jnp.int32)
    onehot = (e[:, None] == jnp.arange(N_EXPERTS, dtype=jnp.int32)[None, :]).astype(F32)
    oh3 = onehot.reshape(n_pairs // blk, blk, N_EXPERTS)
    tri = (jnp.arange(blk)[:, None] >= jnp.arange(blk)[None, :]).astype(F32)
    within = jnp.einsum("ij,bjk->bik", tri, oh3)
    totals = within[:, -1, :]
    before = jnp.cumsum(totals, axis=0) - totals
    csum = (within + before[:, None, :]).reshape(n_pairs, N_EXPERTS)
    counts = jnp.sum(totals, axis=0).astype(jnp.int32)
    padded = ((counts + MOE_TILE - 1) // MOE_TILE) * MOE_TILE
    ends = jnp.cumsum(padded)
    starts = ends - padded
    pair_slot = jnp.sum(onehot * (csum - 1.0 + starts.astype(F32)[None, :]), axis=1).astype(jnp.int32)
    tile_start = jnp.arange(n_tiles, dtype=jnp.int32) * MOE_TILE
    tile_e = jnp.minimum(jnp.sum((tile_start[:, None] >= ends[None, :]).astype(jnp.int32), axis=1), N_EXPERTS - 1)
    e_onehot = tile_e[:, None] == jnp.arange(N_EXPERTS, dtype=jnp.int32)[None, :]
    filled = jnp.sum(jnp.where(e_onehot, (starts + counts)[None, :], 0), axis=1)
    tile_rows = jnp.clip(filled - tile_start, 0, MOE_TILE).astype(jnp.int32)
    last_used = jnp.maximum(ends[-1] // MOE_TILE - 1, 0)
    tile_expert = jnp.where(tile_start < ends[-1], tile_e, tile_e[last_used]).astype(jnp.int32)
    return pair_slot, tile_expert, tile_rows


def _swiglu(xb, wg, wu, wd):
    gate = jnp.dot(xb, wg, preferred_element_type=F32)
    up = jnp.dot(xb, wu, preferred_element_type=F32)
    a = ((gate * _sigmoid(gate)) * up).astype(BF16)
    return jnp.dot(a, wd, preferred_element_type=F32)


def _ffn_kernel(te_ref, rows_ref, xs_ref, wg_ref, wu_ref, wd_ref, ys_ref, wg_s, wu_s, wd_s):
    i = pl.program_id(0)
    new_expert = jnp.logical_or(i == 0, te_ref[i] != te_ref[jnp.maximum(i - 1, 0)])

    @pl.when(new_expert)
    def _():
        wg_s[...] = wg_ref[0].astype(BF16)
        wu_s[...] = wu_ref[0].astype(BF16)
        wd_s[...] = wd_ref[0].astype(BF16)

    @pl.when(rows_ref[i] > 0)
    def _():
        live = lax.broadcasted_iota(jnp.int32, (MOE_TILE, 1), 0) < rows_ref[i]
        xb = _load_unpack(xs_ref, live).astype(BF16)
        ys_ref[...] = _pack_rows(_swiglu(xb, wg_s[...], wu_s[...], wd_s[...]))

    @pl.when(rows_ref[i] == 0)
    def _():
        ys_ref[...] = jnp.zeros_like(ys_ref)


def _expert_ffn(xs, tile_expert, tile_rows, wg, wu, wd, layer):
    n_slots = xs.shape[0]
    d = D_MODEL
    blk = pl.BlockSpec((MOE_TILE, PACK_W), lambda i, te, nr: (i, 0))
    return pl.pallas_call(
        _ffn_kernel,
        grid_spec=pltpu.PrefetchScalarGridSpec(
            num_scalar_prefetch=2,
            grid=(n_slots // MOE_TILE,),
            in_specs=[blk,
                      pl.BlockSpec((None, 1, d, D_EXPERT), lambda i, te, nr: (layer, te[i], 0, 0)),
                      pl.BlockSpec((None, 1, d, D_EXPERT), lambda i, te, nr: (layer, te[i], 0, 0)),
                      pl.BlockSpec((None, 1, D_EXPERT, d), lambda i, te, nr: (layer, te[i], 0, 0))],
            out_specs=blk,
            scratch_shapes=[pltpu.VMEM((d, D_EXPERT), BF16), pltpu.VMEM((d, D_EXPERT), BF16),
                            pltpu.VMEM((D_EXPERT, d), BF16)]),
        out_shape=jax.ShapeDtypeStruct(xs.shape, jnp.int32),
        compiler_params=_params(1),
        name="expert_ffn",
    )(tile_expert, tile_rows, xs, wg, wu, wd)


def _combine_kernel(h_ref, y1_ref, y2_ref, w_ref, sg_ref, su_ref, sd_ref, x_ref, g2_ref, gfin_ref, o_ref,
                    *, final):
    y = _swiglu(_load_unpack(h_ref).astype(BF16), sg_ref[...], su_ref[...], sd_ref[...])
    w = w_ref[...]
    y = y + w[:, 0:1] * _load_unpack(y1_ref) + w[:, 1:2] * _load_unpack(y2_ref)
    xn = x_ref[...] + g2_ref[0] * y
    if final:
        xn = _rms(xn, gfin_ref[...])
    o_ref[...] = xn


def _moe_combine(h2p, yg, w12, sg, su, sd, x2, set_off, x_off, n_rows, mod3, brow, gfin, final):
    d = x2.shape[1]
    tm = min(ROW_TILE, n_rows)
    assert set_off % tm == 0 and x_off % tm == 0 and n_rows % tm == 0
    s_blk, x_blk, n_blk = set_off // tm, x_off // tm, n_rows // tm
    const2 = lambda i: (0, 0)
    pk = lambda o: pl.BlockSpec((tm, PACK_W), lambda i: (i + o, 0))
    x_spec = pl.BlockSpec((tm, d), lambda i: (i + x_blk, 0))
    return pl.pallas_call(
        functools.partial(_combine_kernel, final=final),
        grid=(n_blk,),
        in_specs=[pk(s_blk), pk(0), pk(n_blk),
                  pl.BlockSpec((tm, LANES), lambda i: (i + x_blk, 0)),
                  pl.BlockSpec(sg.shape, const2), pl.BlockSpec(su.shape, const2), pl.BlockSpec(sd.shape, const2),
                  x_spec,
                  _mod_spec(5, lambda i: brow((i + x_blk) * tm)),
                  pl.BlockSpec((1, d), const2)],
        out_specs=x_spec,
        out_shape=jax.ShapeDtypeStruct(x2.shape, F32),
        input_output_aliases={7: 0},
        compiler_params=_params(1),
        name="moe_combine",
    )(h2p, yg, yg, w12, sg, su, sd, x2, mod3, gfin.reshape(1, d))


def _moe_routed(parts, wg, wu, wd, layer, sg, su, sd, mod3, gfin, final):
    h2p = parts[0][0] if len(parts) == 1 else jnp.concatenate([pt[0] for pt in parts], axis=0)
    route = parts[0][1] if len(parts) == 1 else jnp.concatenate([pt[1] for pt in parts], axis=1)
    t_all = h2p.shape[0]
    pair_slot, tile_expert, tile_rows = _dispatch_plan(route)
    n_slots = tile_expert.shape[0] * MOE_TILE
    xs = _sc_scatter_rows(h2p, pair_slot, n_slots)
    ys = _expert_ffn(xs, tile_expert, tile_rows, wg, wu, wd, layer)
    outs, set_off = [], 0
    for _, _, w12, x2, brow in parts:
        t = x2.shape[0]
        n_rows = t // COMBINE_CHUNKS if t % (COMBINE_CHUNKS * ROW_TILE) == 0 else t
        for x_off in range(0, t, n_rows):
            lo = set_off + x_off
            idx = jnp.concatenate([pair_slot[lo:lo + n_rows], pair_slot[t_all + lo:t_all + lo + n_rows]])
            yg = _sc_gather_rows(ys, idx)
            x2 = _moe_combine(h2p, yg, w12, sg, su, sd, x2, lo, x_off, n_rows, mod3, brow, gfin, final)
        outs.append(x2)
        set_off += t
    return outs


def _rope_tables(seq):
    rows = seq // GRID_W
    row = np.repeat(np.arange(rows), GRID_W).astype(np.float32)
    col = np.tile(np.arange(GRID_W), rows).astype(np.float32)
    axis_dim = QK_ROPE // 2
    inv = (ROPE_THETA ** (-np.arange(0, axis_dim, 2, dtype=np.float32) / axis_dim)).astype(np.float32)
    ang = np.concatenate([row[:, None] * inv, col[:, None] * inv], axis=-1).astype(np.float32)
    cos, sin = np.cos(ang), np.sin(ang)
    zero = np.zeros_like(cos)
    return (jnp.asarray(np.concatenate([cos, cos, zero, zero], axis=1)),
            jnp.asarray(np.concatenate([-sin, sin, zero, zero], axis=1)))


def _rope_group(w, start):
    half = QK_ROPE // 2
    x1, x2 = w[..., start:start + half], w[..., start + half:start + 2 * half]
    return jnp.concatenate([x1, x2, x2, x1], axis=-1)


def _w_in_relayout_kernel(w_ref, o_ref):
    half = QK_ROPE // 2
    x1, x2 = KV_RANK, KV_RANK + half
    pieces = [(0, KV_RANK), (KV_COLS, Q_RANK), (x1, half), (x2, half), (x2, half), (x1, half),
              (Q_END, o_ref.shape[0] - LAT_W)]
    dst = 0
    for src, n in pieces:
        o_ref[dst:dst + n, :] = w_ref[src:src + n, :].astype(BF16)
        dst += n


def _relayout_w_in(w):
    depth, d, n_in = w.shape
    tc = 256
    return pl.pallas_call(
        _w_in_relayout_kernel,
        grid=(depth, d // tc),
        in_specs=[pl.BlockSpec((None, n_in, tc), lambda l, i: (l, 0, i))],
        out_specs=pl.BlockSpec((None, P_COLS, tc), lambda l, i: (l, 0, i)),
        out_shape=jax.ShapeDtypeStruct((depth, P_COLS, d), BF16),
        compiler_params=_params(2),
        name="w_in_relayout",
    )(jnp.swapaxes(w, 1, 2))


def _relayout_w_qb(w):
    dq = QK_NOPE + QK_ROPE
    cols = []
    for h in range(N_HEADS):
        cols += [w[:, h * dq:h * dq + QK_NOPE], _rope_group(w, h * dq + QK_NOPE)]
    return jnp.concatenate(cols, axis=1).astype(BF16)


def kernel(x, c, ctx, c_ctx, w_ada, b_ada, norm_mix_g, norm_ffn_g, w_in, q_norm_g, kv_norm_g, w_qb, w_kvb, w_pool, pool_scale, w_out, w_router, router_bias, w_exp_gate, w_exp_up, w_exp_down, w_sh_gate, w_sh_up, w_sh_down, final_norm_g):
    bsz, seq, d = x.shape
    n_ctx = ctx.shape[1]
    depth = w_ada.shape[0]
    assert bsz < MOD_ROWS and d == D_MODEL

    c_rows = jnp.concatenate([c, c_ctx[None], jnp.zeros((MOD_ROWS - bsz - 1, d), F32)], axis=0)
    mod = _ada_mod(c_rows, w_ada, b_ada)

    cos_t, sin_t = _rope_tables(seq)
    ones_t = jnp.concatenate([jnp.ones((n_ctx, 2 * (QK_ROPE // 2)), F32),
                              jnp.zeros((n_ctx, LANES - QK_ROPE), F32)], axis=1)
    zeros_t = jnp.zeros((n_ctx, LANES), F32)
    wr_t = w_router.T.astype(BF16)
    w_in_r = _relayout_w_in(w_in)
    lat_row = lambda r: r // seq
    ctx_row = lambda r: bsz

    x2 = x.reshape(bsz * seq, d)
    xc2 = ctx.reshape(bsz * n_ctx, d)
    for l in range(depth):
        last = l == depth - 1
        mod3 = mod[l].reshape(MOD_ROWS * N_MOD, 1, d)
        wq = _relayout_w_qb(w_qb[l])
        w_kv3 = w_kvb[l].reshape(KV_RANK, N_HEADS, QK_NOPE + V_DIM)
        wk = w_kv3[:, :, :QK_NOPE].reshape(KV_RANK, N_HEADS * QK_NOPE).astype(BF16)
        wvt = (w_kv3[:, :, QK_NOPE:].transpose(1, 2, 0).reshape(N_HEADS * V_DIM, KV_RANK)
               .astype(BF16))
        wp = w_pool[l].astype(BF16)
        wo = w_out[l].astype(BF16)
        sg, su, sd = w_sh_gate[l].astype(BF16), w_sh_up[l].astype(BF16), w_sh_down[l].astype(BF16)
        moe = functools.partial(_moe_routed, wg=w_exp_gate, wu=w_exp_up, wd=w_exp_down, layer=l,
                                sg=sg, su=su, sd=sd, mod3=mod3, gfin=final_norm_g)

        p = _in_proj(x2, mod3, lat_row, norm_mix_g[l], w_in_r, l, P_COLS)
        pc = _in_proj(xc2, mod3, ctx_row, norm_mix_g[l], w_in_r, l, LAT_W if last else P_COLS)
        q, k_lat, v_lat = _qkv(p, bsz, seq, cos_t, sin_t, q_norm_g[l], kv_norm_g[l], wq, wk, wvt, True)
        ctx_out = _qkv(pc, bsz, n_ctx, ones_t, zeros_t, q_norm_g[l], kv_norm_g[l], wq, wk, wvt, not last)
        k_ctx, v_ctx = ctx_out[-2:]
        y = _attention(q, [k_ctx, k_lat], [v_ctx, v_lat])
        x2, h2p, route, w12 = _mixer_out(y, p, x2, mod3, lat_row, seq, wp, pool_scale[l], wo,
                                         norm_ffn_g[l], wr_t, router_bias)
        if last:
            x2, = moe([(h2p, route, w12, x2, lat_row)], final=True)
        else:
            yc = _attention(ctx_out[0], [k_ctx], [v_ctx])
            xc2, h2c, route_c, w12c = _mixer_out(yc, pc, xc2, mod3, ctx_row, n_ctx, wp, pool_scale[l], wo,
                                                 norm_ffn_g[l], wr_t, router_bias)
            x2, xc2 = moe([(h2p, route, w12, x2, lat_row), (h2c, route_c, w12c, xc2, ctx_row)], final=False)
    return x2.reshape(bsz, seq, d)
```

```python
import functools
import math

import jax
import jax.numpy as jnp
import numpy as np
from jax import lax
from jax.experimental import pallas as pl
from jax.experimental.pallas import tpu as pltpu
from jax.experimental.pallas import tpu_sc as plsc

F32 = jnp.float32
BF16 = jnp.bfloat16

D_MODEL = 2048
GRID_W = 64
N_HEADS = 16
QK_NOPE = 128
QK_ROPE = 64
V_DIM = 128
Q_RANK = 384
KV_RANK = 512
ROPE_THETA = 10000.0
POOL_DIM = 1024
POOL_GROUP_DIM = 256
POOL_WINDOWS = (2, 4, 8, 16)
KV_COLS = KV_RANK + QK_ROPE
Q_END = KV_COLS + Q_RANK
N_EXPERTS = 16
N_GROUPS = 4
EXPERTS_PER_GROUP = 4
D_EXPERT = 512
EPS = 1e-6
N_MOD = 6

LANES = 128
HEAD_W = 2 * LANES
LAT_W = KV_RANK + Q_RANK + LANES
P_COLS = LAT_W + POOL_DIM + 2 * D_MODEL
POOL_HALO = 16
MOD_ROWS = 8
Q_SCALE = (1.0 / math.sqrt(QK_NOPE + QK_ROPE)) * math.log2(math.e)
V7X_VMEM_BYTES = 64 * 1024 * 1024
VMEM_LIMIT = V7X_VMEM_BYTES - 8 * 1024 * 1024
SUBLANES = 8
PACK_W = D_MODEL // 2
ROW_CHUNKS = PACK_W // LANES
ROUTE_ROWS = SUBLANES
MOE_TILE = 512
SC_WINDOW = 128
ATTN_TQ = 256
ATTN_KB = 512
V_HEAD_GROUP = 8
ATTN_HEADS_PER_STEP = 2
ATTN_PAIRS_PER_ITER = 3
ROW_TILE = 512
IN_PROJ_COL_CHUNK = 512
MIX_CHAIN_ROWS = 256
COMBINE_CHUNKS = 2


def _sigmoid(x):
    return 1.0 / (1.0 + jnp.exp(-x))


def _pack_rows(y):
    half = y.shape[1] // 2
    return pltpu.pack_elementwise([y[:, :half], y[:, half:]], packed_dtype=BF16)


def _load_unpack(ref, live=None):
    w = ref[...]
    if live is not None:
        w = jnp.where(live, w, 0)
    lo = pltpu.unpack_elementwise(w, index=0, packed_dtype=BF16, unpacked_dtype=F32)
    hi = pltpu.unpack_elementwise(w, index=1, packed_dtype=BF16, unpacked_dtype=F32)
    return jnp.concatenate([lo, hi], axis=1)


def _rms(xf, g):
    ms = jnp.mean(xf * xf, axis=-1, keepdims=True)
    return xf * lax.rsqrt(ms + EPS) * g


def _params(n_axes):
    return pltpu.CompilerParams(dimension_semantics=("arbitrary",) * n_axes,
                                vmem_limit_bytes=VMEM_LIMIT)


def _ada_kernel(c_ref, w_ref, b_ref, o_ref):
    cf = c_ref[...]
    a = (cf * _sigmoid(cf)).astype(BF16)
    o_ref[0] = jnp.dot(a, w_ref[0].astype(BF16), preferred_element_type=F32) + b_ref[0]


def _ada_mod(c_rows, w_ada, b_ada):
    depth, d, n = w_ada.shape
    tn = 1024
    return pl.pallas_call(
        _ada_kernel,
        grid=(depth, n // tn),
        in_specs=[pl.BlockSpec((MOD_ROWS, d), lambda l, j: (0, 0)),
                  pl.BlockSpec((1, d, tn), lambda l, j: (l, 0, j)),
                  pl.BlockSpec((1, 1, tn), lambda l, j: (l, 0, j))],
        out_specs=pl.BlockSpec((1, MOD_ROWS, tn), lambda l, j: (l, 0, j)),
        out_shape=jax.ShapeDtypeStruct((depth, MOD_ROWS, n), F32),
        compiler_params=_params(2),
        name="ada_mod",
    )(c_rows, w_ada, b_ada.reshape(depth, 1, n))


def _mod_spec(k, brow):
    return pl.BlockSpec((1, 1, D_MODEL), lambda i, *_: (brow(i) * N_MOD + k, 0, 0))


def _inproj_kernel(x_ref, g_ref, sh_ref, sc_ref, w_ref, o_ref, h_scr, *, tn):
    y = _rms(x_ref[...], g_ref[...])
    h_scr[...] = (y * (1.0 + sc_ref[0]) + sh_ref[0]).astype(BF16)
    for c0 in range(0, o_ref.shape[1], tn):
        acc = lax.dot_general(h_scr[...], w_ref[c0:c0 + tn, :], (((1,), (1,)), ((), ())),
                              preferred_element_type=F32)
        if c0 >= LAT_W + POOL_DIM:
            acc = _sigmoid(acc)
        o_ref[:, c0:c0 + tn] = acc.astype(BF16)


def _in_proj(x2, mod3, brow, g, w, layer, n_cols):
    t, d = x2.shape
    tm = min(ROW_TILE, t)
    return pl.pallas_call(
        functools.partial(_inproj_kernel, tn=IN_PROJ_COL_CHUNK),
        grid=(t // tm,),
        in_specs=[pl.BlockSpec((tm, d), lambda i: (i, 0)),
                  pl.BlockSpec((1, d), lambda i: (0, 0)),
                  _mod_spec(0, lambda i: brow(i * tm)),
                  _mod_spec(1, lambda i: brow(i * tm)),
                  pl.BlockSpec((None, n_cols, d), lambda i: (layer, 0, 0), pipeline_mode=pl.Buffered(1))],
        out_specs=pl.BlockSpec((tm, n_cols), lambda i: (i, 0)),
        out_shape=jax.ShapeDtypeStruct((t, n_cols), BF16),
        scratch_shapes=[pltpu.VMEM((tm, d), BF16)],
        compiler_params=_params(1),
        name="in_proj",
    )(x2, g.reshape(1, d), mod3, mod3, w)


def _qkv_kernel(lat_ref, cos_ref, sin_ref, gq_ref, gkv_ref, wq_ref, wk_ref, wvt_ref, *out_refs, with_q):
    if with_q:
        q_ref, k_ref, v_ref = out_refs
    else:
        k_ref, v_ref = out_refs
    cos = cos_ref[...]
    sin = sin_ref[...]

    def rope(grp):
        return grp * cos + pltpu.roll(grp, 2 * (QK_ROPE // 2), axis=1) * sin

    kvn = _rms(lat_ref[:, :KV_RANK].astype(F32), gkv_ref[...]).astype(BF16)
    kpe = rope(lat_ref[:, KV_RANK + Q_RANK:].astype(F32)).astype(BF16)
    for h in range(0, N_HEADS, 2):
        kn = jnp.dot(kvn, wk_ref[:, h * QK_NOPE:(h + 2) * QK_NOPE], preferred_element_type=F32)
        for hh in range(2):
            k_ref[0, h + hh, :, :QK_NOPE] = kn[:, hh * QK_NOPE:(hh + 1) * QK_NOPE].astype(BF16)
            k_ref[0, h + hh, :, QK_NOPE:] = kpe
    for h in range(0, N_HEADS, V_HEAD_GROUP):
        vt = lax.dot_general(wvt_ref[h * V_DIM:(h + V_HEAD_GROUP) * V_DIM, :], kvn,
                             (((1,), (1,)), ((), ())), preferred_element_type=F32)
        for hh in range(V_HEAD_GROUP):
            v_ref[0, h + hh] = vt[hh * V_DIM:(hh + 1) * V_DIM, :].astype(BF16)
    if with_q:
        qn = _rms(lat_ref[:, KV_RANK:KV_RANK + Q_RANK].astype(F32), gq_ref[...]).astype(BF16)
        for h in range(N_HEADS):
            qh = jnp.dot(qn, wq_ref[:, h * HEAD_W:(h + 1) * HEAD_W], preferred_element_type=F32)
            q_ref[0, h, :, :QK_NOPE] = (qh[:, :QK_NOPE] * Q_SCALE).astype(BF16)
            q_ref[0, h, :, QK_NOPE:] = (rope(qh[:, QK_NOPE:]) * Q_SCALE).astype(BF16)


def _qkv(p, b, n, cos_t, sin_t, gq, gkv, wq, wk, wvt, with_q):
    tm = min(ROW_TILE, n)
    tpb = n // tm
    rope_tiles = cos_t.shape[0] // tm
    head_spec = pl.BlockSpec((1, N_HEADS, tm, HEAD_W), lambda i: (i // tpb, 0, i % tpb, 0))
    tab_spec = pl.BlockSpec((tm, LANES), lambda i: ((i % tpb) % rope_tiles, 0))
    out_shape = [jax.ShapeDtypeStruct((b, N_HEADS, n, HEAD_W), BF16),
                 jax.ShapeDtypeStruct((b, N_HEADS, V_DIM, n), BF16)]
    out_specs = [head_spec,
                 pl.BlockSpec((1, N_HEADS, V_DIM, tm), lambda i: (i // tpb, 0, 0, i % tpb))]
    if with_q:
        out_shape = [jax.ShapeDtypeStruct((b, N_HEADS, n, HEAD_W), BF16)] + out_shape
        out_specs = [head_spec] + out_specs
    return pl.pallas_call(
        functools.partial(_qkv_kernel, with_q=with_q),
        grid=(b * tpb,),
        in_specs=[pl.BlockSpec((tm, LAT_W), lambda i: (i, 0)),
                  tab_spec, tab_spec,
                  pl.BlockSpec((1, Q_RANK), lambda i: (0, 0)),
                  pl.BlockSpec((1, KV_RANK), lambda i: (0, 0)),
                  pl.BlockSpec(wq.shape, lambda i: (0, 0)),
                  pl.BlockSpec(wk.shape, lambda i: (0, 0)),
                  pl.BlockSpec(wvt.shape, lambda i: (0, 0))],
        out_specs=out_specs,
        out_shape=out_shape,
        compiler_params=_params(1),
        name="qkv",
    )(p, cos_t, sin_t, gq.reshape(1, Q_RANK), gkv.reshape(1, KV_RANK), wq, wk, wvt)


def _row_fold(x, op):
    parts = [x[r:r + SUBLANES] for r in range(0, x.shape[0], SUBLANES)]
    a, b = parts[0], parts[1]
    for i in range(2, len(parts) - 1, 2):
        a, b = op(a, parts[i]), op(b, parts[i + 1])
    if len(parts) % 2:
        a = op(a, parts[-1])
    return op(a, b)


def _attn_kernel(q_ref, *refs, n_seg, kb, tq):
    k_refs, vt_refs = refs[:n_seg], refs[n_seg:2 * n_seg]
    o_ref, s_a, s_b, m_a, m_b = refs[2 * n_seg:]
    chunks = []
    off = 0
    for kr, vr in zip(k_refs, vt_refs):
        nk = kr.shape[2]
        for c0 in range(0, nk, kb):
            n = min(kb, nk - c0)
            chunks.append((kr, vr, c0, n, off))
            off += n

    n_heads, nt = q_ref.shape[1], q_ref.shape[2] // tq
    bufs = ((s_a, m_a), (s_b, m_b))

    def tile_rows(t):
        return pl.ds(pl.multiple_of(t * tq, tq), tq)

    def scores(g, t, s_buf, m_buf):
        rows = tile_rows(t)
        q = q_ref[0, g, rows, :]
        mp = None
        for kr, vr, c0, n, o in chunks:
            s = lax.dot_general(kr[0, g, c0:c0 + n, :], q, (((1,), (1,)), ((), ())),
                                preferred_element_type=F32)
            s_buf[o:o + n, :] = s
            m = _row_fold(s, jnp.maximum)
            mp = m if mp is None else jnp.maximum(mp, m)
        m_buf[...] = mp

    def values(g, t, s_buf, m_buf):
        rows = tile_rows(t)
        mrow = jnp.max(m_buf[...], axis=0, keepdims=True)
        lp = None
        acc = None
        for kr, vr, c0, n, o in chunks:
            pr = jnp.exp2(s_buf[o:o + n, :] - mrow)
            ls = _row_fold(pr, jnp.add)
            lp = ls if lp is None else lp + ls
            pv = jnp.dot(vr[0, g, :, c0:c0 + n], pr.astype(BF16), preferred_element_type=F32)
            acc = pv if acc is None else acc + pv
        l = jnp.sum(lp, axis=0, keepdims=True)
        o_ref[0, g, rows, :] = (acc / l).T.astype(BF16)

    def phase(g, t, parity):
        scores(g, t + 1, *bufs[1 - parity])
        values(g, t, *bufs[parity])

    scores(0, 0, *bufs[0])
    for g in range(n_heads):
        par0 = (g * nt) % 2
        n_pairs = (nt - 1) // 2

        def pair(j, g=g, par0=par0):
            phase(g, 2 * j, par0)
            phase(g, 2 * j + 1, 1 - par0)

        def pairs(jj, carry, pair=pair):
            for k in range(ATTN_PAIRS_PER_ITER):
                pair(jj * ATTN_PAIRS_PER_ITER + k)
            return carry

        n_iter = n_pairs // ATTN_PAIRS_PER_ITER
        if n_iter:
            lax.fori_loop(0, n_iter, pairs, 0)
        for j in range(n_iter * ATTN_PAIRS_PER_ITER, n_pairs):
            pair(j)
        if (nt - 1) % 2:
            phase(g, nt - 2, (par0 + nt - 2) % 2)
        last = bufs[(par0 + nt - 1) % 2]
        if g + 1 < n_heads:
            scores(g + 1, 0, *bufs[((g + 1) * nt) % 2])
        values(g, nt - 1, *last)


def _attention(q, ks, vts):
    b, hh, n, _ = q.shape
    tq = min(ATTN_TQ, n)
    hg = ATTN_HEADS_PER_STEP if n > tq else hh
    assert hh % hg == 0
    nk_total = sum(k.shape[2] for k in ks)
    seg_spec = lambda a: pl.BlockSpec((1, hg) + a.shape[2:], lambda bi, h: (bi, h, 0, 0))
    return pl.pallas_call(
        functools.partial(_attn_kernel, n_seg=len(ks), kb=ATTN_KB, tq=tq),
        grid=(b, hh // hg),
        in_specs=[pl.BlockSpec((1, hg, n, HEAD_W), lambda bi, h: (bi, h, 0, 0))]
                 + [seg_spec(a) for a in ks] + [seg_spec(a) for a in vts],
        out_specs=pl.BlockSpec((1, hg, n, V_DIM), lambda bi, h: (bi, h, 0, 0)),
        out_shape=jax.ShapeDtypeStruct((b, hh, n, V_DIM), BF16),
        scratch_shapes=[pltpu.VMEM((nk_total, tq), F32), pltpu.VMEM((nk_total, tq), F32),
                        pltpu.VMEM((SUBLANES, tq), F32), pltpu.VMEM((SUBLANES, tq), F32)],
        compiler_params=_params(2),
        name="attention",
    )(q, *ks, *vts)


def _top2sum(a, b, c, d):
    s1, t1 = jnp.maximum(a, b), jnp.minimum(a, b)
    s2, t2 = jnp.maximum(c, d), jnp.minimum(c, d)
    return jnp.maximum(s1, s2) + jnp.maximum(jnp.minimum(s1, s2), jnp.maximum(t1, t2))


def _route(logits_t, bias):
    sc = _sigmoid(logits_t)
    sel = sc + bias
    sel_r = [sel[e:e + 1, :] for e in range(N_EXPERTS)]
    sc_r = [sc[e:e + 1, :] for e in range(N_EXPERTS)]
    epg = EXPERTS_PER_GROUP
    gs = [_top2sum(*sel_r[g * epg:(g + 1) * epg]) for g in range(N_GROUPS)]
    best, gi = gs[0], jnp.zeros(gs[0].shape, jnp.int32)
    for g in range(1, N_GROUPS):
        upd = gs[g] > best
        best = jnp.where(upd, gs[g], best)
        gi = jnp.where(upd, g, gi)

    def pick_group(rows, k):
        r = rows[k]
        for g in range(1, N_GROUPS):
            r = jnp.where(gi == g, rows[g * epg + k], r)
        return r

    in_sel = [pick_group(sel_r, k) for k in range(epg)]
    in_sc = [pick_group(sc_r, k) for k in range(epg)]
    b1, i1 = in_sel[0], jnp.zeros(gi.shape, jnp.int32)
    for k in range(1, epg):
        upd = in_sel[k] > b1
        b1 = jnp.where(upd, in_sel[k], b1)
        i1 = jnp.where(upd, k, i1)
    b2, i2 = None, None
    for k in range(epg):
        cand = jnp.where(i1 == k, -jnp.inf, in_sel[k])
        if b2 is None:
            b2, i2 = cand, jnp.zeros(gi.shape, jnp.int32)
        else:
            upd = cand > b2
            b2 = jnp.where(upd, cand, b2)
            i2 = jnp.where(upd, k, i2)

    def pick_local(idx):
        r = in_sc[0]
        for k in range(1, epg):
            r = jnp.where(idx == k, in_sc[k], r)
        return r

    s1, s2 = pick_local(i1), pick_local(i2)
    den = s1 + s2
    return gi * epg + i1, gi * epg + i2, s1 / den, s2 / den


def _mixout_kernel(y_ref, u_ref, up_ref, un_ref, gm_ref, gp_ref, wp_ref, ps_ref, wo_ref, x_ref,
                   g1_ref, gf_ref, sh2_ref, sc2_ref, wr_ref, rb_ref,
                   xo_ref, h2_ref, route_ref, w12_ref, *, tm, tpb, n_seq):
    i = pl.program_id(0)
    uext = jnp.concatenate([up_ref[...], u_ref[...], un_ref[...]], axis=0)
    sub = min(MIX_CHAIN_ROWS, tm)

    def mix_stage(r0):
        rows = slice(r0, r0 + sub)
        base = (i % tpb) * tm + r0
        r = lax.broadcasted_iota(jnp.int32, (sub, sub + 2 * POOL_HALO), 0)
        c = lax.broadcasted_iota(jnp.int32, (sub, sub + 2 * POOL_HALO), 1)
        rel = c - POOL_HALO - r
        jpos = base - POOL_HALO + c
        valid = (jpos >= 0) & (jpos < n_seq)
        tpos = base + lax.broadcasted_iota(jnp.int32, (sub, 1), 0)
        uwin = uext[r0:r0 + sub + 2 * POOL_HALO]
        uc = uext[r0 + POOL_HALO:r0 + POOL_HALO + sub]
        parts = []
        for g, w in enumerate(POOL_WINDOWS):
            sl = slice(g * POOL_GROUP_DIM, (g + 1) * POOL_GROUP_DIM)
            band = jnp.where(valid & (rel >= -(w // 2)) & (rel < w // 2), 1.0, 0.0).astype(BF16)
            cnt = jnp.clip(tpos + w // 2, 0, n_seq) - jnp.clip(tpos - w // 2, 0, n_seq)
            wsum = jnp.dot(band, uwin[:, sl], preferred_element_type=F32)
            z = wsum / cnt.astype(F32) - uc[:, sl].astype(F32)
            parts.append(jnp.dot(z.astype(BF16), wp_ref[g], preferred_element_type=F32))
        ypool = jnp.concatenate(parts, axis=1) * ps_ref[...]
        y_mla = jnp.concatenate([y_ref[0, h, rows, :] for h in range(N_HEADS)], axis=1)
        mix = gm_ref[rows, :].astype(F32) * y_mla.astype(F32) + gp_ref[rows, :].astype(F32) * ypool
        return jnp.dot(mix.astype(BF16), wo_ref[...], preferred_element_type=F32)

    def norm_route_stage(r0, y):
        rows = slice(r0, r0 + sub)
        xn = x_ref[rows, :] + g1_ref[0] * y
        xo_ref[rows, :] = xn
        h2f = _rms(xn, gf_ref[...]) * (1.0 + sc2_ref[0]) + sh2_ref[0]
        h2_ref[rows, :] = _pack_rows(h2f)
        logits_t = lax.dot_general(wr_ref[...], h2f.astype(BF16), (((1,), (1,)), ((), ())),
                                   preferred_element_type=F32)
        e1, e2, w1, w2 = _route(logits_t, rb_ref[...])
        zero = jnp.zeros((ROUTE_ROWS - 4, sub), F32)
        route_ref[:, rows] = jnp.concatenate([e1.astype(F32), e2.astype(F32), w1, w2, zero], axis=0)
        w12_ref[rows, :] = jnp.concatenate([w1, w2, jnp.zeros((LANES - 2, sub), F32)], axis=0).T

    starts = list(range(0, tm, sub))
    pending = None
    for r0 in starts:
        y = mix_stage(r0)
        if pending is not None:
            norm_route_stage(*pending)
        pending = (r0, y)
    norm_route_stage(*pending)


def _mixer_out(y, p, x2, mod3, brow, n_seq, wp, ps, wo, gf, wr_t, rb):
    t, d = x2.shape
    tm = min(2 * MIX_CHAIN_ROWS, n_seq)
    tpb = n_seq // tm
    hpt = tm // POOL_HALO
    n_halo = t // POOL_HALO
    row = lambda i: (i, 0)
    const2 = lambda i: (0, 0)
    mrow = lambda i: brow(i * tm)
    return pl.pallas_call(
        functools.partial(_mixout_kernel, tm=tm, tpb=tpb, n_seq=n_seq),
        grid=(t // tm,),
        in_specs=[pl.BlockSpec((1, N_HEADS, tm, V_DIM), lambda i: (i // tpb, 0, i % tpb, 0)),
                  pl.BlockSpec((tm, POOL_DIM), lambda i: (i, LAT_W // POOL_DIM)),
                  pl.BlockSpec((POOL_HALO, POOL_DIM),
                               lambda i: (jnp.maximum(i * hpt - 1, 0), LAT_W // POOL_DIM)),
                  pl.BlockSpec((POOL_HALO, POOL_DIM),
                               lambda i: (jnp.minimum((i + 1) * hpt, n_halo - 1), LAT_W // POOL_DIM)),
                  pl.BlockSpec((tm, d), lambda i: (i, (LAT_W + POOL_DIM) // d)),
                  pl.BlockSpec((tm, d), lambda i: (i, (LAT_W + POOL_DIM) // d + 1)),
                  pl.BlockSpec(wp.shape, lambda i: (0, 0, 0)),
                  pl.BlockSpec((1, d), const2),
                  pl.BlockSpec((d, d), const2, pipeline_mode=pl.Buffered(1)),
                  pl.BlockSpec((tm, d), row),
                  _mod_spec(2, mrow),
                  pl.BlockSpec((1, d), const2),
                  _mod_spec(3, mrow),
                  _mod_spec(4, mrow),
                  pl.BlockSpec((N_EXPERTS, d), const2),
                  pl.BlockSpec((N_EXPERTS, 1), const2)],
        out_specs=[pl.BlockSpec((tm, d), row),
                   pl.BlockSpec((tm, PACK_W), row),
                   pl.BlockSpec((ROUTE_ROWS, tm), lambda i: (0, i)),
                   pl.BlockSpec((tm, LANES), row)],
        out_shape=[jax.ShapeDtypeStruct((t, d), F32),
                   jax.ShapeDtypeStruct((t, PACK_W), jnp.int32),
                   jax.ShapeDtypeStruct((ROUTE_ROWS, t), F32),
                   jax.ShapeDtypeStruct((t, LANES), F32)],
        compiler_params=_params(1),
        name="mixer_out",
    )(y, p, p, p, p, p, wp, ps.reshape(1, d), wo, x2, mod3, gf.reshape(1, d), mod3, mod3, wr_t,
      rb.reshape(N_EXPERTS, 1))


def _chunk_view(x):
    n = x.shape[0]
    return (x.reshape(n // SUBLANES, SUBLANES, ROW_CHUNKS, LANES).transpose(0, 2, 1, 3)
            .reshape(n * ROW_CHUNKS, LANES))


def _row_view(c):
    n = c.shape[0] // ROW_CHUNKS
    return (c.reshape(n // SUBLANES, ROW_CHUNKS, SUBLANES, LANES).transpose(0, 2, 1, 3)
            .reshape(n, PACK_W))


def _sc_index_chunks(idx):
    m = idx.shape[0]
    per_row = LANES // ROW_CHUNKS
    assert m % per_row == 0
    lane = np.arange(LANES)
    src = (lane // (SUBLANES * ROW_CHUNKS)) * SUBLANES + lane % SUBLANES
    sel = jnp.asarray(np.equal.outer(np.arange(per_row), src), BF16)
    rows = idx.reshape(m // per_row, per_row)
    pick = lambda v: jnp.dot(v.astype(BF16), sel, preferred_element_type=F32)
    r = (pick(rows >> 8) * 256.0 + pick(rows & 255)).astype(jnp.int32)
    j = jnp.asarray((lane % (SUBLANES * ROW_CHUNKS)) // SUBLANES, jnp.int32)[None, :]
    ids = (r // SUBLANES) * (SUBLANES * ROW_CHUNKS) + j * SUBLANES + r % SUBLANES
    return ids.reshape(1, m * ROW_CHUNKS)


def _sc_mesh():
    return plsc.VectorSubcoreMesh(core_axis_name="core", subcore_axis_name="subcore")


def _sc_gather_rows(table, idx):
    n, m = table.shape[0], idx.shape[0]
    n_idx = m * ROW_CHUNKS
    assert n_idx % SC_WINDOW == 0 and n % SUBLANES == 0 and m % SUBLANES == 0

    @pl.kernel(out_type=jax.ShapeDtypeStruct((n_idx, LANES), jnp.int32), mesh=_sc_mesh())
    def gather(t_hbm, i_hbm, o_hbm):
        def body(i_vmem, o_vmem):
            pltpu.sync_copy(t_hbm.at[i_vmem.at[0]], o_vmem)

        pltpu.emit_pipeline(
            body,
            grid=(n_idx // SC_WINDOW,),
            in_specs=[pl.BlockSpec((1, SC_WINDOW), index_map=lambda i: (0, i))],
            out_specs=[pl.BlockSpec((SC_WINDOW, LANES), index_map=lambda i: (i, 0))],
            core_axis_name=("core", "subcore"),
            dimension_semantics=(pltpu.PARALLEL,),
        )(i_hbm, o_hbm)

    return _row_view(gather(_chunk_view(table), _sc_index_chunks(idx)))


def _sc_scatter_rows(rows, idx, n_out):
    n, m = rows.shape[0], idx.shape[0]
    n_idx = m * ROW_CHUNKS
    src_steps = n * ROW_CHUNKS // SC_WINDOW
    assert n_idx % SC_WINDOW == 0 and (n * ROW_CHUNKS) % SC_WINDOW == 0 and m % n == 0
    assert n % SUBLANES == 0 and n_out % SUBLANES == 0

    @pl.kernel(out_type=jax.ShapeDtypeStruct((n_out * ROW_CHUNKS, LANES), jnp.int32), mesh=_sc_mesh())
    def scatter(x_hbm, i_hbm, o_hbm):
        def body(x_vmem, i_vmem):
            pltpu.sync_copy(x_vmem, o_hbm.at[i_vmem.at[0]])

        pltpu.emit_pipeline(
            body,
            grid=(n_idx // SC_WINDOW,),
            in_specs=[pl.BlockSpec((SC_WINDOW, LANES), index_map=lambda i: (i % src_steps, 0)),
                      pl.BlockSpec((1, SC_WINDOW), index_map=lambda i: (0, i))],
            out_specs=[],
            core_axis_name=("core", "subcore"),
            dimension_semantics=(pltpu.PARALLEL,),
        )(x_hbm, i_hbm)

    return _row_view(scatter(_chunk_view(rows), _sc_index_chunks(idx)))


def _dispatch_plan(route):
    t = route.shape[1]
    n_pairs = 2 * t
    n_tiles = n_pairs // MOE_TILE + N_EXPERTS
    blk = LANES
    e = jnp.concatenate([route[0], route[1]]).astype(jnp.int32)
    onehot = (e[:, None] == jnp.arange(N_EXPERTS, dtype=jnp.int32)[None, :]).astype(F32)
    oh3 = onehot.reshape(n_pairs // blk, blk, N_EXPERTS)
    tri = (jnp.arange(blk)[:, None] >= jnp.arange(blk)[None, :]).astype(F32)
    within = jnp.einsum("ij,bjk->bik", tri, oh3)
    totals = within[:, -1, :]
    before = jnp.cumsum(totals, axis=0) - totals
    csum = (within + before[:, None, :]).reshape(n_pairs, N_EXPERTS)
    counts = jnp.sum(totals, axis=0).astype(jnp.int32)
    padded = ((counts + MOE_TILE - 1) // MOE_TILE) * MOE_TILE
    ends = jnp.cumsum(padded)
    starts = ends - padded
    pair_slot = jnp.sum(onehot * (csum - 1.0 + starts.astype(F32)[None, :]), axis=1).astype(jnp.int32)
    tile_start = jnp.arange(n_tiles, dtype=jnp.int32) * MOE_TILE
    tile_e = jnp.minimum(jnp.sum((tile_start[:, None] >= ends[None, :]).astype(jnp.int32), axis=1), N_EXPERTS - 1)
    e_onehot = tile_e[:, None] == jnp.arange(N_EXPERTS, dtype=jnp.int32)[None, :]
    filled = jnp.sum(jnp.where(e_onehot, (starts + counts)[None, :], 0), axis=1)
    tile_rows = jnp.clip(filled - tile_start, 0, MOE_TILE).astype(jnp.int32)
    last_used = jnp.maximum(ends[-1] // MOE_TILE - 1, 0)
    tile_expert = jnp.where(tile_start < ends[-1], tile_e, tile_e[last_used]).astype(jnp.int32)
    return pair_slot, tile_expert, tile_rows


def _swiglu(xb, wg, wu, wd):
    gate = jnp.dot(xb, wg, preferred_element_type=F32)
    up = jnp.dot(xb, wu, preferred_element_type=F32)
    a = ((gate * _sigmoid(gate)) * up).astype(BF16)
    return jnp.dot(a, wd, preferred_element_type=F32)


def _ffn_kernel(te_ref, rows_ref, xs_ref, wg_ref, wu_ref, wd_ref, ys_ref, wg_s, wu_s, wd_s):
    i = pl.program_id(0)
    new_expert = jnp.logical_or(i == 0, te_ref[i] != te_ref[jnp.maximum(i - 1, 0)])

    @pl.when(new_expert)
    def _():
        wg_s[...] = wg_ref[0].astype(BF16)
        wu_s[...] = wu_ref[0].astype(BF16)
        wd_s[...] = wd_ref[0].astype(BF16)

    @pl.when(rows_ref[i] > 0)
    def _():
        live = lax.broadcasted_iota(jnp.int32, (MOE_TILE, 1), 0) < rows_ref[i]
        xb = _load_unpack(xs_ref, live).astype(BF16)
        ys_ref[...] = _pack_rows(_swiglu(xb, wg_s[...], wu_s[...], wd_s[...]))

    @pl.when(rows_ref[i] == 0)
    def _():
        ys_ref[...] = jnp.zeros_like(ys_ref)


def _expert_ffn(xs, tile_expert, tile_rows, wg, wu, wd, layer):
    n_slots = xs.shape[0]
    d = D_MODEL
    blk = pl.BlockSpec((MOE_TILE, PACK_W), lambda i, te, nr: (i, 0))
    return pl.pallas_call(
        _ffn_kernel,
        grid_spec=pltpu.PrefetchScalarGridSpec(
            num_scalar_prefetch=2,
            grid=(n_slots // MOE_TILE,),
            in_specs=[blk,
                      pl.BlockSpec((None, 1, d, D_EXPERT), lambda i, te, nr: (layer, te[i], 0, 0)),
                      pl.BlockSpec((None, 1, d, D_EXPERT), lambda i, te, nr: (layer, te[i], 0, 0)),
                      pl.BlockSpec((None, 1, D_EXPERT, d), lambda i, te, nr: (layer, te[i], 0, 0))],
            out_specs=blk,
            scratch_shapes=[pltpu.VMEM((d, D_EXPERT), BF16), pltpu.VMEM((d, D_EXPERT), BF16),
                            pltpu.VMEM((D_EXPERT, d), BF16)]),
        out_shape=jax.ShapeDtypeStruct(xs.shape, jnp.int32),
        compiler_params=_params(1),
        name="expert_ffn",
    )(tile_expert, tile_rows, xs, wg, wu, wd)


def _combine_kernel(h_ref, y1_ref, y2_ref, w_ref, sg_ref, su_ref, sd_ref, x_ref, g2_ref, gfin_ref, o_ref,
                    *, final):
    y = _swiglu(_load_unpack(h_ref).astype(BF16), sg_ref[...], su_ref[...], sd_ref[...])
    w = w_ref[...]
    y = y + w[:, 0:1] * _load_unpack(y1_ref) + w[:, 1:2] * _load_unpack(y2_ref)
    xn = x_ref[...] + g2_ref[0] * y
    if final:
        xn = _rms(xn, gfin_ref[...])
    o_ref[...] = xn


def _moe_combine(h2p, yg, w12, sg, su, sd, x2, set_off, x_off, n_rows, mod3, brow, gfin, final):
    d = x2.shape[1]
    tm = min(ROW_TILE, n_rows)
    assert set_off % tm == 0 and x_off % tm == 0 and n_rows % tm == 0
    s_blk, x_blk, n_blk = set_off // tm, x_off // tm, n_rows // tm
    const2 = lambda i: (0, 0)
    pk = lambda o: pl.BlockSpec((tm, PACK_W), lambda i: (i + o, 0))
    x_spec = pl.BlockSpec((tm, d), lambda i: (i + x_blk, 0))
    return pl.pallas_call(
        functools.partial(_combine_kernel, final=final),
        grid=(n_blk,),
        in_specs=[pk(s_blk), pk(0), pk(n_blk),
                  pl.BlockSpec((tm, LANES), lambda i: (i + x_blk, 0)),
                  pl.BlockSpec(sg.shape, const2), pl.BlockSpec(su.shape, const2), pl.BlockSpec(sd.shape, const2),
                  x_spec,
                  _mod_spec(5, lambda i: brow((i + x_blk) * tm)),
                  pl.BlockSpec((1, d), const2)],
        out_specs=x_spec,
        out_shape=jax.ShapeDtypeStruct(x2.shape, F32),
        input_output_aliases={7: 0},
        compiler_params=_params(1),
        name="moe_combine",
    )(h2p, yg, yg, w12, sg, su, sd, x2, mod3, gfin.reshape(1, d))


def _moe_routed(parts, wg, wu, wd, layer, sg, su, sd, mod3, gfin, final):
    h2p = parts[0][0] if len(parts) == 1 else jnp.concatenate([pt[0] for pt in parts], axis=0)
    route = parts[0][1] if len(parts) == 1 else jnp.concatenate([pt[1] for pt in parts], axis=1)
    t_all = h2p.shape[0]
    pair_slot, tile_expert, tile_rows = _dispatch_plan(route)
    n_slots = tile_expert.shape[0] * MOE_TILE
    xs = _sc_scatter_rows(h2p, pair_slot, n_slots)
    ys = _expert_ffn(xs, tile_expert, tile_rows, wg, wu, wd, layer)
    outs, set_off = [], 0
    for _, _, w12, x2, brow in parts:
        t = x2.shape[0]
        n_rows = t // COMBINE_CHUNKS if t % (COMBINE_CHUNKS * ROW_TILE) == 0 else t
        for x_off in range(0, t, n_rows):
            lo = set_off + x_off
            idx = jnp.concatenate([pair_slot[lo:lo + n_rows], pair_slot[t_all + lo:t_all + lo + n_rows]])
            yg = _sc_gather_rows(ys, idx)
            x2 = _moe_combine(h2p, yg, w12, sg, su, sd, x2, lo, x_off, n_rows, mod3, brow, gfin, final)
        outs.append(x2)
        set_off += t
    return outs


def _rope_tables(seq):
    rows = seq // GRID_W
    row = np.repeat(np.arange(rows), GRID_W).astype(np.float32)
    col = np.tile(np.arange(GRID_W), rows).astype(np.float32)
    axis_dim = QK_ROPE // 2
    inv = (ROPE_THETA ** (-np.arange(0, axis_dim, 2, dtype=np.float32) / axis_dim)).astype(np.float32)
    ang = np.concatenate([row[:, None] * inv, col[:, None] * inv], axis=-1).astype(np.float32)
    cos, sin = np.cos(ang), np.sin(ang)
    zero = np.zeros_like(cos)
    return (jnp.asarray(np.concatenate([cos, cos, zero, zero], axis=1)),
            jnp.asarray(np.concatenate([-sin, sin, zero, zero], axis=1)))


def _rope_group(w, start):
    half = QK_ROPE // 2
    x1, x2 = w[..., start:start + half], w[..., start + half:start + 2 * half]
    return jnp.concatenate([x1, x2, x2, x1], axis=-1)


def _w_in_relayout_kernel(w_ref, o_ref):
    half = QK_ROPE // 2
    x1, x2 = KV_RANK, KV_RANK + half
    pieces = [(0, KV_RANK), (KV_COLS, Q_RANK), (x1, half), (x2, half), (x2, half), (x1, half),
              (Q_END, o_ref.shape[0] - LAT_W)]
    dst = 0
    for src, n in pieces:
        o_ref[dst:dst + n, :] = w_ref[src:src + n, :].astype(BF16)
        dst += n


def _relayout_w_in(w):
    depth, d, n_in = w.shape
    tc = 256
    return pl.pallas_call(
        _w_in_relayout_kernel,
        grid=(depth, d // tc),
        in_specs=[pl.BlockSpec((None, n_in, tc), lambda l, i: (l, 0, i))],
        out_specs=pl.BlockSpec((None, P_COLS, tc), lambda l, i: (l, 0, i)),
        out_shape=jax.ShapeDtypeStruct((depth, P_COLS, d), BF16),
        compiler_params=_params(2),
        name="w_in_relayout",
    )(jnp.swapaxes(w, 1, 2))


def _relayout_w_qb(w):
    dq = QK_NOPE + QK_ROPE
    cols = []
    for h in range(N_HEADS):
        cols += [w[:, h * dq:h * dq + QK_NOPE], _rope_group(w, h * dq + QK_NOPE)]
    return jnp.concatenate(cols, axis=1).astype(BF16)


def kernel(x, c, ctx, c_ctx, w_ada, b_ada, norm_mix_g, norm_ffn_g, w_in, q_norm_g, kv_norm_g, w_qb, w_kvb, w_pool, pool_scale, w_out, w_router, router_bias, w_exp_gate, w_exp_up, w_exp_down, w_sh_gate, w_sh_up, w_sh_down, final_norm_g):
    bsz, seq, d = x.shape
    n_ctx = ctx.shape[1]
    depth = w_ada.shape[0]
    assert bsz < MOD_ROWS and d == D_MODEL

    c_rows = jnp.concatenate([c, c_ctx[None], jnp.zeros((MOD_ROWS - bsz - 1, d), F32)], axis=0)
    mod = _ada_mod(c_rows, w_ada, b_ada)

    cos_t, sin_t = _rope_tables(seq)
    ones_t = jnp.concatenate([jnp.ones((n_ctx, 2 * (QK_ROPE // 2)), F32),
                              jnp.zeros((n_ctx, LANES - QK_ROPE), F32)], axis=1)
    zeros_t = jnp.zeros((n_ctx, LANES), F32)
    wr_t = w_router.T.astype(BF16)
    w_in_r = _relayout_w_in(w_in)
    lat_row = lambda r: r // seq
    ctx_row = lambda r: bsz

    x2 = x.reshape(bsz * seq, d)
    xc2 = ctx.reshape(bsz * n_ctx, d)
    for l in range(depth):
        last = l == depth - 1
        mod3 = mod[l].reshape(MOD_ROWS * N_MOD, 1, d)
        wq = _relayout_w_qb(w_qb[l])
        w_kv3 = w_kvb[l].reshape(KV_RANK, N_HEADS, QK_NOPE + V_DIM)
        wk = w_kv3[:, :, :QK_NOPE].reshape(KV_RANK, N_HEADS * QK_NOPE).astype(BF16)
        wvt = (w_kv3[:, :, QK_NOPE:].transpose(1, 2, 0).reshape(N_HEADS * V_DIM, KV_RANK)
               .astype(BF16))
        wp = w_pool[l].astype(BF16)
        wo = w_out[l].astype(BF16)
        sg, su, sd = w_sh_gate[l].astype(BF16), w_sh_up[l].astype(BF16), w_sh_down[l].astype(BF16)
        moe = functools.partial(_moe_routed, wg=w_exp_gate, wu=w_exp_up, wd=w_exp_down, layer=l,
                                sg=sg, su=su, sd=sd, mod3=mod3, gfin=final_norm_g)

        p = _in_proj(x2, mod3, lat_row, norm_mix_g[l], w_in_r, l, P_COLS)
        pc = _in_proj(xc2, mod3, ctx_row, norm_mix_g[l], w_in_r, l, LAT_W if last else P_COLS)
        q, k_lat, v_lat = _qkv(p, bsz, seq, cos_t, sin_t, q_norm_g[l], kv_norm_g[l], wq, wk, wvt, True)
        ctx_out = _qkv(pc, bsz, n_ctx, ones_t, zeros_t, q_norm_g[l], kv_norm_g[l], wq, wk, wvt, not last)
        k_ctx, v_ctx = ctx_out[-2:]
        y = _attention(q, [k_ctx, k_lat], [v_ctx, v_lat])
        x2, h2p, route, w12 = _mixer_out(y, p, x2, mod3, lat_row, seq, wp, pool_scale[l], wo,
                                         norm_ffn_g[l], wr_t, router_bias)
        if last:
            x2, = moe([(h2p, route, w12, x2, lat_row)], final=True)
        else:
            yc = _attention(ctx_out[0], [k_ctx], [v_ctx])
            xc2, h2c, route_c, w12c = _mixer_out(yc, pc, xc2, mod3, ctx_row, n_ctx, wp, pool_scale[l], wo,
                                                 norm_ffn_g[l], wr_t, router_bias)
            x2, xc2 = moe([(h2p, route, w12, x2, lat_row), (h2c, route_c, w12c, xc2, ctx_row)], final=False)
    return x2.reshape(bsz, seq, d)
```

```python
import functools
import math

import jax
import jax.numpy as jnp
import numpy as np
from jax import lax
from jax.experimental import pallas as pl
from jax.experimental.pallas import tpu as pltpu
from jax.experimental.pallas import tpu_sc as plsc

F32 = jnp.float32
BF16 = jnp.bfloat16

D_MODEL = 2048
GRID_W = 64
N_HEADS = 16
QK_NOPE = 128
QK_ROPE = 64
V_DIM = 128
Q_RANK = 384
KV_RANK = 512
ROPE_THETA = 10000.0
POOL_DIM = 1024
POOL_GROUP_DIM = 256
POOL_WINDOWS = (2, 4, 8, 16)
KV_COLS = KV_RANK + QK_ROPE
Q_END = KV_COLS + Q_RANK
N_EXPERTS = 16
N_GROUPS = 4
EXPERTS_PER_GROUP = 4
D_EXPERT = 512
EPS = 1e-6
N_MOD = 6

LANES = 128
HEAD_W = 2 * LANES
LAT_W = KV_RANK + Q_RANK + LANES
P_COLS = LAT_W + POOL_DIM + 2 * D_MODEL
POOL_HALO = 16
MOD_ROWS = 8
Q_SCALE = (1.0 / math.sqrt(QK_NOPE + QK_ROPE)) * math.log2(math.e)
V7X_VMEM_BYTES = 64 * 1024 * 1024
VMEM_LIMIT = V7X_VMEM_BYTES - 8 * 1024 * 1024
SUBLANES = 8
PACK_W = D_MODEL // 2
ROW_CHUNKS = PACK_W // LANES
ROUTE_ROWS = SUBLANES
MOE_TILE = 512
SC_WINDOW = 128
ATTN_TQ = 256
ATTN_KB = 512
V_HEAD_GROUP = 8
ATTN_HEADS_PER_STEP = 2
ATTN_PAIRS_PER_ITER = 3
ROW_TILE = 512
IN_PROJ_COL_CHUNK = 512
MIX_CHAIN_ROWS = 256
COMBINE_CHUNKS = 4


def _sigmoid(x):
    return 1.0 / (1.0 + jnp.exp(-x))


def _pack_rows(y):
    half = y.shape[1] // 2
    return pltpu.pack_elementwise([y[:, :half], y[:, half:]], packed_dtype=BF16)


def _load_unpack(ref, live=None):
    w = ref[...]
    if live is not None:
        w = jnp.where(live, w, 0)
    lo = pltpu.unpack_elementwise(w, index=0, packed_dtype=BF16, unpacked_dtype=F32)
    hi = pltpu.unpack_elementwise(w, index=1, packed_dtype=BF16, unpacked_dtype=F32)
    return jnp.concatenate([lo, hi], axis=1)


def _rms(xf, g):
    ms = jnp.mean(xf * xf, axis=-1, keepdims=True)
    return xf * lax.rsqrt(ms + EPS) * g


def _params(n_axes):
    return pltpu.CompilerParams(dimension_semantics=("arbitrary",) * n_axes,
                                vmem_limit_bytes=VMEM_LIMIT)


def _ada_kernel(c_ref, w_ref, b_ref, o_ref):
    cf = c_ref[...]
    a = (cf * _sigmoid(cf)).astype(BF16)
    o_ref[0] = jnp.dot(a, w_ref[0].astype(BF16), preferred_element_type=F32) + b_ref[0]


def _ada_mod(c_rows, w_ada, b_ada):
    depth, d, n = w_ada.shape
    tn = 1024
    return pl.pallas_call(
        _ada_kernel,
        grid=(depth, n // tn),
        in_specs=[pl.BlockSpec((MOD_ROWS, d), lambda l, j: (0, 0)),
                  pl.BlockSpec((1, d, tn), lambda l, j: (l, 0, j)),
                  pl.BlockSpec((1, 1, tn), lambda l, j: (l, 0, j))],
        out_specs=pl.BlockSpec((1, MOD_ROWS, tn), lambda l, j: (l, 0, j)),
        out_shape=jax.ShapeDtypeStruct((depth, MOD_ROWS, n), F32),
        compiler_params=_params(2),
        name="ada_mod",
    )(c_rows, w_ada, b_ada.reshape(depth, 1, n))


def _mod_spec(k, brow):
    return pl.BlockSpec((1, 1, D_MODEL), lambda i, *_: (brow(i) * N_MOD + k, 0, 0))


def _inproj_kernel(x_ref, g_ref, sh_ref, sc_ref, w_ref, o_ref, h_scr, *, tn):
    y = _rms(x_ref[...], g_ref[...])
    h_scr[...] = (y * (1.0 + sc_ref[0]) + sh_ref[0]).astype(BF16)
    for c0 in range(0, o_ref.shape[1], tn):
        acc = lax.dot_general(h_scr[...], w_ref[c0:c0 + tn, :], (((1,), (1,)), ((), ())),
                              preferred_element_type=F32)
        if c0 >= LAT_W + POOL_DIM:
            acc = _sigmoid(acc)
        o_ref[:, c0:c0 + tn] = acc.astype(BF16)


def _in_proj(x2, mod3, brow, g, w, layer, n_cols):
    t, d = x2.shape
    tm = min(ROW_TILE, t)
    return pl.pallas_call(
        functools.partial(_inproj_kernel, tn=IN_PROJ_COL_CHUNK),
        grid=(t // tm,),
        in_specs=[pl.BlockSpec((tm, d), lambda i: (i, 0)),
                  pl.BlockSpec((1, d), lambda i: (0, 0)),
                  _mod_spec(0, lambda i: brow(i * tm)),
                  _mod_spec(1, lambda i: brow(i * tm)),
                  pl.BlockSpec((None, n_cols, d), lambda i: (layer, 0, 0), pipeline_mode=pl.Buffered(1))],
        out_specs=pl.BlockSpec((tm, n_cols), lambda i: (i, 0)),
        out_shape=jax.ShapeDtypeStruct((t, n_cols), BF16),
        scratch_shapes=[pltpu.VMEM((tm, d), BF16)],
        compiler_params=_params(1),
        name="in_proj",
    )(x2, g.reshape(1, d), mod3, mod3, w)


def _qkv_kernel(lat_ref, cos_ref, sin_ref, gq_ref, gkv_ref, wq_ref, wk_ref, wvt_ref, *out_refs, with_q):
    if with_q:
        q_ref, k_ref, v_ref = out_refs
    else:
        k_ref, v_ref = out_refs
    cos = cos_ref[...]
    sin = sin_ref[...]

    def rope(grp):
        return grp * cos + pltpu.roll(grp, 2 * (QK_ROPE // 2), axis=1) * sin

    kvn = _rms(lat_ref[:, :KV_RANK].astype(F32), gkv_ref[...]).astype(BF16)
    kpe = rope(lat_ref[:, KV_RANK + Q_RANK:].astype(F32)).astype(BF16)
    for h in range(0, N_HEADS, 2):
        kn = jnp.dot(kvn, wk_ref[:, h * QK_NOPE:(h + 2) * QK_NOPE], preferred_element_type=F32)
        for hh in range(2):
            k_ref[0, h + hh, :, :QK_NOPE] = kn[:, hh * QK_NOPE:(hh + 1) * QK_NOPE].astype(BF16)
            k_ref[0, h + hh, :, QK_NOPE:] = kpe
    for h in range(0, N_HEADS, V_HEAD_GROUP):
        vt = lax.dot_general(wvt_ref[h * V_DIM:(h + V_HEAD_GROUP) * V_DIM, :], kvn,
                             (((1,), (1,)), ((), ())), preferred_element_type=F32)
        for hh in range(V_HEAD_GROUP):
            v_ref[0, h + hh] = vt[hh * V_DIM:(hh + 1) * V_DIM, :].astype(BF16)
    if with_q:
        qn = _rms(lat_ref[:, KV_RANK:KV_RANK + Q_RANK].astype(F32), gq_ref[...]).astype(BF16)
        for h in range(N_HEADS):
            qh = jnp.dot(qn, wq_ref[:, h * HEAD_W:(h + 1) * HEAD_W], preferred_element_type=F32)
            q_ref[0, h, :, :QK_NOPE] = (qh[:, :QK_NOPE] * Q_SCALE).astype(BF16)
            q_ref[0, h, :, QK_NOPE:] = (rope(qh[:, QK_NOPE:]) * Q_SCALE).astype(BF16)


def _qkv(p, b, n, cos_t, sin_t, gq, gkv, wq, wk, wvt, with_q):
    tm = min(ROW_TILE, n)
    tpb = n // tm
    rope_tiles = cos_t.shape[0] // tm
    head_spec = pl.BlockSpec((1, N_HEADS, tm, HEAD_W), lambda i: (i // tpb, 0, i % tpb, 0))
    tab_spec = pl.BlockSpec((tm, LANES), lambda i: ((i % tpb) % rope_tiles, 0))
    out_shape = [jax.ShapeDtypeStruct((b, N_HEADS, n, HEAD_W), BF16),
                 jax.ShapeDtypeStruct((b, N_HEADS, V_DIM, n), BF16)]
    out_specs = [head_spec,
                 pl.BlockSpec((1, N_HEADS, V_DIM, tm), lambda i: (i // tpb, 0, 0, i % tpb))]
    if with_q:
        out_shape = [jax.ShapeDtypeStruct((b, N_HEADS, n, HEAD_W), BF16)] + out_shape
        out_specs = [head_spec] + out_specs
    return pl.pallas_call(
        functools.partial(_qkv_kernel, with_q=with_q),
        grid=(b * tpb,),
        in_specs=[pl.BlockSpec((tm, LAT_W), lambda i: (i, 0)),
                  tab_spec, tab_spec,
                  pl.BlockSpec((1, Q_RANK), lambda i: (0, 0)),
                  pl.BlockSpec((1, KV_RANK), lambda i: (0, 0)),
                  pl.BlockSpec(wq.shape, lambda i: (0, 0)),
                  pl.BlockSpec(wk.shape, lambda i: (0, 0)),
                  pl.BlockSpec(wvt.shape, lambda i: (0, 0))],
        out_specs=out_specs,
        out_shape=out_shape,
        compiler_params=_params(1),
        name="qkv",
    )(p, cos_t, sin_t, gq.reshape(1, Q_RANK), gkv.reshape(1, KV_RANK), wq, wk, wvt)


def _row_fold(x, op):
    parts = [x[r:r + SUBLANES] for r in range(0, x.shape[0], SUBLANES)]
    a, b = parts[0], parts[1]
    for i in range(2, len(parts) - 1, 2):
        a, b = op(a, parts[i]), op(b, parts[i + 1])
    if len(parts) % 2:
        a = op(a, parts[-1])
    return op(a, b)


def _attn_kernel(q_ref, *refs, n_seg, kb, tq):
    k_refs, vt_refs = refs[:n_seg], refs[n_seg:2 * n_seg]
    o_ref, s_a, s_b, m_a, m_b = refs[2 * n_seg:]
    chunks = []
    off = 0
    for kr, vr in zip(k_refs, vt_refs):
        nk = kr.shape[2]
        for c0 in range(0, nk, kb):
            n = min(kb, nk - c0)
            chunks.append((kr, vr, c0, n, off))
            off += n

    n_heads, nt = q_ref.shape[1], q_ref.shape[2] // tq
    bufs = ((s_a, m_a), (s_b, m_b))

    def tile_rows(t):
        return pl.ds(pl.multiple_of(t * tq, tq), tq)

    def scores(g, t, s_buf, m_buf):
        rows = tile_rows(t)
        q = q_ref[0, g, rows, :]
        mp = None
        for kr, vr, c0, n, o in chunks:
            s = lax.dot_general(kr[0, g, c0:c0 + n, :], q, (((1,), (1,)), ((), ())),
                                preferred_element_type=F32)
            s_buf[o:o + n, :] = s
            m = _row_fold(s, jnp.maximum)
            mp = m if mp is None else jnp.maximum(mp, m)
        m_buf[...] = mp

    def values(g, t, s_buf, m_buf):
        rows = tile_rows(t)
        mrow = jnp.max(m_buf[...], axis=0, keepdims=True)
        lp = None
        acc = None
        for kr, vr, c0, n, o in chunks:
            pr = jnp.exp2(s_buf[o:o + n, :] - mrow)
            ls = _row_fold(pr, jnp.add)
            lp = ls if lp is None else lp + ls
            pv = jnp.dot(vr[0, g, :, c0:c0 + n], pr.astype(BF16), preferred_element_type=F32)
            acc = pv if acc is None else acc + pv
        l = jnp.sum(lp, axis=0, keepdims=True)
        o_ref[0, g, rows, :] = (acc / l).T.astype(BF16)

    def phase(g, t, parity):
        scores(g, t + 1, *bufs[1 - parity])
        values(g, t, *bufs[parity])

    scores(0, 0, *bufs[0])
    for g in range(n_heads):
        par0 = (g * nt) % 2
        n_pairs = (nt - 1) // 2

        def pair(j, g=g, par0=par0):
            phase(g, 2 * j, par0)
            phase(g, 2 * j + 1, 1 - par0)

        def pairs(jj, carry, pair=pair):
            for k in range(ATTN_PAIRS_PER_ITER):
                pair(jj * ATTN_PAIRS_PER_ITER + k)
            return carry

        n_iter = n_pairs // ATTN_PAIRS_PER_ITER
        if n_iter:
            lax.fori_loop(0, n_iter, pairs, 0)
        for j in range(n_iter * ATTN_PAIRS_PER_ITER, n_pairs):
            pair(j)
        if (nt - 1) % 2:
            phase(g, nt - 2, (par0 + nt - 2) % 2)
        last = bufs[(par0 + nt - 1) % 2]
        if g + 1 < n_heads:
            scores(g + 1, 0, *bufs[((g + 1) * nt) % 2])
        values(g, nt - 1, *last)


def _attention(q, ks, vts):
    b, hh, n, _ = q.shape
    tq = min(ATTN_TQ, n)
    hg = ATTN_HEADS_PER_STEP if n > tq else hh
    assert hh % hg == 0
    nk_total = sum(k.shape[2] for k in ks)
    seg_spec = lambda a: pl.BlockSpec((1, hg) + a.shape[2:], lambda bi, h: (bi, h, 0, 0))
    return pl.pallas_call(
        functools.partial(_attn_kernel, n_seg=len(ks), kb=ATTN_KB, tq=tq),
        grid=(b, hh // hg),
        in_specs=[pl.BlockSpec((1, hg, n, HEAD_W), lambda bi, h: (bi, h, 0, 0))]
                 + [seg_spec(a) for a in ks] + [seg_spec(a) for a in vts],
        out_specs=pl.BlockSpec((1, hg, n, V_DIM), lambda bi, h: (bi, h, 0, 0)),
        out_shape=jax.ShapeDtypeStruct((b, hh, n, V_DIM), BF16),
        scratch_shapes=[pltpu.VMEM((nk_total, tq), F32), pltpu.VMEM((nk_total, tq), F32),
                        pltpu.VMEM((SUBLANES, tq), F32), pltpu.VMEM((SUBLANES, tq), F32)],
        compiler_params=_params(2),
        name="attention",
    )(q, *ks, *vts)


def _top2sum(a, b, c, d):
    s1, t1 = jnp.maximum(a, b), jnp.minimum(a, b)
    s2, t2 = jnp.maximum(c, d), jnp.minimum(c, d)
    return jnp.maximum(s1, s2) + jnp.maximum(jnp.minimum(s1, s2), jnp.maximum(t1, t2))


def _route(logits_t, bias):
    sc = _sigmoid(logits_t)
    sel = sc + bias
    sel_r = [sel[e:e + 1, :] for e in range(N_EXPERTS)]
    sc_r = [sc[e:e + 1, :] for e in range(N_EXPERTS)]
    epg = EXPERTS_PER_GROUP
    gs = [_top2sum(*sel_r[g * epg:(g + 1) * epg]) for g in range(N_GROUPS)]
    best, gi = gs[0], jnp.zeros(gs[0].shape, jnp.int32)
    for g in range(1, N_GROUPS):
        upd = gs[g] > best
        best = jnp.where(upd, gs[g], best)
        gi = jnp.where(upd, g, gi)

    def pick_group(rows, k):
        r = rows[k]
        for g in range(1, N_GROUPS):
            r = jnp.where(gi == g, rows[g * epg + k], r)
        return r

    in_sel = [pick_group(sel_r, k) for k in range(epg)]
    in_sc = [pick_group(sc_r, k) for k in range(epg)]
    b1, i1 = in_sel[0], jnp.zeros(gi.shape, jnp.int32)
    for k in range(1, epg):
        upd = in_sel[k] > b1
        b1 = jnp.where(upd, in_sel[k], b1)
        i1 = jnp.where(upd, k, i1)
    b2, i2 = None, None
    for k in range(epg):
        cand = jnp.where(i1 == k, -jnp.inf, in_sel[k])
        if b2 is None:
            b2, i2 = cand, jnp.zeros(gi.shape, jnp.int32)
        else:
            upd = cand > b2
            b2 = jnp.where(upd, cand, b2)
            i2 = jnp.where(upd, k, i2)

    def pick_local(idx):
        r = in_sc[0]
        for k in range(1, epg):
            r = jnp.where(idx == k, in_sc[k], r)
        return r

    s1, s2 = pick_local(i1), pick_local(i2)
    den = s1 + s2
    return gi * epg + i1, gi * epg + i2, s1 / den, s2 / den


def _mixout_kernel(y_ref, u_ref, up_ref, un_ref, gm_ref, gp_ref, wp_ref, ps_ref, wo_ref, x_ref,
                   g1_ref, gf_ref, sh2_ref, sc2_ref, wr_ref, rb_ref,
                   xo_ref, h2_ref, route_ref, w12_ref, *, tm, tpb, n_seq):
    i = pl.program_id(0)
    uext = jnp.concatenate([up_ref[...], u_ref[...], un_ref[...]], axis=0)
    sub = min(MIX_CHAIN_ROWS, tm)

    def mix_stage(r0):
        rows = slice(r0, r0 + sub)
        base = (i % tpb) * tm + r0
        r = lax.broadcasted_iota(jnp.int32, (sub, sub + 2 * POOL_HALO), 0)
        c = lax.broadcasted_iota(jnp.int32, (sub, sub + 2 * POOL_HALO), 1)
        rel = c - POOL_HALO - r
        jpos = base - POOL_HALO + c
        valid = (jpos >= 0) & (jpos < n_seq)
        tpos = base + lax.broadcasted_iota(jnp.int32, (sub, 1), 0)
        uwin = uext[r0:r0 + sub + 2 * POOL_HALO]
        uc = uext[r0 + POOL_HALO:r0 + POOL_HALO + sub]
        parts = []
        for g, w in enumerate(POOL_WINDOWS):
            sl = slice(g * POOL_GROUP_DIM, (g + 1) * POOL_GROUP_DIM)
            band = jnp.where(valid & (rel >= -(w // 2)) & (rel < w // 2), 1.0, 0.0).astype(BF16)
            cnt = jnp.clip(tpos + w // 2, 0, n_seq) - jnp.clip(tpos - w // 2, 0, n_seq)
            wsum = jnp.dot(band, uwin[:, sl], preferred_element_type=F32)
            z = wsum / cnt.astype(F32) - uc[:, sl].astype(F32)
            parts.append(jnp.dot(z.astype(BF16), wp_ref[g], preferred_element_type=F32))
        ypool = jnp.concatenate(parts, axis=1) * ps_ref[...]
        y_mla = jnp.concatenate([y_ref[0, h, rows, :] for h in range(N_HEADS)], axis=1)
        mix = gm_ref[rows, :].astype(F32) * y_mla.astype(F32) + gp_ref[rows, :].astype(F32) * ypool
        return jnp.dot(mix.astype(BF16), wo_ref[...], preferred_element_type=F32)

    def norm_route_stage(r0, y):
        rows = slice(r0, r0 + sub)
        xn = x_ref[rows, :] + g1_ref[0] * y
        xo_ref[rows, :] = xn
        h2f = _rms(xn, gf_ref[...]) * (1.0 + sc2_ref[0]) + sh2_ref[0]
        h2_ref[rows, :] = _pack_rows(h2f)
        logits_t = lax.dot_general(wr_ref[...], h2f.astype(BF16), (((1,), (1,)), ((), ())),
                                   preferred_element_type=F32)
        e1, e2, w1, w2 = _route(logits_t, rb_ref[...])
        zero = jnp.zeros((ROUTE_ROWS - 4, sub), F32)
        route_ref[:, rows] = jnp.concatenate([e1.astype(F32), e2.astype(F32), w1, w2, zero], axis=0)
        w12_ref[rows, :] = jnp.concatenate([w1, w2, jnp.zeros((LANES - 2, sub), F32)], axis=0).T

    starts = list(range(0, tm, sub))
    pending = None
    for r0 in starts:
        y = mix_stage(r0)
        if pending is not None:
            norm_route_stage(*pending)
        pending = (r0, y)
    norm_route_stage(*pending)


def _mixer_out(y, p, x2, mod3, brow, n_seq, wp, ps, wo, gf, wr_t, rb):
    t, d = x2.shape
    tm = min(2 * MIX_CHAIN_ROWS, n_seq)
    tpb = n_seq // tm
    hpt = tm // POOL_HALO
    n_halo = t // POOL_HALO
    row = lambda i: (i, 0)
    const2 = lambda i: (0, 0)
    mrow = lambda i: brow(i * tm)
    return pl.pallas_call(
        functools.partial(_mixout_kernel, tm=tm, tpb=tpb, n_seq=n_seq),
        grid=(t // tm,),
        in_specs=[pl.BlockSpec((1, N_HEADS, tm, V_DIM), lambda i: (i // tpb, 0, i % tpb, 0)),
                  pl.BlockSpec((tm, POOL_DIM), lambda i: (i, LAT_W // POOL_DIM)),
                  pl.BlockSpec((POOL_HALO, POOL_DIM),
                               lambda i: (jnp.maximum(i * hpt - 1, 0), LAT_W // POOL_DIM)),
                  pl.BlockSpec((POOL_HALO, POOL_DIM),
                               lambda i: (jnp.minimum((i + 1) * hpt, n_halo - 1), LAT_W // POOL_DIM)),
                  pl.BlockSpec((tm, d), lambda i: (i, (LAT_W + POOL_DIM) // d)),
                  pl.BlockSpec((tm, d), lambda i: (i, (LAT_W + POOL_DIM) // d + 1)),
                  pl.BlockSpec(wp.shape, lambda i: (0, 0, 0)),
                  pl.BlockSpec((1, d), const2),
                  pl.BlockSpec((d, d), const2, pipeline_mode=pl.Buffered(1)),
                  pl.BlockSpec((tm, d), row),
                  _mod_spec(2, mrow),
                  pl.BlockSpec((1, d), const2),
                  _mod_spec(3, mrow),
                  _mod_spec(4, mrow),
                  pl.BlockSpec((N_EXPERTS, d), const2),
                  pl.BlockSpec((N_EXPERTS, 1), const2)],
        out_specs=[pl.BlockSpec((tm, d), row),
                   pl.BlockSpec((tm, PACK_W), row),
                   pl.BlockSpec((ROUTE_ROWS, tm), lambda i: (0, i)),
                   pl.BlockSpec((tm, LANES), row)],
        out_shape=[jax.ShapeDtypeStruct((t, d), F32),
                   jax.ShapeDtypeStruct((t, PACK_W), jnp.int32),
                   jax.ShapeDtypeStruct((ROUTE_ROWS, t), F32),
                   jax.ShapeDtypeStruct((t, LANES), F32)],
        compiler_params=_params(1),
        name="mixer_out",
    )(y, p, p, p, p, p, wp, ps.reshape(1, d), wo, x2, mod3, gf.reshape(1, d), mod3, mod3, wr_t,
      rb.reshape(N_EXPERTS, 1))


def _chunk_view(x):
    n = x.shape[0]
    return (x.reshape(n // SUBLANES, SUBLANES, ROW_CHUNKS, LANES).transpose(0, 2, 1, 3)
            .reshape(n * ROW_CHUNKS, LANES))


def _row_view(c):
    n = c.shape[0] // ROW_CHUNKS
    return (c.reshape(n // SUBLANES, ROW_CHUNKS, SUBLANES, LANES).transpose(0, 2, 1, 3)
            .reshape(n, PACK_W))


def _sc_index_chunks(idx):
    m = idx.shape[0]
    per_row = LANES // ROW_CHUNKS
    assert m % per_row == 0
    lane = np.arange(LANES)
    src = (lane // (SUBLANES * ROW_CHUNKS)) * SUBLANES + lane % SUBLANES
    sel = jnp.asarray(np.equal.outer(np.arange(per_row), src), BF16)
    rows = idx.reshape(m // per_row, per_row)
    pick = lambda v: jnp.dot(v.astype(BF16), sel, preferred_element_type=F32)
    r = (pick(rows >> 8) * 256.0 + pick(rows & 255)).astype(jnp.int32)
    j = jnp.asarray((lane % (SUBLANES * ROW_CHUNKS)) // SUBLANES, jnp.int32)[None, :]
    ids = (r // SUBLANES) * (SUBLANES * ROW_CHUNKS) + j * SUBLANES + r % SUBLANES
    return ids.reshape(1, m * ROW_CHUNKS)


def _sc_mesh():
    return plsc.VectorSubcoreMesh(core_axis_name="core", subcore_axis_name="subcore")


def _sc_gather_rows(table, idx):
    n, m = table.shape[0], idx.shape[0]
    n_idx = m * ROW_CHUNKS
    assert n_idx % SC_WINDOW == 0 and n % SUBLANES == 0 and m % SUBLANES == 0

    @pl.kernel(out_type=jax.ShapeDtypeStruct((n_idx, LANES), jnp.int32), mesh=_sc_mesh())
    def gather(t_hbm, i_hbm, o_hbm):
        def body(i_vmem, o_vmem):
            pltpu.sync_copy(t_hbm.at[i_vmem.at[0]], o_vmem)

        pltpu.emit_pipeline(
            body,
            grid=(n_idx // SC_WINDOW,),
            in_specs=[pl.BlockSpec((1, SC_WINDOW), index_map=lambda i: (0, i))],
            out_specs=[pl.BlockSpec((SC_WINDOW, LANES), index_map=lambda i: (i, 0))],
            core_axis_name=("core", "subcore"),
            dimension_semantics=(pltpu.PARALLEL,),
        )(i_hbm, o_hbm)

    return _row_view(gather(_chunk_view(table), _sc_index_chunks(idx)))


def _sc_scatter_rows(rows, idx, n_out):
    n, m = rows.shape[0], idx.shape[0]
    n_idx = m * ROW_CHUNKS
    src_steps = n * ROW_CHUNKS // SC_WINDOW
    assert n_idx % SC_WINDOW == 0 and (n * ROW_CHUNKS) % SC_WINDOW == 0 and m % n == 0
    assert n % SUBLANES == 0 and n_out % SUBLANES == 0

    @pl.kernel(out_type=jax.ShapeDtypeStruct((n_out * ROW_CHUNKS, LANES), jnp.int32), mesh=_sc_mesh())
    def scatter(x_hbm, i_hbm, o_hbm):
        def body(x_vmem, i_vmem):
            pltpu.sync_copy(x_vmem, o_hbm.at[i_vmem.at[0]])

        pltpu.emit_pipeline(
            body,
            grid=(n_idx // SC_WINDOW,),
            in_specs=[pl.BlockSpec((SC_WINDOW, LANES), index_map=lambda i: (i % src_steps, 0)),
                      pl.BlockSpec((1, SC_WINDOW), index_map=lambda i: (0, i))],
            out_specs=[],
            core_axis_name=("core", "subcore"),
            dimension_semantics=(pltpu.PARALLEL,),
        )(x_hbm, i_hbm)

    return _row_view(scatter(_chunk_view(rows), _sc_index_chunks(idx)))


def _dispatch_plan(route):
    t = route.shape[1]
    n_pairs = 2 * t
    n_tiles = n_pairs // MOE_TILE + N_EXPERTS
    blk = LANES
    e = jnp.concatenate([route[0], route[1]]).astype(jnp.int32)
    onehot = (e[:, None] == jnp.arange(N_EXPERTS, dtype=jnp.int32)[None, :]).astype(F32)
    oh3 = onehot.reshape(n_pairs // blk, blk, N_EXPERTS)
    tri = (jnp.arange(blk)[:, None] >= jnp.arange(blk)[None, :]).astype(F32)
    within = jnp.einsum("ij,bjk->bik", tri, oh3)
    totals = within[:, -1, :]
    before = jnp.cumsum(totals, axis=0) - totals
    csum = (within + before[:, None, :]).reshape(n_pairs, N_EXPERTS)
    counts = jnp.sum(totals, axis=0).astype(jnp.int32)
    padded = ((counts + MOE_TILE - 1) // MOE_TILE) * MOE_TILE
    ends = jnp.cumsum(padded)
    starts = ends - padded
    pair_slot = jnp.sum(onehot * (csum - 1.0 + starts.astype(F32)[None, :]), axis=1).astype(jnp.int32)
    tile_start = jnp.arange(n_tiles, dtype=jnp.int32) * MOE_TILE
    tile_e = jnp.minimum(jnp.sum((tile_start[:, None] >= ends[None, :]).astype(jnp.int32), axis=1), N_EXPERTS - 1)
    e_onehot = tile_e[:, None] == jnp.arange(N_EXPERTS, dtype=jnp.int32)[None, :]
    filled = jnp.sum(jnp.where(e_onehot, (starts + counts)[None, :], 0), axis=1)
    tile_rows = jnp.clip(filled - tile_start, 0, MOE_TILE).astype(jnp.int32)
    last_used = jnp.maximum(ends[-1] // MOE_TILE - 1, 0)
    tile_expert = jnp.where(tile_start < ends[-1], tile_e, tile_e[last_used]).astype(jnp.int32)
    tile_block = jnp.minimum(jnp.arange(n_tiles, dtype=jnp.int32), last_used).astype(jnp.int32)
    return pair_slot, tile_expert, tile_rows, tile_block


def _swiglu(xb, wg, wu, wd):
    gate = jnp.dot(xb, wg, preferred_element_type=F32)
    up = jnp.dot(xb, wu, preferred_element_type=F32)
    a = ((gate * _sigmoid(gate)) * up).astype(BF16)
    return jnp.dot(a, wd, preferred_element_type=F32)


def _ffn_kernel(te_ref, rows_ref, blk_ref, xs_ref, wg_ref, wu_ref, wd_ref, ys_ref, wg_s, wu_s, wd_s):
    i = pl.program_id(0)
    new_expert = jnp.logical_or(i == 0, te_ref[i] != te_ref[jnp.maximum(i - 1, 0)])

    @pl.when(new_expert)
    def _():
        wg_s[...] = wg_ref[0].astype(BF16)
        wu_s[...] = wu_ref[0].astype(BF16)
        wd_s[...] = wd_ref[0].astype(BF16)

    @pl.when(rows_ref[i] > 0)
    def _():
        live = lax.broadcasted_iota(jnp.int32, (MOE_TILE, 1), 0) < rows_ref[i]
        xb = _load_unpack(xs_ref, live).astype(BF16)
        ys_ref[...] = _pack_rows(_swiglu(xb, wg_s[...], wu_s[...], wd_s[...]))


def _expert_ffn(xs, tile_expert, tile_rows, tile_block, wg, wu, wd, layer):
    n_slots = xs.shape[0]
    d = D_MODEL
    blk = pl.BlockSpec((MOE_TILE, PACK_W), lambda i, te, nr, tb: (tb[i], 0))
    return pl.pallas_call(
        _ffn_kernel,
        grid_spec=pltpu.PrefetchScalarGridSpec(
            num_scalar_prefetch=3,
            grid=(n_slots // MOE_TILE,),
            in_specs=[blk,
                      pl.BlockSpec((None, 1, d, D_EXPERT), lambda i, te, nr, tb: (layer, te[i], 0, 0)),
                      pl.BlockSpec((None, 1, d, D_EXPERT), lambda i, te, nr, tb: (layer, te[i], 0, 0)),
                      pl.BlockSpec((None, 1, D_EXPERT, d), lambda i, te, nr, tb: (layer, te[i], 0, 0))],
            out_specs=blk,
            scratch_shapes=[pltpu.VMEM((d, D_EXPERT), BF16), pltpu.VMEM((d, D_EXPERT), BF16),
                            pltpu.VMEM((D_EXPERT, d), BF16)]),
        out_shape=jax.ShapeDtypeStruct(xs.shape, jnp.int32),
        compiler_params=_params(1),
        name="expert_ffn",
    )(tile_expert, tile_rows, tile_block, xs, wg, wu, wd)


def _combine_kernel(h_ref, y1_ref, y2_ref, w_ref, sg_ref, su_ref, sd_ref, x_ref, g2_ref, gfin_ref, o_ref,
                    *, final):
    y = _swiglu(_load_unpack(h_ref).astype(BF16), sg_ref[...], su_ref[...], sd_ref[...])
    w = w_ref[...]
    y = y + w[:, 0:1] * _load_unpack(y1_ref) + w[:, 1:2] * _load_unpack(y2_ref)
    xn = x_ref[...] + g2_ref[0] * y
    if final:
        xn = _rms(xn, gfin_ref[...])
    o_ref[...] = xn


def _moe_combine(h2p, yg, w12, sg, su, sd, x2, set_off, x_off, n_rows, mod3, brow, gfin, final):
    d = x2.shape[1]
    tm = min(ROW_TILE, n_rows)
    assert set_off % tm == 0 and x_off % tm == 0 and n_rows % tm == 0
    s_blk, x_blk, n_blk = set_off // tm, x_off // tm, n_rows // tm
    const2 = lambda i: (0, 0)
    pk = lambda o: pl.BlockSpec((tm, PACK_W), lambda i: (i + o, 0))
    x_spec = pl.BlockSpec((tm, d), lambda i: (i + x_blk, 0))
    return pl.pallas_call(
        functools.partial(_combine_kernel, final=final),
        grid=(n_blk,),
        in_specs=[pk(s_blk), pk(0), pk(n_blk),
                  pl.BlockSpec((tm, LANES), lambda i: (i + x_blk, 0)),
                  pl.BlockSpec(sg.shape, const2), pl.BlockSpec(su.shape, const2), pl.BlockSpec(sd.shape, const2),
                  x_spec,
                  _mod_spec(5, lambda i: brow((i + x_blk) * tm)),
                  pl.BlockSpec((1, d), const2)],
        out_specs=x_spec,
        out_shape=jax.ShapeDtypeStruct(x2.shape, F32),
        input_output_aliases={7: 0},
        compiler_params=_params(1),
        name="moe_combine",
    )(h2p, yg, yg, w12, sg, su, sd, x2, mod3, gfin.reshape(1, d))


def _moe_routed(parts, wg, wu, wd, layer, sg, su, sd, mod3, gfin, final):
    h2p = parts[0][0] if len(parts) == 1 else jnp.concatenate([pt[0] for pt in parts], axis=0)
    route = parts[0][1] if len(parts) == 1 else jnp.concatenate([pt[1] for pt in parts], axis=1)
    t_all = h2p.shape[0]
    pair_slot, tile_expert, tile_rows, tile_block = _dispatch_plan(route)
    n_slots = tile_expert.shape[0] * MOE_TILE
    xs = _sc_scatter_rows(h2p, pair_slot, n_slots)
    ys = _expert_ffn(xs, tile_expert, tile_rows, tile_block, wg, wu, wd, layer)
    outs, set_off = [], 0
    for _, _, w12, x2, brow in parts:
        t = x2.shape[0]
        n_rows = t // COMBINE_CHUNKS if t % (COMBINE_CHUNKS * ROW_TILE) == 0 else t
        for x_off in range(0, t, n_rows):
            lo = set_off + x_off
            idx = jnp.concatenate([pair_slot[lo:lo + n_rows], pair_slot[t_all + lo:t_all + lo + n_rows]])
            yg = _sc_gather_rows(ys, idx)
            x2 = _moe_combine(h2p, yg, w12, sg, su, sd, x2, lo, x_off, n_rows, mod3, brow, gfin, final)
        outs.append(x2)
        set_off += t
    return outs


def _rope_tables(seq):
    rows = seq // GRID_W
    row = np.repeat(np.arange(rows), GRID_W).astype(np.float32)
    col = np.tile(np.arange(GRID_W), rows).astype(np.float32)
    axis_dim = QK_ROPE // 2
    inv = (ROPE_THETA ** (-np.arange(0, axis_dim, 2, dtype=np.float32) / axis_dim)).astype(np.float32)
    ang = np.concatenate([row[:, None] * inv, col[:, None] * inv], axis=-1).astype(np.float32)
    cos, sin = np.cos(ang), np.sin(ang)
    zero = np.zeros_like(cos)
    return (jnp.asarray(np.concatenate([cos, cos, zero, zero], axis=1)),
            jnp.asarray(np.concatenate([-sin, sin, zero, zero], axis=1)))


def _rope_group(w, start):
    half = QK_ROPE // 2
    x1, x2 = w[..., start:start + half], w[..., start + half:start + 2 * half]
    return jnp.concatenate([x1, x2, x2, x1], axis=-1)


def _w_in_relayout_kernel(w_ref, o_ref):
    half = QK_ROPE // 2
    x1, x2 = KV_RANK, KV_RANK + half
    pieces = [(0, KV_RANK), (KV_COLS, Q_RANK), (x1, half), (x2, half), (x2, half), (x1, half),
              (Q_END, o_ref.shape[0] - LAT_W)]
    dst = 0
    for src, n in pieces:
        o_ref[dst:dst + n, :] = w_ref[src:src + n, :].astype(BF16)
        dst += n


def _relayout_w_in(w):
    depth, d, n_in = w.shape
    tc = 256
    return pl.pallas_call(
        _w_in_relayout_kernel,
        grid=(depth, d // tc),
        in_specs=[pl.BlockSpec((None, n_in, tc), lambda l, i: (l, 0, i))],
        out_specs=pl.BlockSpec((None, P_COLS, tc), lambda l, i: (l, 0, i)),
        out_shape=jax.ShapeDtypeStruct((depth, P_COLS, d), BF16),
        compiler_params=_params(2),
        name="w_in_relayout",
    )(jnp.swapaxes(w, 1, 2))


def _relayout_w_qb(w):
    dq = QK_NOPE + QK_ROPE
    cols = []
    for h in range(N_HEADS):
        cols += [w[:, h * dq:h * dq + QK_NOPE], _rope_group(w, h * dq + QK_NOPE)]
    return jnp.concatenate(cols, axis=1).astype(BF16)


def kernel(x, c, ctx, c_ctx, w_ada, b_ada, norm_mix_g, norm_ffn_g, w_in, q_norm_g, kv_norm_g, w_qb, w_kvb, w_pool, pool_scale, w_out, w_router, router_bias, w_exp_gate, w_exp_up, w_exp_down, w_sh_gate, w_sh_up, w_sh_down, final_norm_g):
    bsz, seq, d = x.shape
    n_ctx = ctx.shape[1]
    depth = w_ada.shape[0]
    assert bsz < MOD_ROWS and d == D_MODEL

    c_rows = jnp.concatenate([c, c_ctx[None], jnp.zeros((MOD_ROWS - bsz - 1, d), F32)], axis=0)
    mod = _ada_mod(c_rows, w_ada, b_ada)

    cos_t, sin_t = _rope_tables(seq)
    ones_t = jnp.concatenate([jnp.ones((n_ctx, 2 * (QK_ROPE // 2)), F32),
                              jnp.zeros((n_ctx, LANES - QK_ROPE), F32)], axis=1)
    zeros_t = jnp.zeros((n_ctx, LANES), F32)
    wr_t = w_router.T.astype(BF16)
    w_in_r = _relayout_w_in(w_in)
    lat_row = lambda r: r // seq
    ctx_row = lambda r: bsz

    x2 = x.reshape(bsz * seq, d)
    xc2 = ctx.reshape(bsz * n_ctx, d)
    for l in range(depth):
        last = l == depth - 1
        mod3 = mod[l].reshape(MOD_ROWS * N_MOD, 1, d)
        wq = _relayout_w_qb(w_qb[l])
        w_kv3 = w_kvb[l].reshape(KV_RANK, N_HEADS, QK_NOPE + V_DIM)
        wk = w_kv3[:, :, :QK_NOPE].reshape(KV_RANK, N_HEADS * QK_NOPE).astype(BF16)
        wvt = (w_kv3[:, :, QK_NOPE:].transpose(1, 2, 0).reshape(N_HEADS * V_DIM, KV_RANK)
               .astype(BF16))
        wp = w_pool[l].astype(BF16)
        wo = w_out[l].astype(BF16)
        sg, su, sd = w_sh_gate[l].astype(BF16), w_sh_up[l].astype(BF16), w_sh_down[l].astype(BF16)
        moe = functools.partial(_moe_routed, wg=w_exp_gate, wu=w_exp_up, wd=w_exp_down, layer=l,
                                sg=sg, su=su, sd=sd, mod3=mod3, gfin=final_norm_g)

        p = _in_proj(x2, mod3, lat_row, norm_mix_g[l], w_in_r, l, P_COLS)
        pc = _in_proj(xc2, mod3, ctx_row, norm_mix_g[l], w_in_r, l, LAT_W if last else P_COLS)
        q, k_lat, v_lat = _qkv(p, bsz, seq, cos_t, sin_t, q_norm_g[l], kv_norm_g[l], wq, wk, wvt, True)
        ctx_out = _qkv(pc, bsz, n_ctx, ones_t, zeros_t, q_norm_g[l], kv_norm_g[l], wq, wk, wvt, not last)
        k_ctx, v_ctx = ctx_out[-2:]
        y = _attention(q, [k_ctx, k_lat], [v_ctx, v_lat])
        x2, h2p, route, w12 = _mixer_out(y, p, x2, mod3, lat_row, seq, wp, pool_scale[l], wo,
                                         norm_ffn_g[l], wr_t, router_bias)
        if last:
            x2, = moe([(h2p, route, w12, x2, lat_row)], final=True)
        else:
            yc = _attention(ctx_out[0], [k_ctx], [v_ctx])
            xc2, h2c, route_c, w12c = _mixer_out(yc, pc, xc2, mod3, ctx_row, n_ctx, wp, pool_scale[l], wo,
                                                 norm_ffn_g[l], wr_t, router_bias)
            x2, xc2 = moe([(h2p, route, w12, x2, lat_row), (h2c, route_c, w12c, xc2, ctx_row)], final=False)
    return x2.reshape(bsz, seq, d)
```

```python
import functools
import math

import jax
import jax.numpy as jnp
import numpy as np
from jax import lax
from jax.experimental import pallas as pl
from jax.experimental.pallas import tpu as pltpu
from jax.experimental.pallas import tpu_sc as plsc

F32 = jnp.float32
BF16 = jnp.bfloat16

D_MODEL = 2048
GRID_W = 64
N_HEADS = 16
QK_NOPE = 128
QK_ROPE = 64
V_DIM = 128
Q_RANK = 384
KV_RANK = 512
ROPE_THETA = 10000.0
POOL_DIM = 1024
POOL_GROUP_DIM = 256
POOL_WINDOWS = (2, 4, 8, 16)
KV_COLS = KV_RANK + QK_ROPE
Q_END = KV_COLS + Q_RANK
N_EXPERTS = 16
N_GROUPS = 4
EXPERTS_PER_GROUP = 4
D_EXPERT = 512
EPS = 1e-6
N_MOD = 6

LANES = 128
HEAD_W = 2 * LANES
LAT_W = KV_RANK + Q_RANK + LANES
P_COLS = LAT_W + POOL_DIM + 2 * D_MODEL
POOL_HALO = 16
MOD_ROWS = 8
Q_SCALE = (1.0 / math.sqrt(QK_NOPE + QK_ROPE)) * math.log2(math.e)
V7X_VMEM_BYTES = 64 * 1024 * 1024
VMEM_LIMIT = V7X_VMEM_BYTES - 8 * 1024 * 1024
SUBLANES = 8
PACK_W = D_MODEL // 2
ROW_CHUNKS = PACK_W // LANES
ROUTE_ROWS = SUBLANES
MOE_TILE = 512
SC_WINDOW = 128
ATTN_TQ = 256
ATTN_KB = 512
V_HEAD_GROUP = 8
ATTN_HEADS_PER_STEP = 2
ATTN_PAIRS_PER_ITER = 3
ROW_TILE = 512
IN_PROJ_COL_CHUNK = 1024
MIX_CHAIN_ROWS = 256
COMBINE_CHUNKS = 2


def _sigmoid(x):
    return 1.0 / (1.0 + jnp.exp(-x))


def _pack_rows(y):
    half = y.shape[1] // 2
    return pltpu.pack_elementwise([y[:, :half], y[:, half:]], packed_dtype=BF16)


def _load_unpack(ref, live=None):
    w = ref[...]
    if live is not None:
        w = jnp.where(live, w, 0)
    lo = pltpu.unpack_elementwise(w, index=0, packed_dtype=BF16, unpacked_dtype=F32)
    hi = pltpu.unpack_elementwise(w, index=1, packed_dtype=BF16, unpacked_dtype=F32)
    return jnp.concatenate([lo, hi], axis=1)


def _rms(xf, g):
    ms = jnp.mean(xf * xf, axis=-1, keepdims=True)
    return xf * lax.rsqrt(ms + EPS) * g


def _params(n_axes):
    return pltpu.CompilerParams(dimension_semantics=("arbitrary",) * n_axes,
                                vmem_limit_bytes=VMEM_LIMIT)


def _ada_kernel(c_ref, w_ref, b_ref, o_ref):
    cf = c_ref[...]
    a = (cf * _sigmoid(cf)).astype(BF16)
    o_ref[0] = jnp.dot(a, w_ref[0].astype(BF16), preferred_element_type=F32) + b_ref[0]


def _ada_mod(c_rows, w_ada, b_ada):
    depth, d, n = w_ada.shape
    tn = 1024
    return pl.pallas_call(
        _ada_kernel,
        grid=(depth, n // tn),
        in_specs=[pl.BlockSpec((MOD_ROWS, d), lambda l, j: (0, 0)),
                  pl.BlockSpec((1, d, tn), lambda l, j: (l, 0, j)),
                  pl.BlockSpec((1, 1, tn), lambda l, j: (l, 0, j))],
        out_specs=pl.BlockSpec((1, MOD_ROWS, tn), lambda l, j: (l, 0, j)),
        out_shape=jax.ShapeDtypeStruct((depth, MOD_ROWS, n), F32),
        compiler_params=_params(2),
        name="ada_mod",
    )(c_rows, w_ada, b_ada.reshape(depth, 1, n))


def _mod_spec(k, brow):
    return pl.BlockSpec((1, 1, D_MODEL), lambda i, *_: (brow(i) * N_MOD + k, 0, 0))


def _inproj_kernel(x_ref, g_ref, sh_ref, sc_ref, w_ref, o_ref, h_scr, *, tn):
    y = _rms(x_ref[...], g_ref[...])
    h_scr[...] = (y * (1.0 + sc_ref[0]) + sh_ref[0]).astype(BF16)
    for c0 in range(0, o_ref.shape[1], tn):
        acc = lax.dot_general(h_scr[...], w_ref[c0:c0 + tn, :], (((1,), (1,)), ((), ())),
                              preferred_element_type=F32)
        if c0 >= LAT_W + POOL_DIM:
            acc = _sigmoid(acc)
        o_ref[:, c0:c0 + tn] = acc.astype(BF16)


def _in_proj(x2, mod3, brow, g, w, n_cols):
    t, d = x2.shape
    tm = min(ROW_TILE, t)
    return pl.pallas_call(
        functools.partial(_inproj_kernel, tn=IN_PROJ_COL_CHUNK),
        grid=(t // tm,),
        in_specs=[pl.BlockSpec((tm, d), lambda i: (i, 0)),
                  pl.BlockSpec((1, d), lambda i: (0, 0)),
                  _mod_spec(0, lambda i: brow(i * tm)),
                  _mod_spec(1, lambda i: brow(i * tm)),
                  pl.BlockSpec((n_cols, d), lambda i: (0, 0), pipeline_mode=pl.Buffered(1))],
        out_specs=pl.BlockSpec((tm, n_cols), lambda i: (i, 0)),
        out_shape=jax.ShapeDtypeStruct((t, n_cols), BF16),
        scratch_shapes=[pltpu.VMEM((tm, d), BF16)],
        compiler_params=_params(1),
        name="in_proj",
    )(x2, g.reshape(1, d), mod3, mod3, w)


def _qkv_kernel(lat_ref, cos_ref, sin_ref, gq_ref, gkv_ref, wq_ref, wk_ref, wvt_ref, *out_refs, with_q):
    if with_q:
        q_ref, k_ref, v_ref = out_refs
    else:
        k_ref, v_ref = out_refs
    cos = cos_ref[...]
    sin = sin_ref[...]

    def rope(grp):
        return grp * cos + pltpu.roll(grp, 2 * (QK_ROPE // 2), axis=1) * sin

    kvn = _rms(lat_ref[:, :KV_RANK].astype(F32), gkv_ref[...]).astype(BF16)
    kpe = rope(lat_ref[:, KV_RANK + Q_RANK:].astype(F32)).astype(BF16)
    for h in range(0, N_HEADS, 2):
        kn = jnp.dot(kvn, wk_ref[:, h * QK_NOPE:(h + 2) * QK_NOPE], preferred_element_type=F32)
        for hh in range(2):
            k_ref[0, h + hh, :, :QK_NOPE] = kn[:, hh * QK_NOPE:(hh + 1) * QK_NOPE].astype(BF16)
            k_ref[0, h + hh, :, QK_NOPE:] = kpe
    for h in range(0, N_HEADS, V_HEAD_GROUP):
        vt = lax.dot_general(wvt_ref[h * V_DIM:(h + V_HEAD_GROUP) * V_DIM, :], kvn,
                             (((1,), (1,)), ((), ())), preferred_element_type=F32)
        for hh in range(V_HEAD_GROUP):
            v_ref[0, h + hh] = vt[hh * V_DIM:(hh + 1) * V_DIM, :].astype(BF16)
    if with_q:
        qn = _rms(lat_ref[:, KV_RANK:KV_RANK + Q_RANK].astype(F32), gq_ref[...]).astype(BF16)
        for h in range(N_HEADS):
            qh = jnp.dot(qn, wq_ref[:, h * HEAD_W:(h + 1) * HEAD_W], preferred_element_type=F32)
            q_ref[0, h, :, :QK_NOPE] = (qh[:, :QK_NOPE] * Q_SCALE).astype(BF16)
            q_ref[0, h, :, QK_NOPE:] = (rope(qh[:, QK_NOPE:]) * Q_SCALE).astype(BF16)


def _qkv(p, b, n, cos_t, sin_t, gq, gkv, wq, wk, wvt, with_q):
    tm = min(ROW_TILE, n)
    tpb = n // tm
    rope_tiles = cos_t.shape[0] // tm
    head_spec = pl.BlockSpec((1, N_HEADS, tm, HEAD_W), lambda i: (i // tpb, 0, i % tpb, 0))
    tab_spec = pl.BlockSpec((tm, LANES), lambda i: ((i % tpb) % rope_tiles, 0))
    out_shape = [jax.ShapeDtypeStruct((b, N_HEADS, n, HEAD_W), BF16),
                 jax.ShapeDtypeStruct((b, N_HEADS, V_DIM, n), BF16)]
    out_specs = [head_spec,
                 pl.BlockSpec((1, N_HEADS, V_DIM, tm), lambda i: (i // tpb, 0, 0, i % tpb))]
    if with_q:
        out_shape = [jax.ShapeDtypeStruct((b, N_HEADS, n, HEAD_W), BF16)] + out_shape
        out_specs = [head_spec] + out_specs
    return pl.pallas_call(
        functools.partial(_qkv_kernel, with_q=with_q),
        grid=(b * tpb,),
        in_specs=[pl.BlockSpec((tm, LAT_W), lambda i: (i, 0)),
                  tab_spec, tab_spec,
                  pl.BlockSpec((1, Q_RANK), lambda i: (0, 0)),
                  pl.BlockSpec((1, KV_RANK), lambda i: (0, 0)),
                  pl.BlockSpec(wq.shape, lambda i: (0, 0)),
                  pl.BlockSpec(wk.shape, lambda i: (0, 0)),
                  pl.BlockSpec(wvt.shape, lambda i: (0, 0))],
        out_specs=out_specs,
        out_shape=out_shape,
        compiler_params=_params(1),
        name="qkv",
    )(p, cos_t, sin_t, gq.reshape(1, Q_RANK), gkv.reshape(1, KV_RANK), wq, wk, wvt)


def _row_fold(x, op):
    parts = [x[r:r + SUBLANES] for r in range(0, x.shape[0], SUBLANES)]
    a, b = parts[0], parts[1]
    for i in range(2, len(parts) - 1, 2):
        a, b = op(a, parts[i]), op(b, parts[i + 1])
    if len(parts) % 2:
        a = op(a, parts[-1])
    return op(a, b)


def _attn_kernel(q_ref, *refs, n_seg, kb, tq):
    k_refs, vt_refs = refs[:n_seg], refs[n_seg:2 * n_seg]
    o_ref, s_a, s_b, m_a, m_b = refs[2 * n_seg:]
    chunks = []
    off = 0
    for kr, vr in zip(k_refs, vt_refs):
        nk = kr.shape[2]
        for c0 in range(0, nk, kb):
            n = min(kb, nk - c0)
            chunks.append((kr, vr, c0, n, off))
            off += n

    n_heads, nt = q_ref.shape[1], q_ref.shape[2] // tq
    bufs = ((s_a, m_a), (s_b, m_b))

    def tile_rows(t):
        return pl.ds(pl.multiple_of(t * tq, tq), tq)

    def scores(g, t, s_buf, m_buf):
        rows = tile_rows(t)
        q = q_ref[0, g, rows, :]
        mp = None
        for kr, vr, c0, n, o in chunks:
            s = lax.dot_general(kr[0, g, c0:c0 + n, :], q, (((1,), (1,)), ((), ())),
                                preferred_element_type=F32)
            s_buf[o:o + n, :] = s
            m = _row_fold(s, jnp.maximum)
            mp = m if mp is None else jnp.maximum(mp, m)
        m_buf[...] = mp

    def values(g, t, s_buf, m_buf):
        rows = tile_rows(t)
        mrow = jnp.max(m_buf[...], axis=0, keepdims=True)
        lp = None
        acc = None
        for kr, vr, c0, n, o in chunks:
            pr = jnp.exp2(s_buf[o:o + n, :] - mrow)
            ls = _row_fold(pr, jnp.add)
            lp = ls if lp is None else lp + ls
            pv = jnp.dot(vr[0, g, :, c0:c0 + n], pr.astype(BF16), preferred_element_type=F32)
            acc = pv if acc is None else acc + pv
        l = jnp.sum(lp, axis=0, keepdims=True)
        o_ref[0, g, rows, :] = (acc / l).T.astype(BF16)

    def phase(g, t, parity):
        scores(g, t + 1, *bufs[1 - parity])
        values(g, t, *bufs[parity])

    scores(0, 0, *bufs[0])
    for g in range(n_heads):
        par0 = (g * nt) % 2
        n_pairs = (nt - 1) // 2

        def pair(j, g=g, par0=par0):
            phase(g, 2 * j, par0)
            phase(g, 2 * j + 1, 1 - par0)

        def pairs(jj, carry, pair=pair):
            for k in range(ATTN_PAIRS_PER_ITER):
                pair(jj * ATTN_PAIRS_PER_ITER + k)
            return carry

        n_iter = n_pairs // ATTN_PAIRS_PER_ITER
        if n_iter:
            lax.fori_loop(0, n_iter, pairs, 0)
        for j in range(n_iter * ATTN_PAIRS_PER_ITER, n_pairs):
            pair(j)
        if (nt - 1) % 2:
            phase(g, nt - 2, (par0 + nt - 2) % 2)
        last = bufs[(par0 + nt - 1) % 2]
        if g + 1 < n_heads:
            scores(g + 1, 0, *bufs[((g + 1) * nt) % 2])
        values(g, nt - 1, *last)


def _attention(q, ks, vts):
    b, hh, n, _ = q.shape
    tq = min(ATTN_TQ, n)
    hg = ATTN_HEADS_PER_STEP if n > tq else hh
    assert hh % hg == 0
    nk_total = sum(k.shape[2] for k in ks)
    seg_spec = lambda a: pl.BlockSpec((1, hg) + a.shape[2:], lambda bi, h: (bi, h, 0, 0))
    return pl.pallas_call(
        functools.partial(_attn_kernel, n_seg=len(ks), kb=ATTN_KB, tq=tq),
        grid=(b, hh // hg),
        in_specs=[pl.BlockSpec((1, hg, n, HEAD_W), lambda bi, h: (bi, h, 0, 0))]
                 + [seg_spec(a) for a in ks] + [seg_spec(a) for a in vts],
        out_specs=pl.BlockSpec((1, hg, n, V_DIM), lambda bi, h: (bi, h, 0, 0)),
        out_shape=jax.ShapeDtypeStruct((b, hh, n, V_DIM), BF16),
        scratch_shapes=[pltpu.VMEM((nk_total, tq), F32), pltpu.VMEM((nk_total, tq), F32),
                        pltpu.VMEM((SUBLANES, tq), F32), pltpu.VMEM((SUBLANES, tq), F32)],
        compiler_params=_params(2),
        name="attention",
    )(q, *ks, *vts)


def _top2sum(a, b, c, d):
    s1, t1 = jnp.maximum(a, b), jnp.minimum(a, b)
    s2, t2 = jnp.maximum(c, d), jnp.minimum(c, d)
    return jnp.maximum(s1, s2) + jnp.maximum(jnp.minimum(s1, s2), jnp.maximum(t1, t2))


def _route(logits_t, bias):
    sc = _sigmoid(logits_t)
    sel = sc + bias
    sel_r = [sel[e:e + 1, :] for e in range(N_EXPERTS)]
    sc_r = [sc[e:e + 1, :] for e in range(N_EXPERTS)]
    epg = EXPERTS_PER_GROUP
    gs = [_top2sum(*sel_r[g * epg:(g + 1) * epg]) for g in range(N_GROUPS)]
    best, gi = gs[0], jnp.zeros(gs[0].shape, jnp.int32)
    for g in range(1, N_GROUPS):
        upd = gs[g] > best
        best = jnp.where(upd, gs[g], best)
        gi = jnp.where(upd, g, gi)

    def pick_group(rows, k):
        r = rows[k]
        for g in range(1, N_GROUPS):
            r = jnp.where(gi == g, rows[g * epg + k], r)
        return r

    in_sel = [pick_group(sel_r, k) for k in range(epg)]
    in_sc = [pick_group(sc_r, k) for k in range(epg)]
    b1, i1 = in_sel[0], jnp.zeros(gi.shape, jnp.int32)
    for k in range(1, epg):
        upd = in_sel[k] > b1
        b1 = jnp.where(upd, in_sel[k], b1)
        i1 = jnp.where(upd, k, i1)
    b2, i2 = None, None
    for k in range(epg):
        cand = jnp.where(i1 == k, -jnp.inf, in_sel[k])
        if b2 is None:
            b2, i2 = cand, jnp.zeros(gi.shape, jnp.int32)
        else:
            upd = cand > b2
            b2 = jnp.where(upd, cand, b2)
            i2 = jnp.where(upd, k, i2)

    def pick_local(idx):
        r = in_sc[0]
        for k in range(1, epg):
            r = jnp.where(idx == k, in_sc[k], r)
        return r

    s1, s2 = pick_local(i1), pick_local(i2)
    den = s1 + s2
    return gi * epg + i1, gi * epg + i2, s1 / den, s2 / den


def _mixout_kernel(y_ref, u_ref, up_ref, un_ref, gm_ref, gp_ref, wp_ref, ps_ref, wo_ref, x_ref,
                   g1_ref, gf_ref, sh2_ref, sc2_ref, wr_ref, rb_ref,
                   xo_ref, h2_ref, route_ref, w12_ref, *, tm, tpb, n_seq):
    i = pl.program_id(0)
    uext = jnp.concatenate([up_ref[...], u_ref[...], un_ref[...]], axis=0)
    sub = min(MIX_CHAIN_ROWS, tm)

    def mix_stage(r0):
        rows = slice(r0, r0 + sub)
        base = (i % tpb) * tm + r0
        r = lax.broadcasted_iota(jnp.int32, (sub, sub + 2 * POOL_HALO), 0)
        c = lax.broadcasted_iota(jnp.int32, (sub, sub + 2 * POOL_HALO), 1)
        rel = c - POOL_HALO - r
        jpos = base - POOL_HALO + c
        valid = (jpos >= 0) & (jpos < n_seq)
        tpos = base + lax.broadcasted_iota(jnp.int32, (sub, 1), 0)
        uwin = uext[r0:r0 + sub + 2 * POOL_HALO]
        uc = uext[r0 + POOL_HALO:r0 + POOL_HALO + sub]
        parts = []
        for g, w in enumerate(POOL_WINDOWS):
            sl = slice(g * POOL_GROUP_DIM, (g + 1) * POOL_GROUP_DIM)
            band = jnp.where(valid & (rel >= -(w // 2)) & (rel < w // 2), 1.0, 0.0).astype(BF16)
            cnt = jnp.clip(tpos + w // 2, 0, n_seq) - jnp.clip(tpos - w // 2, 0, n_seq)
            wsum = jnp.dot(band, uwin[:, sl], preferred_element_type=F32)
            z = wsum / cnt.astype(F32) - uc[:, sl].astype(F32)
            parts.append(jnp.dot(z.astype(BF16), wp_ref[g], preferred_element_type=F32))
        ypool = jnp.concatenate(parts, axis=1) * ps_ref[...]
        y_mla = jnp.concatenate([y_ref[0, h, rows, :] for h in range(N_HEADS)], axis=1)
        mix = gm_ref[rows, :].astype(F32) * y_mla.astype(F32) + gp_ref[rows, :].astype(F32) * ypool
        return jnp.dot(mix.astype(BF16), wo_ref[...], preferred_element_type=F32)

    def norm_route_stage(r0, y):
        rows = slice(r0, r0 + sub)
        xn = x_ref[rows, :] + g1_ref[0] * y
        xo_ref[rows, :] = xn
        h2f = _rms(xn, gf_ref[...]) * (1.0 + sc2_ref[0]) + sh2_ref[0]
        h2_ref[rows, :] = _pack_rows(h2f)
        logits_t = lax.dot_general(wr_ref[...], h2f.astype(BF16), (((1,), (1,)), ((), ())),
                                   preferred_element_type=F32)
        e1, e2, w1, w2 = _route(logits_t, rb_ref[...])
        zero = jnp.zeros((ROUTE_ROWS - 4, sub), F32)
        route_ref[:, rows] = jnp.concatenate([e1.astype(F32), e2.astype(F32), w1, w2, zero], axis=0)
        w12_ref[rows, :] = jnp.concatenate([w1, w2, jnp.zeros((LANES - 2, sub), F32)], axis=0).T

    starts = list(range(0, tm, sub))
    pending = None
    for r0 in starts:
        y = mix_stage(r0)
        if pending is not None:
            norm_route_stage(*pending)
        pending = (r0, y)
    norm_route_stage(*pending)


def _mixer_out(y, p, x2, mod3, brow, n_seq, wp, ps, wo, gf, wr_t, rb):
    t, d = x2.shape
    tm = min(2 * MIX_CHAIN_ROWS, n_seq)
    tpb = n_seq // tm
    hpt = tm // POOL_HALO
    n_halo = t // POOL_HALO
    row = lambda i: (i, 0)
    const2 = lambda i: (0, 0)
    mrow = lambda i: brow(i * tm)
    return pl.pallas_call(
        functools.partial(_mixout_kernel, tm=tm, tpb=tpb, n_seq=n_seq),
        grid=(t // tm,),
        in_specs=[pl.BlockSpec((1, N_HEADS, tm, V_DIM), lambda i: (i // tpb, 0, i % tpb, 0)),
                  pl.BlockSpec((tm, POOL_DIM), lambda i: (i, LAT_W // POOL_DIM)),
                  pl.BlockSpec((POOL_HALO, POOL_DIM),
                               lambda i: (jnp.maximum(i * hpt - 1, 0), LAT_W // POOL_DIM)),
                  pl.BlockSpec((POOL_HALO, POOL_DIM),
                               lambda i: (jnp.minimum((i + 1) * hpt, n_halo - 1), LAT_W // POOL_DIM)),
                  pl.BlockSpec((tm, d), lambda i: (i, (LAT_W + POOL_DIM) // d)),
                  pl.BlockSpec((tm, d), lambda i: (i, (LAT_W + POOL_DIM) // d + 1)),
                  pl.BlockSpec(wp.shape, lambda i: (0, 0, 0)),
                  pl.BlockSpec((1, d), const2),
                  pl.BlockSpec((d, d), const2, pipeline_mode=pl.Buffered(1)),
                  pl.BlockSpec((tm, d), row),
                  _mod_spec(2, mrow),
                  pl.BlockSpec((1, d), const2),
                  _mod_spec(3, mrow),
                  _mod_spec(4, mrow),
                  pl.BlockSpec((N_EXPERTS, d), const2),
                  pl.BlockSpec((N_EXPERTS, 1), const2)],
        out_specs=[pl.BlockSpec((tm, d), row),
                   pl.BlockSpec((tm, PACK_W), row),
                   pl.BlockSpec((ROUTE_ROWS, tm), lambda i: (0, i)),
                   pl.BlockSpec((tm, LANES), row)],
        out_shape=[jax.ShapeDtypeStruct((t, d), F32),
                   jax.ShapeDtypeStruct((t, PACK_W), jnp.int32),
                   jax.ShapeDtypeStruct((ROUTE_ROWS, t), F32),
                   jax.ShapeDtypeStruct((t, LANES), F32)],
        compiler_params=_params(1),
        name="mixer_out",
    )(y, p, p, p, p, p, wp, ps.reshape(1, d), wo, x2, mod3, gf.reshape(1, d), mod3, mod3, wr_t,
      rb.reshape(N_EXPERTS, 1))


def _chunk_view(x):
    n = x.shape[0]
    return (x.reshape(n // SUBLANES, SUBLANES, ROW_CHUNKS, LANES).transpose(0, 2, 1, 3)
            .reshape(n * ROW_CHUNKS, LANES))


def _row_view(c):
    n = c.shape[0] // ROW_CHUNKS
    return (c.reshape(n // SUBLANES, ROW_CHUNKS, SUBLANES, LANES).transpose(0, 2, 1, 3)
            .reshape(n, PACK_W))


def _sc_index_chunks(idx):
    m = idx.shape[0]
    per_row = LANES // ROW_CHUNKS
    assert m % per_row == 0
    lane = np.arange(LANES)
    src = (lane // (SUBLANES * ROW_CHUNKS)) * SUBLANES + lane % SUBLANES
    sel = jnp.asarray(np.equal.outer(np.arange(per_row), src), BF16)
    rows = idx.reshape(m // per_row, per_row)
    pick = lambda v: jnp.dot(v.astype(BF16), sel, preferred_element_type=F32)
    r = (pick(rows >> 8) * 256.0 + pick(rows & 255)).astype(jnp.int32)
    j = jnp.asarray((lane % (SUBLANES * ROW_CHUNKS)) // SUBLANES, jnp.int32)[None, :]
    ids = (r // SUBLANES) * (SUBLANES * ROW_CHUNKS) + j * SUBLANES + r % SUBLANES
    return ids.reshape(1, m * ROW_CHUNKS)


def _sc_mesh():
    return plsc.VectorSubcoreMesh(core_axis_name="core", subcore_axis_name="subcore")


def _sc_gather_rows(table, idx):
    n, m = table.shape[0], idx.shape[0]
    n_idx = m * ROW_CHUNKS
    assert n_idx % SC_WINDOW == 0 and n % SUBLANES == 0 and m % SUBLANES == 0

    @pl.kernel(out_type=jax.ShapeDtypeStruct((n_idx, LANES), jnp.int32), mesh=_sc_mesh())
    def gather(t_hbm, i_hbm, o_hbm):
        def body(i_vmem, o_vmem):
            pltpu.sync_copy(t_hbm.at[i_vmem.at[0]], o_vmem)

        pltpu.emit_pipeline(
            body,
            grid=(n_idx // SC_WINDOW,),
            in_specs=[pl.BlockSpec((1, SC_WINDOW), index_map=lambda i: (0, i))],
            out_specs=[pl.BlockSpec((SC_WINDOW, LANES), index_map=lambda i: (i, 0))],
            core_axis_name=("core", "subcore"),
            dimension_semantics=(pltpu.PARALLEL,),
        )(i_hbm, o_hbm)

    return _row_view(gather(_chunk_view(table), _sc_index_chunks(idx)))


def _sc_scatter_rows(rows, idx, n_out):
    n, m = rows.shape[0], idx.shape[0]
    n_idx = m * ROW_CHUNKS
    src_steps = n * ROW_CHUNKS // SC_WINDOW
    assert n_idx % SC_WINDOW == 0 and (n * ROW_CHUNKS) % SC_WINDOW == 0 and m % n == 0
    assert n % SUBLANES == 0 and n_out % SUBLANES == 0

    @pl.kernel(out_type=jax.ShapeDtypeStruct((n_out * ROW_CHUNKS, LANES), jnp.int32), mesh=_sc_mesh())
    def scatter(x_hbm, i_hbm, o_hbm):
        def body(x_vmem, i_vmem):
            pltpu.sync_copy(x_vmem, o_hbm.at[i_vmem.at[0]])

        pltpu.emit_pipeline(
            body,
            grid=(n_idx // SC_WINDOW,),
            in_specs=[pl.BlockSpec((SC_WINDOW, LANES), index_map=lambda i: (i % src_steps, 0)),
                      pl.BlockSpec((1, SC_WINDOW), index_map=lambda i: (0, i))],
            out_specs=[],
            core_axis_name=("core", "subcore"),
            dimension_semantics=(pltpu.PARALLEL,),
        )(x_hbm, i_hbm)

    return _row_view(scatter(_chunk_view(rows), _sc_index_chunks(idx)))


def _dispatch_plan(route):
    t = route.shape[1]
    n_pairs = 2 * t
    n_tiles = n_pairs // MOE_TILE + N_EXPERTS
    blk = LANES
    e = jnp.concatenate([route[0], route[1]]).astype(jnp.int32)
    onehot = (e[:, None] == jnp.arange(N_EXPERTS, dtype=jnp.int32)[None, :]).astype(F32)
    oh3 = onehot.reshape(n_pairs // blk, blk, N_EXPERTS)
    tri = (jnp.arange(blk)[:, None] >= jnp.arange(blk)[None, :]).astype(F32)
    within = jnp.einsum("ij,bjk->bik", tri, oh3)
    totals = within[:, -1, :]
    before = jnp.cumsum(totals, axis=0) - totals
    csum = (within + before[:, None, :]).reshape(n_pairs, N_EXPERTS)
    counts = jnp.sum(totals, axis=0).astype(jnp.int32)
    padded = ((counts + MOE_TILE - 1) // MOE_TILE) * MOE_TILE
    ends = jnp.cumsum(padded)
    starts = ends - padded
    pair_slot = jnp.sum(onehot * (csum - 1.0 + starts.astype(F32)[None, :]), axis=1).astype(jnp.int32)
    tile_start = jnp.arange(n_tiles, dtype=jnp.int32) * MOE_TILE
    tile_e = jnp.minimum(jnp.sum((tile_start[:, None] >= ends[None, :]).astype(jnp.int32), axis=1), N_EXPERTS - 1)
    e_onehot = tile_e[:, None] == jnp.arange(N_EXPERTS, dtype=jnp.int32)[None, :]
    filled = jnp.sum(jnp.where(e_onehot, (starts + counts)[None, :], 0), axis=1)
    tile_rows = jnp.clip(filled - tile_start, 0, MOE_TILE).astype(jnp.int32)
    last_used = jnp.maximum(ends[-1] // MOE_TILE - 1, 0)
    tile_expert = jnp.where(tile_start < ends[-1], tile_e, tile_e[last_used]).astype(jnp.int32)
    tile_block = jnp.minimum(jnp.arange(n_tiles, dtype=jnp.int32), last_used).astype(jnp.int32)
    return pair_slot, tile_expert, tile_rows, tile_block


def _swiglu(xb, wg, wu, wd):
    gate = jnp.dot(xb, wg, preferred_element_type=F32)
    up = jnp.dot(xb, wu, preferred_element_type=F32)
    a = ((gate * _sigmoid(gate)) * up).astype(BF16)
    return jnp.dot(a, wd, preferred_element_type=F32)


def _ffn_kernel(te_ref, rows_ref, blk_ref, xs_ref, wg_ref, wu_ref, wd_ref, ys_ref, wg_s, wu_s, wd_s):
    i = pl.program_id(0)
    new_expert = jnp.logical_or(i == 0, te_ref[i] != te_ref[jnp.maximum(i - 1, 0)])

    @pl.when(new_expert)
    def _():
        wg_s[...] = wg_ref[0].astype(BF16)
        wu_s[...] = wu_ref[0].astype(BF16)
        wd_s[...] = wd_ref[0].astype(BF16)

    @pl.when(rows_ref[i] > 0)
    def _():
        live = lax.broadcasted_iota(jnp.int32, (MOE_TILE, 1), 0) < rows_ref[i]
        xb = _load_unpack(xs_ref, live).astype(BF16)
        ys_ref[...] = _pack_rows(_swiglu(xb, wg_s[...], wu_s[...], wd_s[...]))


def _expert_ffn(xs, tile_expert, tile_rows, tile_block, wg, wu, wd, layer):
    n_slots = xs.shape[0]
    d = D_MODEL
    blk = pl.BlockSpec((MOE_TILE, PACK_W), lambda i, te, nr, tb: (tb[i], 0))
    return pl.pallas_call(
        _ffn_kernel,
        grid_spec=pltpu.PrefetchScalarGridSpec(
            num_scalar_prefetch=3,
            grid=(n_slots // MOE_TILE,),
            in_specs=[blk,
                      pl.BlockSpec((None, 1, d, D_EXPERT), lambda i, te, nr, tb: (layer, te[i], 0, 0)),
                      pl.BlockSpec((None, 1, d, D_EXPERT), lambda i, te, nr, tb: (layer, te[i], 0, 0)),
                      pl.BlockSpec((None, 1, D_EXPERT, d), lambda i, te, nr, tb: (layer, te[i], 0, 0))],
            out_specs=blk,
            scratch_shapes=[pltpu.VMEM((d, D_EXPERT), BF16), pltpu.VMEM((d, D_EXPERT), BF16),
                            pltpu.VMEM((D_EXPERT, d), BF16)]),
        out_shape=jax.ShapeDtypeStruct(xs.shape, jnp.int32),
        compiler_params=_params(1),
        name="expert_ffn",
    )(tile_expert, tile_rows, tile_block, xs, wg, wu, wd)


def _combine_kernel(h_ref, y1_ref, y2_ref, w_ref, sg_ref, su_ref, sd_ref, x_ref, g2_ref, gfin_ref, o_ref,
                    *, final):
    y = _swiglu(_load_unpack(h_ref).astype(BF16), sg_ref[...], su_ref[...], sd_ref[...])
    w = w_ref[...]
    y = y + w[:, 0:1] * _load_unpack(y1_ref) + w[:, 1:2] * _load_unpack(y2_ref)
    xn = x_ref[...] + g2_ref[0] * y
    if final:
        xn = _rms(xn, gfin_ref[...])
    o_ref[...] = xn


def _moe_combine(h2p, yg, w12, sg, su, sd, x2, set_off, x_off, n_rows, mod3, brow, gfin, final):
    d = x2.shape[1]
    tm = min(ROW_TILE, n_rows)
    assert set_off % tm == 0 and x_off % tm == 0 and n_rows % tm == 0
    s_blk, x_blk, n_blk = set_off // tm, x_off // tm, n_rows // tm
    const2 = lambda i: (0, 0)
    pk = lambda o: pl.BlockSpec((tm, PACK_W), lambda i: (i + o, 0))
    x_spec = pl.BlockSpec((tm, d), lambda i: (i + x_blk, 0))
    return pl.pallas_call(
        functools.partial(_combine_kernel, final=final),
        grid=(n_blk,),
        in_specs=[pk(s_blk), pk(0), pk(n_blk),
                  pl.BlockSpec((tm, LANES), lambda i: (i + x_blk, 0)),
                  pl.BlockSpec(sg.shape, const2), pl.BlockSpec(su.shape, const2), pl.BlockSpec(sd.shape, const2),
                  x_spec,
                  _mod_spec(5, lambda i: brow((i + x_blk) * tm)),
                  pl.BlockSpec((1, d), const2)],
        out_specs=x_spec,
        out_shape=jax.ShapeDtypeStruct(x2.shape, F32),
        input_output_aliases={7: 0},
        compiler_params=_params(1),
        name="moe_combine",
    )(h2p, yg, yg, w12, sg, su, sd, x2, mod3, gfin.reshape(1, d))


def _moe_routed(parts, wg, wu, wd, layer, sg, su, sd, mod3, gfin, final):
    h2p = parts[0][0] if len(parts) == 1 else jnp.concatenate([pt[0] for pt in parts], axis=0)
    route = parts[0][1] if len(parts) == 1 else jnp.concatenate([pt[1] for pt in parts], axis=1)
    t_all = h2p.shape[0]
    pair_slot, tile_expert, tile_rows, tile_block = _dispatch_plan(route)
    n_slots = tile_expert.shape[0] * MOE_TILE
    xs = _sc_scatter_rows(h2p, pair_slot, n_slots)
    ys = _expert_ffn(xs, tile_expert, tile_rows, tile_block, wg, wu, wd, layer)
    outs, set_off = [], 0
    for _, _, w12, x2, brow in parts:
        t = x2.shape[0]
        n_rows = t // COMBINE_CHUNKS if t % (COMBINE_CHUNKS * ROW_TILE) == 0 else t
        for x_off in range(0, t, n_rows):
            lo = set_off + x_off
            idx = jnp.concatenate([pair_slot[lo:lo + n_rows], pair_slot[t_all + lo:t_all + lo + n_rows]])
            yg = _sc_gather_rows(ys, idx)
            x2 = _moe_combine(h2p, yg, w12, sg, su, sd, x2, lo, x_off, n_rows, mod3, brow, gfin, final)
        outs.append(x2)
        set_off += t
    return outs


def _rope_tables(seq):
    rows = seq // GRID_W
    row = np.repeat(np.arange(rows), GRID_W).astype(np.float32)
    col = np.tile(np.arange(GRID_W), rows).astype(np.float32)
    axis_dim = QK_ROPE // 2
    inv = (ROPE_THETA ** (-np.arange(0, axis_dim, 2, dtype=np.float32) / axis_dim)).astype(np.float32)
    ang = np.concatenate([row[:, None] * inv, col[:, None] * inv], axis=-1).astype(np.float32)
    cos, sin = np.cos(ang), np.sin(ang)
    zero = np.zeros_like(cos)
    return (jnp.asarray(np.concatenate([cos, cos, zero, zero], axis=1)),
            jnp.asarray(np.concatenate([-sin, sin, zero, zero], axis=1)))


def _rope_group(w, start):
    half = QK_ROPE // 2
    x1, x2 = w[..., start:start + half], w[..., start + half:start + 2 * half]
    return jnp.concatenate([x1, x2, x2, x1], axis=-1)


def _w_in_relayout_kernel(w_ref, o_ref):
    half = QK_ROPE // 2
    x1, x2 = KV_RANK, KV_RANK + half
    pieces = [(0, KV_RANK), (KV_COLS, Q_RANK), (x1, half), (x2, half), (x2, half), (x1, half),
              (Q_END, o_ref.shape[0] - LAT_W)]
    dst = 0
    for src, n in pieces:
        o_ref[dst:dst + n, :] = w_ref[src:src + n, :].astype(BF16)
        dst += n


def _relayout_w_in(w, layer):
    _, d, n_in = w.shape
    tc = 256
    return pl.pallas_call(
        _w_in_relayout_kernel,
        grid=(d // tc,),
        in_specs=[pl.BlockSpec((None, n_in, tc), lambda i: (layer, 0, i))],
        out_specs=pl.BlockSpec((P_COLS, tc), lambda i: (0, i)),
        out_shape=jax.ShapeDtypeStruct((P_COLS, d), BF16),
        compiler_params=_params(1),
        name="w_in_relayout",
    )(jnp.swapaxes(w, 1, 2))


def _relayout_w_qb(w):
    dq = QK_NOPE + QK_ROPE
    cols = []
    for h in range(N_HEADS):
        cols += [w[:, h * dq:h * dq + QK_NOPE], _rope_group(w, h * dq + QK_NOPE)]
    return jnp.concatenate(cols, axis=1).astype(BF16)


def kernel(x, c, ctx, c_ctx, w_ada, b_ada, norm_mix_g, norm_ffn_g, w_in, q_norm_g, kv_norm_g, w_qb, w_kvb, w_pool, pool_scale, w_out, w_router, router_bias, w_exp_gate, w_exp_up, w_exp_down, w_sh_gate, w_sh_up, w_sh_down, final_norm_g):
    bsz, seq, d = x.shape
    n_ctx = ctx.shape[1]
    depth = w_ada.shape[0]
    assert bsz < MOD_ROWS and d == D_MODEL

    c_rows = jnp.concatenate([c, c_ctx[None], jnp.zeros((MOD_ROWS - bsz - 1, d), F32)], axis=0)
    mod = _ada_mod(c_rows, w_ada, b_ada)

    cos_t, sin_t = _rope_tables(seq)
    ones_t = jnp.concatenate([jnp.ones((n_ctx, 2 * (QK_ROPE // 2)), F32),
                              jnp.zeros((n_ctx, LANES - QK_ROPE), F32)], axis=1)
    zeros_t = jnp.zeros((n_ctx, LANES), F32)
    wr_t = w_router.T.astype(BF16)
    lat_row = lambda r: r // seq
    ctx_row = lambda r: bsz

    x2 = x.reshape(bsz * seq, d)
    xc2 = ctx.reshape(bsz * n_ctx, d)
    for l in range(depth):
        last = l == depth - 1
        mod3 = mod[l].reshape(MOD_ROWS * N_MOD, 1, d)
        wq = _relayout_w_qb(w_qb[l])
        w_kv3 = w_kvb[l].reshape(KV_RANK, N_HEADS, QK_NOPE + V_DIM)
        wk = w_kv3[:, :, :QK_NOPE].reshape(KV_RANK, N_HEADS * QK_NOPE).astype(BF16)
        wvt = (w_kv3[:, :, QK_NOPE:].transpose(1, 2, 0).reshape(N_HEADS * V_DIM, KV_RANK)
               .astype(BF16))
        wp = w_pool[l].astype(BF16)
        wo = w_out[l].astype(BF16)
        sg, su, sd = w_sh_gate[l].astype(BF16), w_sh_up[l].astype(BF16), w_sh_down[l].astype(BF16)
        moe = functools.partial(_moe_routed, wg=w_exp_gate, wu=w_exp_up, wd=w_exp_down, layer=l,
                                sg=sg, su=su, sd=sd, mod3=mod3, gfin=final_norm_g)

        w_in_r = _relayout_w_in(w_in, l)
        p = _in_proj(x2, mod3, lat_row, norm_mix_g[l], w_in_r, P_COLS)
        pc = _in_proj(xc2, mod3, ctx_row, norm_mix_g[l], w_in_r, LAT_W if last else P_COLS)
        q, k_lat, v_lat = _qkv(p, bsz, seq, cos_t, sin_t, q_norm_g[l], kv_norm_g[l], wq, wk, wvt, True)
        ctx_out = _qkv(pc, bsz, n_ctx, ones_t, zeros_t, q_norm_g[l], kv_norm_g[l], wq, wk, wvt, not last)
        k_ctx, v_ctx = ctx_out[-2:]
        y = _attention(q, [k_ctx, k_lat], [v_ctx, v_lat])
        x2, h2p, route, w12 = _mixer_out(y, p, x2, mod3, lat_row, seq, wp, pool_scale[l], wo,
                                         norm_ffn_g[l], wr_t, router_bias)
        if last:
            x2, = moe([(h2p, route, w12, x2, lat_row)], final=True)
        else:
            yc = _attention(ctx_out[0], [k_ctx], [v_ctx])
            xc2, h2c, route_c, w12c = _mixer_out(yc, pc, xc2, mod3, ctx_row, n_ctx, wp, pool_scale[l], wo,
                                                 norm_ffn_g[l], wr_t, router_bias)
            x2, xc2 = moe([(h2p, route, w12, x2, lat_row), (h2c, route_c, w12c, xc2, ctx_row)], final=False)
    return x2.reshape(bsz, seq, d)
```

```python
import functools
import math

import jax
import jax.numpy as jnp
import numpy as np
from jax import lax
from jax.experimental import pallas as pl
from jax.experimental.pallas import tpu as pltpu
from jax.experimental.pallas import tpu_sc as plsc

F32 = jnp.float32
BF16 = jnp.bfloat16

D_MODEL = 2048
GRID_W = 64
N_HEADS = 16
QK_NOPE = 128
QK_ROPE = 64
V_DIM = 128
Q_RANK = 384
KV_RANK = 512
ROPE_THETA = 10000.0
POOL_DIM = 1024
POOL_GROUP_DIM = 256
POOL_WINDOWS = (2, 4, 8, 16)
KV_COLS = KV_RANK + QK_ROPE
Q_END = KV_COLS + Q_RANK
N_EXPERTS = 16
N_GROUPS = 4
EXPERTS_PER_GROUP = 4
D_EXPERT = 512
EPS = 1e-6
N_MOD = 6

LANES = 128
HEAD_W = 2 * LANES
LAT_W = KV_RANK + Q_RANK + LANES
P_COLS = LAT_W + POOL_DIM + 2 * D_MODEL
POOL_HALO = 16
MOD_ROWS = 8
Q_SCALE = (1.0 / math.sqrt(QK_NOPE + QK_ROPE)) * math.log2(math.e)
V7X_VMEM_BYTES = 64 * 1024 * 1024
VMEM_LIMIT = V7X_VMEM_BYTES - 8 * 1024 * 1024
SUBLANES = 8
PACK_W = D_MODEL // 2
ROW_CHUNKS = PACK_W // LANES
ROUTE_ROWS = SUBLANES
MOE_TILE = 512
SC_WINDOW = 128
MAX_INDEXED_ROWS = 1 << 16
ATTN_TQ = 256
ATTN_KB = 512
V_HEAD_GROUP = 8
ATTN_HEADS_PER_STEP = 2
ATTN_PAIRS_PER_ITER = 3
ROW_TILE = 512
IN_PROJ_COL_CHUNK = 1024
MIX_CHAIN_ROWS = 256
COMBINE_CHUNKS = 2


def _sigmoid(x):
    return 1.0 / (1.0 + jnp.exp(-x))


def _pack_rows(y):
    half = y.shape[1] // 2
    return pltpu.pack_elementwise([y[:, :half], y[:, half:]], packed_dtype=BF16)


def _load_unpack(ref, live=None):
    w = ref[...]
    if live is not None:
        w = jnp.where(live, w, 0)
    lo = pltpu.unpack_elementwise(w, index=0, packed_dtype=BF16, unpacked_dtype=F32)
    hi = pltpu.unpack_elementwise(w, index=1, packed_dtype=BF16, unpacked_dtype=F32)
    return jnp.concatenate([lo, hi], axis=1)


def _rms(xf, g):
    ms = jnp.mean(xf * xf, axis=-1, keepdims=True)
    return xf * lax.rsqrt(ms + EPS) * g


def _params(n_axes):
    return pltpu.CompilerParams(dimension_semantics=("arbitrary",) * n_axes,
                                vmem_limit_bytes=VMEM_LIMIT)


def _ada_kernel(c_ref, w_ref, b_ref, o_ref):
    cf = c_ref[...]
    a = (cf * _sigmoid(cf)).astype(BF16)
    o_ref[...] = jnp.dot(a, w_ref[...].astype(BF16), preferred_element_type=F32) + b_ref[0]


def _ada_mod(c_rows, w_ada, b_ada, layer):
    depth, d, n = w_ada.shape
    tn = 1024
    return pl.pallas_call(
        _ada_kernel,
        grid=(n // tn,),
        in_specs=[pl.BlockSpec((MOD_ROWS, d), lambda j: (0, 0)),
                  pl.BlockSpec((None, d, tn), lambda j: (layer, 0, j)),
                  pl.BlockSpec((None, 1, tn), lambda j: (layer, 0, j))],
        out_specs=pl.BlockSpec((MOD_ROWS, tn), lambda j: (0, j)),
        out_shape=jax.ShapeDtypeStruct((MOD_ROWS, n), F32),
        compiler_params=_params(1),
        name="ada_mod",
    )(c_rows, w_ada, b_ada.reshape(depth, 1, n))


def _mod_spec(k, brow):
    return pl.BlockSpec((1, 1, D_MODEL), lambda i, *_: (brow(i) * N_MOD + k, 0, 0))


def _inproj_kernel(x_ref, g_ref, sh_ref, sc_ref, w_ref, o_ref, h_scr, *, tn):
    y = _rms(x_ref[...], g_ref[...])
    h_scr[...] = (y * (1.0 + sc_ref[0]) + sh_ref[0]).astype(BF16)
    for c0 in range(0, o_ref.shape[1], tn):
        acc = lax.dot_general(h_scr[...], w_ref[c0:c0 + tn, :], (((1,), (1,)), ((), ())),
                              preferred_element_type=F32)
        if c0 >= LAT_W + POOL_DIM:
            acc = _sigmoid(acc)
        o_ref[:, c0:c0 + tn] = acc.astype(BF16)


def _in_proj(x2, mod3, brow, g, w, n_cols):
    t, d = x2.shape
    tm = min(ROW_TILE, t)
    return pl.pallas_call(
        functools.partial(_inproj_kernel, tn=IN_PROJ_COL_CHUNK),
        grid=(t // tm,),
        in_specs=[pl.BlockSpec((tm, d), lambda i: (i, 0)),
                  pl.BlockSpec((1, d), lambda i: (0, 0)),
                  _mod_spec(0, lambda i: brow(i * tm)),
                  _mod_spec(1, lambda i: brow(i * tm)),
                  pl.BlockSpec((n_cols, d), lambda i: (0, 0), pipeline_mode=pl.Buffered(1))],
        out_specs=pl.BlockSpec((tm, n_cols), lambda i: (i, 0)),
        out_shape=jax.ShapeDtypeStruct((t, n_cols), BF16),
        scratch_shapes=[pltpu.VMEM((tm, d), BF16)],
        compiler_params=_params(1),
        name="in_proj",
    )(x2, g.reshape(1, d), mod3, mod3, w)


def _qkv_kernel(lat_ref, cos_ref, sin_ref, gq_ref, gkv_ref, wq_ref, wk_ref, wvt_ref, *out_refs, with_q):
    if with_q:
        q_ref, k_ref, v_ref = out_refs
    else:
        k_ref, v_ref = out_refs
    cos = cos_ref[...]
    sin = sin_ref[...]

    def rope(grp):
        return grp * cos + pltpu.roll(grp, 2 * (QK_ROPE // 2), axis=1) * sin

    kvn = _rms(lat_ref[:, :KV_RANK].astype(F32), gkv_ref[...]).astype(BF16)
    kpe = rope(lat_ref[:, KV_RANK + Q_RANK:].astype(F32)).astype(BF16)
    for h in range(0, N_HEADS, 2):
        kn = jnp.dot(kvn, wk_ref[:, h * QK_NOPE:(h + 2) * QK_NOPE], preferred_element_type=F32)
        for hh in range(2):
            k_ref[0, h + hh, :, :QK_NOPE] = kn[:, hh * QK_NOPE:(hh + 1) * QK_NOPE].astype(BF16)
            k_ref[0, h + hh, :, QK_NOPE:] = kpe
    for h in range(0, N_HEADS, V_HEAD_GROUP):
        vt = lax.dot_general(wvt_ref[h * V_DIM:(h + V_HEAD_GROUP) * V_DIM, :], kvn,
                             (((1,), (1,)), ((), ())), preferred_element_type=F32)
        for hh in range(V_HEAD_GROUP):
            v_ref[0, h + hh] = vt[hh * V_DIM:(hh + 1) * V_DIM, :].astype(BF16)
    if with_q:
        qn = _rms(lat_ref[:, KV_RANK:KV_RANK + Q_RANK].astype(F32), gq_ref[...]).astype(BF16)
        for h in range(N_HEADS):
            qh = jnp.dot(qn, wq_ref[:, h * HEAD_W:(h + 1) * HEAD_W], preferred_element_type=F32)
            q_ref[0, h, :, :QK_NOPE] = (qh[:, :QK_NOPE] * Q_SCALE).astype(BF16)
            q_ref[0, h, :, QK_NOPE:] = (rope(qh[:, QK_NOPE:]) * Q_SCALE).astype(BF16)


def _qkv(p, b, n, cos_t, sin_t, gq, gkv, wq, wk, wvt, with_q):
    tm = min(ROW_TILE, n)
    tpb = n // tm
    rope_tiles = cos_t.shape[0] // tm
    head_spec = pl.BlockSpec((1, N_HEADS, tm, HEAD_W), lambda i: (i // tpb, 0, i % tpb, 0))
    tab_spec = pl.BlockSpec((tm, LANES), lambda i: ((i % tpb) % rope_tiles, 0))
    out_shape = [jax.ShapeDtypeStruct((b, N_HEADS, n, HEAD_W), BF16),
                 jax.ShapeDtypeStruct((b, N_HEADS, V_DIM, n), BF16)]
    out_specs = [head_spec,
                 pl.BlockSpec((1, N_HEADS, V_DIM, tm), lambda i: (i // tpb, 0, 0, i % tpb))]
    if with_q:
        out_shape = [jax.ShapeDtypeStruct((b, N_HEADS, n, HEAD_W), BF16)] + out_shape
        out_specs = [head_spec] + out_specs
    return pl.pallas_call(
        functools.partial(_qkv_kernel, with_q=with_q),
        grid=(b * tpb,),
        in_specs=[pl.BlockSpec((tm, LAT_W), lambda i: (i, 0)),
                  tab_spec, tab_spec,
                  pl.BlockSpec((1, Q_RANK), lambda i: (0, 0)),
                  pl.BlockSpec((1, KV_RANK), lambda i: (0, 0)),
                  pl.BlockSpec(wq.shape, lambda i: (0, 0)),
                  pl.BlockSpec(wk.shape, lambda i: (0, 0)),
                  pl.BlockSpec(wvt.shape, lambda i: (0, 0))],
        out_specs=out_specs,
        out_shape=out_shape,
        compiler_params=_params(1),
        name="qkv",
    )(p, cos_t, sin_t, gq.reshape(1, Q_RANK), gkv.reshape(1, KV_RANK), wq, wk, wvt)


def _row_fold(x, op):
    parts = [x[r:r + SUBLANES] for r in range(0, x.shape[0], SUBLANES)]
    a, b = parts[0], parts[1]
    for i in range(2, len(parts) - 1, 2):
        a, b = op(a, parts[i]), op(b, parts[i + 1])
    if len(parts) % 2:
        a = op(a, parts[-1])
    return op(a, b)


def _attn_kernel(q_ref, *refs, n_seg, kb, tq):
    k_refs, vt_refs = refs[:n_seg], refs[n_seg:2 * n_seg]
    o_ref, s_a, s_b, m_a, m_b = refs[2 * n_seg:]
    chunks = []
    off = 0
    for kr, vr in zip(k_refs, vt_refs):
        nk = kr.shape[2]
        for c0 in range(0, nk, kb):
            n = min(kb, nk - c0)
            chunks.append((kr, vr, c0, n, off))
            off += n

    n_heads, nt = q_ref.shape[1], q_ref.shape[2] // tq
    bufs = ((s_a, m_a), (s_b, m_b))

    def tile_rows(t):
        return pl.ds(pl.multiple_of(t * tq, tq), tq)

    def scores(g, t, s_buf, m_buf):
        rows = tile_rows(t)
        q = q_ref[0, g, rows, :]
        mp = None
        for kr, vr, c0, n, o in chunks:
            s = lax.dot_general(kr[0, g, c0:c0 + n, :], q, (((1,), (1,)), ((), ())),
                                preferred_element_type=F32)
            s_buf[o:o + n, :] = s
            m = _row_fold(s, jnp.maximum)
            mp = m if mp is None else jnp.maximum(mp, m)
        m_buf[...] = mp

    def values(g, t, s_buf, m_buf):
        rows = tile_rows(t)
        mrow = jnp.max(m_buf[...], axis=0, keepdims=True)
        lp = None
        acc = None
        for kr, vr, c0, n, o in chunks:
            pr = jnp.exp2(s_buf[o:o + n, :] - mrow)
            ls = _row_fold(pr, jnp.add)
            lp = ls if lp is None else lp + ls
            pv = jnp.dot(vr[0, g, :, c0:c0 + n], pr.astype(BF16), preferred_element_type=F32)
            acc = pv if acc is None else acc + pv
        l = jnp.sum(lp, axis=0, keepdims=True)
        o_ref[0, g, rows, :] = (acc / l).T.astype(BF16)

    def phase(g, t, parity):
        scores(g, t + 1, *bufs[1 - parity])
        values(g, t, *bufs[parity])

    scores(0, 0, *bufs[0])
    for g in range(n_heads):
        par0 = (g * nt) % 2
        n_pairs = (nt - 1) // 2

        def pair(j, g=g, par0=par0):
            phase(g, 2 * j, par0)
            phase(g, 2 * j + 1, 1 - par0)

        def pairs(jj, carry, pair=pair):
            for k in range(ATTN_PAIRS_PER_ITER):
                pair(jj * ATTN_PAIRS_PER_ITER + k)
            return carry

        n_iter = n_pairs // ATTN_PAIRS_PER_ITER
        if n_iter:
            lax.fori_loop(0, n_iter, pairs, 0)
        for j in range(n_iter * ATTN_PAIRS_PER_ITER, n_pairs):
            pair(j)
        if (nt - 1) % 2:
            phase(g, nt - 2, (par0 + nt - 2) % 2)
        last = bufs[(par0 + nt - 1) % 2]
        if g + 1 < n_heads:
            scores(g + 1, 0, *bufs[((g + 1) * nt) % 2])
        values(g, nt - 1, *last)


def _attention(q, ks, vts):
    b, hh, n, _ = q.shape
    tq = min(ATTN_TQ, n)
    hg = ATTN_HEADS_PER_STEP if n > tq else hh
    assert hh % hg == 0
    nk_total = sum(k.shape[2] for k in ks)
    seg_spec = lambda a: pl.BlockSpec((1, hg) + a.shape[2:], lambda bi, h: (bi, h, 0, 0))
    return pl.pallas_call(
        functools.partial(_attn_kernel, n_seg=len(ks), kb=ATTN_KB, tq=tq),
        grid=(b, hh // hg),
        in_specs=[pl.BlockSpec((1, hg, n, HEAD_W), lambda bi, h: (bi, h, 0, 0))]
                 + [seg_spec(a) for a in ks] + [seg_spec(a) for a in vts],
        out_specs=pl.BlockSpec((1, hg, n, V_DIM), lambda bi, h: (bi, h, 0, 0)),
        out_shape=jax.ShapeDtypeStruct((b, hh, n, V_DIM), BF16),
        scratch_shapes=[pltpu.VMEM((nk_total, tq), F32), pltpu.VMEM((nk_total, tq), F32),
                        pltpu.VMEM((SUBLANES, tq), F32), pltpu.VMEM((SUBLANES, tq), F32)],
        compiler_params=_params(2),
        name="attention",
    )(q, *ks, *vts)


def _top2sum(a, b, c, d):
    s1, t1 = jnp.maximum(a, b), jnp.minimum(a, b)
    s2, t2 = jnp.maximum(c, d), jnp.minimum(c, d)
    return jnp.maximum(s1, s2) + jnp.maximum(jnp.minimum(s1, s2), jnp.maximum(t1, t2))


def _route(logits_t, bias):
    sc = _sigmoid(logits_t)
    sel = sc + bias
    sel_r = [sel[e:e + 1, :] for e in range(N_EXPERTS)]
    sc_r = [sc[e:e + 1, :] for e in range(N_EXPERTS)]
    epg = EXPERTS_PER_GROUP
    gs = [_top2sum(*sel_r[g * epg:(g + 1) * epg]) for g in range(N_GROUPS)]
    best, gi = gs[0], jnp.zeros(gs[0].shape, jnp.int32)
    for g in range(1, N_GROUPS):
        upd = gs[g] > best
        best = jnp.where(upd, gs[g], best)
        gi = jnp.where(upd, g, gi)

    def pick_group(rows, k):
        r = rows[k]
        for g in range(1, N_GROUPS):
            r = jnp.where(gi == g, rows[g * epg + k], r)
        return r

    in_sel = [pick_group(sel_r, k) for k in range(epg)]
    in_sc = [pick_group(sc_r, k) for k in range(epg)]
    b1, i1 = in_sel[0], jnp.zeros(gi.shape, jnp.int32)
    for k in range(1, epg):
        upd = in_sel[k] > b1
        b1 = jnp.where(upd, in_sel[k], b1)
        i1 = jnp.where(upd, k, i1)
    b2, i2 = None, None
    for k in range(epg):
        cand = jnp.where(i1 == k, -jnp.inf, in_sel[k])
        if b2 is None:
            b2, i2 = cand, jnp.zeros(gi.shape, jnp.int32)
        else:
            upd = cand > b2
            b2 = jnp.where(upd, cand, b2)
            i2 = jnp.where(upd, k, i2)

    def pick_local(idx):
        r = in_sc[0]
        for k in range(1, epg):
            r = jnp.where(idx == k, in_sc[k], r)
        return r

    s1, s2 = pick_local(i1), pick_local(i2)
    den = s1 + s2
    return gi * epg + i1, gi * epg + i2, s1 / den, s2 / den


def _mixout_kernel(y_ref, u_ref, up_ref, un_ref, gm_ref, gp_ref, wp_ref, ps_ref, wo_ref, x_ref,
                   g1_ref, gf_ref, sh2_ref, sc2_ref, wr_ref, rb_ref,
                   xo_ref, h2_ref, route_ref, w12_ref, *, tm, tpb, n_seq):
    i = pl.program_id(0)
    uext = jnp.concatenate([up_ref[...], u_ref[...], un_ref[...]], axis=0)
    sub = min(MIX_CHAIN_ROWS, tm)

    def mix_stage(r0):
        rows = slice(r0, r0 + sub)
        base = (i % tpb) * tm + r0
        r = lax.broadcasted_iota(jnp.int32, (sub, sub + 2 * POOL_HALO), 0)
        c = lax.broadcasted_iota(jnp.int32, (sub, sub + 2 * POOL_HALO), 1)
        rel = c - POOL_HALO - r
        jpos = base - POOL_HALO + c
        valid = (jpos >= 0) & (jpos < n_seq)
        tpos = base + lax.broadcasted_iota(jnp.int32, (sub, 1), 0)
        uwin = uext[r0:r0 + sub + 2 * POOL_HALO]
        uc = uext[r0 + POOL_HALO:r0 + POOL_HALO + sub]
        parts = []
        for g, w in enumerate(POOL_WINDOWS):
            sl = slice(g * POOL_GROUP_DIM, (g + 1) * POOL_GROUP_DIM)
            band = jnp.where(valid & (rel >= -(w // 2)) & (rel < w // 2), 1.0, 0.0).astype(BF16)
            cnt = jnp.clip(tpos + w // 2, 0, n_seq) - jnp.clip(tpos - w // 2, 0, n_seq)
            wsum = jnp.dot(band, uwin[:, sl], preferred_element_type=F32)
            z = wsum / cnt.astype(F32) - uc[:, sl].astype(F32)
            parts.append(jnp.dot(z.astype(BF16), wp_ref[g], preferred_element_type=F32))
        ypool = jnp.concatenate(parts, axis=1) * ps_ref[...]
        y_mla = jnp.concatenate([y_ref[0, h, rows, :] for h in range(N_HEADS)], axis=1)
        mix = gm_ref[rows, :].astype(F32) * y_mla.astype(F32) + gp_ref[rows, :].astype(F32) * ypool
        return jnp.dot(mix.astype(BF16), wo_ref[...], preferred_element_type=F32)

    def norm_route_stage(r0, y):
        rows = slice(r0, r0 + sub)
        xn = x_ref[rows, :] + g1_ref[0] * y
        xo_ref[rows, :] = xn
        h2f = _rms(xn, gf_ref[...]) * (1.0 + sc2_ref[0]) + sh2_ref[0]
        h2_ref[rows, :] = _pack_rows(h2f)
        logits_t = lax.dot_general(wr_ref[...], h2f.astype(BF16), (((1,), (1,)), ((), ())),
                                   preferred_element_type=F32)
        e1, e2, w1, w2 = _route(logits_t, rb_ref[...])
        zero = jnp.zeros((ROUTE_ROWS - 4, sub), F32)
        route_ref[:, rows] = jnp.concatenate([e1.astype(F32), e2.astype(F32), w1, w2, zero], axis=0)
        w12_ref[rows, :] = jnp.concatenate([w1, w2, jnp.zeros((LANES - 2, sub), F32)], axis=0).T

    starts = list(range(0, tm, sub))
    pending = None
    for r0 in starts:
        y = mix_stage(r0)
        if pending is not None:
            norm_route_stage(*pending)
        pending = (r0, y)
    norm_route_stage(*pending)


def _mixer_out(y, p, x2, mod3, brow, n_seq, wp, ps, wo, gf, wr_t, rb):
    t, d = x2.shape
    tm = min(2 * MIX_CHAIN_ROWS, n_seq)
    tpb = n_seq // tm
    hpt = tm // POOL_HALO
    n_halo = t // POOL_HALO
    row = lambda i: (i, 0)
    const2 = lambda i: (0, 0)
    mrow = lambda i: brow(i * tm)
    return pl.pallas_call(
        functools.partial(_mixout_kernel, tm=tm, tpb=tpb, n_seq=n_seq),
        grid=(t // tm,),
        in_specs=[pl.BlockSpec((1, N_HEADS, tm, V_DIM), lambda i: (i // tpb, 0, i % tpb, 0)),
                  pl.BlockSpec((tm, POOL_DIM), lambda i: (i, LAT_W // POOL_DIM)),
                  pl.BlockSpec((POOL_HALO, POOL_DIM),
                               lambda i: (jnp.maximum(i * hpt - 1, 0), LAT_W // POOL_DIM)),
                  pl.BlockSpec((POOL_HALO, POOL_DIM),
                               lambda i: (jnp.minimum((i + 1) * hpt, n_halo - 1), LAT_W // POOL_DIM)),
                  pl.BlockSpec((tm, d), lambda i: (i, (LAT_W + POOL_DIM) // d)),
                  pl.BlockSpec((tm, d), lambda i: (i, (LAT_W + POOL_DIM) // d + 1)),
                  pl.BlockSpec(wp.shape, lambda i: (0, 0, 0)),
                  pl.BlockSpec((1, d), const2),
                  pl.BlockSpec((d, d), const2, pipeline_mode=pl.Buffered(1)),
                  pl.BlockSpec((tm, d), row),
                  _mod_spec(2, mrow),
                  pl.BlockSpec((1, d), const2),
                  _mod_spec(3, mrow),
                  _mod_spec(4, mrow),
                  pl.BlockSpec((N_EXPERTS, d), const2),
                  pl.BlockSpec((N_EXPERTS, 1), const2)],
        out_specs=[pl.BlockSpec((tm, d), row),
                   pl.BlockSpec((tm, PACK_W), row),
                   pl.BlockSpec((ROUTE_ROWS, tm), lambda i: (0, i)),
                   pl.BlockSpec((tm, LANES), row)],
        out_shape=[jax.ShapeDtypeStruct((t, d), F32),
                   jax.ShapeDtypeStruct((t, PACK_W), jnp.int32),
                   jax.ShapeDtypeStruct((ROUTE_ROWS, t), F32),
                   jax.ShapeDtypeStruct((t, LANES), F32)],
        compiler_params=_params(1),
        name="mixer_out",
    )(y, p, p, p, p, p, wp, ps.reshape(1, d), wo, x2, mod3, gf.reshape(1, d), mod3, mod3, wr_t,
      rb.reshape(N_EXPERTS, 1))


def _chunk_view(x):
    n = x.shape[0]
    return (x.reshape(n // SUBLANES, SUBLANES, ROW_CHUNKS, LANES).transpose(0, 2, 1, 3)
            .reshape(n * ROW_CHUNKS, LANES))


def _row_view(c):
    n = c.shape[0] // ROW_CHUNKS
    return (c.reshape(n // SUBLANES, ROW_CHUNKS, SUBLANES, LANES).transpose(0, 2, 1, 3)
            .reshape(n, PACK_W))


def _sc_index_chunks(idx):
    m = idx.shape[0]
    per_row = LANES // ROW_CHUNKS
    assert m % per_row == 0
    lane = np.arange(LANES)
    src = (lane // (SUBLANES * ROW_CHUNKS)) * SUBLANES + lane % SUBLANES
    sel = jnp.asarray(np.equal.outer(np.arange(per_row), src), BF16)
    rows = idx.reshape(m // per_row, per_row)
    pick = lambda v: jnp.dot(v.astype(BF16), sel, preferred_element_type=F32)
    r = (pick(rows >> 8) * 256.0 + pick(rows & 255)).astype(jnp.int32)
    j = jnp.asarray((lane % (SUBLANES * ROW_CHUNKS)) // SUBLANES, jnp.int32)[None, :]
    ids = (r // SUBLANES) * (SUBLANES * ROW_CHUNKS) + j * SUBLANES + r % SUBLANES
    return ids.reshape(1, m * ROW_CHUNKS)


def _sc_mesh():
    return plsc.VectorSubcoreMesh(core_axis_name="core", subcore_axis_name="subcore")


def _sc_gather_rows(table, idx):
    n, m = table.shape[0], idx.shape[0]
    n_idx = m * ROW_CHUNKS
    assert n_idx % SC_WINDOW == 0 and n % SUBLANES == 0 and m % SUBLANES == 0
    assert n <= MAX_INDEXED_ROWS

    @pl.kernel(out_type=jax.ShapeDtypeStruct((n_idx, LANES), jnp.int32), mesh=_sc_mesh())
    def gather(t_hbm, i_hbm, o_hbm):
        def body(i_vmem, o_vmem):
            pltpu.sync_copy(t_hbm.at[i_vmem.at[0]], o_vmem)

        pltpu.emit_pipeline(
            body,
            grid=(n_idx // SC_WINDOW,),
            in_specs=[pl.BlockSpec((1, SC_WINDOW), index_map=lambda i: (0, i))],
            out_specs=[pl.BlockSpec((SC_WINDOW, LANES), index_map=lambda i: (i, 0))],
            core_axis_name=("core", "subcore"),
            dimension_semantics=(pltpu.PARALLEL,),
        )(i_hbm, o_hbm)

    return _row_view(gather(_chunk_view(table), _sc_index_chunks(idx)))


def _sc_scatter_rows(rows, idx, n_out):
    n, m = rows.shape[0], idx.shape[0]
    n_idx = m * ROW_CHUNKS
    src_steps = n * ROW_CHUNKS // SC_WINDOW
    assert n_idx % SC_WINDOW == 0 and (n * ROW_CHUNKS) % SC_WINDOW == 0 and m % n == 0
    assert n_out <= MAX_INDEXED_ROWS
    assert n % SUBLANES == 0 and n_out % SUBLANES == 0

    @pl.kernel(out_type=jax.ShapeDtypeStruct((n_out * ROW_CHUNKS, LANES), jnp.int32), mesh=_sc_mesh())
    def scatter(x_hbm, i_hbm, o_hbm):
        def body(x_vmem, i_vmem):
            pltpu.sync_copy(x_vmem, o_hbm.at[i_vmem.at[0]])

        pltpu.emit_pipeline(
            body,
            grid=(n_idx // SC_WINDOW,),
            in_specs=[pl.BlockSpec((SC_WINDOW, LANES), index_map=lambda i: (i % src_steps, 0)),
                      pl.BlockSpec((1, SC_WINDOW), index_map=lambda i: (0, i))],
            out_specs=[],
            core_axis_name=("core", "subcore"),
            dimension_semantics=(pltpu.PARALLEL,),
        )(x_hbm, i_hbm)

    return _row_view(scatter(_chunk_view(rows), _sc_index_chunks(idx)))


def _dispatch_plan(route):
    t = route.shape[1]
    n_pairs = 2 * t
    n_tiles = n_pairs // MOE_TILE + N_EXPERTS
    blk = LANES
    e = jnp.concatenate([route[0], route[1]]).astype(jnp.int32)
    onehot = (e[:, None] == jnp.arange(N_EXPERTS, dtype=jnp.int32)[None, :]).astype(F32)
    oh3 = onehot.reshape(n_pairs // blk, blk, N_EXPERTS)
    tri = (jnp.arange(blk)[:, None] >= jnp.arange(blk)[None, :]).astype(F32)
    within = jnp.einsum("ij,bjk->bik", tri, oh3)
    totals = within[:, -1, :]
    before = jnp.cumsum(totals, axis=0) - totals
    csum = (within + before[:, None, :]).reshape(n_pairs, N_EXPERTS)
    counts = jnp.sum(totals, axis=0).astype(jnp.int32)
    padded = ((counts + MOE_TILE - 1) // MOE_TILE) * MOE_TILE
    ends = jnp.cumsum(padded)
    starts = ends - padded
    pair_slot = jnp.sum(onehot * (csum - 1.0 + starts.astype(F32)[None, :]), axis=1).astype(jnp.int32)
    tile_start = jnp.arange(n_tiles, dtype=jnp.int32) * MOE_TILE
    tile_e = jnp.minimum(jnp.sum((tile_start[:, None] >= ends[None, :]).astype(jnp.int32), axis=1), N_EXPERTS - 1)
    e_onehot = tile_e[:, None] == jnp.arange(N_EXPERTS, dtype=jnp.int32)[None, :]
    filled = jnp.sum(jnp.where(e_onehot, (starts + counts)[None, :], 0), axis=1)
    tile_rows = jnp.clip(filled - tile_start, 0, MOE_TILE).astype(jnp.int32)
    last_used = jnp.maximum(ends[-1] // MOE_TILE - 1, 0)
    tile_expert = jnp.where(tile_start < ends[-1], tile_e, tile_e[last_used]).astype(jnp.int32)
    tile_block = jnp.minimum(jnp.arange(n_tiles, dtype=jnp.int32), last_used).astype(jnp.int32)
    return pair_slot, tile_expert, tile_rows, tile_block


def _swiglu(xb, wg, wu, wd):
    gate = jnp.dot(xb, wg, preferred_element_type=F32)
    up = jnp.dot(xb, wu, preferred_element_type=F32)
    a = ((gate * _sigmoid(gate)) * up).astype(BF16)
    return jnp.dot(a, wd, preferred_element_type=F32)


def _ffn_kernel(te_ref, rows_ref, blk_ref, xs_ref, wg_ref, wu_ref, wd_ref, ys_ref, wg_s, wu_s, wd_s):
    i = pl.program_id(0)
    new_expert = jnp.logical_or(i == 0, te_ref[i] != te_ref[jnp.maximum(i - 1, 0)])

    @pl.when(new_expert)
    def _():
        wg_s[...] = wg_ref[0].astype(BF16)
        wu_s[...] = wu_ref[0].astype(BF16)
        wd_s[...] = wd_ref[0].astype(BF16)

    @pl.when(rows_ref[i] > 0)
    def _():
        live = lax.broadcasted_iota(jnp.int32, (MOE_TILE, 1), 0) < rows_ref[i]
        xb = _load_unpack(xs_ref, live).astype(BF16)
        ys_ref[...] = _pack_rows(_swiglu(xb, wg_s[...], wu_s[...], wd_s[...]))


def _expert_ffn(xs, tile_expert, tile_rows, tile_block, wg, wu, wd, layer):
    n_slots = xs.shape[0]
    d = D_MODEL
    blk = pl.BlockSpec((MOE_TILE, PACK_W), lambda i, te, nr, tb: (tb[i], 0))
    return pl.pallas_call(
        _ffn_kernel,
        grid_spec=pltpu.PrefetchScalarGridSpec(
            num_scalar_prefetch=3,
            grid=(n_slots // MOE_TILE,),
            in_specs=[blk,
                      pl.BlockSpec((None, 1, d, D_EXPERT), lambda i, te, nr, tb: (layer, te[i], 0, 0)),
                      pl.BlockSpec((None, 1, d, D_EXPERT), lambda i, te, nr, tb: (layer, te[i], 0, 0)),
                      pl.BlockSpec((None, 1, D_EXPERT, d), lambda i, te, nr, tb: (layer, te[i], 0, 0))],
            out_specs=blk,
            scratch_shapes=[pltpu.VMEM((d, D_EXPERT), BF16), pltpu.VMEM((d, D_EXPERT), BF16),
                            pltpu.VMEM((D_EXPERT, d), BF16)]),
        out_shape=jax.ShapeDtypeStruct(xs.shape, jnp.int32),
        compiler_params=_params(1),
        name="expert_ffn",
    )(tile_expert, tile_rows, tile_block, xs, wg, wu, wd)


def _combine_kernel(h_ref, y1_ref, y2_ref, w_ref, sg_ref, su_ref, sd_ref, x_ref, g2_ref, gfin_ref, o_ref,
                    *, final):
    y = _swiglu(_load_unpack(h_ref).astype(BF16), sg_ref[...], su_ref[...], sd_ref[...])
    w = w_ref[...]
    y = y + w[:, 0:1] * _load_unpack(y1_ref) + w[:, 1:2] * _load_unpack(y2_ref)
    xn = x_ref[...] + g2_ref[0] * y
    if final:
        xn = _rms(xn, gfin_ref[...])
    o_ref[...] = xn


def _moe_combine(h2p, yg, w12, sg, su, sd, x2, set_off, x_off, n_rows, mod3, brow, gfin, final):
    d = x2.shape[1]
    tm = min(ROW_TILE, n_rows)
    assert set_off % tm == 0 and x_off % tm == 0 and n_rows % tm == 0
    s_blk, x_blk, n_blk = set_off // tm, x_off // tm, n_rows // tm
    const2 = lambda i: (0, 0)
    pk = lambda o: pl.BlockSpec((tm, PACK_W), lambda i: (i + o, 0))
    x_spec = pl.BlockSpec((tm, d), lambda i: (i + x_blk, 0))
    return pl.pallas_call(
        functools.partial(_combine_kernel, final=final),
        grid=(n_blk,),
        in_specs=[pk(s_blk), pk(0), pk(n_blk),
                  pl.BlockSpec((tm, LANES), lambda i: (i + x_blk, 0)),
                  pl.BlockSpec(sg.shape, const2), pl.BlockSpec(su.shape, const2), pl.BlockSpec(sd.shape, const2),
                  x_spec,
                  _mod_spec(5, lambda i: brow((i + x_blk) * tm)),
                  pl.BlockSpec((1, d), const2)],
        out_specs=x_spec,
        out_shape=jax.ShapeDtypeStruct(x2.shape, F32),
        input_output_aliases={7: 0},
        compiler_params=_params(1),
        name="moe_combine",
    )(h2p, yg, yg, w12, sg, su, sd, x2, mod3, gfin.reshape(1, d))


def _moe_routed(parts, wg, wu, wd, layer, sg, su, sd, mod3, gfin, final):
    h2p = parts[0][0] if len(parts) == 1 else jnp.concatenate([pt[0] for pt in parts], axis=0)
    route = parts[0][1] if len(parts) == 1 else jnp.concatenate([pt[1] for pt in parts], axis=1)
    t_all = h2p.shape[0]
    pair_slot, tile_expert, tile_rows, tile_block = _dispatch_plan(route)
    n_slots = tile_expert.shape[0] * MOE_TILE
    xs = _sc_scatter_rows(h2p, pair_slot, n_slots)
    ys = _expert_ffn(xs, tile_expert, tile_rows, tile_block, wg, wu, wd, layer)
    outs, set_off = [], 0
    for _, _, w12, x2, brow in parts:
        t = x2.shape[0]
        n_rows = t // COMBINE_CHUNKS if t % (COMBINE_CHUNKS * ROW_TILE) == 0 else t
        for x_off in range(0, t, n_rows):
            lo = set_off + x_off
            idx = jnp.concatenate([pair_slot[lo:lo + n_rows], pair_slot[t_all + lo:t_all + lo + n_rows]])
            yg = _sc_gather_rows(ys, idx)
            x2 = _moe_combine(h2p, yg, w12, sg, su, sd, x2, lo, x_off, n_rows, mod3, brow, gfin, final)
        outs.append(x2)
        set_off += t
    return outs


def _rope_tables(seq):
    rows = seq // GRID_W
    row = np.repeat(np.arange(rows), GRID_W).astype(np.float32)
    col = np.tile(np.arange(GRID_W), rows).astype(np.float32)
    axis_dim = QK_ROPE // 2
    inv = (ROPE_THETA ** (-np.arange(0, axis_dim, 2, dtype=np.float32) / axis_dim)).astype(np.float32)
    ang = np.concatenate([row[:, None] * inv, col[:, None] * inv], axis=-1).astype(np.float32)
    cos, sin = np.cos(ang), np.sin(ang)
    zero = np.zeros_like(cos)
    return (jnp.asarray(np.concatenate([cos, cos, zero, zero], axis=1)),
            jnp.asarray(np.concatenate([-sin, sin, zero, zero], axis=1)))


def _rope_group(w, start):
    half = QK_ROPE // 2
    x1, x2 = w[..., start:start + half], w[..., start + half:start + 2 * half]
    return jnp.concatenate([x1, x2, x2, x1], axis=-1)


def _w_in_relayout_kernel(w_ref, o_ref):
    half = QK_ROPE // 2
    x1, x2 = KV_RANK, KV_RANK + half
    pieces = [(0, KV_RANK), (KV_COLS, Q_RANK), (x1, half), (x2, half), (x2, half), (x1, half),
              (Q_END, o_ref.shape[0] - LAT_W)]
    dst = 0
    for src, n in pieces:
        o_ref[dst:dst + n, :] = w_ref[src:src + n, :].astype(BF16)
        dst += n


def _relayout_w_in(w, layer):
    _, d, n_in = w.shape
    tc = 256
    return pl.pallas_call(
        _w_in_relayout_kernel,
        grid=(d // tc,),
        in_specs=[pl.BlockSpec((None, n_in, tc), lambda i: (layer, 0, i))],
        out_specs=pl.BlockSpec((P_COLS, tc), lambda i: (0, i)),
        out_shape=jax.ShapeDtypeStruct((P_COLS, d), BF16),
        compiler_params=_params(1),
        name="w_in_relayout",
    )(jnp.swapaxes(w, 1, 2))


def _relayout_w_qb(w):
    dq = QK_NOPE + QK_ROPE
    cols = []
    for h in range(N_HEADS):
        cols += [w[:, h * dq:h * dq + QK_NOPE], _rope_group(w, h * dq + QK_NOPE)]
    return jnp.concatenate(cols, axis=1).astype(BF16)


def kernel(x, c, ctx, c_ctx, w_ada, b_ada, norm_mix_g, norm_ffn_g, w_in, q_norm_g, kv_norm_g, w_qb, w_kvb, w_pool, pool_scale, w_out, w_router, router_bias, w_exp_gate, w_exp_up, w_exp_down, w_sh_gate, w_sh_up, w_sh_down, final_norm_g):
    bsz, seq, d = x.shape
    n_ctx = ctx.shape[1]
    depth = w_ada.shape[0]
    assert bsz < MOD_ROWS and d == D_MODEL

    c_rows = jnp.concatenate([c, c_ctx[None], jnp.zeros((MOD_ROWS - bsz - 1, d), F32)], axis=0)

    cos_t, sin_t = _rope_tables(seq)
    ones_t = jnp.concatenate([jnp.ones((n_ctx, 2 * (QK_ROPE // 2)), F32),
                              jnp.zeros((n_ctx, LANES - QK_ROPE), F32)], axis=1)
    zeros_t = jnp.zeros((n_ctx, LANES), F32)
    wr_t = w_router.T.astype(BF16)
    lat_row = lambda r: r // seq
    ctx_row = lambda r: bsz

    x2 = x.reshape(bsz * seq, d)
    xc2 = ctx.reshape(bsz * n_ctx, d)
    for l in range(depth):
        last = l == depth - 1
        mod3 = _ada_mod(c_rows, w_ada, b_ada, l).reshape(MOD_ROWS * N_MOD, 1, d)
        wq = _relayout_w_qb(w_qb[l])
        w_kv3 = w_kvb[l].reshape(KV_RANK, N_HEADS, QK_NOPE + V_DIM)
        wk = w_kv3[:, :, :QK_NOPE].reshape(KV_RANK, N_HEADS * QK_NOPE).astype(BF16)
        wvt = (w_kv3[:, :, QK_NOPE:].transpose(1, 2, 0).reshape(N_HEADS * V_DIM, KV_RANK)
               .astype(BF16))
        wp = w_pool[l].astype(BF16)
        wo = w_out[l].astype(BF16)
        sg, su, sd = w_sh_gate[l].astype(BF16), w_sh_up[l].astype(BF16), w_sh_down[l].astype(BF16)
        moe = functools.partial(_moe_routed, wg=w_exp_gate, wu=w_exp_up, wd=w_exp_down, layer=l,
                                sg=sg, su=su, sd=sd, mod3=mod3, gfin=final_norm_g)

        w_in_r = _relayout_w_in(w_in, l)
        p = _in_proj(x2, mod3, lat_row, norm_mix_g[l], w_in_r, P_COLS)
        pc = _in_proj(xc2, mod3, ctx_row, norm_mix_g[l], w_in_r, LAT_W if last else P_COLS)
        q, k_lat, v_lat = _qkv(p, bsz, seq, cos_t, sin_t, q_norm_g[l], kv_norm_g[l], wq, wk, wvt, True)
        ctx_out = _qkv(pc, bsz, n_ctx, ones_t, zeros_t, q_norm_g[l], kv_norm_g[l], wq, wk, wvt, not last)
        k_ctx, v_ctx = ctx_out[-2:]
        y = _attention(q, [k_ctx, k_lat], [v_ctx, v_lat])
        x2, h2p, route, w12 = _mixer_out(y, p, x2, mod3, lat_row, seq, wp, pool_scale[l], wo,
                                         norm_ffn_g[l], wr_t, router_bias)
        if last:
            x2, = moe([(h2p, route, w12, x2, lat_row)], final=True)
        else:
            yc = _attention(ctx_out[0], [k_ctx], [v_ctx])
            xc2, h2c, route_c, w12c = _mixer_out(yc, pc, xc2, mod3, ctx_row, n_ctx, wp, pool_scale[l], wo,
                                                 norm_ffn_g[l], wr_t, router_bias)
            x2, xc2 = moe([(h2p, route, w12, x2, lat_row), (h2c, route_c, w12c, xc2, ctx_row)], final=False)
    return x2.reshape(bsz, seq, d)
```

```python
import functools
import math

import jax
import jax.numpy as jnp
import numpy as np
from jax import lax
from jax.experimental import pallas as pl
from jax.experimental.pallas import tpu as pltpu
from jax.experimental.pallas import tpu_sc as plsc

F32 = jnp.float32
BF16 = jnp.bfloat16

D_MODEL = 2048
GRID_W = 64
N_HEADS = 16
QK_NOPE = 128
QK_ROPE = 64
V_DIM = 128
Q_RANK = 384
KV_RANK = 512
ROPE_THETA = 10000.0
POOL_DIM = 1024
POOL_GROUP_DIM = 256
POOL_WINDOWS = (2, 4, 8, 16)
KV_COLS = KV_RANK + QK_ROPE
Q_END = KV_COLS + Q_RANK
N_EXPERTS = 16
N_GROUPS = 4
EXPERTS_PER_GROUP = 4
D_EXPERT = 512
EPS = 1e-6
N_MOD = 6

LANES = 128
HEAD_W = 2 * LANES
LAT_W = KV_RANK + Q_RANK + LANES
P_COLS = LAT_W + POOL_DIM + 2 * D_MODEL
POOL_HALO = 16
MOD_ROWS = 8
Q_SCALE = (1.0 / math.sqrt(QK_NOPE + QK_ROPE)) * math.log2(math.e)
V7X_VMEM_BYTES = 64 * 1024 * 1024
VMEM_LIMIT = V7X_VMEM_BYTES - 8 * 1024 * 1024
SUBLANES = 8
PACK_W = D_MODEL // 2
ROW_CHUNKS = PACK_W // LANES
ROUTE_ROWS = SUBLANES
MOE_TILE = 512
SC_WINDOW = 128
MAX_INDEXED_ROWS = 1 << 16
ATTN_TQ = 512
ATTN_KB = 512
V_HEAD_GROUP = 8
ATTN_HEADS_PER_STEP = 2
ATTN_PAIRS_PER_ITER = 3
ROW_TILE = 512
IN_PROJ_COL_CHUNK = 1024
MIX_CHAIN_ROWS = 256
COMBINE_CHUNKS = 2


def _sigmoid(x):
    return 1.0 / (1.0 + jnp.exp(-x))


def _pack_rows(y):
    half = y.shape[1] // 2
    return pltpu.pack_elementwise([y[:, :half], y[:, half:]], packed_dtype=BF16)


def _load_unpack(ref, live=None):
    w = ref[...]
    if live is not None:
        w = jnp.where(live, w, 0)
    lo = pltpu.unpack_elementwise(w, index=0, packed_dtype=BF16, unpacked_dtype=F32)
    hi = pltpu.unpack_elementwise(w, index=1, packed_dtype=BF16, unpacked_dtype=F32)
    return jnp.concatenate([lo, hi], axis=1)


def _rms(xf, g):
    ms = jnp.mean(xf * xf, axis=-1, keepdims=True)
    return xf * lax.rsqrt(ms + EPS) * g


def _params(n_axes):
    return pltpu.CompilerParams(dimension_semantics=("arbitrary",) * n_axes,
                                vmem_limit_bytes=VMEM_LIMIT)


def _ada_kernel(c_ref, w_ref, b_ref, o_ref):
    cf = c_ref[...]
    a = (cf * _sigmoid(cf)).astype(BF16)
    o_ref[...] = jnp.dot(a, w_ref[...].astype(BF16), preferred_element_type=F32) + b_ref[0]


def _ada_mod(c_rows, w_ada, b_ada, layer):
    depth, d, n = w_ada.shape
    tn = 1024
    return pl.pallas_call(
        _ada_kernel,
        grid=(n // tn,),
        in_specs=[pl.BlockSpec((MOD_ROWS, d), lambda j: (0, 0)),
                  pl.BlockSpec((None, d, tn), lambda j: (layer, 0, j)),
                  pl.BlockSpec((None, 1, tn), lambda j: (layer, 0, j))],
        out_specs=pl.BlockSpec((MOD_ROWS, tn), lambda j: (0, j)),
        out_shape=jax.ShapeDtypeStruct((MOD_ROWS, n), F32),
        compiler_params=_params(1),
        name="ada_mod",
    )(c_rows, w_ada, b_ada.reshape(depth, 1, n))


def _mod_spec(k, brow):
    return pl.BlockSpec((1, 1, D_MODEL), lambda i, *_: (brow(i) * N_MOD + k, 0, 0))


def _inproj_kernel(x_ref, g_ref, sh_ref, sc_ref, w_ref, o_ref, h_scr, *, tn):
    y = _rms(x_ref[...], g_ref[...])
    h_scr[...] = (y * (1.0 + sc_ref[0]) + sh_ref[0]).astype(BF16)
    for c0 in range(0, o_ref.shape[1], tn):
        acc = lax.dot_general(h_scr[...], w_ref[c0:c0 + tn, :], (((1,), (1,)), ((), ())),
                              preferred_element_type=F32)
        if c0 >= LAT_W + POOL_DIM:
            acc = _sigmoid(acc)
        o_ref[:, c0:c0 + tn] = acc.astype(BF16)


def _in_proj(x2, mod3, brow, g, w, n_cols):
    t, d = x2.shape
    tm = min(ROW_TILE, t)
    return pl.pallas_call(
        functools.partial(_inproj_kernel, tn=IN_PROJ_COL_CHUNK),
        grid=(t // tm,),
        in_specs=[pl.BlockSpec((tm, d), lambda i: (i, 0)),
                  pl.BlockSpec((1, d), lambda i: (0, 0)),
                  _mod_spec(0, lambda i: brow(i * tm)),
                  _mod_spec(1, lambda i: brow(i * tm)),
                  pl.BlockSpec((n_cols, d), lambda i: (0, 0), pipeline_mode=pl.Buffered(1))],
        out_specs=pl.BlockSpec((tm, n_cols), lambda i: (i, 0)),
        out_shape=jax.ShapeDtypeStruct((t, n_cols), BF16),
        scratch_shapes=[pltpu.VMEM((tm, d), BF16)],
        compiler_params=_params(1),
        name="in_proj",
    )(x2, g.reshape(1, d), mod3, mod3, w)


def _qkv_kernel(lat_ref, cos_ref, sin_ref, gq_ref, gkv_ref, wq_ref, wk_ref, wvt_ref, *out_refs, with_q):
    if with_q:
        q_ref, k_ref, v_ref = out_refs
    else:
        k_ref, v_ref = out_refs
    cos = cos_ref[...]
    sin = sin_ref[...]

    def rope(grp):
        return grp * cos + pltpu.roll(grp, 2 * (QK_ROPE // 2), axis=1) * sin

    kvn = _rms(lat_ref[:, :KV_RANK].astype(F32), gkv_ref[...]).astype(BF16)
    kpe = rope(lat_ref[:, KV_RANK + Q_RANK:].astype(F32)).astype(BF16)
    for h in range(0, N_HEADS, 2):
        kn = jnp.dot(kvn, wk_ref[:, h * QK_NOPE:(h + 2) * QK_NOPE], preferred_element_type=F32)
        for hh in range(2):
            k_ref[0, h + hh, :, :QK_NOPE] = kn[:, hh * QK_NOPE:(hh + 1) * QK_NOPE].astype(BF16)
            k_ref[0, h + hh, :, QK_NOPE:] = kpe
    for h in range(0, N_HEADS, V_HEAD_GROUP):
        vt = lax.dot_general(wvt_ref[h * V_DIM:(h + V_HEAD_GROUP) * V_DIM, :], kvn,
                             (((1,), (1,)), ((), ())), preferred_element_type=F32)
        for hh in range(V_HEAD_GROUP):
            v_ref[0, h + hh] = vt[hh * V_DIM:(hh + 1) * V_DIM, :].astype(BF16)
    if with_q:
        qn = _rms(lat_ref[:, KV_RANK:KV_RANK + Q_RANK].astype(F32), gq_ref[...]).astype(BF16)
        for h in range(N_HEADS):
            qh = jnp.dot(qn, wq_ref[:, h * HEAD_W:(h + 1) * HEAD_W], preferred_element_type=F32)
            q_ref[0, h, :, :QK_NOPE] = (qh[:, :QK_NOPE] * Q_SCALE).astype(BF16)
            q_ref[0, h, :, QK_NOPE:] = (rope(qh[:, QK_NOPE:]) * Q_SCALE).astype(BF16)


def _qkv(p, b, n, cos_t, sin_t, gq, gkv, wq, wk, wvt, with_q):
    tm = min(ROW_TILE, n)
    tpb = n // tm
    rope_tiles = cos_t.shape[0] // tm
    head_spec = pl.BlockSpec((1, N_HEADS, tm, HEAD_W), lambda i: (i // tpb, 0, i % tpb, 0))
    tab_spec = pl.BlockSpec((tm, LANES), lambda i: ((i % tpb) % rope_tiles, 0))
    out_shape = [jax.ShapeDtypeStruct((b, N_HEADS, n, HEAD_W), BF16),
                 jax.ShapeDtypeStruct((b, N_HEADS, V_DIM, n), BF16)]
    out_specs = [head_spec,
                 pl.BlockSpec((1, N_HEADS, V_DIM, tm), lambda i: (i // tpb, 0, 0, i % tpb))]
    if with_q:
        out_shape = [jax.ShapeDtypeStruct((b, N_HEADS, n, HEAD_W), BF16)] + out_shape
        out_specs = [head_spec] + out_specs
    return pl.pallas_call(
        functools.partial(_qkv_kernel, with_q=with_q),
        grid=(b * tpb,),
        in_specs=[pl.BlockSpec((tm, LAT_W), lambda i: (i, 0)),
                  tab_spec, tab_spec,
                  pl.BlockSpec((1, Q_RANK), lambda i: (0, 0)),
                  pl.BlockSpec((1, KV_RANK), lambda i: (0, 0)),
                  pl.BlockSpec(wq.shape, lambda i: (0, 0)),
                  pl.BlockSpec(wk.shape, lambda i: (0, 0)),
                  pl.BlockSpec(wvt.shape, lambda i: (0, 0))],
        out_specs=out_specs,
        out_shape=out_shape,
        compiler_params=_params(1),
        name="qkv",
    )(p, cos_t, sin_t, gq.reshape(1, Q_RANK), gkv.reshape(1, KV_RANK), wq, wk, wvt)


def _row_fold(x, op):
    parts = [x[r:r + SUBLANES] for r in range(0, x.shape[0], SUBLANES)]
    a, b = parts[0], parts[1]
    for i in range(2, len(parts) - 1, 2):
        a, b = op(a, parts[i]), op(b, parts[i + 1])
    if len(parts) % 2:
        a = op(a, parts[-1])
    return op(a, b)


def _attn_kernel(q_ref, *refs, n_seg, kb, tq):
    k_refs, vt_refs = refs[:n_seg], refs[n_seg:2 * n_seg]
    o_ref, s_a, s_b, m_a, m_b = refs[2 * n_seg:]
    chunks = []
    off = 0
    for kr, vr in zip(k_refs, vt_refs):
        nk = kr.shape[2]
        for c0 in range(0, nk, kb):
            n = min(kb, nk - c0)
            chunks.append((kr, vr, c0, n, off))
            off += n

    n_heads, nt = q_ref.shape[1], q_ref.shape[2] // tq
    bufs = ((s_a, m_a), (s_b, m_b))

    def tile_rows(t):
        return pl.ds(pl.multiple_of(t * tq, tq), tq)

    def scores(g, t, s_buf, m_buf):
        rows = tile_rows(t)
        q = q_ref[0, g, rows, :]
        mp = None
        for kr, vr, c0, n, o in chunks:
            s = lax.dot_general(kr[0, g, c0:c0 + n, :], q, (((1,), (1,)), ((), ())),
                                preferred_element_type=F32)
            s_buf[o:o + n, :] = s
            m = _row_fold(s, jnp.maximum)
            mp = m if mp is None else jnp.maximum(mp, m)
        m_buf[...] = mp

    def values(g, t, s_buf, m_buf):
        rows = tile_rows(t)
        mrow = jnp.max(m_buf[...], axis=0, keepdims=True)
        lp = None
        acc = None
        for kr, vr, c0, n, o in chunks:
            pr = jnp.exp2(s_buf[o:o + n, :] - mrow)
            ls = _row_fold(pr, jnp.add)
            lp = ls if lp is None else lp + ls
            pv = jnp.dot(vr[0, g, :, c0:c0 + n], pr.astype(BF16), preferred_element_type=F32)
            acc = pv if acc is None else acc + pv
        l = jnp.sum(lp, axis=0, keepdims=True)
        o_ref[0, g, rows, :] = (acc / l).T.astype(BF16)

    def phase(g, t, parity):
        scores(g, t + 1, *bufs[1 - parity])
        values(g, t, *bufs[parity])

    scores(0, 0, *bufs[0])
    for g in range(n_heads):
        par0 = (g * nt) % 2
        n_pairs = (nt - 1) // 2

        def pair(j, g=g, par0=par0):
            phase(g, 2 * j, par0)
            phase(g, 2 * j + 1, 1 - par0)

        def pairs(jj, carry, pair=pair):
            for k in range(ATTN_PAIRS_PER_ITER):
                pair(jj * ATTN_PAIRS_PER_ITER + k)
            return carry

        n_iter = n_pairs // ATTN_PAIRS_PER_ITER
        if n_iter:
            lax.fori_loop(0, n_iter, pairs, 0)
        for j in range(n_iter * ATTN_PAIRS_PER_ITER, n_pairs):
            pair(j)
        if (nt - 1) % 2:
            phase(g, nt - 2, (par0 + nt - 2) % 2)
        last = bufs[(par0 + nt - 1) % 2]
        if g + 1 < n_heads:
            scores(g + 1, 0, *bufs[((g + 1) * nt) % 2])
        values(g, nt - 1, *last)


def _attention(q, ks, vts):
    b, hh, n, _ = q.shape
    tq = min(ATTN_TQ, n)
    hg = ATTN_HEADS_PER_STEP if n > tq else hh
    assert hh % hg == 0
    nk_total = sum(k.shape[2] for k in ks)
    seg_spec = lambda a: pl.BlockSpec((1, hg) + a.shape[2:], lambda bi, h: (bi, h, 0, 0))
    return pl.pallas_call(
        functools.partial(_attn_kernel, n_seg=len(ks), kb=ATTN_KB, tq=tq),
        grid=(b, hh // hg),
        in_specs=[pl.BlockSpec((1, hg, n, HEAD_W), lambda bi, h: (bi, h, 0, 0))]
                 + [seg_spec(a) for a in ks] + [seg_spec(a) for a in vts],
        out_specs=pl.BlockSpec((1, hg, n, V_DIM), lambda bi, h: (bi, h, 0, 0)),
        out_shape=jax.ShapeDtypeStruct((b, hh, n, V_DIM), BF16),
        scratch_shapes=[pltpu.VMEM((nk_total, tq), F32), pltpu.VMEM((nk_total, tq), F32),
                        pltpu.VMEM((SUBLANES, tq), F32), pltpu.VMEM((SUBLANES, tq), F32)],
        compiler_params=_params(2),
        name="attention",
    )(q, *ks, *vts)


def _top2sum(a, b, c, d):
    s1, t1 = jnp.maximum(a, b), jnp.minimum(a, b)
    s2, t2 = jnp.maximum(c, d), jnp.minimum(c, d)
    return jnp.maximum(s1, s2) + jnp.maximum(jnp.minimum(s1, s2), jnp.maximum(t1, t2))


def _route(logits_t, bias):
    sc = _sigmoid(logits_t)
    sel = sc + bias
    sel_r = [sel[e:e + 1, :] for e in range(N_EXPERTS)]
    sc_r = [sc[e:e + 1, :] for e in range(N_EXPERTS)]
    epg = EXPERTS_PER_GROUP
    gs = [_top2sum(*sel_r[g * epg:(g + 1) * epg]) for g in range(N_GROUPS)]
    best, gi = gs[0], jnp.zeros(gs[0].shape, jnp.int32)
    for g in range(1, N_GROUPS):
        upd = gs[g] > best
        best = jnp.where(upd, gs[g], best)
        gi = jnp.where(upd, g, gi)

    def pick_group(rows, k):
        r = rows[k]
        for g in range(1, N_GROUPS):
            r = jnp.where(gi == g, rows[g * epg + k], r)
        return r

    in_sel = [pick_group(sel_r, k) for k in range(epg)]
    in_sc = [pick_group(sc_r, k) for k in range(epg)]
    b1, i1 = in_sel[0], jnp.zeros(gi.shape, jnp.int32)
    for k in range(1, epg):
        upd = in_sel[k] > b1
        b1 = jnp.where(upd, in_sel[k], b1)
        i1 = jnp.where(upd, k, i1)
    b2, i2 = None, None
    for k in range(epg):
        cand = jnp.where(i1 == k, -jnp.inf, in_sel[k])
        if b2 is None:
            b2, i2 = cand, jnp.zeros(gi.shape, jnp.int32)
        else:
            upd = cand > b2
            b2 = jnp.where(upd, cand, b2)
            i2 = jnp.where(upd, k, i2)

    def pick_local(idx):
        r = in_sc[0]
        for k in range(1, epg):
            r = jnp.where(idx == k, in_sc[k], r)
        return r

    s1, s2 = pick_local(i1), pick_local(i2)
    den = s1 + s2
    return gi * epg + i1, gi * epg + i2, s1 / den, s2 / den


def _mixout_kernel(y_ref, u_ref, up_ref, un_ref, gm_ref, gp_ref, wp_ref, ps_ref, wo_ref, x_ref,
                   g1_ref, gf_ref, sh2_ref, sc2_ref, wr_ref, rb_ref,
                   xo_ref, h2_ref, route_ref, w12_ref, *, tm, tpb, n_seq):
    i = pl.program_id(0)
    uext = jnp.concatenate([up_ref[...], u_ref[...], un_ref[...]], axis=0)
    sub = min(MIX_CHAIN_ROWS, tm)

    def mix_stage(r0):
        rows = slice(r0, r0 + sub)
        base = (i % tpb) * tm + r0
        r = lax.broadcasted_iota(jnp.int32, (sub, sub + 2 * POOL_HALO), 0)
        c = lax.broadcasted_iota(jnp.int32, (sub, sub + 2 * POOL_HALO), 1)
        rel = c - POOL_HALO - r
        jpos = base - POOL_HALO + c
        valid = (jpos >= 0) & (jpos < n_seq)
        tpos = base + lax.broadcasted_iota(jnp.int32, (sub, 1), 0)
        uwin = uext[r0:r0 + sub + 2 * POOL_HALO]
        uc = uext[r0 + POOL_HALO:r0 + POOL_HALO + sub]
        parts = []
        for g, w in enumerate(POOL_WINDOWS):
            sl = slice(g * POOL_GROUP_DIM, (g + 1) * POOL_GROUP_DIM)
            band = jnp.where(valid & (rel >= -(w // 2)) & (rel < w // 2), 1.0, 0.0).astype(BF16)
            cnt = jnp.clip(tpos + w // 2, 0, n_seq) - jnp.clip(tpos - w // 2, 0, n_seq)
            wsum = jnp.dot(band, uwin[:, sl], preferred_element_type=F32)
            z = wsum / cnt.astype(F32) - uc[:, sl].astype(F32)
            parts.append(jnp.dot(z.astype(BF16), wp_ref[g], preferred_element_type=F32))
        ypool = jnp.concatenate(parts, axis=1) * ps_ref[...]
        y_mla = jnp.concatenate([y_ref[0, h, rows, :] for h in range(N_HEADS)], axis=1)
        mix = gm_ref[rows, :].astype(F32) * y_mla.astype(F32) + gp_ref[rows, :].astype(F32) * ypool
        return jnp.dot(mix.astype(BF16), wo_ref[...], preferred_element_type=F32)

    def norm_route_stage(r0, y):
        rows = slice(r0, r0 + sub)
        xn = x_ref[rows, :] + g1_ref[0] * y
        xo_ref[rows, :] = xn
        h2f = _rms(xn, gf_ref[...]) * (1.0 + sc2_ref[0]) + sh2_ref[0]
        h2_ref[rows, :] = _pack_rows(h2f)
        logits_t = lax.dot_general(wr_ref[...], h2f.astype(BF16), (((1,), (1,)), ((), ())),
                                   preferred_element_type=F32)
        e1, e2, w1, w2 = _route(logits_t, rb_ref[...])
        zero = jnp.zeros((ROUTE_ROWS - 4, sub), F32)
        route_ref[:, rows] = jnp.concatenate([e1.astype(F32), e2.astype(F32), w1, w2, zero], axis=0)
        w12_ref[rows, :] = jnp.concatenate([w1, w2, jnp.zeros((LANES - 2, sub), F32)], axis=0).T

    starts = list(range(0, tm, sub))
    pending = None
    for r0 in starts:
        y = mix_stage(r0)
        if pending is not None:
            norm_route_stage(*pending)
        pending = (r0, y)
    norm_route_stage(*pending)


def _mixer_out(y, p, x2, mod3, brow, n_seq, wp, ps, wo, gf, wr_t, rb):
    t, d = x2.shape
    tm = min(2 * MIX_CHAIN_ROWS, n_seq)
    tpb = n_seq // tm
    hpt = tm // POOL_HALO
    n_halo = t // POOL_HALO
    row = lambda i: (i, 0)
    const2 = lambda i: (0, 0)
    mrow = lambda i: brow(i * tm)
    return pl.pallas_call(
        functools.partial(_mixout_kernel, tm=tm, tpb=tpb, n_seq=n_seq),
        grid=(t // tm,),
        in_specs=[pl.BlockSpec((1, N_HEADS, tm, V_DIM), lambda i: (i // tpb, 0, i % tpb, 0)),
                  pl.BlockSpec((tm, POOL_DIM), lambda i: (i, LAT_W // POOL_DIM)),
                  pl.BlockSpec((POOL_HALO, POOL_DIM),
                               lambda i: (jnp.maximum(i * hpt - 1, 0), LAT_W // POOL_DIM)),
                  pl.BlockSpec((POOL_HALO, POOL_DIM),
                               lambda i: (jnp.minimum((i + 1) * hpt, n_halo - 1), LAT_W // POOL_DIM)),
                  pl.BlockSpec((tm, d), lambda i: (i, (LAT_W + POOL_DIM) // d)),
                  pl.BlockSpec((tm, d), lambda i: (i, (LAT_W + POOL_DIM) // d + 1)),
                  pl.BlockSpec(wp.shape, lambda i: (0, 0, 0)),
                  pl.BlockSpec((1, d), const2),
                  pl.BlockSpec((d, d), const2, pipeline_mode=pl.Buffered(1)),
                  pl.BlockSpec((tm, d), row),
                  _mod_spec(2, mrow),
                  pl.BlockSpec((1, d), const2),
                  _mod_spec(3, mrow),
                  _mod_spec(4, mrow),
                  pl.BlockSpec((N_EXPERTS, d), const2),
                  pl.BlockSpec((N_EXPERTS, 1), const2)],
        out_specs=[pl.BlockSpec((tm, d), row),
                   pl.BlockSpec((tm, PACK_W), row),
                   pl.BlockSpec((ROUTE_ROWS, tm), lambda i: (0, i)),
                   pl.BlockSpec((tm, LANES), row)],
        out_shape=[jax.ShapeDtypeStruct((t, d), F32),
                   jax.ShapeDtypeStruct((t, PACK_W), jnp.int32),
                   jax.ShapeDtypeStruct((ROUTE_ROWS, t), F32),
                   jax.ShapeDtypeStruct((t, LANES), F32)],
        compiler_params=_params(1),
        name="mixer_out",
    )(y, p, p, p, p, p, wp, ps.reshape(1, d), wo, x2, mod3, gf.reshape(1, d), mod3, mod3, wr_t,
      rb.reshape(N_EXPERTS, 1))


def _chunk_view(x):
    n = x.shape[0]
    return (x.reshape(n // SUBLANES, SUBLANES, ROW_CHUNKS, LANES).transpose(0, 2, 1, 3)
            .reshape(n * ROW_CHUNKS, LANES))


def _row_view(c):
    n = c.shape[0] // ROW_CHUNKS
    return (c.reshape(n // SUBLANES, ROW_CHUNKS, SUBLANES, LANES).transpose(0, 2, 1, 3)
            .reshape(n, PACK_W))


def _sc_index_chunks(idx):
    m = idx.shape[0]
    per_row = LANES // ROW_CHUNKS
    assert m % per_row == 0
    lane = np.arange(LANES)
    src = (lane // (SUBLANES * ROW_CHUNKS)) * SUBLANES + lane % SUBLANES
    sel = jnp.asarray(np.equal.outer(np.arange(per_row), src), BF16)
    rows = idx.reshape(m // per_row, per_row)
    pick = lambda v: jnp.dot(v.astype(BF16), sel, preferred_element_type=F32)
    r = (pick(rows >> 8) * 256.0 + pick(rows & 255)).astype(jnp.int32)
    j = jnp.asarray((lane % (SUBLANES * ROW_CHUNKS)) // SUBLANES, jnp.int32)[None, :]
    ids = (r // SUBLANES) * (SUBLANES * ROW_CHUNKS) + j * SUBLANES + r % SUBLANES
    return ids.reshape(1, m * ROW_CHUNKS)


def _sc_mesh():
    return plsc.VectorSubcoreMesh(core_axis_name="core", subcore_axis_name="subcore")


def _sc_gather_rows(table, idx):
    n, m = table.shape[0], idx.shape[0]
    n_idx = m * ROW_CHUNKS
    assert n_idx % SC_WINDOW == 0 and n % SUBLANES == 0 and m % SUBLANES == 0
    assert n <= MAX_INDEXED_ROWS

    @pl.kernel(out_type=jax.ShapeDtypeStruct((n_idx, LANES), jnp.int32), mesh=_sc_mesh())
    def gather(t_hbm, i_hbm, o_hbm):
        def body(i_vmem, o_vmem):
            pltpu.sync_copy(t_hbm.at[i_vmem.at[0]], o_vmem)

        pltpu.emit_pipeline(
            body,
            grid=(n_idx // SC_WINDOW,),
            in_specs=[pl.BlockSpec((1, SC_WINDOW), index_map=lambda i: (0, i))],
            out_specs=[pl.BlockSpec((SC_WINDOW, LANES), index_map=lambda i: (i, 0))],
            core_axis_name=("core", "subcore"),
            dimension_semantics=(pltpu.PARALLEL,),
        )(i_hbm, o_hbm)

    return _row_view(gather(_chunk_view(table), _sc_index_chunks(idx)))


def _sc_scatter_rows(rows, idx, n_out):
    n, m = rows.shape[0], idx.shape[0]
    n_idx = m * ROW_CHUNKS
    src_steps = n * ROW_CHUNKS // SC_WINDOW
    assert n_idx % SC_WINDOW == 0 and (n * ROW_CHUNKS) % SC_WINDOW == 0 and m % n == 0
    assert n_out <= MAX_INDEXED_ROWS
    assert n % SUBLANES == 0 and n_out % SUBLANES == 0

    @pl.kernel(out_type=jax.ShapeDtypeStruct((n_out * ROW_CHUNKS, LANES), jnp.int32), mesh=_sc_mesh())
    def scatter(x_hbm, i_hbm, o_hbm):
        def body(x_vmem, i_vmem):
            pltpu.sync_copy(x_vmem, o_hbm.at[i_vmem.at[0]])

        pltpu.emit_pipeline(
            body,
            grid=(n_idx // SC_WINDOW,),
            in_specs=[pl.BlockSpec((SC_WINDOW, LANES), index_map=lambda i: (i % src_steps, 0)),
                      pl.BlockSpec((1, SC_WINDOW), index_map=lambda i: (0, i))],
            out_specs=[],
            core_axis_name=("core", "subcore"),
            dimension_semantics=(pltpu.PARALLEL,),
        )(x_hbm, i_hbm)

    return _row_view(scatter(_chunk_view(rows), _sc_index_chunks(idx)))


def _dispatch_plan(route):
    t = route.shape[1]
    n_pairs = 2 * t
    n_tiles = n_pairs // MOE_TILE + N_EXPERTS
    blk = LANES
    e = jnp.concatenate([route[0], route[1]]).astype(jnp.int32)
    onehot = (e[:, None] == jnp.arange(N_EXPERTS, dtype=jnp.int32)[None, :]).astype(F32)
    oh3 = onehot.reshape(n_pairs // blk, blk, N_EXPERTS)
    tri = (jnp.arange(blk)[:, None] >= jnp.arange(blk)[None, :]).astype(F32)
    within = jnp.einsum("ij,bjk->bik", tri, oh3)
    totals = within[:, -1, :]
    before = jnp.cumsum(totals, axis=0) - totals
    csum = (within + before[:, None, :]).reshape(n_pairs, N_EXPERTS)
    counts = jnp.sum(totals, axis=0).astype(jnp.int32)
    padded = ((counts + MOE_TILE - 1) // MOE_TILE) * MOE_TILE
    ends = jnp.cumsum(padded)
    starts = ends - padded
    pair_slot = jnp.sum(onehot * (csum - 1.0 + starts.astype(F32)[None, :]), axis=1).astype(jnp.int32)
    tile_start = jnp.arange(n_tiles, dtype=jnp.int32) * MOE_TILE
    tile_e = jnp.minimum(jnp.sum((tile_start[:, None] >= ends[None, :]).astype(jnp.int32), axis=1), N_EXPERTS - 1)
    e_onehot = tile_e[:, None] == jnp.arange(N_EXPERTS, dtype=jnp.int32)[None, :]
    filled = jnp.sum(jnp.where(e_onehot, (starts + counts)[None, :], 0), axis=1)
    tile_rows = jnp.clip(filled - tile_start, 0, MOE_TILE).astype(jnp.int32)
    last_used = jnp.maximum(ends[-1] // MOE_TILE - 1, 0)
    tile_expert = jnp.where(tile_start < ends[-1], tile_e, tile_e[last_used]).astype(jnp.int32)
    tile_block = jnp.minimum(jnp.arange(n_tiles, dtype=jnp.int32), last_used).astype(jnp.int32)
    return pair_slot, tile_expert, tile_rows, tile_block


def _swiglu(xb, wg, wu, wd):
    gate = jnp.dot(xb, wg, preferred_element_type=F32)
    up = jnp.dot(xb, wu, preferred_element_type=F32)
    a = ((gate * _sigmoid(gate)) * up).astype(BF16)
    return jnp.dot(a, wd, preferred_element_type=F32)


def _ffn_kernel(te_ref, rows_ref, blk_ref, xs_ref, wg_ref, wu_ref, wd_ref, ys_ref, wg_s, wu_s, wd_s):
    i = pl.program_id(0)
    new_expert = jnp.logical_or(i == 0, te_ref[i] != te_ref[jnp.maximum(i - 1, 0)])

    @pl.when(new_expert)
    def _():
        wg_s[...] = wg_ref[0].astype(BF16)
        wu_s[...] = wu_ref[0].astype(BF16)
        wd_s[...] = wd_ref[0].astype(BF16)

    @pl.when(rows_ref[i] > 0)
    def _():
        live = lax.broadcasted_iota(jnp.int32, (MOE_TILE, 1), 0) < rows_ref[i]
        xb = _load_unpack(xs_ref, live).astype(BF16)
        ys_ref[...] = _pack_rows(_swiglu(xb, wg_s[...], wu_s[...], wd_s[...]))


def _expert_ffn(xs, tile_expert, tile_rows, tile_block, wg, wu, wd, layer):
    n_slots = xs.shape[0]
    d = D_MODEL
    blk = pl.BlockSpec((MOE_TILE, PACK_W), lambda i, te, nr, tb: (tb[i], 0))
    return pl.pallas_call(
        _ffn_kernel,
        grid_spec=pltpu.PrefetchScalarGridSpec(
            num_scalar_prefetch=3,
            grid=(n_slots // MOE_TILE,),
            in_specs=[blk,
                      pl.BlockSpec((None, 1, d, D_EXPERT), lambda i, te, nr, tb: (layer, te[i], 0, 0)),
                      pl.BlockSpec((None, 1, d, D_EXPERT), lambda i, te, nr, tb: (layer, te[i], 0, 0)),
                      pl.BlockSpec((None, 1, D_EXPERT, d), lambda i, te, nr, tb: (layer, te[i], 0, 0))],
            out_specs=blk,
            scratch_shapes=[pltpu.VMEM((d, D_EXPERT), BF16), pltpu.VMEM((d, D_EXPERT), BF16),
                            pltpu.VMEM((D_EXPERT, d), BF16)]),
        out_shape=jax.ShapeDtypeStruct(xs.shape, jnp.int32),
        compiler_params=_params(1),
        name="expert_ffn",
    )(tile_expert, tile_rows, tile_block, xs, wg, wu, wd)


def _combine_kernel(h_ref, y1_ref, y2_ref, w_ref, sg_ref, su_ref, sd_ref, x_ref, g2_ref, gfin_ref, o_ref,
                    *, final):
    y = _swiglu(_load_unpack(h_ref).astype(BF16), sg_ref[...], su_ref[...], sd_ref[...])
    w = w_ref[...]
    y = y + w[:, 0:1] * _load_unpack(y1_ref) + w[:, 1:2] * _load_unpack(y2_ref)
    xn = x_ref[...] + g2_ref[0] * y
    if final:
        xn = _rms(xn, gfin_ref[...])
    o_ref[...] = xn


def _moe_combine(h2p, yg, w12, sg, su, sd, x2, set_off, x_off, n_rows, mod3, brow, gfin, final):
    d = x2.shape[1]
    tm = min(ROW_TILE, n_rows)
    assert set_off % tm == 0 and x_off % tm == 0 and n_rows % tm == 0
    s_blk, x_blk, n_blk = set_off // tm, x_off // tm, n_rows // tm
    const2 = lambda i: (0, 0)
    pk = lambda o: pl.BlockSpec((tm, PACK_W), lambda i: (i + o, 0))
    x_spec = pl.BlockSpec((tm, d), lambda i: (i + x_blk, 0))
    return pl.pallas_call(
        functools.partial(_combine_kernel, final=final),
        grid=(n_blk,),
        in_specs=[pk(s_blk), pk(0), pk(n_blk),
                  pl.BlockSpec((tm, LANES), lambda i: (i + x_blk, 0)),
                  pl.BlockSpec(sg.shape, const2), pl.BlockSpec(su.shape, const2), pl.BlockSpec(sd.shape, const2),
                  x_spec,
                  _mod_spec(5, lambda i: brow((i + x_blk) * tm)),
                  pl.BlockSpec((1, d), const2)],
        out_specs=x_spec,
        out_shape=jax.ShapeDtypeStruct(x2.shape, F32),
        input_output_aliases={7: 0},
        compiler_params=_params(1),
        name="moe_combine",
    )(h2p, yg, yg, w12, sg, su, sd, x2, mod3, gfin.reshape(1, d))


def _moe_routed(parts, wg, wu, wd, layer, sg, su, sd, mod3, gfin, final):
    h2p = parts[0][0] if len(parts) == 1 else jnp.concatenate([pt[0] for pt in parts], axis=0)
    route = parts[0][1] if len(parts) == 1 else jnp.concatenate([pt[1] for pt in parts], axis=1)
    t_all = h2p.shape[0]
    pair_slot, tile_expert, tile_rows, tile_block = _dispatch_plan(route)
    n_slots = tile_expert.shape[0] * MOE_TILE
    xs = _sc_scatter_rows(h2p, pair_slot, n_slots)
    ys = _expert_ffn(xs, tile_expert, tile_rows, tile_block, wg, wu, wd, layer)
    outs, set_off = [], 0
    for _, _, w12, x2, brow in parts:
        t = x2.shape[0]
        n_rows = t // COMBINE_CHUNKS if t % (COMBINE_CHUNKS * ROW_TILE) == 0 else t
        for x_off in range(0, t, n_rows):
            lo = set_off + x_off
            idx = jnp.concatenate([pair_slot[lo:lo + n_rows], pair_slot[t_all + lo:t_all + lo + n_rows]])
            yg = _sc_gather_rows(ys, idx)
            x2 = _moe_combine(h2p, yg, w12, sg, su, sd, x2, lo, x_off, n_rows, mod3, brow, gfin, final)
        outs.append(x2)
        set_off += t
    return outs


def _rope_tables(seq):
    rows = seq // GRID_W
    row = np.repeat(np.arange(rows), GRID_W).astype(np.float32)
    col = np.tile(np.arange(GRID_W), rows).astype(np.float32)
    axis_dim = QK_ROPE // 2
    inv = (ROPE_THETA ** (-np.arange(0, axis_dim, 2, dtype=np.float32) / axis_dim)).astype(np.float32)
    ang = np.concatenate([row[:, None] * inv, col[:, None] * inv], axis=-1).astype(np.float32)
    cos, sin = np.cos(ang), np.sin(ang)
    zero = np.zeros_like(cos)
    return (jnp.asarray(np.concatenate([cos, cos, zero, zero], axis=1)),
            jnp.asarray(np.concatenate([-sin, sin, zero, zero], axis=1)))


def _rope_group(w, start):
    half = QK_ROPE // 2
    x1, x2 = w[..., start:start + half], w[..., start + half:start + 2 * half]
    return jnp.concatenate([x1, x2, x2, x1], axis=-1)


def _w_in_relayout_kernel(w_ref, o_ref):
    half = QK_ROPE // 2
    x1, x2 = KV_RANK, KV_RANK + half
    pieces = [(0, KV_RANK), (KV_COLS, Q_RANK), (x1, half), (x2, half), (x2, half), (x1, half),
              (Q_END, o_ref.shape[0] - LAT_W)]
    dst = 0
    for src, n in pieces:
        o_ref[dst:dst + n, :] = w_ref[src:src + n, :].astype(BF16)
        dst += n


def _relayout_w_in(w, layer):
    _, d, n_in = w.shape
    tc = 256
    return pl.pallas_call(
        _w_in_relayout_kernel,
        grid=(d // tc,),
        in_specs=[pl.BlockSpec((None, n_in, tc), lambda i: (layer, 0, i))],
        out_specs=pl.BlockSpec((P_COLS, tc), lambda i: (0, i)),
        out_shape=jax.ShapeDtypeStruct((P_COLS, d), BF16),
        compiler_params=_params(1),
        name="w_in_relayout",
    )(jnp.swapaxes(w, 1, 2))


def _relayout_w_qb(w):
    dq = QK_NOPE + QK_ROPE
    cols = []
    for h in range(N_HEADS):
        cols += [w[:, h * dq:h * dq + QK_NOPE], _rope_group(w, h * dq + QK_NOPE)]
    return jnp.concatenate(cols, axis=1).astype(BF16)


def kernel(x, c, ctx, c_ctx, w_ada, b_ada, norm_mix_g, norm_ffn_g, w_in, q_norm_g, kv_norm_g, w_qb, w_kvb, w_pool, pool_scale, w_out, w_router, router_bias, w_exp_gate, w_exp_up, w_exp_down, w_sh_gate, w_sh_up, w_sh_down, final_norm_g):
    bsz, seq, d = x.shape
    n_ctx = ctx.shape[1]
    depth = w_ada.shape[0]
    assert bsz < MOD_ROWS and d == D_MODEL

    c_rows = jnp.concatenate([c, c_ctx[None], jnp.zeros((MOD_ROWS - bsz - 1, d), F32)], axis=0)

    cos_t, sin_t = _rope_tables(seq)
    ones_t = jnp.concatenate([jnp.ones((n_ctx, 2 * (QK_ROPE // 2)), F32),
                              jnp.zeros((n_ctx, LANES - QK_ROPE), F32)], axis=1)
    zeros_t = jnp.zeros((n_ctx, LANES), F32)
    wr_t = w_router.T.astype(BF16)
    lat_row = lambda r: r // seq
    ctx_row = lambda r: bsz

    x2 = x.reshape(bsz * seq, d)
    xc2 = ctx.reshape(bsz * n_ctx, d)
    for l in range(depth):
        last = l == depth - 1
        mod3 = _ada_mod(c_rows, w_ada, b_ada, l).reshape(MOD_ROWS * N_MOD, 1, d)
        wq = _relayout_w_qb(w_qb[l])
        w_kv3 = w_kvb[l].reshape(KV_RANK, N_HEADS, QK_NOPE + V_DIM)
        wk = w_kv3[:, :, :QK_NOPE].reshape(KV_RANK, N_HEADS * QK_NOPE).astype(BF16)
        wvt = (w_kv3[:, :, QK_NOPE:].transpose(1, 2, 0).reshape(N_HEADS * V_DIM, KV_RANK)
               .astype(BF16))
        wp = w_pool[l].astype(BF16)
        wo = w_out[l].astype(BF16)
        sg, su, sd = w_sh_gate[l].astype(BF16), w_sh_up[l].astype(BF16), w_sh_down[l].astype(BF16)
        moe = functools.partial(_moe_routed, wg=w_exp_gate, wu=w_exp_up, wd=w_exp_down, layer=l,
                                sg=sg, su=su, sd=sd, mod3=mod3, gfin=final_norm_g)

        w_in_r = _relayout_w_in(w_in, l)
        p = _in_proj(x2, mod3, lat_row, norm_mix_g[l], w_in_r, P_COLS)
        pc = _in_proj(xc2, mod3, ctx_row, norm_mix_g[l], w_in_r, LAT_W if last else P_COLS)
        q, k_lat, v_lat = _qkv(p, bsz, seq, cos_t, sin_t, q_norm_g[l], kv_norm_g[l], wq, wk, wvt, True)
        ctx_out = _qkv(pc, bsz, n_ctx, ones_t, zeros_t, q_norm_g[l], kv_norm_g[l], wq, wk, wvt, not last)
        k_ctx, v_ctx = ctx_out[-2:]
        y = _attention(q, [k_ctx, k_lat], [v_ctx, v_lat])
        x2, h2p, route, w12 = _mixer_out(y, p, x2, mod3, lat_row, seq, wp, pool_scale[l], wo,
                                         norm_ffn_g[l], wr_t, router_bias)
        if last:
            x2, = moe([(h2p, route, w12, x2, lat_row)], final=True)
        else:
            yc = _attention(ctx_out[0], [k_ctx], [v_ctx])
            xc2, h2c, route_c, w12c = _mixer_out(yc, pc, xc2, mod3, ctx_row, n_ctx, wp, pool_scale[l], wo,
                                                 norm_ffn_g[l], wr_t, router_bias)
            x2, xc2 = moe([(h2p, route, w12, x2, lat_row), (h2c, route_c, w12c, xc2, ctx_row)], final=False)
    return x2.reshape(bsz, seq, d)
```
